```python
import jax, jax.numpy as jnp
from jax import lax
import numpy as np

D_MODEL = 1024
BATCH = 16
SEQ = 2048
DEPTH = 2

N_GROUPS = 4
GROUP_HEADS = 4
GROUP_WIDTH = D_MODEL // N_GROUPS
HEAD_DIM = GROUP_WIDTH // GROUP_HEADS
D_MIX = N_GROUPS * GROUP_WIDTH
OFF_RG_X = 0
OFF_RG_G = GROUP_WIDTH
OFF_FOX = 2 * GROUP_WIDTH
OFF_FOX_F = 5 * GROUP_WIDTH
OFF_SB = OFF_FOX_F + GROUP_HEADS
OFF_SC = OFF_SB + 3 * GROUP_WIDTH
N_IN = OFF_SC + 3 * GROUP_WIDTH
RG_CONV = 4
RGLRU_C = 8.0
SC_CONV = 3
Q_BLOCK = 128
N_MEM = 256
XA_HEADS = 4
XA_HEAD_DIM = D_MODEL // XA_HEADS
PEER_HEADS = 8
N_KEYS = 128
N_EXPERTS = N_KEYS * N_KEYS
PEER_QDIM = 256
PEER_HALF = PEER_QDIM // 2
PEER_TOPK = 16
PEER_TOKEN_BLOCK = 128
ALPHA = (2.0 * DEPTH) ** 0.25
BETA = (8.0 * DEPTH) ** -0.25
LN_EPS = 1e-5

kernel_name = 'hymba_style_hybrid_peer_deepnorm'

F32 = jnp.float32


def layer_norm(x, g, b):
    xf = x.astype(F32)
    mu = jnp.mean(xf, axis=-1, keepdims=True)
    xc = xf - mu
    var = jnp.mean(xc * xc, axis=-1, keepdims=True)
    return (xc * lax.rsqrt(var + LN_EPS) * g.astype(F32) + b.astype(F32)).astype(x.dtype)


def group_rms(y):
    yf = y.astype(F32)
    return yf * lax.rsqrt(jnp.mean(yf * yf, axis=-1, keepdims=True) + 1e-6)


def causal_depthwise_conv(x, w):
    K = w.shape[0]
    S = x.shape[1]
    xp = jnp.pad(x, ((0, 0), (K - 1, 0), (0, 0)))
    y = xp[:, 0:S] * w[0]
    for k in range(1, K):
        y = y + xp[:, k:k + S] * w[k]
    return y


def block_diag_linear(x, w, b):
    B, S, _ = x.shape
    H, d, _ = w.shape
    y = jnp.einsum('bshi,hij->bshj', x.reshape(B, S, H, d), w)
    return y.reshape(B, S, H * d) + b


def rg_lru_branch(x, gate, conv_w, conv_b, wa, ba, wi, bi, lam):
    xc = (causal_depthwise_conv(x, conv_w) + conv_b).astype(F32)
    r = jax.nn.sigmoid(block_diag_linear(xc, wa.astype(F32), ba.astype(F32)))
    i = jax.nn.sigmoid(block_diag_linear(xc, wi.astype(F32), bi.astype(F32)))
    log_a = -RGLRU_C * r * jax.nn.softplus(-lam.astype(F32))
    a = jnp.exp(log_a)
    u = jnp.sqrt(-jnp.expm1(2.0 * log_a)) * (i * xc)

    def combine(left, right):
        a_l, b_l = left
        a_r, b_r = right
        return a_l * a_r, a_r * b_l + b_r

    _, h = lax.associative_scan(combine, (a, u), axis=1)
    return (h * jax.nn.gelu(gate.astype(F32))).astype(x.dtype)


def split_heads(t):
    B, S, _ = t.shape
    return t.reshape(B, S, GROUP_HEADS, HEAD_DIM).transpose(0, 2, 1, 3)


def merge_heads(t):
    B, H, S, d = t.shape
    return t.transpose(0, 2, 1, 3).reshape(B, S, H * d)


def to_blocks(t):
    nb = t.shape[2] // Q_BLOCK
    t = t.reshape(t.shape[:2] + (nb, Q_BLOCK) + t.shape[3:])
    return jnp.moveaxis(t, 2, 0)


def from_blocks(o):
    o = jnp.moveaxis(o, 0, 2)
    return o.reshape(o.shape[0], o.shape[1], -1, o.shape[-1])


def forgetting_attention(q, k, v, f_logit, b_f):
    S, d = q.shape[2], q.shape[3]
    log_f = jax.nn.log_sigmoid(f_logit.astype(F32) + b_f.astype(F32))
    cum = jnp.cumsum(log_f, axis=1).transpose(0, 2, 1)
    key_pos = jnp.arange(S)
    scale = d ** -0.5

    def block(args):
        qi, ci, blk = args
        q_pos = blk * Q_BLOCK + jnp.arange(Q_BLOCK)
        logits = (jnp.einsum('bhqd,bhkd->bhqk', qi, k).astype(F32) * scale
                  + ci[..., :, None] - cum[..., None, :])
        logits = jnp.where(key_pos[None, :] <= q_pos[:, None], logits, -jnp.inf)
        p = jax.nn.softmax(logits, axis=-1)
        return jnp.einsum('bhqk,bhkd->bhqd', p.astype(v.dtype), v)

    out = lax.map(block, (to_blocks(q), to_blocks(cum), jnp.arange(S // Q_BLOCK)))
    return from_blocks(out)


def stick_breaking_attention(q, k, v):
    S, d = q.shape[2], q.shape[3]
    key_pos = jnp.arange(S)
    scale = d ** -0.5

    def block(args):
        qi, blk = args
        q_pos = blk * Q_BLOCK + jnp.arange(Q_BLOCK)
        z = jnp.einsum('bhqd,bhkd->bhqk', qi, k).astype(F32) * scale
        strict = key_pos[None, :] < q_pos[:, None]
        log_1m = jnp.where(strict, jax.nn.log_sigmoid(-z), 0.0)
        tail = lax.cumsum(log_1m, axis=3, reverse=True) - log_1m
        log_w = jnp.where(strict, jax.nn.log_sigmoid(z) + tail, -jnp.inf)
        w = jnp.exp(log_w)
        return jnp.einsum('bhqk,bhkd->bhqd', w.astype(v.dtype), v)

    out = lax.map(block, (to_blocks(q), jnp.arange(S // Q_BLOCK)))
    return from_blocks(out)


def hybrid_mixer(x, w_in, w_out, rg_conv_w, rg_conv_b, rg_wa, rg_ba, rg_wi, rg_bi,
                 rg_lambda, fox_bf, sc_conv_w, mix_norm_g):
    GW = GROUP_WIDTH
    p = x @ w_in
    y_rg = rg_lru_branch(p[..., OFF_RG_X:OFF_RG_X + GW], p[..., OFF_RG_G:OFF_RG_G + GW],
                         rg_conv_w, rg_conv_b, rg_wa, rg_ba, rg_wi, rg_bi, rg_lambda)
    fq = split_heads(p[..., OFF_FOX:OFF_FOX + GW])
    fk = split_heads(p[..., OFF_FOX + GW:OFF_FOX + 2 * GW])
    fv = split_heads(p[..., OFF_FOX + 2 * GW:OFF_FOX + 3 * GW])
    y_fox = merge_heads(forgetting_attention(fq, fk, fv,
                                             p[..., OFF_FOX_F:OFF_FOX_F + GROUP_HEADS], fox_bf))
    sq = split_heads(p[..., OFF_SB:OFF_SB + GW])
    sk = split_heads(p[..., OFF_SB + GW:OFF_SB + 2 * GW])
    sv = split_heads(p[..., OFF_SB + 2 * GW:OFF_SB + 3 * GW])
    y_sb = merge_heads(stick_breaking_attention(sq, sk, sv))
    bg = p[..., OFF_SC:OFF_SC + GW]
    cg = p[..., OFF_SC + GW:OFF_SC + 2 * GW]
    hh = p[..., OFF_SC + 2 * GW:OFF_SC + 3 * GW]
    y_sc = bg * causal_depthwise_conv(cg * hh, sc_conv_w)
    y = jnp.concatenate([group_rms(y_rg), group_rms(y_fox), group_rms(y_sb), group_rms(y_sc)],
                        axis=-1) * mix_norm_g.astype(F32)
    return y.astype(x.dtype) @ w_out


def memory_cross_attention(x, mem, wq, wkv, wo):
    B, S, D = x.shape
    M = mem.shape[1]
    q = (x @ wq).reshape(B, S, XA_HEADS, XA_HEAD_DIM)
    kv = mem @ wkv
    k = kv[..., :D].reshape(B, M, XA_HEADS, XA_HEAD_DIM)
    v = kv[..., D:].reshape(B, M, XA_HEADS, XA_HEAD_DIM)
    logits = jnp.einsum('bshd,bmhd->bhsm', q, k).astype(F32) * XA_HEAD_DIM ** -0.5
    p = jax.nn.softmax(logits, axis=-1)
    o = jnp.einsum('bhsm,bmhd->bshd', p.astype(v.dtype), v).reshape(B, S, D)
    return o @ wo


def peer_ffn(x, wq, k1, k2, u, v):
    B, S, D = x.shape
    qry = (x @ wq).reshape(B, S, PEER_HEADS, PEER_QDIM)
    s1 = jnp.einsum('bshc,nc->bshn', qry[..., :PEER_HALF], k1).astype(F32)
    s2 = jnp.einsum('bshc,nc->bshn', qry[..., PEER_HALF:], k2).astype(F32)
    v1, i1 = lax.top_k(s1, PEER_TOPK)
    v2, i2 = lax.top_k(s2, PEER_TOPK)
    cand = (v1[..., :, None] + v2[..., None, :]).reshape(B, S, PEER_HEADS, PEER_TOPK * PEER_TOPK)
    sc, ci = lax.top_k(cand, PEER_TOPK)
    e1 = jnp.take_along_axis(i1, ci // PEER_TOPK, axis=-1)
    e2 = jnp.take_along_axis(i2, ci % PEER_TOPK, axis=-1)
    expert = e1 * N_KEYS + e2
    gate = jax.nn.softmax(sc, axis=-1)
    nt = (B * S) // PEER_TOKEN_BLOCK
    E = PEER_HEADS * PEER_TOPK
    xt = x.reshape(nt, PEER_TOKEN_BLOCK, D)
    et = expert.reshape(nt, PEER_TOKEN_BLOCK, E)
    gt = gate.reshape(nt, PEER_TOKEN_BLOCK, E)

    def block(args):
        xb, eb, gb = args
        act = jax.nn.gelu(jnp.einsum('td,ted->te', xb, u[eb]).astype(F32), approximate=False)
        coef = (gb * act).astype(x.dtype)
        return jnp.einsum('te,ted->td', coef, v[eb])

    return lax.map(block, (xt, et, gt)).reshape(B, S, D)


def setup_inputs(seed: int = 0) -> dict:
    key = jax.random.key(seed)
    keys = iter(jax.random.split(key, 40))
    L, D, GW, GH, HD = DEPTH, D_MODEL, GROUP_WIDTH, GROUP_HEADS, HEAD_DIM

    def nrm(shape, scale):
        return jax.random.normal(next(keys), shape, F32) * scale

    a0 = jax.random.uniform(next(keys), (L, GW), F32, minval=0.9, maxval=0.999)
    s0 = a0 ** (1.0 / RGLRU_C)
    rg_lambda = jnp.log(s0) - jnp.log1p(-s0)
    return {
        'x': nrm((BATCH, SEQ, D), 1.0),
        'mem': nrm((BATCH, N_MEM, D), 1.0),
        'w_in': nrm((L, D, N_IN), D ** -0.5),
        'w_out': nrm((L, D_MIX, D), D_MIX ** -0.5 * BETA),
        'rg_conv_w': nrm((L, RG_CONV, GW), RG_CONV ** -0.5),
        'rg_conv_b': nrm((L, GW), 0.01),
        'rg_wa': nrm((L, GH, HD, HD), HD ** -0.5),
        'rg_ba': nrm((L, GW), 0.01),
        'rg_wi': nrm((L, GH, HD, HD), HD ** -0.5),
        'rg_bi': nrm((L, GW), 0.01),
        'rg_lambda': rg_lambda,
        'fox_bf': jax.random.uniform(next(keys), (L, GH), F32, minval=1.0, maxval=4.0),
        'sc_conv_w': nrm((L, SC_CONV, GW), SC_CONV ** -0.5),
        'mix_norm_g': 1.0 + nrm((L, D_MIX), 0.01),
        'ln1_g': 1.0 + nrm((L, D), 0.01),
        'ln1_b': nrm((L, D), 0.01),
        'xa_wq': nrm((L, D, D), D ** -0.5),
        'xa_wkv': nrm((L, D, 2 * D), D ** -0.5),
        'xa_wo': nrm((L, D, D), D ** -0.5 * BETA),
        'ln2_g': 1.0 + nrm((L, D), 0.01),
        'ln2_b': nrm((L, D), 0.01),
        'peer_wq': nrm((L, D, PEER_HEADS * PEER_QDIM), D ** -0.5),
        'peer_k1': nrm((L, N_KEYS, PEER_HALF), PEER_HALF ** -0.5),
        'peer_k2': nrm((L, N_KEYS, PEER_HALF), PEER_HALF ** -0.5),
        'peer_u': nrm((L, N_EXPERTS, D), D ** -0.5),
        'peer_v': nrm((L, N_EXPERTS, D), PEER_HEADS ** -0.5 * BETA),
        'ln3_g': 1.0 + nrm((L, D), 0.01),
        'ln3_b': nrm((L, D), 0.01),
    }


def reference(x, mem, w_in, w_out, rg_conv_w, rg_conv_b, rg_wa, rg_ba, rg_wi, rg_bi,
              rg_lambda, fox_bf, sc_conv_w, mix_norm_g, ln1_g, ln1_b, xa_wq, xa_wkv, xa_wo,
              ln2_g, ln2_b, peer_wq, peer_k1, peer_k2, peer_u, peer_v, ln3_g, ln3_b):
    for l in range(DEPTH):
        mix = hybrid_mixer(x, w_in[l], w_out[l], rg_conv_w[l], rg_conv_b[l], rg_wa[l], rg_ba[l],
                           rg_wi[l], rg_bi[l], rg_lambda[l], fox_bf[l], sc_conv_w[l],
                           mix_norm_g[l])
        x = layer_norm(ALPHA * x + mix, ln1_g[l], ln1_b[l])
        xa = memory_cross_attention(x, mem, xa_wq[l], xa_wkv[l], xa_wo[l])
        x = layer_norm(ALPHA * x + xa, ln2_g[l], ln2_b[l])
        ff = peer_ffn(x, peer_wq[l], peer_k1[l], peer_k2[l], peer_u[l], peer_v[l])
        x = layer_norm(ALPHA * x + ff, ln3_g[l], ln3_b[l])
    return x
```

```python
import functools
import math

import jax
import jax.numpy as jnp
from jax import lax
from jax.experimental import pallas as pl
from jax.experimental.pallas import tpu as pltpu

F32 = jnp.float32
BF16 = jnp.bfloat16
I32 = jnp.int32

D_MODEL = 1024
GROUP_WIDTH = 256
GROUP_HEADS = 4
HEAD_DIM = 64
N_IN = 2820
RGLRU_C = 8.0
XA_HEADS = 4
XA_HEAD_DIM = D_MODEL // XA_HEADS
PEER_HEADS = 8
N_KEYS = 128
PEER_HALF = 128
PEER_TOPK = 16
DEPTH = 2
ALPHA = (2.0 * DEPTH) ** 0.25
LN_EPS = 1e-5

SUBLANES = 8
LANES = 128
WORD_ROWS = D_MODEL // (2 * LANES)
ROW_CHUNKS = D_MODEL // LANES
CHUNK_SHIFT = ROW_CHUNKS.bit_length() - 1

VMEM_LIMIT = 48 * 1024 * 1024


def _params(sem, vmem=VMEM_LIMIT):
    return pltpu.CompilerParams(dimension_semantics=sem, vmem_limit_bytes=vmem)


def _layer_norm(z, g, b):
    mu = jnp.mean(z, axis=-1, keepdims=True)
    zc = z - mu
    var = jnp.mean(zc * zc, axis=-1, keepdims=True)
    return zc * lax.rsqrt(var + LN_EPS) * g + b


def _log_sigmoid(z):
    return jnp.minimum(z, 0.0) - jnp.log1p(jnp.exp(-jnp.abs(z)))


def _nt_dot(a, b):
    return lax.dot_general(a, b, (((1,), (1,)), ((), ())), preferred_element_type=F32)


C_RG, C_FOX, C_SB, C_SC, C_F, C_END = 0, 512, 1280, 2048, 2816, 2944


def _inproj_kernel(x_ref, w_ref, rg_ref, fox_ref, sb_ref, sc_ref, f_ref):
    xb = x_ref[...].astype(BF16)

    def mm(lo, hi):
        return jnp.dot(xb, w_ref[:, lo:hi], preferred_element_type=F32)

    rg_ref[...] = mm(C_RG, C_FOX)
    fox_ref[...] = mm(C_FOX, C_SB).astype(BF16)
    sb_ref[...] = mm(C_SB, C_SC).astype(BF16)
    sc_ref[...] = mm(C_SC, C_F)
    f_ref[...] = mm(C_F, C_END)


def _inproj(x2d, w_cat, tm=512):
    T, D = x2d.shape
    widths = (C_FOX - C_RG, C_SB - C_FOX, C_SC - C_SB, C_F - C_SC, C_END - C_F)
    dtypes = (F32, BF16, BF16, F32, F32)
    return pl.pallas_call(
        _inproj_kernel,
        grid=(T // tm,),
        in_specs=[pl.BlockSpec((tm, D), lambda i: (i, 0)),
                  pl.BlockSpec((D, C_END), lambda i: (0, 0))],
        out_specs=[pl.BlockSpec((tm, w), lambda i: (i, 0)) for w in widths],
        out_shape=[jax.ShapeDtypeStruct((T, w), dt) for w, dt in zip(widths, dtypes)],
        compiler_params=_params(("parallel",)),
        name="inproj",
    )(x2d, w_cat)


def _rgsc_kernel(rg_ref, sc_ref, f_ref, cw_ref, cb_ref, wa_ref, ba_ref, wi_ref, bi_ref, lam_ref,
                 scw_ref, fb_ref, yrg_ref, ysc_ref, cum_ref, xprev, chprev, hprev, cprev, *, ts):
    GW = GROUP_WIDTH

    @pl.when(pl.program_id(1) == 0)
    def _():
        xprev[...] = jnp.zeros_like(xprev)
        chprev[...] = jnp.zeros_like(chprev)
        hprev[...] = jnp.zeros_like(hprev)
        cprev[...] = jnp.zeros_like(cprev)

    row = lax.broadcasted_iota(I32, (ts, GW), 0)

    def delayed(prev, cur, d):
        ext = jnp.concatenate([prev, cur], axis=0)
        return pltpu.roll(ext, d, 0)[SUBLANES:]

    xr = rg_ref[0, :, :GW]
    gate = rg_ref[0, :, GW:]
    xp = xprev[...]
    cw = cw_ref[...]
    xc = (delayed(xp, xr, 3) * cw[0:1] + delayed(xp, xr, 2) * cw[1:2]
          + delayed(xp, xr, 1) * cw[2:3] + xr * cw[3:4] + cb_ref[...])
    xprev[...] = xr[ts - SUBLANES:]
    xcb = xc.astype(BF16)
    r = jax.nn.sigmoid(jnp.dot(xcb, wa_ref[...], preferred_element_type=F32) + ba_ref[...])
    ig = jax.nn.sigmoid(jnp.dot(xcb, wi_ref[...], preferred_element_type=F32) + bi_ref[...])
    z = -lam_ref[...]
    softplus = jnp.maximum(z, 0.0) + jnp.log1p(jnp.exp(-jnp.abs(z)))
    log_a = -RGLRU_C * r * softplus
    a = jnp.exp(log_a)
    u = jnp.sqrt(-jnp.tanh(log_a) * (a * a + 1.0)) * (ig * xc)
    acc_a, acc_b = a, u
    d = 1
    while d < ts:
        keep = row >= d
        a_s = jnp.where(keep, pltpu.roll(acc_a, d, 0), 1.0)
        b_s = jnp.where(keep, pltpu.roll(acc_b, d, 0), 0.0)
        acc_b = acc_a * b_s + acc_b
        acc_a = acc_a * a_s
        d *= 2
    h = acc_b + acc_a * hprev[...]
    hprev[...] = h[ts - 1:]
    c0 = math.sqrt(2.0 / math.pi)
    gelu = 0.5 * gate * (1.0 + jnp.tanh(c0 * (gate + 0.044715 * gate * gate * gate)))
    yrg_ref[0] = h * gelu

    bg = sc_ref[0, :, :GW]
    ch = sc_ref[0, :, GW:2 * GW] * sc_ref[0, :, 2 * GW:]
    cp = chprev[...]
    sw = scw_ref[...]
    ysc_ref[0] = bg * (delayed(cp, ch, 2) * sw[0:1] + delayed(cp, ch, 1) * sw[1:2] + ch * sw[2:3])
    chprev[...] = ch[ts - SUBLANES:]

    rowf = lax.broadcasted_iota(I32, (ts, LANES), 0)
    c = _log_sigmoid(f_ref[0] + fb_ref[...])
    d = 1
    while d < ts:
        c = c + jnp.where(rowf >= d, pltpu.roll(c, d, 0), 0.0)
        d *= 2
    c = c + cprev[...]
    cprev[...] = c[ts - 1:]
    cum_ref[0] = c


def _rgsc(rg, sc, f, cw, cb, wa, ba, wi, bi, lam, scw, fb, ts=512):
    B, S, _ = rg.shape
    GW = GROUP_WIDTH

    def full(a):
        return pl.BlockSpec(a.shape, lambda b, t: (0,) * a.ndim)

    def seq(w):
        return pl.BlockSpec((1, ts, w), lambda b, t: (b, t, 0))

    params = (cw, cb, wa, ba, wi, bi, lam, scw, fb)
    return pl.pallas_call(
        functools.partial(_rgsc_kernel, ts=ts),
        grid=(B, S // ts),
        in_specs=[seq(2 * GW), seq(3 * GW), seq(LANES)] + [full(p) for p in params],
        out_specs=[seq(GW), seq(GW), seq(LANES)],
        out_shape=[jax.ShapeDtypeStruct((B, S, GW), F32), jax.ShapeDtypeStruct((B, S, GW), F32),
                   jax.ShapeDtypeStruct((B, S, LANES), F32)],
        scratch_shapes=[pltpu.VMEM((SUBLANES, GW), F32), pltpu.VMEM((SUBLANES, GW), F32),
                        pltpu.VMEM((1, GW), F32), pltpu.VMEM((1, LANES), F32)],
        compiler_params=_params(("parallel", "arbitrary")),
        name="rgsc",
    )(rg, sc, f, *params)


def _fox_kernel(q_ref, k_ref, v_ref, cc_ref, cr_ref, o_ref, *, tq):
    qi = pl.program_id(1)
    scale = HEAD_DIM ** -0.5
    rowi = lax.broadcasted_iota(I32, (tq, tq), 0)
    coli = lax.broadcasted_iota(I32, (tq, tq), 1)
    causal = coli <= rowi
    outs = []
    for h in range(GROUP_HEADS):
        q = q_ref[0, h]
        cc = cc_ref[0, h]

        def step(ki, carry, diag, h=h, q=q, cc=cc):
            m, l, acc = carry
            off = pl.multiple_of(ki * tq, tq)
            k = k_ref[0, h, pl.ds(off, tq), :]
            v = v_ref[0, h, pl.ds(off, tq), :]
            s = _nt_dot(q, k) * scale + cc - cr_ref[0, h, ki]
            if diag:
                s = jnp.where(causal, s, -jnp.inf)
            m_new = jnp.maximum(m, jnp.max(s, axis=-1, keepdims=True))
            alpha = jnp.exp(m - m_new)
            p = jnp.exp(s - m_new)
            l = alpha * l + jnp.sum(p, axis=-1, keepdims=True)
            acc = alpha * acc + jnp.dot(p.astype(BF16), v, preferred_element_type=F32)
            return m_new, l, acc

        init = (jnp.full((tq, 1), -jnp.inf, F32), jnp.zeros((tq, 1), F32),
                jnp.zeros((tq, HEAD_DIM), F32))
        carry = lax.fori_loop(0, qi, functools.partial(step, diag=False), init)
        _, l, acc = step(qi, carry, True)
        outs.append(acc / l)
    o_ref[0] = jnp.concatenate(outs, axis=-1)


def _fox(q, k, v, cc, cr, tq=256):
    B, H, S, d = q.shape
    return pl.pallas_call(
        functools.partial(_fox_kernel, tq=tq),
        grid=(B, S // tq),
        in_specs=[pl.BlockSpec((1, H, tq, d), lambda b, i: (b, 0, i, 0)),
                  pl.BlockSpec((1, H, S, d), lambda b, i: (b, 0, 0, 0)),
                  pl.BlockSpec((1, H, S, d), lambda b, i: (b, 0, 0, 0)),
                  pl.BlockSpec((1, H, tq, 1), lambda b, i: (b, 0, i, 0)),
                  pl.BlockSpec((1, H, S // tq, 1, tq), lambda b, i: (b, 0, 0, 0, 0))],
        out_specs=pl.BlockSpec((1, tq, H * d), lambda b, i: (b, i, 0)),
        out_shape=jax.ShapeDtypeStruct((B, S, H * d), F32),
        compiler_params=_params(("parallel", "arbitrary")),
        name="fox_attn",
    )(q, k, v, cc, cr)


def _sb_kernel(q_ref, k_ref, v_ref, o_ref, *, tq):
    qi = pl.program_id(1)
    scale = HEAD_DIM ** -0.5
    rowi = lax.broadcasted_iota(I32, (tq, tq), 0)
    coli = lax.broadcasted_iota(I32, (tq, tq), 1)
    strict = coli < rowi
    later = jnp.where(rowi > coli, 1.0, 0.0).astype(BF16)
    outs = []
    for h in range(GROUP_HEADS):
        q = q_ref[0, h]

        def step(ki, carry, diag, h=h, q=q):
            rest, acc = carry
            off = pl.multiple_of(ki * tq, tq)
            k = k_ref[0, h, pl.ds(off, tq), :]
            v = v_ref[0, h, pl.ds(off, tq), :]
            z = _nt_dot(q, k) * scale
            ls = _log_sigmoid(z)
            l1m = ls - z
            if diag:
                l1m = jnp.where(strict, l1m, 0.0)
            hi = l1m.astype(BF16)
            lo = (l1m - hi.astype(F32)).astype(BF16)
            tail = (jnp.dot(hi, later, preferred_element_type=F32)
                    + jnp.dot(lo, later, preferred_element_type=F32) + rest)
            w = jnp.exp(ls + tail)
            if diag:
                w = jnp.where(strict, w, 0.0)
            acc = acc + jnp.dot(w.astype(BF16), v, preferred_element_type=F32)
            rest = rest + jnp.sum(l1m, axis=-1, keepdims=True)
            return rest, acc

        carry = (jnp.zeros((tq, 1), F32), jnp.zeros((tq, HEAD_DIM), F32))
        carry = step(qi, carry, True)
        _, acc = lax.fori_loop(
            0, qi, lambda j, c: step(qi - 1 - j, c, False), carry)
        outs.append(acc)
    o_ref[0] = jnp.concatenate(outs, axis=-1)


def _sb(q, k, v, tq=256):
    B, H, S, d = q.shape
    return pl.pallas_call(
        functools.partial(_sb_kernel, tq=tq),
        grid=(B, S // tq),
        in_specs=[pl.BlockSpec((1, H, tq, d), lambda b, i: (b, 0, i, 0)),
                  pl.BlockSpec((1, H, S, d), lambda b, i: (b, 0, 0, 0)),
                  pl.BlockSpec((1, H, S, d), lambda b, i: (b, 0, 0, 0))],
        out_specs=pl.BlockSpec((1, tq, H * d), lambda b, i: (b, i, 0)),
        out_shape=jax.ShapeDtypeStruct((B, S, H * d), F32),
        compiler_params=_params(("parallel", "arbitrary")),
        name="sb_attn",
    )(q, k, v)


def _mixout_kernel(y0_ref, y1_ref, y2_ref, y3_ref, g_ref, w_ref, x_ref, lg_ref, lb_ref, o_ref):
    def rms(y):
        return y * lax.rsqrt(jnp.mean(y * y, axis=-1, keepdims=True) + 1e-6)

    y = jnp.concatenate([rms(r[...]) for r in (y0_ref, y1_ref, y2_ref, y3_ref)], axis=-1)
    y = (y * g_ref[...]).astype(BF16)
    mix = jnp.dot(y, w_ref[...], preferred_element_type=F32)
    o_ref[...] = _layer_norm(ALPHA * x_ref[...] + mix, lg_ref[...], lb_ref[...])


def _mixout(ys, g, w, x2d, lg, lb, tm=512):
    T, D = x2d.shape
    GW = GROUP_WIDTH

    def full(a):
        return pl.BlockSpec(a.shape, lambda i: (0,) * a.ndim)

    return pl.pallas_call(
        _mixout_kernel,
        grid=(T // tm,),
        in_specs=[pl.BlockSpec((tm, GW), lambda i: (i, 0))] * 4
        + [full(g), full(w), pl.BlockSpec((tm, D), lambda i: (i, 0)), full(lg), full(lb)],
        out_specs=pl.BlockSpec((tm, D), lambda i: (i, 0)),
        out_shape=jax.ShapeDtypeStruct((T, D), F32),
        compiler_params=_params(("parallel",)),
        name="mixout",
    )(*ys, g, w, x2d, lg, lb)


def _kv_kernel(m_ref, w_ref, k_ref, v_ref):
    kv = jnp.dot(m_ref[...].astype(BF16), w_ref[...], preferred_element_type=F32)
    k_ref[...] = kv[:, :D_MODEL].astype(BF16)
    v_ref[...] = kv[:, D_MODEL:].astype(BF16)


def _kv(mem2d, wkv, tm=512):
    M, D = mem2d.shape
    return pl.pallas_call(
        _kv_kernel,
        grid=(M // tm,),
        in_specs=[pl.BlockSpec((tm, D), lambda i: (i, 0)),
                  pl.BlockSpec((D, 2 * D), lambda i: (0, 0))],
        out_specs=[pl.BlockSpec((tm, D), lambda i: (i, 0))] * 2,
        out_shape=[jax.ShapeDtypeStruct((M, D), BF16)] * 2,
        compiler_params=_params(("parallel",)),
        name="mem_kv",
    )(mem2d, wkv)


def _xattn_kernel(x_ref, wq_ref, k_ref, v_ref, wo_ref, lg_ref, lb_ref, o_ref):
    x = x_ref[...]
    q = jnp.dot(x.astype(BF16), wq_ref[...], preferred_element_type=F32).astype(BF16)
    scale = XA_HEAD_DIM ** -0.5
    outs = []
    for h in range(XA_HEADS):
        sl = slice(h * XA_HEAD_DIM, (h + 1) * XA_HEAD_DIM)
        s = _nt_dot(q[:, sl], k_ref[0, :, sl]) * scale
        p = jnp.exp(s - jnp.max(s, axis=-1, keepdims=True))
        p = p / jnp.sum(p, axis=-1, keepdims=True)
        outs.append(jnp.dot(p.astype(BF16), v_ref[0, :, sl], preferred_element_type=F32))
    o = jnp.concatenate(outs, axis=-1).astype(BF16)
    xa = jnp.dot(o, wo_ref[...], preferred_element_type=F32)
    o_ref[...] = _layer_norm(ALPHA * x + xa, lg_ref[...], lb_ref[...])


def _xattn(x2d, wq, k, v, wo, lg, lb, seq_len, tm=512):
    T, D = x2d.shape
    M = k.shape[1]
    per_seq = seq_len // tm

    def full(a):
        return pl.BlockSpec(a.shape, lambda i: (0,) * a.ndim)

    return pl.pallas_call(
        _xattn_kernel,
        grid=(T // tm,),
        in_specs=[pl.BlockSpec((tm, D), lambda i: (i, 0)), full(wq),
                  pl.BlockSpec((1, M, D), lambda i: (i // per_seq, 0, 0)),
                  pl.BlockSpec((1, M, D), lambda i: (i // per_seq, 0, 0)),
                  full(wo), full(lg), full(lb)],
        out_specs=pl.BlockSpec((tm, D), lambda i: (i, 0)),
        out_shape=jax.ShapeDtypeStruct((T, D), F32),
        compiler_params=_params(("parallel",)),
        name="xattn",
    )(x2d, wq, k, v, wo, lg, lb)


def _staircase():
    return [(a, b) for a in range(PEER_TOPK) for b in range(PEER_TOPK // (a + 1))]


def _route_kernel(x_ref, wq_ref, k1_ref, k2_ref, e_ref, g_ref, v_scr, i_scr, *, ts):
    xb = x_ref[...].astype(BF16)
    key_id = lax.broadcasted_iota(I32, (N_KEYS, ts), 0).astype(F32)
    qd = 2 * PEER_HALF
    for h in range(PEER_HEADS):
        qry = jnp.dot(xb, wq_ref[:, h * qd:(h + 1) * qd], preferred_element_type=F32)
        for half, kref in ((0, k1_ref), (1, k2_ref)):
            qh = qry[:, half * PEER_HALF:(half + 1) * PEER_HALF].astype(BF16)
            s0 = _nt_dot(kref[...], qh)

            def pick(it, s, h=h, half=half):
                m = jnp.max(s, axis=0, keepdims=True)
                idx = jnp.min(jnp.where(s == m, key_id, float(N_KEYS)), axis=0, keepdims=True)
                v_scr[half, it, h:h + 1, :] = m
                i_scr[half, it, h:h + 1, :] = idx.astype(I32)
                return jnp.where(key_id == idx, -jnp.inf, s)

            lax.fori_loop(0, PEER_TOPK, pick, s0)

    cells = _staircase()
    cand = tuple(v_scr[0, a] + v_scr[1, b] for a, b in cells)
    expert = [i_scr[0, a] * N_KEYS + i_scr[1, b] for a, b in cells]

    def select(it, carry):
        cand, top = carry
        best_v, best_e, best_c = cand[0], expert[0], jnp.zeros((PEER_HEADS, ts), I32)
        for c in range(1, len(cells)):
            better = cand[c] > best_v
            best_v = jnp.where(better, cand[c], best_v)
            best_e = jnp.where(better, expert[c], best_e)
            best_c = jnp.where(better, c, best_c)
        top = jnp.where(it == 0, best_v, top)
        e_ref[0, it] = best_e * WORD_ROWS
        v_scr[0, it] = best_v
        cand = tuple(jnp.where(best_c == c, -jnp.inf, cand[c]) for c in range(len(cells)))
        return cand, top

    _, top = lax.fori_loop(0, PEER_TOPK, select, (cand, jnp.zeros((PEER_HEADS, ts), F32)))
    ex = [jnp.exp(v_scr[0, it] - top) for it in range(PEER_TOPK)]
    den = ex[0]
    for it in range(1, PEER_TOPK):
        den = den + ex[it]
    for it in range(PEER_TOPK):
        g_ref[0, it] = ex[it] / den


def _route(x2d, wq, k1, k2, ts=256):
    T, D = x2d.shape
    nb = T // ts

    def full(a):
        return pl.BlockSpec(a.shape, lambda i: (0,) * a.ndim)

    blk = pl.BlockSpec((1, PEER_TOPK, PEER_HEADS, ts), lambda i: (i, 0, 0, 0))
    return pl.pallas_call(
        functools.partial(_route_kernel, ts=ts),
        grid=(nb,),
        in_specs=[pl.BlockSpec((ts, D), lambda i: (i, 0)), full(wq), full(k1), full(k2)],
        out_specs=[blk, blk],
        out_shape=[jax.ShapeDtypeStruct((nb, PEER_TOPK, PEER_HEADS, ts), I32),
                   jax.ShapeDtypeStruct((nb, PEER_TOPK, PEER_HEADS, ts), F32)],
        scratch_shapes=[pltpu.VMEM((2, PEER_TOPK, PEER_HEADS, ts), F32),
                        pltpu.VMEM((2, PEER_TOPK, PEER_HEADS, ts), I32)],
        compiler_params=_params(("parallel",)),
        name="peer_route",
    )(x2d, wq, k1, k2)


E_PER_TOK = PEER_HEADS * PEER_TOPK
TOK_GROUP = SUBLANES


def _gather_rows(ids_ref, tab_ref, rows_scr, tok0):
    for tt in range(TOK_GROUP):
        base = (tok0 + tt) * E_PER_TOK
        for j in range(E_PER_TOK):
            r0 = pl.multiple_of(ids_ref[0, 0, base + j], WORD_ROWS)
            rows_scr[tt, j * WORD_ROWS:(j + 1) * WORD_ROWS, :] = tab_ref[pl.ds(r0, WORD_ROWS), :]


def _chunk_of_col():
    col = lax.broadcasted_iota(I32, (ROW_CHUNKS, E_PER_TOK * ROW_CHUNKS), 1)
    row = lax.broadcasted_iota(I32, (ROW_CHUNKS, E_PER_TOK * ROW_CHUNKS), 0)
    return (col & (ROW_CHUNKS - 1)) == row


def _peer_score_kernel(ids_ref, x_ref, tab_ref, s_ref, rows_scr, part_scr, *, tb):
    own_chunk = _chunk_of_col()
    c_id = lax.broadcasted_iota(I32, (E_PER_TOK * ROW_CHUNKS, E_PER_TOK), 0)
    e_id = lax.broadcasted_iota(I32, (E_PER_TOK * ROW_CHUNKS, E_PER_TOK), 1)
    fold = jnp.where(c_id >> CHUNK_SHIFT == e_id, 1.0, 0.0).astype(BF16)

    def group(g, _):
        tok0 = g * TOK_GROUP
        _gather_rows(ids_ref, tab_ref, rows_scr, tok0)
        for tt in range(TOK_GROUP):
            rows = pltpu.bitcast(rows_scr[tt], BF16)
            xt = x_ref[tok0 + tt].astype(BF16)
            full = _nt_dot(xt, rows)
            part_scr[tt * ROW_CHUNKS:(tt + 1) * ROW_CHUNKS, :] = jnp.where(own_chunk, full, 0.0)
        part = part_scr[...]
        hi = part.astype(BF16)
        lo = (part - hi.astype(F32)).astype(BF16)
        sc = (jnp.dot(hi, fold, preferred_element_type=F32)
              + jnp.dot(lo, fold, preferred_element_type=F32))
        sc = jnp.sum(sc.reshape(TOK_GROUP, ROW_CHUNKS, E_PER_TOK), axis=1)
        s_ref[pl.ds(pl.multiple_of(tok0, TOK_GROUP), TOK_GROUP), :] = sc
        return 0

    lax.fori_loop(0, tb // TOK_GROUP, group, 0)


def _peer_mix_kernel(ids_ref, s_ref, gate_ref, tab_ref, o_ref, rows_scr, *, tb):
    own_chunk = _chunk_of_col()
    e_id = lax.broadcasted_iota(I32, (E_PER_TOK, E_PER_TOK * ROW_CHUNKS), 0)
    c_id = lax.broadcasted_iota(I32, (E_PER_TOK, E_PER_TOK * ROW_CHUNKS), 1)
    spread = jnp.where(c_id >> CHUNK_SHIFT == e_id, 1.0, 0.0).astype(BF16)

    def group(g, _):
        tok0 = g * TOK_GROUP
        sl = pl.ds(pl.multiple_of(tok0, TOK_GROUP), TOK_GROUP)
        _gather_rows(ids_ref, tab_ref, rows_scr, tok0)
        s = s_ref[sl, :]
        act = 0.5 * s * (1.0 + lax.erf(s * (2.0 ** -0.5)))
        coef = (gate_ref[sl, :] * act).astype(BF16)
        coef = jnp.dot(coef, spread, preferred_element_type=F32)
        for tt in range(TOK_GROUP):
            rows = pltpu.bitcast(rows_scr[tt], BF16)
            ct = jnp.broadcast_to(coef[tt:tt + 1], (ROW_CHUNKS, E_PER_TOK * ROW_CHUNKS))
            ct = jnp.where(own_chunk, ct, 0.0).astype(BF16)
            o_ref[tok0 + tt] = jnp.dot(ct, rows, preferred_element_type=F32)
        return 0

    lax.fori_loop(0, tb // TOK_GROUP, group, 0)


def _peer_experts(ids, gate, x2d, tab_u, tab_v, tb=32):
    T, D = x2d.shape
    nb = T // tb
    ids3 = ids.reshape(nb, 1, tb * E_PER_TOK)
    x3 = x2d.reshape(T, ROW_CHUNKS, LANES)
    ids_spec = pl.BlockSpec((1, 1, tb * E_PER_TOK), lambda i: (i, 0, 0), memory_space=pltpu.SMEM)
    tab_spec = pl.BlockSpec(tab_u.shape, lambda i: (0, 0), pipeline_mode=pl.Buffered(1))
    tok_spec = pl.BlockSpec((tb, E_PER_TOK), lambda i: (i, 0))
    row_spec = pl.BlockSpec((tb, ROW_CHUNKS, LANES), lambda i: (i, 0, 0))
    rows_scr = pltpu.VMEM((TOK_GROUP, E_PER_TOK * WORD_ROWS, LANES), I32)
    s = pl.pallas_call(
        functools.partial(_peer_score_kernel, tb=tb),
        grid=(nb,),
        in_specs=[ids_spec, row_spec, tab_spec],
        out_specs=tok_spec,
        out_shape=jax.ShapeDtypeStruct((T, E_PER_TOK), F32),
        scratch_shapes=[rows_scr, pltpu.VMEM((TOK_GROUP * ROW_CHUNKS, E_PER_TOK * ROW_CHUNKS), F32)],
        compiler_params=_params(("arbitrary",)),
        name="peer_score",
    )(ids3, x3, tab_u)
    ff = pl.pallas_call(
        functools.partial(_peer_mix_kernel, tb=tb),
        grid=(nb,),
        in_specs=[ids_spec, tok_spec, tok_spec, tab_spec],
        out_specs=row_spec,
        out_shape=jax.ShapeDtypeStruct((T, ROW_CHUNKS, LANES), F32),
        scratch_shapes=[rows_scr],
        compiler_params=_params(("arbitrary",)),
        name="peer_mix",
    )(ids3, s, gate, tab_v)
    return ff.reshape(T, D)


def _resln_kernel(x_ref, y_ref, lg_ref, lb_ref, o_ref):
    o_ref[...] = _layer_norm(ALPHA * x_ref[...] + y_ref[...], lg_ref[...], lb_ref[...])


def _resln(x2d, y2d, lg, lb, tm=512):
    T, D = x2d.shape
    blk = pl.BlockSpec((tm, D), lambda i: (i, 0))
    par = pl.BlockSpec((1, D), lambda i: (0, 0))
    return pl.pallas_call(
        _resln_kernel,
        grid=(T // tm,),
        in_specs=[blk, blk, par, par],
        out_specs=blk,
        out_shape=jax.ShapeDtypeStruct((T, D), F32),
        compiler_params=_params(("parallel",)),
        name="res_ln",
    )(x2d, y2d, lg, lb)


def _pack_table(t):
    E = t.shape[0]
    b = lax.bitcast_convert_type(t.astype(BF16), jnp.uint16).astype(jnp.uint32)
    b = b.reshape(E, WORD_ROWS, 2, LANES)
    words = b[:, :, 0, :] | (b[:, :, 1, :] << 16)
    return lax.bitcast_convert_type(words, I32).reshape(E * WORD_ROWS, LANES)


def _block_diag(w):
    H, d, _ = w.shape
    out = jnp.zeros((H * d, H * d), w.dtype)
    for h in range(H):
        out = out.at[h * d:(h + 1) * d, h * d:(h + 1) * d].set(w[h])
    return out


def _cat_w_in(w):
    GW = GROUP_WIDTH
    off_f = 5 * GW
    off_sb = off_f + GROUP_HEADS
    off_sc = off_sb + 3 * GW
    pad = jnp.zeros((w.shape[0], LANES - GROUP_HEADS), w.dtype)
    return jnp.concatenate([w[:, :off_f], w[:, off_sb:off_sc], w[:, off_sc:N_IN],
                            w[:, off_f:off_sb], pad], axis=1).astype(BF16)


def _heads(t, B, S):
    t = t.reshape(B, S, 3, GROUP_HEADS, HEAD_DIM).transpose(2, 0, 3, 1, 4)
    return t[0], t[1], t[2]


def _row(v, width=None):
    v = v.reshape(1, -1).astype(F32)
    if width is not None and v.shape[1] < width:
        v = jnp.pad(v, ((0, 0), (0, width - v.shape[1])))
    return v


def _layer(x2d, kmem, vmem_, B, S, p, tq=256):
    T = B * S
    rg, fox, sb, sc, f = _inproj(x2d, _cat_w_in(p["w_in"]))
    yrg, ysc, cum = _rgsc(
        rg.reshape(B, S, -1), sc.reshape(B, S, -1), f.reshape(B, S, -1),
        p["rg_conv_w"], _row(p["rg_conv_b"]), _block_diag(p["rg_wa"]).astype(BF16), _row(p["rg_ba"]),
        _block_diag(p["rg_wi"]).astype(BF16), _row(p["rg_bi"]), _row(p["rg_lambda"]),
        p["sc_conv_w"], _row(p["fox_bf"], LANES))
    cumh = cum[:, :, :GROUP_HEADS].transpose(0, 2, 1)
    fq, fk, fv = _heads(fox, B, S)
    yfox = _fox(fq, fk, fv, cumh[..., None], cumh.reshape(B, GROUP_HEADS, S // tq, 1, tq), tq=tq)
    sq, sk, sv = _heads(sb, B, S)
    ysb = _sb(sq, sk, sv, tq=tq)
    ys = (yrg.reshape(T, -1), yfox.reshape(T, -1), ysb.reshape(T, -1), ysc.reshape(T, -1))
    x1 = _mixout(ys, _row(p["mix_norm_g"]), p["w_out"].astype(BF16), x2d,
                 _row(p["ln1_g"]), _row(p["ln1_b"]))
    x2 = _xattn(x1, p["xa_wq"].astype(BF16), kmem, vmem_, p["xa_wo"].astype(BF16),
                _row(p["ln2_g"]), _row(p["ln2_b"]), S)
    ids, gate = _route(x2, p["peer_wq"].astype(BF16), p["peer_k1"].astype(BF16),
                       p["peer_k2"].astype(BF16))
    ids = ids.transpose(0, 3, 2, 1).reshape(T, E_PER_TOK)
    gate = gate.transpose(0, 3, 2, 1).reshape(T, E_PER_TOK)
    ff = _peer_experts(ids, gate, x2, _pack_table(p["peer_u"]), _pack_table(p["peer_v"]))
    return _resln(x2, ff, _row(p["ln3_g"]), _row(p["ln3_b"]))


_LAYER_PARAMS = ("w_in", "w_out", "rg_conv_w", "rg_conv_b", "rg_wa", "rg_ba", "rg_wi", "rg_bi",
                 "rg_lambda", "fox_bf", "sc_conv_w", "mix_norm_g", "ln1_g", "ln1_b", "xa_wq",
                 "xa_wkv", "xa_wo", "ln2_g", "ln2_b", "peer_wq", "peer_k1", "peer_k2", "peer_u",
                 "peer_v", "ln3_g", "ln3_b")


def kernel(x, mem, w_in, w_out, rg_conv_w, rg_conv_b, rg_wa, rg_ba, rg_wi, rg_bi, rg_lambda, fox_bf, sc_conv_w, mix_norm_g, ln1_g, ln1_b, xa_wq, xa_wkv, xa_wo, ln2_g, ln2_b, peer_wq, peer_k1, peer_k2, peer_u, peer_v, ln3_g, ln3_b):
    stacked = dict(zip(_LAYER_PARAMS, (
        w_in, w_out, rg_conv_w, rg_conv_b, rg_wa, rg_ba, rg_wi, rg_bi, rg_lambda, fox_bf,
        sc_conv_w, mix_norm_g, ln1_g, ln1_b, xa_wq, xa_wkv, xa_wo, ln2_g, ln2_b, peer_wq,
        peer_k1, peer_k2, peer_u, peer_v, ln3_g, ln3_b)))
    B, S, D = x.shape
    M = mem.shape[1]
    x2d = x.reshape(B * S, D)
    mem2d = mem.reshape(B * M, D)
    for l in range(w_in.shape[0]):
        p = {k: v[l] for k, v in stacked.items()}
        kmem, vmem_ = _kv(mem2d, p["xa_wkv"].astype(BF16))
        x2d = _layer(x2d, kmem.reshape(B, M, D), vmem_.reshape(B, M, D), B, S, p)
    return x2d.reshape(B, S, D)
```

```python
import functools
import math

import jax
import jax.numpy as jnp
from jax import lax
from jax.experimental import pallas as pl
from jax.experimental.pallas import tpu as pltpu

F32 = jnp.float32
BF16 = jnp.bfloat16
I32 = jnp.int32

D_MODEL = 1024
GROUP_WIDTH = 256
GROUP_HEADS = 4
HEAD_DIM = 64
N_IN = 2820
RGLRU_C = 8.0
XA_HEADS = 4
XA_HEAD_DIM = D_MODEL // XA_HEADS
PEER_HEADS = 8
N_KEYS = 128
PEER_HALF = 128
PEER_TOPK = 16
DEPTH = 2
ALPHA = (2.0 * DEPTH) ** 0.25
LN_EPS = 1e-5

SUBLANES = 8
LANES = 128
WORD_ROWS = D_MODEL // (2 * LANES)
ROW_CHUNKS = D_MODEL // LANES
CHUNK_SHIFT = ROW_CHUNKS.bit_length() - 1

VMEM_LIMIT = 48 * 1024 * 1024


def _params(sem, vmem=VMEM_LIMIT):
    return pltpu.CompilerParams(dimension_semantics=sem, vmem_limit_bytes=vmem)


def _layer_norm(z, g, b):
    mu = jnp.mean(z, axis=-1, keepdims=True)
    zc = z - mu
    var = jnp.mean(zc * zc, axis=-1, keepdims=True)
    return zc * lax.rsqrt(var + LN_EPS) * g + b


def _log_sigmoid(z):
    return jnp.minimum(z, 0.0) - jnp.log1p(jnp.exp(-jnp.abs(z)))


def _nt_dot(a, b):
    return lax.dot_general(a, b, (((1,), (1,)), ((), ())), preferred_element_type=F32)


C_RG, C_FOX, C_SB, C_SC, C_F, C_END = 0, 512, 1280, 2048, 2816, 2944


def _inproj_kernel(x_ref, w_ref, rg_ref, fox_ref, sb_ref, sc_ref, f_ref):
    xb = x_ref[...].astype(BF16)

    def mm(lo, hi):
        return jnp.dot(xb, w_ref[:, lo:hi], preferred_element_type=F32)

    rg_ref[...] = mm(C_RG, C_FOX)
    fox_ref[...] = mm(C_FOX, C_SB).astype(BF16)
    sb_ref[...] = mm(C_SB, C_SC).astype(BF16)
    sc_ref[...] = mm(C_SC, C_F)
    f_ref[...] = mm(C_F, C_END)


def _inproj(x2d, w_cat, tm=512):
    T, D = x2d.shape
    widths = (C_FOX - C_RG, C_SB - C_FOX, C_SC - C_SB, C_F - C_SC, C_END - C_F)
    dtypes = (F32, BF16, BF16, F32, F32)
    return pl.pallas_call(
        _inproj_kernel,
        grid=(T // tm,),
        in_specs=[pl.BlockSpec((tm, D), lambda i: (i, 0)),
                  pl.BlockSpec((D, C_END), lambda i: (0, 0))],
        out_specs=[pl.BlockSpec((tm, w), lambda i: (i, 0)) for w in widths],
        out_shape=[jax.ShapeDtypeStruct((T, w), dt) for w, dt in zip(widths, dtypes)],
        compiler_params=_params(("parallel",)),
        name="inproj",
    )(x2d, w_cat)


def _rgsc_kernel(rg_ref, sc_ref, f_ref, cw_ref, cb_ref, wa_ref, ba_ref, wi_ref, bi_ref, lam_ref,
                 scw_ref, fb_ref, yrg_ref, ysc_ref, cum_ref, xprev, chprev, hprev, cprev, *, ts):
    GW = GROUP_WIDTH

    @pl.when(pl.program_id(1) == 0)
    def _():
        xprev[...] = jnp.zeros_like(xprev)
        chprev[...] = jnp.zeros_like(chprev)
        hprev[...] = jnp.zeros_like(hprev)
        cprev[...] = jnp.zeros_like(cprev)

    row = lax.broadcasted_iota(I32, (ts, GW), 0)

    def delayed(prev, cur, d):
        ext = jnp.concatenate([prev, cur], axis=0)
        return pltpu.roll(ext, d, 0)[SUBLANES:]

    xr = rg_ref[0, :, :GW]
    gate = rg_ref[0, :, GW:]
    xp = xprev[...]
    cw = cw_ref[...]
    xc = (delayed(xp, xr, 3) * cw[0:1] + delayed(xp, xr, 2) * cw[1:2]
          + delayed(xp, xr, 1) * cw[2:3] + xr * cw[3:4] + cb_ref[...])
    xprev[...] = xr[ts - SUBLANES:]
    xcb = xc.astype(BF16)
    r = jax.nn.sigmoid(jnp.dot(xcb, wa_ref[...], preferred_element_type=F32) + ba_ref[...])
    ig = jax.nn.sigmoid(jnp.dot(xcb, wi_ref[...], preferred_element_type=F32) + bi_ref[...])
    z = -lam_ref[...]
    softplus = jnp.maximum(z, 0.0) + jnp.log1p(jnp.exp(-jnp.abs(z)))
    log_a = -RGLRU_C * r * softplus
    a = jnp.exp(log_a)
    u = jnp.sqrt(-jnp.tanh(log_a) * (a * a + 1.0)) * (ig * xc)
    acc_a, acc_b = a, u
    d = 1
    while d < ts:
        keep = row >= d
        a_s = jnp.where(keep, pltpu.roll(acc_a, d, 0), 1.0)
        b_s = jnp.where(keep, pltpu.roll(acc_b, d, 0), 0.0)
        acc_b = acc_a * b_s + acc_b
        acc_a = acc_a * a_s
        d *= 2
    h = acc_b + acc_a * hprev[...]
    hprev[...] = h[ts - 1:]
    c0 = math.sqrt(2.0 / math.pi)
    gelu = 0.5 * gate * (1.0 + jnp.tanh(c0 * (gate + 0.044715 * gate * gate * gate)))
    yrg_ref[0] = h * gelu

    bg = sc_ref[0, :, :GW]
    ch = sc_ref[0, :, GW:2 * GW] * sc_ref[0, :, 2 * GW:]
    cp = chprev[...]
    sw = scw_ref[...]
    ysc_ref[0] = bg * (delayed(cp, ch, 2) * sw[0:1] + delayed(cp, ch, 1) * sw[1:2] + ch * sw[2:3])
    chprev[...] = ch[ts - SUBLANES:]

    rowf = lax.broadcasted_iota(I32, (ts, LANES), 0)
    c = _log_sigmoid(f_ref[0] + fb_ref[...])
    d = 1
    while d < ts:
        c = c + jnp.where(rowf >= d, pltpu.roll(c, d, 0), 0.0)
        d *= 2
    c = c + cprev[...]
    cprev[...] = c[ts - 1:]
    cum_ref[0] = c


def _rgsc(rg, sc, f, cw, cb, wa, ba, wi, bi, lam, scw, fb, ts=512):
    B, S, _ = rg.shape
    GW = GROUP_WIDTH

    def full(a):
        return pl.BlockSpec(a.shape, lambda b, t: (0,) * a.ndim)

    def seq(w):
        return pl.BlockSpec((1, ts, w), lambda b, t: (b, t, 0))

    params = (cw, cb, wa, ba, wi, bi, lam, scw, fb)
    return pl.pallas_call(
        functools.partial(_rgsc_kernel, ts=ts),
        grid=(B, S // ts),
        in_specs=[seq(2 * GW), seq(3 * GW), seq(LANES)] + [full(p) for p in params],
        out_specs=[seq(GW), seq(GW), seq(LANES)],
        out_shape=[jax.ShapeDtypeStruct((B, S, GW), F32), jax.ShapeDtypeStruct((B, S, GW), F32),
                   jax.ShapeDtypeStruct((B, S, LANES), F32)],
        scratch_shapes=[pltpu.VMEM((SUBLANES, GW), F32), pltpu.VMEM((SUBLANES, GW), F32),
                        pltpu.VMEM((1, GW), F32), pltpu.VMEM((1, LANES), F32)],
        compiler_params=_params(("parallel", "arbitrary")),
        name="rgsc",
    )(rg, sc, f, *params)


def _fox_kernel(q_ref, k_ref, v_ref, cc_ref, cr_ref, o_ref, *, tq):
    qi = pl.program_id(1)
    scale = HEAD_DIM ** -0.5
    rowi = lax.broadcasted_iota(I32, (tq, tq), 0)
    coli = lax.broadcasted_iota(I32, (tq, tq), 1)
    causal = coli <= rowi
    outs = []
    for h in range(GROUP_HEADS):
        q = q_ref[0, h]
        cc = cc_ref[0, h]

        def step(ki, carry, diag, h=h, q=q, cc=cc):
            m, l, acc = carry
            off = pl.multiple_of(ki * tq, tq)
            k = k_ref[0, h, pl.ds(off, tq), :]
            v = v_ref[0, h, pl.ds(off, tq), :]
            s = _nt_dot(q, k) * scale + cc - cr_ref[0, h, ki]
            if diag:
                s = jnp.where(causal, s, -jnp.inf)
            m_new = jnp.maximum(m, jnp.max(s, axis=-1, keepdims=True))
            alpha = jnp.exp(m - m_new)
            p = jnp.exp(s - m_new)
            l = alpha * l + jnp.sum(p, axis=-1, keepdims=True)
            acc = alpha * acc + jnp.dot(p.astype(BF16), v, preferred_element_type=F32)
            return m_new, l, acc

        init = (jnp.full((tq, 1), -jnp.inf, F32), jnp.zeros((tq, 1), F32),
                jnp.zeros((tq, HEAD_DIM), F32))
        carry = lax.fori_loop(0, qi, functools.partial(step, diag=False), init)
        _, l, acc = step(qi, carry, True)
        outs.append(acc / l)
    o_ref[0] = jnp.concatenate(outs, axis=-1)


def _fox(q, k, v, cc, cr, tq=256):
    B, H, S, d = q.shape
    return pl.pallas_call(
        functools.partial(_fox_kernel, tq=tq),
        grid=(B, S // tq),
        in_specs=[pl.BlockSpec((1, H, tq, d), lambda b, i: (b, 0, i, 0)),
                  pl.BlockSpec((1, H, S, d), lambda b, i: (b, 0, 0, 0)),
                  pl.BlockSpec((1, H, S, d), lambda b, i: (b, 0, 0, 0)),
                  pl.BlockSpec((1, H, tq, 1), lambda b, i: (b, 0, i, 0)),
                  pl.BlockSpec((1, H, S // tq, 1, tq), lambda b, i: (b, 0, 0, 0, 0))],
        out_specs=pl.BlockSpec((1, tq, H * d), lambda b, i: (b, i, 0)),
        out_shape=jax.ShapeDtypeStruct((B, S, H * d), F32),
        compiler_params=_params(("parallel", "arbitrary")),
        name="fox_attn",
    )(q, k, v, cc, cr)


EXP_UNDERFLOW = -104.0


def _sb_kernel(q_ref, k_ref, v_ref, o_ref, *, tq):
    qi = pl.program_id(1)
    scale = HEAD_DIM ** -0.5
    rowi = lax.broadcasted_iota(I32, (tq, tq), 0)
    coli = lax.broadcasted_iota(I32, (tq, tq), 1)
    strict = coli < rowi
    later = jnp.where(rowi > coli, 1.0, 0.0).astype(BF16)
    outs = []
    for h in range(GROUP_HEADS):
        q = q_ref[0, h]

        def step(ki, carry, diag, h=h, q=q):
            rest, acc = carry
            off = pl.multiple_of(ki * tq, tq)
            k = k_ref[0, h, pl.ds(off, tq), :]
            v = v_ref[0, h, pl.ds(off, tq), :]
            z = _nt_dot(q, k) * scale
            ls = _log_sigmoid(z)
            l1m = ls - z
            if diag:
                l1m = jnp.where(strict, l1m, 0.0)
            hi = l1m.astype(BF16)
            lo = (l1m - hi.astype(F32)).astype(BF16)
            tail = (jnp.dot(hi, later, preferred_element_type=F32)
                    + jnp.dot(lo, later, preferred_element_type=F32) + rest)
            w = jnp.exp(ls + tail)
            if diag:
                w = jnp.where(strict, w, 0.0)
            acc = acc + jnp.dot(w.astype(BF16), v, preferred_element_type=F32)
            rest = rest + jnp.sum(l1m, axis=-1, keepdims=True)
            return rest, acc

        carry = (jnp.zeros((tq, 1), F32), jnp.zeros((tq, HEAD_DIM), F32))
        rest, acc = step(qi, carry, True)

        def more(c):
            j, rest, _ = c
            return jnp.logical_and(j < qi, jnp.max(rest) > EXP_UNDERFLOW)

        def further(c):
            j, rest, acc = c
            rest, acc = step(qi - 1 - j, (rest, acc), False)
            return j + 1, rest, acc

        _, _, acc = lax.while_loop(more, further, (jnp.int32(0), rest, acc))
        outs.append(acc)
    o_ref[0] = jnp.concatenate(outs, axis=-1)


def _sb(q, k, v, tq=256):
    B, H, S, d = q.shape
    return pl.pallas_call(
        functools.partial(_sb_kernel, tq=tq),
        grid=(B, S // tq),
        in_specs=[pl.BlockSpec((1, H, tq, d), lambda b, i: (b, 0, i, 0)),
                  pl.BlockSpec((1, H, S, d), lambda b, i: (b, 0, 0, 0)),
                  pl.BlockSpec((1, H, S, d), lambda b, i: (b, 0, 0, 0))],
        out_specs=pl.BlockSpec((1, tq, H * d), lambda b, i: (b, i, 0)),
        out_shape=jax.ShapeDtypeStruct((B, S, H * d), F32),
        compiler_params=_params(("parallel", "arbitrary")),
        name="sb_attn",
    )(q, k, v)


def _mixout_kernel(y0_ref, y1_ref, y2_ref, y3_ref, g_ref, w_ref, x_ref, lg_ref, lb_ref, o_ref):
    def rms(y):
        return y * lax.rsqrt(jnp.mean(y * y, axis=-1, keepdims=True) + 1e-6)

    y = jnp.concatenate([rms(r[...]) for r in (y0_ref, y1_ref, y2_ref, y3_ref)], axis=-1)
    y = (y * g_ref[...]).astype(BF16)
    mix = jnp.dot(y, w_ref[...], preferred_element_type=F32)
    o_ref[...] = _layer_norm(ALPHA * x_ref[...] + mix, lg_ref[...], lb_ref[...])


def _mixout(ys, g, w, x2d, lg, lb, tm=512):
    T, D = x2d.shape
    GW = GROUP_WIDTH

    def full(a):
        return pl.BlockSpec(a.shape, lambda i: (0,) * a.ndim)

    return pl.pallas_call(
        _mixout_kernel,
        grid=(T // tm,),
        in_specs=[pl.BlockSpec((tm, GW), lambda i: (i, 0))] * 4
        + [full(g), full(w), pl.BlockSpec((tm, D), lambda i: (i, 0)), full(lg), full(lb)],
        out_specs=pl.BlockSpec((tm, D), lambda i: (i, 0)),
        out_shape=jax.ShapeDtypeStruct((T, D), F32),
        compiler_params=_params(("parallel",)),
        name="mixout",
    )(*ys, g, w, x2d, lg, lb)


def _kv_kernel(m_ref, w_ref, k_ref, v_ref):
    kv = jnp.dot(m_ref[...].astype(BF16), w_ref[...], preferred_element_type=F32)
    k_ref[...] = kv[:, :D_MODEL].astype(BF16)
    v_ref[...] = kv[:, D_MODEL:].astype(BF16)


def _kv(mem2d, wkv, tm=512):
    M, D = mem2d.shape
    return pl.pallas_call(
        _kv_kernel,
        grid=(M // tm,),
        in_specs=[pl.BlockSpec((tm, D), lambda i: (i, 0)),
                  pl.BlockSpec((D, 2 * D), lambda i: (0, 0))],
        out_specs=[pl.BlockSpec((tm, D), lambda i: (i, 0))] * 2,
        out_shape=[jax.ShapeDtypeStruct((M, D), BF16)] * 2,
        compiler_params=_params(("parallel",)),
        name="mem_kv",
    )(mem2d, wkv)


def _xattn_kernel(x_ref, wq_ref, k_ref, v_ref, wo_ref, lg_ref, lb_ref, o_ref):
    x = x_ref[...]
    q = jnp.dot(x.astype(BF16), wq_ref[...], preferred_element_type=F32).astype(BF16)
    scale = XA_HEAD_DIM ** -0.5
    outs = []
    for h in range(XA_HEADS):
        sl = slice(h * XA_HEAD_DIM, (h + 1) * XA_HEAD_DIM)
        s = _nt_dot(q[:, sl], k_ref[0, :, sl]) * scale
        p = jnp.exp(s - jnp.max(s, axis=-1, keepdims=True))
        p = p / jnp.sum(p, axis=-1, keepdims=True)
        outs.append(jnp.dot(p.astype(BF16), v_ref[0, :, sl], preferred_element_type=F32))
    o = jnp.concatenate(outs, axis=-1).astype(BF16)
    xa = jnp.dot(o, wo_ref[...], preferred_element_type=F32)
    o_ref[...] = _layer_norm(ALPHA * x + xa, lg_ref[...], lb_ref[...])


def _xattn(x2d, wq, k, v, wo, lg, lb, seq_len, tm=512):
    T, D = x2d.shape
    M = k.shape[1]
    per_seq = seq_len // tm

    def full(a):
        return pl.BlockSpec(a.shape, lambda i: (0,) * a.ndim)

    return pl.pallas_call(
        _xattn_kernel,
        grid=(T // tm,),
        in_specs=[pl.BlockSpec((tm, D), lambda i: (i, 0)), full(wq),
                  pl.BlockSpec((1, M, D), lambda i: (i // per_seq, 0, 0)),
                  pl.BlockSpec((1, M, D), lambda i: (i // per_seq, 0, 0)),
                  full(wo), full(lg), full(lb)],
        out_specs=pl.BlockSpec((tm, D), lambda i: (i, 0)),
        out_shape=jax.ShapeDtypeStruct((T, D), F32),
        compiler_params=_params(("parallel",)),
        name="xattn",
    )(x2d, wq, k, v, wo, lg, lb)


def _staircase():
    return [(a, b) for a in range(PEER_TOPK) for b in range(PEER_TOPK // (a + 1))]


def _route_kernel(x_ref, wq_ref, k1_ref, k2_ref, e_ref, g_ref, v_scr, i_scr, *, ts):
    xb = x_ref[...].astype(BF16)
    key_id = lax.broadcasted_iota(I32, (N_KEYS, ts), 0).astype(F32)
    qd = 2 * PEER_HALF
    for h in range(PEER_HEADS):
        qry = jnp.dot(xb, wq_ref[:, h * qd:(h + 1) * qd], preferred_element_type=F32)
        for half, kref in ((0, k1_ref), (1, k2_ref)):
            qh = qry[:, half * PEER_HALF:(half + 1) * PEER_HALF].astype(BF16)
            s0 = _nt_dot(kref[...], qh)

            def pick(it, s, h=h, half=half):
                m = jnp.max(s, axis=0, keepdims=True)
                idx = jnp.min(jnp.where(s == m, key_id, float(N_KEYS)), axis=0, keepdims=True)
                v_scr[half, it, h:h + 1, :] = m
                i_scr[half, it, h:h + 1, :] = idx.astype(I32)
                return jnp.where(key_id == idx, -jnp.inf, s)

            lax.fori_loop(0, PEER_TOPK, pick, s0)

    cells = _staircase()
    cand = tuple(v_scr[0, a] + v_scr[1, b] for a, b in cells)
    expert = [i_scr[0, a] * N_KEYS + i_scr[1, b] for a, b in cells]

    def select(it, carry):
        cand, top = carry
        best_v, best_e, best_c = cand[0], expert[0], jnp.zeros((PEER_HEADS, ts), I32)
        for c in range(1, len(cells)):
            better = cand[c] > best_v
            best_v = jnp.where(better, cand[c], best_v)
            best_e = jnp.where(better, expert[c], best_e)
            best_c = jnp.where(better, c, best_c)
        top = jnp.where(it == 0, best_v, top)
        e_ref[0, it] = best_e * WORD_ROWS
        v_scr[0, it] = best_v
        cand = tuple(jnp.where(best_c == c, -jnp.inf, cand[c]) for c in range(len(cells)))
        return cand, top

    _, top = lax.fori_loop(0, PEER_TOPK, select, (cand, jnp.zeros((PEER_HEADS, ts), F32)))
    ex = [jnp.exp(v_scr[0, it] - top) for it in range(PEER_TOPK)]
    den = ex[0]
    for it in range(1, PEER_TOPK):
        den = den + ex[it]
    for it in range(PEER_TOPK):
        g_ref[0, it] = ex[it] / den


def _route(x2d, wq, k1, k2, ts=256):
    T, D = x2d.shape
    nb = T // ts

    def full(a):
        return pl.BlockSpec(a.shape, lambda i: (0,) * a.ndim)

    blk = pl.BlockSpec((1, PEER_TOPK, PEER_HEADS, ts), lambda i: (i, 0, 0, 0))
    return pl.pallas_call(
        functools.partial(_route_kernel, ts=ts),
        grid=(nb,),
        in_specs=[pl.BlockSpec((ts, D), lambda i: (i, 0)), full(wq), full(k1), full(k2)],
        out_specs=[blk, blk],
        out_shape=[jax.ShapeDtypeStruct((nb, PEER_TOPK, PEER_HEADS, ts), I32),
                   jax.ShapeDtypeStruct((nb, PEER_TOPK, PEER_HEADS, ts), F32)],
        scratch_shapes=[pltpu.VMEM((2, PEER_TOPK, PEER_HEADS, ts), F32),
                        pltpu.VMEM((2, PEER_TOPK, PEER_HEADS, ts), I32)],
        compiler_params=_params(("parallel",)),
        name="peer_route",
    )(x2d, wq, k1, k2)


E_PER_TOK = PEER_HEADS * PEER_TOPK
TOK_GROUP = SUBLANES


def _gather_rows(ids_ref, tab_ref, rows_scr, tok0):
    for tt in range(TOK_GROUP):
        base = (tok0 + tt) * E_PER_TOK
        for j in range(E_PER_TOK):
            r0 = pl.multiple_of(ids_ref[0, 0, base + j], WORD_ROWS)
            rows_scr[tt, j * WORD_ROWS:(j + 1) * WORD_ROWS, :] = tab_ref[pl.ds(r0, WORD_ROWS), :]


def _chunk_of_col():
    col = lax.broadcasted_iota(I32, (ROW_CHUNKS, E_PER_TOK * ROW_CHUNKS), 1)
    row = lax.broadcasted_iota(I32, (ROW_CHUNKS, E_PER_TOK * ROW_CHUNKS), 0)
    return (col & (ROW_CHUNKS - 1)) == row


def _peer_score_kernel(ids_ref, x_ref, tab_ref, s_ref, rows_a, rows_b, part_scr, *, tb):
    own_chunk = _chunk_of_col()
    c_id = lax.broadcasted_iota(I32, (E_PER_TOK * ROW_CHUNKS, E_PER_TOK), 0)
    e_id = lax.broadcasted_iota(I32, (E_PER_TOK * ROW_CHUNKS, E_PER_TOK), 1)
    fold = jnp.where(c_id >> CHUNK_SHIFT == e_id, 1.0, 0.0).astype(BF16)

    def gather(t, rows_ref):
        base = t * E_PER_TOK
        for j in range(E_PER_TOK):
            r0 = pl.multiple_of(ids_ref[0, 0, base + j], WORD_ROWS)
            rows_ref[j * WORD_ROWS:(j + 1) * WORD_ROWS, :] = tab_ref[pl.ds(r0, WORD_ROWS), :]

    def products(t, rows_ref):
        rows = pltpu.bitcast(rows_ref[...], BF16)
        xt = x_ref[t].astype(BF16)
        full = _nt_dot(xt, rows)
        part_scr[pl.ds(pl.multiple_of(t * ROW_CHUNKS, ROW_CHUNKS), ROW_CHUNKS), :] = (
            jnp.where(own_chunk, full, 0.0))

    gather(0, rows_a)

    def pair(i, _):
        t = 2 * i
        gather(t + 1, rows_b)
        products(t, rows_a)
        gather(jnp.minimum(t + 2, tb - 1), rows_a)
        products(t + 1, rows_b)
        return 0

    lax.fori_loop(0, tb // 2, pair, 0)
    part = part_scr[...]
    hi = part.astype(BF16)
    lo = (part - hi.astype(F32)).astype(BF16)
    sc = (jnp.dot(hi, fold, preferred_element_type=F32)
          + jnp.dot(lo, fold, preferred_element_type=F32))
    s_ref[...] = jnp.sum(sc.reshape(tb, ROW_CHUNKS, E_PER_TOK), axis=1)


def _peer_mix_kernel(ids_ref, s_ref, gate_ref, tab_ref, o_ref, rows_scr, *, tb):
    own_chunk = _chunk_of_col()
    e_id = lax.broadcasted_iota(I32, (E_PER_TOK, E_PER_TOK * ROW_CHUNKS), 0)
    c_id = lax.broadcasted_iota(I32, (E_PER_TOK, E_PER_TOK * ROW_CHUNKS), 1)
    spread = jnp.where(c_id >> CHUNK_SHIFT == e_id, 1.0, 0.0).astype(BF16)

    def group(g, _):
        tok0 = g * TOK_GROUP
        sl = slice(tok0, tok0 + TOK_GROUP)
        _gather_rows(ids_ref, tab_ref, rows_scr, tok0)
        s = s_ref[sl, :]
        act = 0.5 * s * (1.0 + lax.erf(s * (2.0 ** -0.5)))
        coef = (gate_ref[sl, :] * act).astype(BF16)
        coef = jnp.dot(coef, spread, preferred_element_type=F32)
        for tt in range(TOK_GROUP):
            rows = pltpu.bitcast(rows_scr[tt], BF16)
            ct = jnp.broadcast_to(coef[tt:tt + 1], (ROW_CHUNKS, E_PER_TOK * ROW_CHUNKS))
            ct = jnp.where(own_chunk, ct, 0.0).astype(BF16)
            o_ref[tok0 + tt] = jnp.dot(ct, rows, preferred_element_type=F32)

    for g in range(tb // TOK_GROUP):
        group(g, 0)


def _peer_experts(ids, gate, x2d, tab_u, tab_v, tb=32):
    T, D = x2d.shape
    nb = T // tb
    ids3 = ids.reshape(nb, 1, tb * E_PER_TOK)
    x3 = x2d.reshape(T, ROW_CHUNKS, LANES)
    ids_spec = pl.BlockSpec((1, 1, tb * E_PER_TOK), lambda i: (i, 0, 0), memory_space=pltpu.SMEM)
    tab_spec = pl.BlockSpec(tab_u.shape, lambda i: (0, 0), pipeline_mode=pl.Buffered(1))
    tok_spec = pl.BlockSpec((tb, E_PER_TOK), lambda i: (i, 0))
    row_spec = pl.BlockSpec((tb, ROW_CHUNKS, LANES), lambda i: (i, 0, 0))
    rows_scr = pltpu.VMEM((TOK_GROUP, E_PER_TOK * WORD_ROWS, LANES), I32)
    rows_one = pltpu.VMEM((E_PER_TOK * WORD_ROWS, LANES), I32)
    s = pl.pallas_call(
        functools.partial(_peer_score_kernel, tb=tb),
        grid=(nb,),
        in_specs=[ids_spec, row_spec, tab_spec],
        out_specs=tok_spec,
        out_shape=jax.ShapeDtypeStruct((T, E_PER_TOK), F32),
        scratch_shapes=[rows_one, rows_one, pltpu.VMEM((tb * ROW_CHUNKS, E_PER_TOK * ROW_CHUNKS), F32)],
        compiler_params=_params(("arbitrary",)),
        name="peer_score",
    )(ids3, x3, tab_u)
    ff = pl.pallas_call(
        functools.partial(_peer_mix_kernel, tb=tb),
        grid=(nb,),
        in_specs=[ids_spec, tok_spec, tok_spec, tab_spec],
        out_specs=row_spec,
        out_shape=jax.ShapeDtypeStruct((T, ROW_CHUNKS, LANES), F32),
        scratch_shapes=[rows_scr],
        compiler_params=_params(("arbitrary",)),
        name="peer_mix",
    )(ids3, s, gate, tab_v)
    return ff.reshape(T, D)


def _resln_kernel(x_ref, y_ref, lg_ref, lb_ref, o_ref):
    o_ref[...] = _layer_norm(ALPHA * x_ref[...] + y_ref[...], lg_ref[...], lb_ref[...])


def _resln(x2d, y2d, lg, lb, tm=512):
    T, D = x2d.shape
    blk = pl.BlockSpec((tm, D), lambda i: (i, 0))
    par = pl.BlockSpec((1, D), lambda i: (0, 0))
    return pl.pallas_call(
        _resln_kernel,
        grid=(T // tm,),
        in_specs=[blk, blk, par, par],
        out_specs=blk,
        out_shape=jax.ShapeDtypeStruct((T, D), F32),
        compiler_params=_params(("parallel",)),
        name="res_ln",
    )(x2d, y2d, lg, lb)


def _pack_table(t):
    E = t.shape[0]
    b = lax.bitcast_convert_type(t.astype(BF16), jnp.uint16).astype(jnp.uint32)
    b = b.reshape(E, WORD_ROWS, 2, LANES)
    words = b[:, :, 0, :] | (b[:, :, 1, :] << 16)
    return lax.bitcast_convert_type(words, I32).reshape(E * WORD_ROWS, LANES)


def _block_diag(w):
    H, d, _ = w.shape
    out = jnp.zeros((H * d, H * d), w.dtype)
    for h in range(H):
        out = out.at[h * d:(h + 1) * d, h * d:(h + 1) * d].set(w[h])
    return out


def _cat_w_in(w):
    GW = GROUP_WIDTH
    off_f = 5 * GW
    off_sb = off_f + GROUP_HEADS
    off_sc = off_sb + 3 * GW
    pad = jnp.zeros((w.shape[0], LANES - GROUP_HEADS), w.dtype)
    return jnp.concatenate([w[:, :off_f], w[:, off_sb:off_sc], w[:, off_sc:N_IN],
                            w[:, off_f:off_sb], pad], axis=1).astype(BF16)


def _heads(t, B, S):
    t = t.reshape(B, S, 3, GROUP_HEADS, HEAD_DIM).transpose(2, 0, 3, 1, 4)
    return t[0], t[1], t[2]


def _row(v, width=None):
    v = v.reshape(1, -1).astype(F32)
    if width is not None and v.shape[1] < width:
        v = jnp.pad(v, ((0, 0), (0, width - v.shape[1])))
    return v


def _layer(x2d, kmem, vmem_, B, S, p, tq=256):
    T = B * S
    rg, fox, sb, sc, f = _inproj(x2d, _cat_w_in(p["w_in"]))
    yrg, ysc, cum = _rgsc(
        rg.reshape(B, S, -1), sc.reshape(B, S, -1), f.reshape(B, S, -1),
        p["rg_conv_w"], _row(p["rg_conv_b"]), _block_diag(p["rg_wa"]).astype(BF16), _row(p["rg_ba"]),
        _block_diag(p["rg_wi"]).astype(BF16), _row(p["rg_bi"]), _row(p["rg_lambda"]),
        p["sc_conv_w"], _row(p["fox_bf"], LANES))
    cumh = cum[:, :, :GROUP_HEADS].transpose(0, 2, 1)
    fq, fk, fv = _heads(fox, B, S)
    yfox = _fox(fq, fk, fv, cumh[..., None], cumh.reshape(B, GROUP_HEADS, S // tq, 1, tq), tq=tq)
    sq, sk, sv = _heads(sb, B, S)
    ysb = _sb(sq, sk, sv, tq=tq)
    ys = (yrg.reshape(T, -1), yfox.reshape(T, -1), ysb.reshape(T, -1), ysc.reshape(T, -1))
    x1 = _mixout(ys, _row(p["mix_norm_g"]), p["w_out"].astype(BF16), x2d,
                 _row(p["ln1_g"]), _row(p["ln1_b"]))
    x2 = _xattn(x1, p["xa_wq"].astype(BF16), kmem, vmem_, p["xa_wo"].astype(BF16),
                _row(p["ln2_g"]), _row(p["ln2_b"]), S)
    ids, gate = _route(x2, p["peer_wq"].astype(BF16), p["peer_k1"].astype(BF16),
                       p["peer_k2"].astype(BF16))
    ids = ids.transpose(0, 3, 2, 1).reshape(T, E_PER_TOK)
    gate = gate.transpose(0, 3, 2, 1).reshape(T, E_PER_TOK)
    ff = _peer_experts(ids, gate, x2, _pack_table(p["peer_u"]), _pack_table(p["peer_v"]))
    return _resln(x2, ff, _row(p["ln3_g"]), _row(p["ln3_b"]))


_LAYER_PARAMS = ("w_in", "w_out", "rg_conv_w", "rg_conv_b", "rg_wa", "rg_ba", "rg_wi", "rg_bi",
                 "rg_lambda", "fox_bf", "sc_conv_w", "mix_norm_g", "ln1_g", "ln1_b", "xa_wq",
                 "xa_wkv", "xa_wo", "ln2_g", "ln2_b", "peer_wq", "peer_k1", "peer_k2", "peer_u",
                 "peer_v", "ln3_g", "ln3_b")


def kernel(x, mem, w_in, w_out, rg_conv_w, rg_conv_b, rg_wa, rg_ba, rg_wi, rg_bi, rg_lambda, fox_bf, sc_conv_w, mix_norm_g, ln1_g, ln1_b, xa_wq, xa_wkv, xa_wo, ln2_g, ln2_b, peer_wq, peer_k1, peer_k2, peer_u, peer_v, ln3_g, ln3_b):
    stacked = dict(zip(_LAYER_PARAMS, (
        w_in, w_out, rg_conv_w, rg_conv_b, rg_wa, rg_ba, rg_wi, rg_bi, rg_lambda, fox_bf,
        sc_conv_w, mix_norm_g, ln1_g, ln1_b, xa_wq, xa_wkv, xa_wo, ln2_g, ln2_b, peer_wq,
        peer_k1, peer_k2, peer_u, peer_v, ln3_g, ln3_b)))
    B, S, D = x.shape
    M = mem.shape[1]
    x2d = x.reshape(B * S, D)
    mem2d = mem.reshape(B * M, D)
    for l in range(w_in.shape[0]):
        p = {k: v[l] for k, v in stacked.items()}
        kmem, vmem_ = _kv(mem2d, p["xa_wkv"].astype(BF16))
        x2d = _layer(x2d, kmem.reshape(B, M, D), vmem_.reshape(B, M, D), B, S, p)
    return x2d.reshape(B, S, D)
```

```python
import functools
import math

import jax
import jax.numpy as jnp
from jax import lax
from jax.experimental import pallas as pl
from jax.experimental.pallas import tpu as pltpu

F32 = jnp.float32
BF16 = jnp.bfloat16
I32 = jnp.int32

D_MODEL = 1024
GROUP_WIDTH = 256
GROUP_HEADS = 4
HEAD_DIM = 64
N_IN = 2820
RGLRU_C = 8.0
XA_HEADS = 4
XA_HEAD_DIM = D_MODEL // XA_HEADS
PEER_HEADS = 8
N_KEYS = 128
PEER_HALF = 128
PEER_TOPK = 16
DEPTH = 2
ALPHA = (2.0 * DEPTH) ** 0.25
LN_EPS = 1e-5

SUBLANES = 8
LANES = 128
WORD_ROWS = D_MODEL // (2 * LANES)
ROW_CHUNKS = D_MODEL // LANES
CHUNK_SHIFT = ROW_CHUNKS.bit_length() - 1

VMEM_LIMIT = 48 * 1024 * 1024


def _params(sem, vmem=VMEM_LIMIT):
    return pltpu.CompilerParams(dimension_semantics=sem, vmem_limit_bytes=vmem)


def _layer_norm(z, g, b):
    mu = jnp.mean(z, axis=-1, keepdims=True)
    zc = z - mu
    var = jnp.mean(zc * zc, axis=-1, keepdims=True)
    return zc * lax.rsqrt(var + LN_EPS) * g + b


def _log_sigmoid(z):
    return jnp.minimum(z, 0.0) - jnp.log1p(jnp.exp(-jnp.abs(z)))


def _nt_dot(a, b):
    return lax.dot_general(a, b, (((1,), (1,)), ((), ())), preferred_element_type=F32)


C_RG, C_FOX, C_SB, C_SC, C_F, C_END = 0, 512, 1280, 2048, 2816, 2944


def _inproj_kernel(x_ref, w_ref, rg_ref, fox_ref, sb_ref, sc_ref, f_ref):
    xb = x_ref[...].astype(BF16)

    def mm(lo, hi):
        return jnp.dot(xb, w_ref[:, lo:hi], preferred_element_type=F32)

    def heads(ref, lo, hi):
        qkv = mm(lo, hi)
        for j in range(3 * GROUP_HEADS):
            ref[0, j] = qkv[:, j * HEAD_DIM:(j + 1) * HEAD_DIM].astype(BF16)

    rg_ref[...] = mm(C_RG, C_FOX)
    heads(fox_ref, C_FOX, C_SB)
    heads(sb_ref, C_SB, C_SC)
    sc_ref[...] = mm(C_SC, C_F)
    f_ref[...] = mm(C_F, C_END)


def _inproj(x2d, w_cat, B, S, tm=512):
    T, D = x2d.shape
    per_seq = S // tm
    flat = ((C_FOX - C_RG, F32), (C_F - C_SC, F32), (C_END - C_F, F32))
    flat_specs = [pl.BlockSpec((tm, w), lambda i: (i, 0)) for w, _ in flat]
    flat_shapes = [jax.ShapeDtypeStruct((T, w), dt) for w, dt in flat]
    head_spec = pl.BlockSpec((1, 3 * GROUP_HEADS, tm, HEAD_DIM),
                             lambda i: (i // per_seq, 0, i % per_seq, 0))
    head_shape = jax.ShapeDtypeStruct((B, 3 * GROUP_HEADS, S, HEAD_DIM), BF16)
    return pl.pallas_call(
        _inproj_kernel,
        grid=(T // tm,),
        in_specs=[pl.BlockSpec((tm, D), lambda i: (i, 0)),
                  pl.BlockSpec((D, C_END), lambda i: (0, 0))],
        out_specs=[flat_specs[0], head_spec, head_spec, flat_specs[1], flat_specs[2]],
        out_shape=[flat_shapes[0], head_shape, head_shape, flat_shapes[1], flat_shapes[2]],
        compiler_params=_params(("parallel",)),
        name="inproj",
    )(x2d, w_cat)


def _rgsc_kernel(rg_ref, sc_ref, f_ref, cw_ref, cb_ref, wa_ref, ba_ref, wi_ref, bi_ref, lam_ref,
                 scw_ref, fb_ref, yrg_ref, ysc_ref, cum_ref, cumt_ref, xprev, chprev, hprev, cprev,
                 *, ts, tk):
    GW = GROUP_WIDTH

    @pl.when(pl.program_id(1) == 0)
    def _():
        xprev[...] = jnp.zeros_like(xprev)
        chprev[...] = jnp.zeros_like(chprev)
        hprev[...] = jnp.zeros_like(hprev)
        cprev[...] = jnp.zeros_like(cprev)

    row = lax.broadcasted_iota(I32, (ts, GW), 0)

    def delayed(prev, cur, d):
        ext = jnp.concatenate([prev, cur], axis=0)
        return pltpu.roll(ext, d, 0)[SUBLANES:]

    xr = rg_ref[0, :, :GW]
    gate = rg_ref[0, :, GW:]
    xp = xprev[...]
    cw = cw_ref[...]
    xc = (delayed(xp, xr, 3) * cw[0:1] + delayed(xp, xr, 2) * cw[1:2]
          + delayed(xp, xr, 1) * cw[2:3] + xr * cw[3:4] + cb_ref[...])
    xprev[...] = xr[ts - SUBLANES:]
    xcb = xc.astype(BF16)
    r = jax.nn.sigmoid(jnp.dot(xcb, wa_ref[...], preferred_element_type=F32) + ba_ref[...])
    ig = jax.nn.sigmoid(jnp.dot(xcb, wi_ref[...], preferred_element_type=F32) + bi_ref[...])
    z = -lam_ref[...]
    softplus = jnp.maximum(z, 0.0) + jnp.log1p(jnp.exp(-jnp.abs(z)))
    log_a = -RGLRU_C * r * softplus
    a = jnp.exp(log_a)
    u = jnp.sqrt(-jnp.tanh(log_a) * (a * a + 1.0)) * (ig * xc)
    acc_a, acc_b = a, u
    d = 1
    while d < ts:
        keep = row >= d
        a_s = jnp.where(keep, pltpu.roll(acc_a, d, 0), 1.0)
        b_s = jnp.where(keep, pltpu.roll(acc_b, d, 0), 0.0)
        acc_b = acc_a * b_s + acc_b
        acc_a = acc_a * a_s
        d *= 2
    h = acc_b + acc_a * hprev[...]
    hprev[...] = h[ts - 1:]
    c0 = math.sqrt(2.0 / math.pi)
    gelu = 0.5 * gate * (1.0 + jnp.tanh(c0 * (gate + 0.044715 * gate * gate * gate)))
    yrg_ref[0] = h * gelu

    bg = sc_ref[0, :, :GW]
    ch = sc_ref[0, :, GW:2 * GW] * sc_ref[0, :, 2 * GW:]
    cp = chprev[...]
    sw = scw_ref[...]
    ysc_ref[0] = bg * (delayed(cp, ch, 2) * sw[0:1] + delayed(cp, ch, 1) * sw[1:2] + ch * sw[2:3])
    chprev[...] = ch[ts - SUBLANES:]

    rowf = lax.broadcasted_iota(I32, (ts, LANES), 0)
    c = _log_sigmoid(f_ref[0] + fb_ref[...])
    d = 1
    while d < ts:
        c = c + jnp.where(rowf >= d, pltpu.roll(c, d, 0), 0.0)
        d *= 2
    c = c + cprev[...]
    cprev[...] = c[ts - 1:]
    cum_ref[0] = c
    ct = c.T[:SUBLANES]
    for j in range(ts // tk):
        cumt_ref[0, j] = ct[:, j * tk:(j + 1) * tk]


def _rgsc(rg, sc, f, cw, cb, wa, ba, wi, bi, lam, scw, fb, tk, ts=512):
    B, S, _ = rg.shape
    GW = GROUP_WIDTH
    per = ts // tk

    def full(a):
        return pl.BlockSpec(a.shape, lambda b, t: (0,) * a.ndim)

    def seq(w):
        return pl.BlockSpec((1, ts, w), lambda b, t: (b, t, 0))

    params = (cw, cb, wa, ba, wi, bi, lam, scw, fb)
    return pl.pallas_call(
        functools.partial(_rgsc_kernel, ts=ts, tk=tk),
        grid=(B, S // ts),
        in_specs=[seq(2 * GW), seq(3 * GW), seq(LANES)] + [full(p) for p in params],
        out_specs=[seq(GW), seq(GW), seq(LANES),
                   pl.BlockSpec((1, per, SUBLANES, tk), lambda b, t: (b, t, 0, 0))],
        out_shape=[jax.ShapeDtypeStruct((B, S, GW), F32), jax.ShapeDtypeStruct((B, S, GW), F32),
                   jax.ShapeDtypeStruct((B, S, LANES), F32),
                   jax.ShapeDtypeStruct((B, S // tk, SUBLANES, tk), F32)],
        scratch_shapes=[pltpu.VMEM((SUBLANES, GW), F32), pltpu.VMEM((SUBLANES, GW), F32),
                        pltpu.VMEM((1, GW), F32), pltpu.VMEM((1, LANES), F32)],
        compiler_params=_params(("parallel", "arbitrary")),
        name="rgsc",
    )(rg, sc, f, *params)


def _fox_kernel(q_ref, k_ref, v_ref, cc_ref, cr_ref, o_ref, *, tq):
    qi = pl.program_id(1)
    scale = HEAD_DIM ** -0.5
    rowi = lax.broadcasted_iota(I32, (tq, tq), 0)
    coli = lax.broadcasted_iota(I32, (tq, tq), 1)
    causal = coli <= rowi
    outs = []
    for h in range(GROUP_HEADS):
        q = q_ref[0, h]
        cc = cc_ref[0, :, h:h + 1]

        def step(ki, carry, diag, h=h, q=q, cc=cc):
            m, l, acc = carry
            off = pl.multiple_of(ki * tq, tq)
            k = k_ref[0, h, pl.ds(off, tq), :]
            v = v_ref[0, h, pl.ds(off, tq), :]
            s = _nt_dot(q, k) * scale + cc - cr_ref[0, ki, h:h + 1, :]
            if diag:
                s = jnp.where(causal, s, -jnp.inf)
            m_new = jnp.maximum(m, jnp.max(s, axis=-1, keepdims=True))
            alpha = jnp.exp(m - m_new)
            p = jnp.exp(s - m_new)
            l = alpha * l + jnp.sum(p, axis=-1, keepdims=True)
            acc = alpha * acc + jnp.dot(p.astype(BF16), v, preferred_element_type=F32)
            return m_new, l, acc

        init = (jnp.full((tq, 1), -jnp.inf, F32), jnp.zeros((tq, 1), F32),
                jnp.zeros((tq, HEAD_DIM), F32))
        carry = lax.fori_loop(0, qi, functools.partial(step, diag=False), init)
        _, l, acc = step(qi, carry, True)
        outs.append(acc / l)
    o_ref[0] = jnp.concatenate(outs, axis=-1)


def _qkv_specs(S, tq):
    H, d = GROUP_HEADS, HEAD_DIM
    return [pl.BlockSpec((1, H, tq, d), lambda b, i: (b, 0, i, 0)),
            pl.BlockSpec((1, H, S, d), lambda b, i: (b, 1, 0, 0)),
            pl.BlockSpec((1, H, S, d), lambda b, i: (b, 2, 0, 0))]


def _fox(qkv, cum, cumt, tq=256):
    B, _, S, d = qkv.shape
    H = GROUP_HEADS
    return pl.pallas_call(
        functools.partial(_fox_kernel, tq=tq),
        grid=(B, S // tq),
        in_specs=_qkv_specs(S, tq)
        + [pl.BlockSpec((1, tq, LANES), lambda b, i: (b, i, 0)),
           pl.BlockSpec((1, S // tq, SUBLANES, tq), lambda b, i: (b, 0, 0, 0))],
        out_specs=pl.BlockSpec((1, tq, H * d), lambda b, i: (b, i, 0)),
        out_shape=jax.ShapeDtypeStruct((B, S, H * d), F32),
        compiler_params=_params(("parallel", "arbitrary")),
        name="fox_attn",
    )(qkv, qkv, qkv, cum, cumt)


EXP_UNDERFLOW = -104.0


def _sb_kernel(q_ref, k_ref, v_ref, o_ref, *, tq):
    qi = pl.program_id(1)
    scale = HEAD_DIM ** -0.5
    rowi = lax.broadcasted_iota(I32, (tq, tq), 0)
    coli = lax.broadcasted_iota(I32, (tq, tq), 1)
    strict = coli < rowi
    later = jnp.where(rowi > coli, 1.0, 0.0).astype(BF16)
    outs = []
    for h in range(GROUP_HEADS):
        q = q_ref[0, h]

        def step(ki, carry, diag, h=h, q=q):
            rest, acc = carry
            off = pl.multiple_of(ki * tq, tq)
            k = k_ref[0, h, pl.ds(off, tq), :]
            v = v_ref[0, h, pl.ds(off, tq), :]
            z = _nt_dot(q, k) * scale
            ls = _log_sigmoid(z)
            l1m = ls - z
            if diag:
                l1m = jnp.where(strict, l1m, 0.0)
            hi = l1m.astype(BF16)
            lo = (l1m - hi.astype(F32)).astype(BF16)
            tail = (jnp.dot(hi, later, preferred_element_type=F32)
                    + jnp.dot(lo, later, preferred_element_type=F32) + rest)
            w = jnp.exp(ls + tail)
            if diag:
                w = jnp.where(strict, w, 0.0)
            acc = acc + jnp.dot(w.astype(BF16), v, preferred_element_type=F32)
            rest = rest + jnp.sum(l1m, axis=-1, keepdims=True)
            return rest, acc

        carry = (jnp.zeros((tq, 1), F32), jnp.zeros((tq, HEAD_DIM), F32))
        rest, acc = step(qi, carry, True)

        def more(c):
            j, rest, _ = c
            return jnp.logical_and(j < qi, jnp.max(rest) > EXP_UNDERFLOW)

        def further(c):
            j, rest, acc = c
            rest, acc = step(qi - 1 - j, (rest, acc), False)
            return j + 1, rest, acc

        _, _, acc = lax.while_loop(more, further, (jnp.int32(0), rest, acc))
        outs.append(acc)
    o_ref[0] = jnp.concatenate(outs, axis=-1)


def _sb(qkv, tq=256):
    B, _, S, d = qkv.shape
    H = GROUP_HEADS
    return pl.pallas_call(
        functools.partial(_sb_kernel, tq=tq),
        grid=(B, S // tq),
        in_specs=_qkv_specs(S, tq),
        out_specs=pl.BlockSpec((1, tq, H * d), lambda b, i: (b, i, 0)),
        out_shape=jax.ShapeDtypeStruct((B, S, H * d), F32),
        compiler_params=_params(("parallel", "arbitrary")),
        name="sb_attn",
    )(qkv, qkv, qkv)


def _mixout_kernel(y0_ref, y1_ref, y2_ref, y3_ref, g_ref, w_ref, x_ref, lg_ref, lb_ref, o_ref):
    def rms(y):
        return y * lax.rsqrt(jnp.mean(y * y, axis=-1, keepdims=True) + 1e-6)

    y = jnp.concatenate([rms(r[...]) for r in (y0_ref, y1_ref, y2_ref, y3_ref)], axis=-1)
    y = (y * g_ref[...]).astype(BF16)
    mix = jnp.dot(y, w_ref[...], preferred_element_type=F32)
    o_ref[...] = _layer_norm(ALPHA * x_ref[...] + mix, lg_ref[...], lb_ref[...])


def _mixout(ys, g, w, x2d, lg, lb, tm=512):
    T, D = x2d.shape
    GW = GROUP_WIDTH

    def full(a):
        return pl.BlockSpec(a.shape, lambda i: (0,) * a.ndim)

    return pl.pallas_call(
        _mixout_kernel,
        grid=(T // tm,),
        in_specs=[pl.BlockSpec((tm, GW), lambda i: (i, 0))] * 4
        + [full(g), full(w), pl.BlockSpec((tm, D), lambda i: (i, 0)), full(lg), full(lb)],
        out_specs=pl.BlockSpec((tm, D), lambda i: (i, 0)),
        out_shape=jax.ShapeDtypeStruct((T, D), F32),
        compiler_params=_params(("parallel",)),
        name="mixout",
    )(*ys, g, w, x2d, lg, lb)


def _kv_kernel(m_ref, w_ref, k_ref, v_ref):
    kv = jnp.dot(m_ref[...].astype(BF16), w_ref[...], preferred_element_type=F32)
    k_ref[...] = kv[:, :D_MODEL].astype(BF16)
    v_ref[...] = kv[:, D_MODEL:].astype(BF16)


def _kv(mem2d, wkv, tm=512):
    M, D = mem2d.shape
    return pl.pallas_call(
        _kv_kernel,
        grid=(M // tm,),
        in_specs=[pl.BlockSpec((tm, D), lambda i: (i, 0)),
                  pl.BlockSpec((D, 2 * D), lambda i: (0, 0))],
        out_specs=[pl.BlockSpec((tm, D), lambda i: (i, 0))] * 2,
        out_shape=[jax.ShapeDtypeStruct((M, D), BF16)] * 2,
        compiler_params=_params(("parallel",)),
        name="mem_kv",
    )(mem2d, wkv)


PEER_TB = 32


def _chunk_major_spec(tm):
    return pl.BlockSpec((tm // PEER_TB, ROW_CHUNKS * PEER_TB, LANES), lambda i: (i, 0, 0))


def _chunk_major_shape(T):
    return jax.ShapeDtypeStruct((T // PEER_TB, ROW_CHUNKS * PEER_TB, LANES), F32)


def _to_chunk_major(y, ref):
    for g in range(y.shape[0] // PEER_TB):
        for c in range(ROW_CHUNKS):
            ref[g, c * PEER_TB:(c + 1) * PEER_TB, :] = (
                y[g * PEER_TB:(g + 1) * PEER_TB, c * LANES:(c + 1) * LANES])


def _xattn_kernel(x_ref, wq_ref, k_ref, v_ref, wo_ref, lg_ref, lb_ref, o_ref, oc_ref):
    x = x_ref[...]
    q = jnp.dot(x.astype(BF16), wq_ref[...], preferred_element_type=F32).astype(BF16)
    scale = XA_HEAD_DIM ** -0.5
    outs = []
    for h in range(XA_HEADS):
        sl = slice(h * XA_HEAD_DIM, (h + 1) * XA_HEAD_DIM)
        s = _nt_dot(q[:, sl], k_ref[0, :, sl]) * scale
        p = jnp.exp(s - jnp.max(s, axis=-1, keepdims=True))
        p = p / jnp.sum(p, axis=-1, keepdims=True)
        outs.append(jnp.dot(p.astype(BF16), v_ref[0, :, sl], preferred_element_type=F32))
    o = jnp.concatenate(outs, axis=-1).astype(BF16)
    xa = jnp.dot(o, wo_ref[...], preferred_element_type=F32)
    y = _layer_norm(ALPHA * x + xa, lg_ref[...], lb_ref[...])
    o_ref[...] = y
    _to_chunk_major(y, oc_ref)


def _xattn(x2d, wq, k, v, wo, lg, lb, seq_len, tm=512):
    T, D = x2d.shape
    M = k.shape[1]
    per_seq = seq_len // tm

    def full(a):
        return pl.BlockSpec(a.shape, lambda i: (0,) * a.ndim)

    return pl.pallas_call(
        _xattn_kernel,
        grid=(T // tm,),
        in_specs=[pl.BlockSpec((tm, D), lambda i: (i, 0)), full(wq),
                  pl.BlockSpec((1, M, D), lambda i: (i // per_seq, 0, 0)),
                  pl.BlockSpec((1, M, D), lambda i: (i // per_seq, 0, 0)),
                  full(wo), full(lg), full(lb)],
        out_specs=[pl.BlockSpec((tm, D), lambda i: (i, 0)), _chunk_major_spec(tm)],
        out_shape=[jax.ShapeDtypeStruct((T, D), F32), _chunk_major_shape(T)],
        compiler_params=_params(("parallel",)),
        name="xattn",
    )(x2d, wq, k, v, wo, lg, lb)


E_PER_TOK = PEER_HEADS * PEER_TOPK


def _staircase():
    return [(a, b) for a in range(PEER_TOPK) for b in range(PEER_TOPK // (a + 1))]


def _route_kernel(x_ref, wq_ref, k1_ref, k2_ref, e_ref, g_ref, v_scr, i_scr, *, ts):
    xb = x_ref[...].astype(BF16)
    key_id = lax.broadcasted_iota(I32, (N_KEYS, ts), 0).astype(F32)
    qd = 2 * PEER_HALF
    for h in range(PEER_HEADS):
        qry = jnp.dot(xb, wq_ref[:, h * qd:(h + 1) * qd], preferred_element_type=F32)
        for half, kref in ((0, k1_ref), (1, k2_ref)):
            qh = qry[:, half * PEER_HALF:(half + 1) * PEER_HALF].astype(BF16)
            s0 = _nt_dot(kref[...], qh)

            def pick(it, s, h=h, half=half):
                m = jnp.max(s, axis=0, keepdims=True)
                idx = jnp.min(jnp.where(s == m, key_id, float(N_KEYS)), axis=0, keepdims=True)
                v_scr[half, it, h:h + 1, :] = m
                i_scr[half, it, h:h + 1, :] = idx.astype(I32)
                return jnp.where(key_id == idx, -jnp.inf, s)

            lax.fori_loop(0, PEER_TOPK, pick, s0)

    cells = _staircase()
    cand = tuple(v_scr[0, a] + v_scr[1, b] for a, b in cells)
    expert = [i_scr[0, a] * N_KEYS + i_scr[1, b] for a, b in cells]

    def select(it, carry):
        cand, top = carry
        best_v, best_e, best_c = cand[0], expert[0], jnp.zeros((PEER_HEADS, ts), I32)
        for c in range(1, len(cells)):
            better = cand[c] > best_v
            best_v = jnp.where(better, cand[c], best_v)
            best_e = jnp.where(better, expert[c], best_e)
            best_c = jnp.where(better, c, best_c)
        top = jnp.where(it == 0, best_v, top)
        i_scr[0, it] = best_e * WORD_ROWS
        v_scr[0, it] = best_v
        cand = tuple(jnp.where(best_c == c, -jnp.inf, cand[c]) for c in range(len(cells)))
        return cand, top

    _, top = lax.fori_loop(0, PEER_TOPK, select, (cand, jnp.zeros((PEER_HEADS, ts), F32)))
    ex = [jnp.exp(v_scr[0, it] - top) for it in range(PEER_TOPK)]
    den = ex[0]
    for it in range(1, PEER_TOPK):
        den = den + ex[it]
    inv = 1.0 / den
    g_ref[...] = jnp.concatenate([e * inv for e in ex], axis=0).T
    rows = pltpu.bitcast(i_scr[0].reshape(E_PER_TOK, ts), F32)
    e_ref[...] = pltpu.bitcast(rows.T, I32)


def _route(x2d, wq, k1, k2, ts=256):
    T, D = x2d.shape
    nb = T // ts

    def full(a):
        return pl.BlockSpec(a.shape, lambda i: (0,) * a.ndim)

    blk = pl.BlockSpec((ts, E_PER_TOK), lambda i: (i, 0))
    return pl.pallas_call(
        functools.partial(_route_kernel, ts=ts),
        grid=(nb,),
        in_specs=[pl.BlockSpec((ts, D), lambda i: (i, 0)), full(wq), full(k1), full(k2)],
        out_specs=[blk, blk],
        out_shape=[jax.ShapeDtypeStruct((T, E_PER_TOK), I32),
                   jax.ShapeDtypeStruct((T, E_PER_TOK), F32)],
        scratch_shapes=[pltpu.VMEM((2, PEER_TOPK, PEER_HEADS, ts), F32),
                        pltpu.VMEM((2, PEER_TOPK, PEER_HEADS, ts), I32)],
        compiler_params=_params(("parallel",)),
        name="peer_route",
    )(x2d, wq, k1, k2)


TOK_GROUP = SUBLANES


def _gather_rows(ids_ref, tab_ref, rows_scr, tok0):
    for tt in range(TOK_GROUP):
        for j in range(E_PER_TOK):
            r0 = pl.multiple_of(ids_ref[tok0 + tt, j], WORD_ROWS)
            rows_scr[tt, j * WORD_ROWS:(j + 1) * WORD_ROWS, :] = tab_ref[pl.ds(r0, WORD_ROWS), :]


def _token_chunks(tt):
    return pl.ds(tt, ROW_CHUNKS, stride=PEER_TB)


def _chunk_of_col():
    col = lax.broadcasted_iota(I32, (ROW_CHUNKS, E_PER_TOK * ROW_CHUNKS), 1)
    row = lax.broadcasted_iota(I32, (ROW_CHUNKS, E_PER_TOK * ROW_CHUNKS), 0)
    return (col & (ROW_CHUNKS - 1)) == row


def _peer_score_kernel(ids_ref, x_ref, tab_ref, s_ref, rows_scr, part_scr):
    own_chunk = _chunk_of_col()
    c_id = lax.broadcasted_iota(I32, (E_PER_TOK * ROW_CHUNKS, E_PER_TOK), 0)
    e_id = lax.broadcasted_iota(I32, (E_PER_TOK * ROW_CHUNKS, E_PER_TOK), 1)
    fold = jnp.where(c_id >> CHUNK_SHIFT == e_id, 1.0, 0.0).astype(BF16)

    for g in range(PEER_TB // TOK_GROUP):
        tok0 = g * TOK_GROUP
        _gather_rows(ids_ref, tab_ref, rows_scr, tok0)
        for tt in range(TOK_GROUP):
            rows = pltpu.bitcast(rows_scr[tt], BF16)
            xt = x_ref[0, _token_chunks(tok0 + tt), :].astype(BF16)
            full = _nt_dot(xt, rows)
            part_scr[tt * ROW_CHUNKS:(tt + 1) * ROW_CHUNKS, :] = jnp.where(own_chunk, full, 0.0)
        part = part_scr[...]
        hi = part.astype(BF16)
        lo = (part - hi.astype(F32)).astype(BF16)
        sc = (jnp.dot(hi, fold, preferred_element_type=F32)
              + jnp.dot(lo, fold, preferred_element_type=F32))
        sc = jnp.sum(sc.reshape(TOK_GROUP, ROW_CHUNKS, E_PER_TOK), axis=1)
        s_ref[tok0:tok0 + TOK_GROUP, :] = sc


def _peer_mix_kernel(ids_ref, s_ref, gate_ref, tab_ref, o_ref, rows_scr):
    own_chunk = _chunk_of_col()
    e_id = lax.broadcasted_iota(I32, (E_PER_TOK, E_PER_TOK * ROW_CHUNKS), 0)
    c_id = lax.broadcasted_iota(I32, (E_PER_TOK, E_PER_TOK * ROW_CHUNKS), 1)
    spread = jnp.where(c_id >> CHUNK_SHIFT == e_id, 1.0, 0.0).astype(BF16)

    for g in range(PEER_TB // TOK_GROUP):
        tok0 = g * TOK_GROUP
        sl = slice(tok0, tok0 + TOK_GROUP)
        _gather_rows(ids_ref, tab_ref, rows_scr, tok0)
        s = s_ref[sl, :]
        act = 0.5 * s * (1.0 + lax.erf(s * (2.0 ** -0.5)))
        coef = (gate_ref[sl, :] * act).astype(BF16)
        coef = jnp.dot(coef, spread, preferred_element_type=F32)
        for tt in range(TOK_GROUP):
            rows = pltpu.bitcast(rows_scr[tt], BF16)
            ct = jnp.broadcast_to(coef[tt:tt + 1], (ROW_CHUNKS, E_PER_TOK * ROW_CHUNKS))
            ct = jnp.where(own_chunk, ct, 0.0).astype(BF16)
            o_ref[0, _token_chunks(tok0 + tt), :] = jnp.dot(ct, rows, preferred_element_type=F32)


def _peer_experts(ids, gate, xc, tab_u, tab_v):
    T = ids.shape[0]
    tb = PEER_TB
    nb = T // tb
    ids_spec = pl.BlockSpec((tb, E_PER_TOK), lambda i: (i, 0), memory_space=pltpu.SMEM)
    tab_spec = pl.BlockSpec(tab_u.shape, lambda i: (0, 0), pipeline_mode=pl.Buffered(1))
    tok_spec = pl.BlockSpec((tb, E_PER_TOK), lambda i: (i, 0))
    row_spec = _chunk_major_spec(tb)
    rows_scr = pltpu.VMEM((TOK_GROUP, E_PER_TOK * WORD_ROWS, LANES), I32)
    s = pl.pallas_call(
        _peer_score_kernel,
        grid=(nb,),
        in_specs=[ids_spec, row_spec, tab_spec],
        out_specs=tok_spec,
        out_shape=jax.ShapeDtypeStruct((T, E_PER_TOK), F32),
        scratch_shapes=[rows_scr,
                        pltpu.VMEM((TOK_GROUP * ROW_CHUNKS, E_PER_TOK * ROW_CHUNKS), F32)],
        compiler_params=_params(("arbitrary",)),
        name="peer_score",
    )(ids, xc, tab_u)
    return pl.pallas_call(
        _peer_mix_kernel,
        grid=(nb,),
        in_specs=[ids_spec, tok_spec, tok_spec, tab_spec],
        out_specs=row_spec,
        out_shape=_chunk_major_shape(T),
        scratch_shapes=[rows_scr],
        compiler_params=_params(("arbitrary",)),
        name="peer_mix",
    )(ids, s, gate, tab_v)


def _resln_kernel(x_ref, y_ref, lg_ref, lb_ref, o_ref):
    G = x_ref.shape[0]
    z = ALPHA * x_ref[...] + y_ref[...]
    zs = [z[:, c * PEER_TB:(c + 1) * PEER_TB, :] for c in range(ROW_CHUNKS)]
    tot = zs[0]
    for c in range(1, ROW_CHUNKS):
        tot = tot + zs[c]
    mu = jnp.sum(tot, axis=-1, keepdims=True) * (1.0 / D_MODEL)
    zc = [a - mu for a in zs]
    sq = zc[0] * zc[0]
    for c in range(1, ROW_CHUNKS):
        sq = sq + zc[c] * zc[c]
    rstd = lax.rsqrt(jnp.sum(sq, axis=-1, keepdims=True) * (1.0 / D_MODEL) + LN_EPS)
    for c in range(ROW_CHUNKS):
        cols = slice(c * LANES, (c + 1) * LANES)
        y = (zc[c] * rstd).reshape(G * PEER_TB, LANES)
        o_ref[:, cols] = y * lg_ref[:, cols] + lb_ref[:, cols]


def _resln(xc, yc, lg, lb, tm=512):
    T = xc.shape[0] * PEER_TB
    blk = _chunk_major_spec(tm)
    par = pl.BlockSpec((1, D_MODEL), lambda i: (0, 0))
    return pl.pallas_call(
        _resln_kernel,
        grid=(T // tm,),
        in_specs=[blk, blk, par, par],
        out_specs=pl.BlockSpec((tm, D_MODEL), lambda i: (i, 0)),
        out_shape=jax.ShapeDtypeStruct((T, D_MODEL), F32),
        compiler_params=_params(("parallel",)),
        name="res_ln",
    )(xc, yc, lg, lb)


def _pack_table(t):
    E = t.shape[0]
    b = lax.bitcast_convert_type(t.astype(BF16), jnp.uint16).astype(jnp.uint32)
    b = b.reshape(E, WORD_ROWS, 2, LANES)
    words = b[:, :, 0, :] | (b[:, :, 1, :] << 16)
    return lax.bitcast_convert_type(words, I32).reshape(E * WORD_ROWS, LANES)


def _block_diag(w):
    H, d, _ = w.shape
    out = jnp.zeros((H * d, H * d), w.dtype)
    for h in range(H):
        out = out.at[h * d:(h + 1) * d, h * d:(h + 1) * d].set(w[h])
    return out


def _cat_w_in(w):
    GW = GROUP_WIDTH
    off_f = 5 * GW
    off_sb = off_f + GROUP_HEADS
    off_sc = off_sb + 3 * GW
    pad = jnp.zeros((w.shape[0], LANES - GROUP_HEADS), w.dtype)
    return jnp.concatenate([w[:, :off_f], w[:, off_sb:off_sc], w[:, off_sc:N_IN],
                            w[:, off_f:off_sb], pad], axis=1).astype(BF16)


def _heads(t, B, S):
    t = t.reshape(B, S, 3, GROUP_HEADS, HEAD_DIM).transpose(2, 0, 3, 1, 4)
    return t[0], t[1], t[2]


def _row(v, width=None):
    v = v.reshape(1, -1).astype(F32)
    if width is not None and v.shape[1] < width:
        v = jnp.pad(v, ((0, 0), (0, width - v.shape[1])))
    return v


def _layer(x2d, kmem, vmem_, B, S, p, tq=256):
    T = B * S
    rg, fox, sb, sc, f = _inproj(x2d, _cat_w_in(p["w_in"]), B, S)
    yrg, ysc, cum, cumt = _rgsc(
        rg.reshape(B, S, -1), sc.reshape(B, S, -1), f.reshape(B, S, -1),
        p["rg_conv_w"], _row(p["rg_conv_b"]), _block_diag(p["rg_wa"]).astype(BF16), _row(p["rg_ba"]),
        _block_diag(p["rg_wi"]).astype(BF16), _row(p["rg_bi"]), _row(p["rg_lambda"]),
        p["sc_conv_w"], _row(p["fox_bf"], LANES), tq)
    yfox = _fox(fox, cum, cumt, tq=tq)
    ysb = _sb(sb, tq=tq)
    ys = (yrg.reshape(T, -1), yfox.reshape(T, -1), ysb.reshape(T, -1), ysc.reshape(T, -1))
    x1 = _mixout(ys, _row(p["mix_norm_g"]), p["w_out"].astype(BF16), x2d,
                 _row(p["ln1_g"]), _row(p["ln1_b"]))
    x2, x2c = _xattn(x1, p["xa_wq"].astype(BF16), kmem, vmem_, p["xa_wo"].astype(BF16),
                     _row(p["ln2_g"]), _row(p["ln2_b"]), S)
    ids, gate = _route(x2, p["peer_wq"].astype(BF16), p["peer_k1"].astype(BF16),
                       p["peer_k2"].astype(BF16))
    ffc = _peer_experts(ids, gate, x2c, _pack_table(p["peer_u"]), _pack_table(p["peer_v"]))
    return _resln(x2c, ffc, _row(p["ln3_g"]), _row(p["ln3_b"]))


_LAYER_PARAMS = ("w_in", "w_out", "rg_conv_w", "rg_conv_b", "rg_wa", "rg_ba", "rg_wi", "rg_bi",
                 "rg_lambda", "fox_bf", "sc_conv_w", "mix_norm_g", "ln1_g", "ln1_b", "xa_wq",
                 "xa_wkv", "xa_wo", "ln2_g", "ln2_b", "peer_wq", "peer_k1", "peer_k2", "peer_u",
                 "peer_v", "ln3_g", "ln3_b")


def kernel(x, mem, w_in, w_out, rg_conv_w, rg_conv_b, rg_wa, rg_ba, rg_wi, rg_bi, rg_lambda, fox_bf, sc_conv_w, mix_norm_g, ln1_g, ln1_b, xa_wq, xa_wkv, xa_wo, ln2_g, ln2_b, peer_wq, peer_k1, peer_k2, peer_u, peer_v, ln3_g, ln3_b):
    stacked = dict(zip(_LAYER_PARAMS, (
        w_in, w_out, rg_conv_w, rg_conv_b, rg_wa, rg_ba, rg_wi, rg_bi, rg_lambda, fox_bf,
        sc_conv_w, mix_norm_g, ln1_g, ln1_b, xa_wq, xa_wkv, xa_wo, ln2_g, ln2_b, peer_wq,
        peer_k1, peer_k2, peer_u, peer_v, ln3_g, ln3_b)))
    B, S, D = x.shape
    M = mem.shape[1]
    x2d = x.reshape(B * S, D)
    mem2d = mem.reshape(B * M, D)
    for l in range(w_in.shape[0]):
        p = {k: v[l] for k, v in stacked.items()}
        kmem, vmem_ = _kv(mem2d, p["xa_wkv"].astype(BF16))
        x2d = _layer(x2d, kmem.reshape(B, M, D), vmem_.reshape(B, M, D), B, S, p)
    return x2d.reshape(B, S, D)
```

```python
import functools
import math

import jax
import jax.numpy as jnp
from jax import lax
from jax.experimental import pallas as pl
from jax.experimental.pallas import tpu as pltpu

F32 = jnp.float32
BF16 = jnp.bfloat16
I32 = jnp.int32

D_MODEL = 1024
GROUP_WIDTH = 256
GROUP_HEADS = 4
HEAD_DIM = 64
N_IN = 2820
RGLRU_C = 8.0
XA_HEADS = 4
XA_HEAD_DIM = D_MODEL // XA_HEADS
PEER_HEADS = 8
N_KEYS = 128
PEER_HALF = 128
PEER_TOPK = 16
DEPTH = 2
ALPHA = (2.0 * DEPTH) ** 0.25
LN_EPS = 1e-5

SUBLANES = 8
LANES = 128
WORD_ROWS = D_MODEL // (2 * LANES)
ROW_CHUNKS = D_MODEL // LANES
CHUNK_SHIFT = ROW_CHUNKS.bit_length() - 1

VMEM_LIMIT = 48 * 1024 * 1024


def _params(sem, vmem=VMEM_LIMIT):
    return pltpu.CompilerParams(dimension_semantics=sem, vmem_limit_bytes=vmem)


def _layer_norm(z, g, b):
    mu = jnp.mean(z, axis=-1, keepdims=True)
    zc = z - mu
    var = jnp.mean(zc * zc, axis=-1, keepdims=True)
    return zc * lax.rsqrt(var + LN_EPS) * g + b


def _log_sigmoid(z):
    return jnp.minimum(z, 0.0) - jnp.log1p(jnp.exp(-jnp.abs(z)))


def _nt_dot(a, b):
    return lax.dot_general(a, b, (((1,), (1,)), ((), ())), preferred_element_type=F32)


C_RG, C_FOX, C_SB, C_SC, C_F, C_END = 0, 512, 1280, 2048, 2816, 2944


def _inproj_kernel(x_ref, w_ref, rg_ref, fox_ref, sb_ref, sc_ref, f_ref):
    xb = x_ref[...].astype(BF16)

    def mm(lo, hi):
        return jnp.dot(xb, w_ref[:, lo:hi], preferred_element_type=F32)

    def heads(ref, lo, hi):
        qkv = mm(lo, hi)
        for j in range(3 * GROUP_HEADS):
            ref[0, j] = qkv[:, j * HEAD_DIM:(j + 1) * HEAD_DIM].astype(BF16)

    rg_ref[...] = mm(C_RG, C_FOX)
    heads(fox_ref, C_FOX, C_SB)
    heads(sb_ref, C_SB, C_SC)
    sc_ref[...] = mm(C_SC, C_F)
    f_ref[...] = mm(C_F, C_END)


def _inproj(x2d, w_cat, B, S, tm=512):
    T, D = x2d.shape
    per_seq = S // tm
    flat = ((C_FOX - C_RG, F32), (C_F - C_SC, F32), (C_END - C_F, F32))
    flat_specs = [pl.BlockSpec((tm, w), lambda i: (i, 0)) for w, _ in flat]
    flat_shapes = [jax.ShapeDtypeStruct((T, w), dt) for w, dt in flat]
    head_spec = pl.BlockSpec((1, 3 * GROUP_HEADS, tm, HEAD_DIM),
                             lambda i: (i // per_seq, 0, i % per_seq, 0))
    head_shape = jax.ShapeDtypeStruct((B, 3 * GROUP_HEADS, S, HEAD_DIM), BF16)
    return pl.pallas_call(
        _inproj_kernel,
        grid=(T // tm,),
        in_specs=[pl.BlockSpec((tm, D), lambda i: (i, 0)),
                  pl.BlockSpec((D, C_END), lambda i: (0, 0))],
        out_specs=[flat_specs[0], head_spec, head_spec, flat_specs[1], flat_specs[2]],
        out_shape=[flat_shapes[0], head_shape, head_shape, flat_shapes[1], flat_shapes[2]],
        compiler_params=_params(("parallel",)),
        name="inproj",
    )(x2d, w_cat)


def _rgsc_kernel(rg_ref, sc_ref, f_ref, cw_ref, cb_ref, wa_ref, ba_ref, wi_ref, bi_ref, lam_ref,
                 scw_ref, fb_ref, yrg_ref, ysc_ref, cum_ref, cumt_ref, xprev, chprev, hprev, cprev,
                 *, ts, tk):
    GW = GROUP_WIDTH

    @pl.when(pl.program_id(1) == 0)
    def _():
        xprev[...] = jnp.zeros_like(xprev)
        chprev[...] = jnp.zeros_like(chprev)
        hprev[...] = jnp.zeros_like(hprev)
        cprev[...] = jnp.zeros_like(cprev)

    row = lax.broadcasted_iota(I32, (ts, GW), 0)

    def delayed(prev, cur, d):
        ext = jnp.concatenate([prev, cur], axis=0)
        return pltpu.roll(ext, d, 0)[SUBLANES:]

    xr = rg_ref[0, :, :GW]
    gate = rg_ref[0, :, GW:]
    xp = xprev[...]
    cw = cw_ref[...]
    xc = (delayed(xp, xr, 3) * cw[0:1] + delayed(xp, xr, 2) * cw[1:2]
          + delayed(xp, xr, 1) * cw[2:3] + xr * cw[3:4] + cb_ref[...])
    xprev[...] = xr[ts - SUBLANES:]
    xcb = xc.astype(BF16)
    r = jax.nn.sigmoid(jnp.dot(xcb, wa_ref[...], preferred_element_type=F32) + ba_ref[...])
    ig = jax.nn.sigmoid(jnp.dot(xcb, wi_ref[...], preferred_element_type=F32) + bi_ref[...])
    z = -lam_ref[...]
    softplus = jnp.maximum(z, 0.0) + jnp.log1p(jnp.exp(-jnp.abs(z)))
    log_a = -RGLRU_C * r * softplus
    a = jnp.exp(log_a)
    u = jnp.sqrt(-jnp.tanh(log_a) * (a * a + 1.0)) * (ig * xc)
    acc_a, acc_b = a, u
    d = 1
    while d < ts:
        keep = row >= d
        a_s = jnp.where(keep, pltpu.roll(acc_a, d, 0), 1.0)
        b_s = jnp.where(keep, pltpu.roll(acc_b, d, 0), 0.0)
        acc_b = acc_a * b_s + acc_b
        acc_a = acc_a * a_s
        d *= 2
    h = acc_b + acc_a * hprev[...]
    hprev[...] = h[ts - 1:]
    c0 = math.sqrt(2.0 / math.pi)
    gelu = 0.5 * gate * (1.0 + jnp.tanh(c0 * (gate + 0.044715 * gate * gate * gate)))
    yrg_ref[0] = h * gelu

    bg = sc_ref[0, :, :GW]
    ch = sc_ref[0, :, GW:2 * GW] * sc_ref[0, :, 2 * GW:]
    cp = chprev[...]
    sw = scw_ref[...]
    ysc_ref[0] = bg * (delayed(cp, ch, 2) * sw[0:1] + delayed(cp, ch, 1) * sw[1:2] + ch * sw[2:3])
    chprev[...] = ch[ts - SUBLANES:]

    rowf = lax.broadcasted_iota(I32, (ts, LANES), 0)
    c = _log_sigmoid(f_ref[0] + fb_ref[...])
    d = 1
    while d < ts:
        c = c + jnp.where(rowf >= d, pltpu.roll(c, d, 0), 0.0)
        d *= 2
    c = c + cprev[...]
    cprev[...] = c[ts - 1:]
    cum_ref[0] = c
    ct = c.T[:SUBLANES]
    for j in range(ts // tk):
        cumt_ref[0, j] = ct[:, j * tk:(j + 1) * tk]


def _rgsc(rg, sc, f, cw, cb, wa, ba, wi, bi, lam, scw, fb, tk, ts=512):
    B, S, _ = rg.shape
    GW = GROUP_WIDTH
    per = ts // tk

    def full(a):
        return pl.BlockSpec(a.shape, lambda b, t: (0,) * a.ndim)

    def seq(w):
        return pl.BlockSpec((1, ts, w), lambda b, t: (b, t, 0))

    params = (cw, cb, wa, ba, wi, bi, lam, scw, fb)
    return pl.pallas_call(
        functools.partial(_rgsc_kernel, ts=ts, tk=tk),
        grid=(B, S // ts),
        in_specs=[seq(2 * GW), seq(3 * GW), seq(LANES)] + [full(p) for p in params],
        out_specs=[seq(GW), seq(GW), seq(LANES),
                   pl.BlockSpec((1, per, SUBLANES, tk), lambda b, t: (b, t, 0, 0))],
        out_shape=[jax.ShapeDtypeStruct((B, S, GW), F32), jax.ShapeDtypeStruct((B, S, GW), F32),
                   jax.ShapeDtypeStruct((B, S, LANES), F32),
                   jax.ShapeDtypeStruct((B, S // tk, SUBLANES, tk), F32)],
        scratch_shapes=[pltpu.VMEM((SUBLANES, GW), F32), pltpu.VMEM((SUBLANES, GW), F32),
                        pltpu.VMEM((1, GW), F32), pltpu.VMEM((1, LANES), F32)],
        compiler_params=_params(("parallel", "arbitrary")),
        name="rgsc",
    )(rg, sc, f, *params)


def _fox_kernel(q_ref, k_ref, v_ref, cc_ref, cr_ref, o_ref, *, tq):
    qi = pl.program_id(1)
    scale = HEAD_DIM ** -0.5
    rowi = lax.broadcasted_iota(I32, (tq, tq), 0)
    coli = lax.broadcasted_iota(I32, (tq, tq), 1)
    causal = coli <= rowi
    def head_step(h, ki, carry, diag):
        m, l, acc = carry
        off = pl.multiple_of(ki * tq, tq)
        k = k_ref[0, h, pl.ds(off, tq), :]
        v = v_ref[0, h, pl.ds(off, tq), :]
        s = (_nt_dot(q_ref[0, h], k) * scale + cc_ref[0, :, h:h + 1]
             - cr_ref[0, ki, h:h + 1, :])
        if diag:
            s = jnp.where(causal, s, -jnp.inf)
        m_new = jnp.maximum(m, jnp.max(s, axis=-1, keepdims=True))
        alpha = jnp.exp(m - m_new)
        p = jnp.exp(s - m_new)
        l = alpha * l + jnp.sum(p, axis=-1, keepdims=True)
        acc = alpha * acc + jnp.dot(p.astype(BF16), v, preferred_element_type=F32)
        return m_new, l, acc

    def step(ki, carries, diag):
        return tuple(head_step(h, ki, carries[h], diag) for h in range(GROUP_HEADS))

    init = (jnp.full((tq, 1), -jnp.inf, F32), jnp.zeros((tq, 1), F32),
            jnp.zeros((tq, HEAD_DIM), F32))
    carries = lax.fori_loop(0, qi, functools.partial(step, diag=False), (init,) * GROUP_HEADS)
    carries = step(qi, carries, True)
    o_ref[0] = jnp.concatenate([acc / l for _, l, acc in carries], axis=-1)


def _qkv_specs(S, tq):
    H, d = GROUP_HEADS, HEAD_DIM
    return [pl.BlockSpec((1, H, tq, d), lambda b, i: (b, 0, i, 0)),
            pl.BlockSpec((1, H, S, d), lambda b, i: (b, 1, 0, 0)),
            pl.BlockSpec((1, H, S, d), lambda b, i: (b, 2, 0, 0))]


def _fox(qkv, cum, cumt, tq=256):
    B, _, S, d = qkv.shape
    H = GROUP_HEADS
    return pl.pallas_call(
        functools.partial(_fox_kernel, tq=tq),
        grid=(B, S // tq),
        in_specs=_qkv_specs(S, tq)
        + [pl.BlockSpec((1, tq, LANES), lambda b, i: (b, i, 0)),
           pl.BlockSpec((1, S // tq, SUBLANES, tq), lambda b, i: (b, 0, 0, 0))],
        out_specs=pl.BlockSpec((1, tq, H * d), lambda b, i: (b, i, 0)),
        out_shape=jax.ShapeDtypeStruct((B, S, H * d), F32),
        compiler_params=_params(("parallel", "arbitrary")),
        name="fox_attn",
    )(qkv, qkv, qkv, cum, cumt)


EXP_UNDERFLOW = -104.0


def _sb_kernel(q_ref, k_ref, v_ref, o_ref, *, tq):
    qi = pl.program_id(1)
    scale = HEAD_DIM ** -0.5
    rowi = lax.broadcasted_iota(I32, (tq, tq), 0)
    coli = lax.broadcasted_iota(I32, (tq, tq), 1)
    strict = coli < rowi
    later = jnp.where(rowi > coli, 1.0, 0.0).astype(BF16)
    def head_step(h, ki, carry, diag):
        rest, acc = carry
        off = pl.multiple_of(ki * tq, tq)
        k = k_ref[0, h, pl.ds(off, tq), :]
        v = v_ref[0, h, pl.ds(off, tq), :]
        z = _nt_dot(q_ref[0, h], k) * scale
        ls = _log_sigmoid(z)
        l1m = ls - z
        if diag:
            l1m = jnp.where(strict, l1m, 0.0)
        hi = l1m.astype(BF16)
        lo = (l1m - hi.astype(F32)).astype(BF16)
        tail = (jnp.dot(hi, later, preferred_element_type=F32)
                + jnp.dot(lo, later, preferred_element_type=F32) + rest)
        w = jnp.exp(ls + tail)
        if diag:
            w = jnp.where(strict, w, 0.0)
        acc = acc + jnp.dot(w.astype(BF16), v, preferred_element_type=F32)
        rest = rest + jnp.sum(l1m, axis=-1, keepdims=True)
        return rest, acc

    def step(ki, carries, diag):
        return tuple(head_step(h, ki, carries[h], diag) for h in range(GROUP_HEADS))

    init = (jnp.zeros((tq, 1), F32), jnp.zeros((tq, HEAD_DIM), F32))
    carries = step(qi, (init,) * GROUP_HEADS, True)

    def more(c):
        j, carries = c
        top = carries[0][0]
        for rest, _ in carries[1:]:
            top = jnp.maximum(top, rest)
        return jnp.logical_and(j < qi, jnp.max(top) > EXP_UNDERFLOW)

    def further(c):
        j, carries = c
        return j + 1, step(qi - 1 - j, carries, False)

    _, carries = lax.while_loop(more, further, (jnp.int32(0), carries))
    o_ref[0] = jnp.concatenate([acc for _, acc in carries], axis=-1)


def _sb(qkv, tq=256):
    B, _, S, d = qkv.shape
    H = GROUP_HEADS
    return pl.pallas_call(
        functools.partial(_sb_kernel, tq=tq),
        grid=(B, S // tq),
        in_specs=_qkv_specs(S, tq),
        out_specs=pl.BlockSpec((1, tq, H * d), lambda b, i: (b, i, 0)),
        out_shape=jax.ShapeDtypeStruct((B, S, H * d), F32),
        compiler_params=_params(("parallel", "arbitrary")),
        name="sb_attn",
    )(qkv, qkv, qkv)


def _mixout_kernel(y0_ref, y1_ref, y2_ref, y3_ref, g_ref, w_ref, x_ref, lg_ref, lb_ref, o_ref):
    def rms(y):
        return y * lax.rsqrt(jnp.mean(y * y, axis=-1, keepdims=True) + 1e-6)

    y = jnp.concatenate([rms(r[...]) for r in (y0_ref, y1_ref, y2_ref, y3_ref)], axis=-1)
    y = (y * g_ref[...]).astype(BF16)
    mix = jnp.dot(y, w_ref[...], preferred_element_type=F32)
    o_ref[...] = _layer_norm(ALPHA * x_ref[...] + mix, lg_ref[...], lb_ref[...])


def _mixout(ys, g, w, x2d, lg, lb, tm=512):
    T, D = x2d.shape
    GW = GROUP_WIDTH

    def full(a):
        return pl.BlockSpec(a.shape, lambda i: (0,) * a.ndim)

    return pl.pallas_call(
        _mixout_kernel,
        grid=(T // tm,),
        in_specs=[pl.BlockSpec((tm, GW), lambda i: (i, 0))] * 4
        + [full(g), full(w), pl.BlockSpec((tm, D), lambda i: (i, 0)), full(lg), full(lb)],
        out_specs=pl.BlockSpec((tm, D), lambda i: (i, 0)),
        out_shape=jax.ShapeDtypeStruct((T, D), F32),
        compiler_params=_params(("parallel",)),
        name="mixout",
    )(*ys, g, w, x2d, lg, lb)


def _kv_kernel(m_ref, w_ref, k_ref, v_ref):
    kv = jnp.dot(m_ref[...].astype(BF16), w_ref[...], preferred_element_type=F32)
    k_ref[...] = kv[:, :D_MODEL].astype(BF16)
    v_ref[...] = kv[:, D_MODEL:].astype(BF16)


def _kv(mem2d, wkv, tm=512):
    M, D = mem2d.shape
    return pl.pallas_call(
        _kv_kernel,
        grid=(M // tm,),
        in_specs=[pl.BlockSpec((tm, D), lambda i: (i, 0)),
                  pl.BlockSpec((D, 2 * D), lambda i: (0, 0))],
        out_specs=[pl.BlockSpec((tm, D), lambda i: (i, 0))] * 2,
        out_shape=[jax.ShapeDtypeStruct((M, D), BF16)] * 2,
        compiler_params=_params(("parallel",)),
        name="mem_kv",
    )(mem2d, wkv)


PEER_TB = 32


def _chunk_major_spec(tm):
    return pl.BlockSpec((tm // PEER_TB, ROW_CHUNKS * PEER_TB, LANES), lambda i: (i, 0, 0))


def _chunk_major_shape(T):
    return jax.ShapeDtypeStruct((T // PEER_TB, ROW_CHUNKS * PEER_TB, LANES), F32)


def _to_chunk_major(y, ref):
    for g in range(y.shape[0] // PEER_TB):
        for c in range(ROW_CHUNKS):
            ref[g, c * PEER_TB:(c + 1) * PEER_TB, :] = (
                y[g * PEER_TB:(g + 1) * PEER_TB, c * LANES:(c + 1) * LANES])


def _xattn_kernel(x_ref, wq_ref, k_ref, v_ref, wo_ref, lg_ref, lb_ref, o_ref, oc_ref):
    x = x_ref[...]
    q = jnp.dot(x.astype(BF16), wq_ref[...], preferred_element_type=F32).astype(BF16)
    scale = XA_HEAD_DIM ** -0.5
    outs = []
    for h in range(XA_HEADS):
        sl = slice(h * XA_HEAD_DIM, (h + 1) * XA_HEAD_DIM)
        s = _nt_dot(q[:, sl], k_ref[0, :, sl]) * scale
        p = jnp.exp(s - jnp.max(s, axis=-1, keepdims=True))
        p = p / jnp.sum(p, axis=-1, keepdims=True)
        outs.append(jnp.dot(p.astype(BF16), v_ref[0, :, sl], preferred_element_type=F32))
    o = jnp.concatenate(outs, axis=-1).astype(BF16)
    xa = jnp.dot(o, wo_ref[...], preferred_element_type=F32)
    y = _layer_norm(ALPHA * x + xa, lg_ref[...], lb_ref[...])
    o_ref[...] = y
    _to_chunk_major(y, oc_ref)


def _xattn(x2d, wq, k, v, wo, lg, lb, seq_len, tm=512):
    T, D = x2d.shape
    M = k.shape[1]
    per_seq = seq_len // tm

    def full(a):
        return pl.BlockSpec(a.shape, lambda i: (0,) * a.ndim)

    return pl.pallas_call(
        _xattn_kernel,
        grid=(T // tm,),
        in_specs=[pl.BlockSpec((tm, D), lambda i: (i, 0)), full(wq),
                  pl.BlockSpec((1, M, D), lambda i: (i // per_seq, 0, 0)),
                  pl.BlockSpec((1, M, D), lambda i: (i // per_seq, 0, 0)),
                  full(wo), full(lg), full(lb)],
        out_specs=[pl.BlockSpec((tm, D), lambda i: (i, 0)), _chunk_major_spec(tm)],
        out_shape=[jax.ShapeDtypeStruct((T, D), F32), _chunk_major_shape(T)],
        compiler_params=_params(("parallel",)),
        name="xattn",
    )(x2d, wq, k, v, wo, lg, lb)


E_PER_TOK = PEER_HEADS * PEER_TOPK


def _staircase():
    return [(a, b) for a in range(PEER_TOPK) for b in range(PEER_TOPK // (a + 1))]


def _route_kernel(x_ref, wq_ref, k1_ref, k2_ref, e_ref, g_ref, v_scr, i_scr, *, ts):
    xb = x_ref[...].astype(BF16)
    tsub = ts // 2
    key_id = lax.broadcasted_iota(I32, (N_KEYS, tsub), 0).astype(F32)
    qd = 2 * PEER_HALF
    for h in range(PEER_HEADS):
        qry = jnp.dot(xb, wq_ref[:, h * qd:(h + 1) * qd], preferred_element_type=F32)
        for half, kref in ((0, k1_ref), (1, k2_ref)):
            qh = qry[:, half * PEER_HALF:(half + 1) * PEER_HALF].astype(BF16)
            s0 = _nt_dot(kref[...], qh)

            def pick(it, ss, h=h, half=half):
                out = []
                for sub, s in enumerate(ss):
                    cols = slice(sub * tsub, (sub + 1) * tsub)
                    m = jnp.max(s, axis=0, keepdims=True)
                    idx = jnp.min(jnp.where(s == m, key_id, float(N_KEYS)), axis=0,
                                  keepdims=True)
                    v_scr[half, it, h:h + 1, cols] = m
                    i_scr[half, it, h:h + 1, cols] = idx.astype(I32)
                    out.append(jnp.where(key_id == idx, -jnp.inf, s))
                return tuple(out)

            lax.fori_loop(0, PEER_TOPK, pick, (s0[:, :tsub], s0[:, tsub:]))

    cells = _staircase()
    cand = tuple(v_scr[0, a] + v_scr[1, b] for a, b in cells)
    expert = [i_scr[0, a] * N_KEYS + i_scr[1, b] for a, b in cells]

    def select(it, carry):
        cand, top = carry
        best_v, best_e, best_c = cand[0], expert[0], jnp.zeros((PEER_HEADS, ts), I32)
        for c in range(1, len(cells)):
            better = cand[c] > best_v
            best_v = jnp.where(better, cand[c], best_v)
            best_e = jnp.where(better, expert[c], best_e)
            best_c = jnp.where(better, c, best_c)
        top = jnp.where(it == 0, best_v, top)
        i_scr[0, it] = best_e * WORD_ROWS
        v_scr[0, it] = best_v
        cand = tuple(jnp.where(best_c == c, -jnp.inf, cand[c]) for c in range(len(cells)))
        return cand, top

    _, top = lax.fori_loop(0, PEER_TOPK, select, (cand, jnp.zeros((PEER_HEADS, ts), F32)))
    ex = [jnp.exp(v_scr[0, it] - top) for it in range(PEER_TOPK)]
    den = ex[0]
    for it in range(1, PEER_TOPK):
        den = den + ex[it]
    inv = 1.0 / den
    g_ref[...] = jnp.concatenate([e * inv for e in ex], axis=0).T
    rows = pltpu.bitcast(i_scr[0].reshape(E_PER_TOK, ts), F32)
    e_ref[...] = pltpu.bitcast(rows.T, I32)


def _route(x2d, wq, k1, k2, ts=256):
    T, D = x2d.shape
    nb = T // ts

    def full(a):
        return pl.BlockSpec(a.shape, lambda i: (0,) * a.ndim)

    blk = pl.BlockSpec((ts, E_PER_TOK), lambda i: (i, 0))
    return pl.pallas_call(
        functools.partial(_route_kernel, ts=ts),
        grid=(nb,),
        in_specs=[pl.BlockSpec((ts, D), lambda i: (i, 0)), full(wq), full(k1), full(k2)],
        out_specs=[blk, blk],
        out_shape=[jax.ShapeDtypeStruct((T, E_PER_TOK), I32),
                   jax.ShapeDtypeStruct((T, E_PER_TOK), F32)],
        scratch_shapes=[pltpu.VMEM((2, PEER_TOPK, PEER_HEADS, ts), F32),
                        pltpu.VMEM((2, PEER_TOPK, PEER_HEADS, ts), I32)],
        compiler_params=_params(("parallel",)),
        name="peer_route",
    )(x2d, wq, k1, k2)


TOK_GROUP = SUBLANES


N_GROUPS = PEER_TB // TOK_GROUP


def _gather_rows(ids_ref, tab_ref, rows_scr, tok0):
    for tt in range(TOK_GROUP):
        for j in range(E_PER_TOK):
            r0 = pl.multiple_of(ids_ref[tok0 + tt, j], WORD_ROWS)
            rows_scr[tt, j * WORD_ROWS:(j + 1) * WORD_ROWS, :] = tab_ref[pl.ds(r0, WORD_ROWS), :]


def _token_chunks(tt):
    return pl.ds(tt, ROW_CHUNKS, stride=PEER_TB)


def _chunk_of_col():
    col = lax.broadcasted_iota(I32, (ROW_CHUNKS, E_PER_TOK * ROW_CHUNKS), 1)
    row = lax.broadcasted_iota(I32, (ROW_CHUNKS, E_PER_TOK * ROW_CHUNKS), 0)
    return (col & (ROW_CHUNKS - 1)) == row


def _peer_score_kernel(ids_ref, x_ref, tab_ref, s_ref, *scratch):
    rows_bufs, part_bufs = scratch[:N_GROUPS], scratch[N_GROUPS:]
    own_chunk = _chunk_of_col()
    c_id = lax.broadcasted_iota(I32, (E_PER_TOK * ROW_CHUNKS, E_PER_TOK), 0)
    e_id = lax.broadcasted_iota(I32, (E_PER_TOK * ROW_CHUNKS, E_PER_TOK), 1)
    fold = jnp.where(c_id >> CHUNK_SHIFT == e_id, 1.0, 0.0).astype(BF16)

    for g in range(N_GROUPS):
        tok0 = g * TOK_GROUP
        rows_scr, part_scr = rows_bufs[g], part_bufs[g]
        _gather_rows(ids_ref, tab_ref, rows_scr, tok0)
        for tt in range(TOK_GROUP):
            rows = pltpu.bitcast(rows_scr[tt], BF16)
            xt = x_ref[0, _token_chunks(tok0 + tt), :].astype(BF16)
            full = _nt_dot(xt, rows)
            part_scr[tt * ROW_CHUNKS:(tt + 1) * ROW_CHUNKS, :] = jnp.where(own_chunk, full, 0.0)
        part = part_scr[...]
        hi = part.astype(BF16)
        lo = (part - hi.astype(F32)).astype(BF16)
        sc = (jnp.dot(hi, fold, preferred_element_type=F32)
              + jnp.dot(lo, fold, preferred_element_type=F32))
        sc = jnp.sum(sc.reshape(TOK_GROUP, ROW_CHUNKS, E_PER_TOK), axis=1)
        s_ref[tok0:tok0 + TOK_GROUP, :] = sc


def _peer_mix_kernel(ids_ref, s_ref, gate_ref, tab_ref, o_ref, *rows_bufs):
    own_chunk = _chunk_of_col()
    e_id = lax.broadcasted_iota(I32, (E_PER_TOK, E_PER_TOK * ROW_CHUNKS), 0)
    c_id = lax.broadcasted_iota(I32, (E_PER_TOK, E_PER_TOK * ROW_CHUNKS), 1)
    spread = jnp.where(c_id >> CHUNK_SHIFT == e_id, 1.0, 0.0).astype(BF16)

    for g in range(N_GROUPS):
        tok0 = g * TOK_GROUP
        sl = slice(tok0, tok0 + TOK_GROUP)
        rows_scr = rows_bufs[g]
        _gather_rows(ids_ref, tab_ref, rows_scr, tok0)
        s = s_ref[sl, :]
        act = 0.5 * s * (1.0 + lax.erf(s * (2.0 ** -0.5)))
        coef = (gate_ref[sl, :] * act).astype(BF16)
        coef = jnp.dot(coef, spread, preferred_element_type=F32)
        for tt in range(TOK_GROUP):
            rows = pltpu.bitcast(rows_scr[tt], BF16)
            ct = jnp.broadcast_to(coef[tt:tt + 1], (ROW_CHUNKS, E_PER_TOK * ROW_CHUNKS))
            ct = jnp.where(own_chunk, ct, 0.0).astype(BF16)
            o_ref[0, _token_chunks(tok0 + tt), :] = jnp.dot(ct, rows, preferred_element_type=F32)


def _peer_experts(ids, gate, xc, tab_u, tab_v):
    T = ids.shape[0]
    tb = PEER_TB
    nb = T // tb
    ids_spec = pl.BlockSpec((tb, E_PER_TOK), lambda i: (i, 0), memory_space=pltpu.SMEM)
    tab_spec = pl.BlockSpec(tab_u.shape, lambda i: (0, 0), pipeline_mode=pl.Buffered(1))
    tok_spec = pl.BlockSpec((tb, E_PER_TOK), lambda i: (i, 0))
    row_spec = _chunk_major_spec(tb)
    rows_scr = [pltpu.VMEM((TOK_GROUP, E_PER_TOK * WORD_ROWS, LANES), I32)] * N_GROUPS
    part_scr = [pltpu.VMEM((TOK_GROUP * ROW_CHUNKS, E_PER_TOK * ROW_CHUNKS), F32)] * N_GROUPS
    s = pl.pallas_call(
        _peer_score_kernel,
        grid=(nb,),
        in_specs=[ids_spec, row_spec, tab_spec],
        out_specs=tok_spec,
        out_shape=jax.ShapeDtypeStruct((T, E_PER_TOK), F32),
        scratch_shapes=rows_scr + part_scr,
        compiler_params=_params(("arbitrary",)),
        name="peer_score",
    )(ids, xc, tab_u)
    return pl.pallas_call(
        _peer_mix_kernel,
        grid=(nb,),
        in_specs=[ids_spec, tok_spec, tok_spec, tab_spec],
        out_specs=row_spec,
        out_shape=_chunk_major_shape(T),
        scratch_shapes=rows_scr,
        compiler_params=_params(("arbitrary",)),
        name="peer_mix",
    )(ids, s, gate, tab_v)


def _resln_kernel(x_ref, y_ref, lg_ref, lb_ref, o_ref):
    G = x_ref.shape[0]
    z = ALPHA * x_ref[...] + y_ref[...]
    zs = [z[:, c * PEER_TB:(c + 1) * PEER_TB, :] for c in range(ROW_CHUNKS)]
    tot = zs[0]
    for c in range(1, ROW_CHUNKS):
        tot = tot + zs[c]
    mu = jnp.sum(tot, axis=-1, keepdims=True) * (1.0 / D_MODEL)
    zc = [a - mu for a in zs]
    sq = zc[0] * zc[0]
    for c in range(1, ROW_CHUNKS):
        sq = sq + zc[c] * zc[c]
    rstd = lax.rsqrt(jnp.sum(sq, axis=-1, keepdims=True) * (1.0 / D_MODEL) + LN_EPS)
    for c in range(ROW_CHUNKS):
        cols = slice(c * LANES, (c + 1) * LANES)
        y = (zc[c] * rstd).reshape(G * PEER_TB, LANES)
        o_ref[:, cols] = y * lg_ref[:, cols] + lb_ref[:, cols]


def _resln(xc, yc, lg, lb, tm=512):
    T = xc.shape[0] * PEER_TB
    blk = _chunk_major_spec(tm)
    par = pl.BlockSpec((1, D_MODEL), lambda i: (0, 0))
    return pl.pallas_call(
        _resln_kernel,
        grid=(T // tm,),
        in_specs=[blk, blk, par, par],
        out_specs=pl.BlockSpec((tm, D_MODEL), lambda i: (i, 0)),
        out_shape=jax.ShapeDtypeStruct((T, D_MODEL), F32),
        compiler_params=_params(("parallel",)),
        name="res_ln",
    )(xc, yc, lg, lb)


def _pack_kernel(t_ref, o_ref):
    def bf16_bits(v):
        return pltpu.bitcast(v.astype(BF16).astype(F32), jnp.uint32)

    te = t_ref.shape[0]
    for r in range(WORD_ROWS):
        lo = bf16_bits(t_ref[:, (2 * r) * LANES:(2 * r + 1) * LANES])
        hi = bf16_bits(t_ref[:, (2 * r + 1) * LANES:(2 * r + 2) * LANES])
        words = (hi & jnp.uint32(0xFFFF0000)) | (lo >> 16)
        o_ref[pl.ds(r, te, stride=WORD_ROWS), :] = pltpu.bitcast(words, I32)


def _pack_table(t, te=512):
    E, D = t.shape
    return pl.pallas_call(
        _pack_kernel,
        grid=(E // te,),
        in_specs=[pl.BlockSpec((te, D), lambda i: (i, 0))],
        out_specs=pl.BlockSpec((te * WORD_ROWS, LANES), lambda i: (i, 0)),
        out_shape=jax.ShapeDtypeStruct((E * WORD_ROWS, LANES), I32),
        compiler_params=_params(("parallel",)),
        name="pack_table",
    )(t)


def _block_diag(w):
    H, d, _ = w.shape
    out = jnp.zeros((H * d, H * d), w.dtype)
    for h in range(H):
        out = out.at[h * d:(h + 1) * d, h * d:(h + 1) * d].set(w[h])
    return out


def _cat_w_in(w):
    GW = GROUP_WIDTH
    off_f = 5 * GW
    off_sb = off_f + GROUP_HEADS
    off_sc = off_sb + 3 * GW
    pad = jnp.zeros((w.shape[0], LANES - GROUP_HEADS), w.dtype)
    return jnp.concatenate([w[:, :off_f], w[:, off_sb:off_sc], w[:, off_sc:N_IN],
                            w[:, off_f:off_sb], pad], axis=1).astype(BF16)


def _heads(t, B, S):
    t = t.reshape(B, S, 3, GROUP_HEADS, HEAD_DIM).transpose(2, 0, 3, 1, 4)
    return t[0], t[1], t[2]


def _row(v, width=None):
    v = v.reshape(1, -1).astype(F32)
    if width is not None and v.shape[1] < width:
        v = jnp.pad(v, ((0, 0), (0, width - v.shape[1])))
    return v


def _layer(x2d, kmem, vmem_, B, S, p, tq=256):
    T = B * S
    rg, fox, sb, sc, f = _inproj(x2d, _cat_w_in(p["w_in"]), B, S)
    yrg, ysc, cum, cumt = _rgsc(
        rg.reshape(B, S, -1), sc.reshape(B, S, -1), f.reshape(B, S, -1),
        p["rg_conv_w"], _row(p["rg_conv_b"]), _block_diag(p["rg_wa"]).astype(BF16), _row(p["rg_ba"]),
        _block_diag(p["rg_wi"]).astype(BF16), _row(p["rg_bi"]), _row(p["rg_lambda"]),
        p["sc_conv_w"], _row(p["fox_bf"], LANES), tq)
    yfox = _fox(fox, cum, cumt, tq=tq)
    ysb = _sb(sb, tq=tq)
    ys = (yrg.reshape(T, -1), yfox.reshape(T, -1), ysb.reshape(T, -1), ysc.reshape(T, -1))
    x1 = _mixout(ys, _row(p["mix_norm_g"]), p["w_out"].astype(BF16), x2d,
                 _row(p["ln1_g"]), _row(p["ln1_b"]))
    x2, x2c = _xattn(x1, p["xa_wq"].astype(BF16), kmem, vmem_, p["xa_wo"].astype(BF16),
                     _row(p["ln2_g"]), _row(p["ln2_b"]), S)
    ids, gate = _route(x2, p["peer_wq"].astype(BF16), p["peer_k1"].astype(BF16),
                       p["peer_k2"].astype(BF16))
    ffc = _peer_experts(ids, gate, x2c, _pack_table(p["peer_u"]), _pack_table(p["peer_v"]))
    return _resln(x2c, ffc, _row(p["ln3_g"]), _row(p["ln3_b"]))


_LAYER_PARAMS = ("w_in", "w_out", "rg_conv_w", "rg_conv_b", "rg_wa", "rg_ba", "rg_wi", "rg_bi",
                 "rg_lambda", "fox_bf", "sc_conv_w", "mix_norm_g", "ln1_g", "ln1_b", "xa_wq",
                 "xa_wkv", "xa_wo", "ln2_g", "ln2_b", "peer_wq", "peer_k1", "peer_k2", "peer_u",
                 "peer_v", "ln3_g", "ln3_b")


def kernel(x, mem, w_in, w_out, rg_conv_w, rg_conv_b, rg_wa, rg_ba, rg_wi, rg_bi, rg_lambda, fox_bf, sc_conv_w, mix_norm_g, ln1_g, ln1_b, xa_wq, xa_wkv, xa_wo, ln2_g, ln2_b, peer_wq, peer_k1, peer_k2, peer_u, peer_v, ln3_g, ln3_b):
    stacked = dict(zip(_LAYER_PARAMS, (
        w_in, w_out, rg_conv_w, rg_conv_b, rg_wa, rg_ba, rg_wi, rg_bi, rg_lambda, fox_bf,
        sc_conv_w, mix_norm_g, ln1_g, ln1_b, xa_wq, xa_wkv, xa_wo, ln2_g, ln2_b, peer_wq,
        peer_k1, peer_k2, peer_u, peer_v, ln3_g, ln3_b)))
    B, S, D = x.shape
    M = mem.shape[1]
    x2d = x.reshape(B * S, D)
    mem2d = mem.reshape(B * M, D)
    for l in range(w_in.shape[0]):
        p = {k: v[l] for k, v in stacked.items()}
        kmem, vmem_ = _kv(mem2d, p["xa_wkv"].astype(BF16))
        x2d = _layer(x2d, kmem.reshape(B, M, D), vmem_.reshape(B, M, D), B, S, p)
    return x2d.reshape(B, S, D)
```

```python
import functools
import math

import jax
import jax.numpy as jnp
from jax import lax
from jax.experimental import pallas as pl
from jax.experimental.pallas import tpu as pltpu

F32 = jnp.float32
BF16 = jnp.bfloat16
I32 = jnp.int32

D_MODEL = 1024
GROUP_WIDTH = 256
GROUP_HEADS = 4
HEAD_DIM = 64
N_IN = 2820
RGLRU_C = 8.0
XA_HEADS = 4
XA_HEAD_DIM = D_MODEL // XA_HEADS
PEER_HEADS = 8
N_KEYS = 128
PEER_HALF = 128
PEER_TOPK = 16
DEPTH = 2
ALPHA = (2.0 * DEPTH) ** 0.25
LN_EPS = 1e-5

SUBLANES = 8
LANES = 128
WORD_ROWS = D_MODEL // (2 * LANES)
ROW_CHUNKS = D_MODEL // LANES
CHUNK_SHIFT = ROW_CHUNKS.bit_length() - 1

VMEM_LIMIT = 48 * 1024 * 1024


def _params(sem, vmem=VMEM_LIMIT):
    return pltpu.CompilerParams(dimension_semantics=sem, vmem_limit_bytes=vmem)


def _layer_norm(z, g, b):
    mu = jnp.mean(z, axis=-1, keepdims=True)
    zc = z - mu
    var = jnp.mean(zc * zc, axis=-1, keepdims=True)
    return zc * lax.rsqrt(var + LN_EPS) * g + b


def _log_sigmoid(z):
    return jnp.minimum(z, 0.0) - jnp.log1p(jnp.exp(-jnp.abs(z)))


def _nt_dot(a, b):
    return lax.dot_general(a, b, (((1,), (1,)), ((), ())), preferred_element_type=F32)


C_RG, C_FOX, C_SB, C_SC, C_F, C_END = 0, 512, 1280, 2048, 2816, 2944


def _inproj_kernel(x_ref, w_ref, rg_ref, fox_ref, sb_ref, sc_ref, f_ref):
    xb = x_ref[...].astype(BF16)

    def mm(lo, hi):
        return jnp.dot(xb, w_ref[:, lo:hi], preferred_element_type=F32)

    def heads(ref, lo, hi):
        qkv = mm(lo, hi)
        for j in range(3 * GROUP_HEADS):
            ref[0, j] = qkv[:, j * HEAD_DIM:(j + 1) * HEAD_DIM].astype(BF16)

    rg_ref[...] = mm(C_RG, C_FOX)
    heads(fox_ref, C_FOX, C_SB)
    heads(sb_ref, C_SB, C_SC)
    sc_ref[...] = mm(C_SC, C_F)
    f_ref[...] = mm(C_F, C_END)


def _inproj(x2d, w_cat, B, S, tm=512):
    T, D = x2d.shape
    per_seq = S // tm
    flat = ((C_FOX - C_RG, F32), (C_F - C_SC, F32), (C_END - C_F, F32))
    flat_specs = [pl.BlockSpec((tm, w), lambda i: (i, 0)) for w, _ in flat]
    flat_shapes = [jax.ShapeDtypeStruct((T, w), dt) for w, dt in flat]
    head_spec = pl.BlockSpec((1, 3 * GROUP_HEADS, tm, HEAD_DIM),
                             lambda i: (i // per_seq, 0, i % per_seq, 0))
    head_shape = jax.ShapeDtypeStruct((B, 3 * GROUP_HEADS, S, HEAD_DIM), BF16)
    return pl.pallas_call(
        _inproj_kernel,
        grid=(T // tm,),
        in_specs=[pl.BlockSpec((tm, D), lambda i: (i, 0)),
                  pl.BlockSpec((D, C_END), lambda i: (0, 0))],
        out_specs=[flat_specs[0], head_spec, head_spec, flat_specs[1], flat_specs[2]],
        out_shape=[flat_shapes[0], head_shape, head_shape, flat_shapes[1], flat_shapes[2]],
        compiler_params=_params(("parallel",)),
        name="inproj",
    )(x2d, w_cat)


def _rgsc_kernel(rg_ref, sc_ref, f_ref, cw_ref, cb_ref, wa_ref, ba_ref, wi_ref, bi_ref, lam_ref,
                 scw_ref, fb_ref, yrg_ref, ysc_ref, cum_ref, cumt_ref, xprev, chprev, hprev, cprev,
                 *, ts, tk):
    GW = GROUP_WIDTH

    @pl.when(pl.program_id(1) == 0)
    def _():
        xprev[...] = jnp.zeros_like(xprev)
        chprev[...] = jnp.zeros_like(chprev)
        hprev[...] = jnp.zeros_like(hprev)
        cprev[...] = jnp.zeros_like(cprev)

    row = lax.broadcasted_iota(I32, (ts, GW), 0)

    def delayed(prev, cur, d):
        ext = jnp.concatenate([prev, cur], axis=0)
        return pltpu.roll(ext, d, 0)[SUBLANES:]

    xr = rg_ref[0, :, :GW]
    gate = rg_ref[0, :, GW:]
    xp = xprev[...]
    cw = cw_ref[...]
    xc = (delayed(xp, xr, 3) * cw[0:1] + delayed(xp, xr, 2) * cw[1:2]
          + delayed(xp, xr, 1) * cw[2:3] + xr * cw[3:4] + cb_ref[...])
    xprev[...] = xr[ts - SUBLANES:]
    xcb = xc.astype(BF16)
    r = jax.nn.sigmoid(jnp.dot(xcb, wa_ref[...], preferred_element_type=F32) + ba_ref[...])
    ig = jax.nn.sigmoid(jnp.dot(xcb, wi_ref[...], preferred_element_type=F32) + bi_ref[...])
    z = -lam_ref[...]
    softplus = jnp.maximum(z, 0.0) + jnp.log1p(jnp.exp(-jnp.abs(z)))
    log_a = -RGLRU_C * r * softplus
    a = jnp.exp(log_a)
    u = jnp.sqrt(-jnp.tanh(log_a) * (a * a + 1.0)) * (ig * xc)
    acc_a, acc_b = a, u
    d = 1
    while d < ts:
        keep = row >= d
        a_s = jnp.where(keep, pltpu.roll(acc_a, d, 0), 1.0)
        b_s = jnp.where(keep, pltpu.roll(acc_b, d, 0), 0.0)
        acc_b = acc_a * b_s + acc_b
        acc_a = acc_a * a_s
        d *= 2
    h = acc_b + acc_a * hprev[...]
    hprev[...] = h[ts - 1:]
    c0 = math.sqrt(2.0 / math.pi)
    gelu = 0.5 * gate * (1.0 + jnp.tanh(c0 * (gate + 0.044715 * gate * gate * gate)))
    yrg_ref[0] = h * gelu

    bg = sc_ref[0, :, :GW]
    ch = sc_ref[0, :, GW:2 * GW] * sc_ref[0, :, 2 * GW:]
    cp = chprev[...]
    sw = scw_ref[...]
    ysc_ref[0] = bg * (delayed(cp, ch, 2) * sw[0:1] + delayed(cp, ch, 1) * sw[1:2] + ch * sw[2:3])
    chprev[...] = ch[ts - SUBLANES:]

    rowf = lax.broadcasted_iota(I32, (ts, LANES), 0)
    c = _log_sigmoid(f_ref[0] + fb_ref[...])
    d = 1
    while d < ts:
        c = c + jnp.where(rowf >= d, pltpu.roll(c, d, 0), 0.0)
        d *= 2
    c = c + cprev[...]
    cprev[...] = c[ts - 1:]
    cum_ref[0] = c
    ct = c.T[:SUBLANES]
    for j in range(ts // tk):
        cumt_ref[0, j] = ct[:, j * tk:(j + 1) * tk]


def _rgsc(rg, sc, f, cw, cb, wa, ba, wi, bi, lam, scw, fb, tk, ts=512):
    B, S, _ = rg.shape
    GW = GROUP_WIDTH
    per = ts // tk

    def full(a):
        return pl.BlockSpec(a.shape, lambda b, t: (0,) * a.ndim)

    def seq(w):
        return pl.BlockSpec((1, ts, w), lambda b, t: (b, t, 0))

    params = (cw, cb, wa, ba, wi, bi, lam, scw, fb)
    return pl.pallas_call(
        functools.partial(_rgsc_kernel, ts=ts, tk=tk),
        grid=(B, S // ts),
        in_specs=[seq(2 * GW), seq(3 * GW), seq(LANES)] + [full(p) for p in params],
        out_specs=[seq(GW), seq(GW), seq(LANES),
                   pl.BlockSpec((1, per, SUBLANES, tk), lambda b, t: (b, t, 0, 0))],
        out_shape=[jax.ShapeDtypeStruct((B, S, GW), F32), jax.ShapeDtypeStruct((B, S, GW), F32),
                   jax.ShapeDtypeStruct((B, S, LANES), F32),
                   jax.ShapeDtypeStruct((B, S // tk, SUBLANES, tk), F32)],
        scratch_shapes=[pltpu.VMEM((SUBLANES, GW), F32), pltpu.VMEM((SUBLANES, GW), F32),
                        pltpu.VMEM((1, GW), F32), pltpu.VMEM((1, LANES), F32)],
        compiler_params=_params(("parallel", "arbitrary")),
        name="rgsc",
    )(rg, sc, f, *params)


def _fox_kernel(q_ref, k_ref, v_ref, cc_ref, cr_ref, o_ref, *, tq):
    qi = pl.program_id(1)
    scale = HEAD_DIM ** -0.5
    rowi = lax.broadcasted_iota(I32, (tq, tq), 0)
    coli = lax.broadcasted_iota(I32, (tq, tq), 1)
    causal = coli <= rowi
    def head_step(h, ki, carry, diag):
        m, l, acc = carry
        off = pl.multiple_of(ki * tq, tq)
        k = k_ref[0, h, pl.ds(off, tq), :]
        v = v_ref[0, h, pl.ds(off, tq), :]
        s = (_nt_dot(q_ref[0, h], k) * scale + cc_ref[0, :, h:h + 1]
             - cr_ref[0, ki, h:h + 1, :])
        if diag:
            s = jnp.where(causal, s, -jnp.inf)
        m_new = jnp.maximum(m, jnp.max(s, axis=-1, keepdims=True))
        alpha = jnp.exp(m - m_new)
        p = jnp.exp(s - m_new)
        l = alpha * l + jnp.sum(p, axis=-1, keepdims=True)
        acc = alpha * acc + jnp.dot(p.astype(BF16), v, preferred_element_type=F32)
        return m_new, l, acc

    def step(ki, carries, diag):
        return tuple(head_step(h, ki, carries[h], diag) for h in range(GROUP_HEADS))

    init = (jnp.full((tq, 1), -jnp.inf, F32), jnp.zeros((tq, 1), F32),
            jnp.zeros((tq, HEAD_DIM), F32))
    carries = lax.fori_loop(0, qi, functools.partial(step, diag=False), (init,) * GROUP_HEADS)
    carries = step(qi, carries, True)
    o_ref[0] = jnp.concatenate([acc / l for _, l, acc in carries], axis=-1)


def _qkv_specs(S, tq):
    H, d = GROUP_HEADS, HEAD_DIM
    return [pl.BlockSpec((1, H, tq, d), lambda b, i: (b, 0, i, 0)),
            pl.BlockSpec((1, H, S, d), lambda b, i: (b, 1, 0, 0)),
            pl.BlockSpec((1, H, S, d), lambda b, i: (b, 2, 0, 0))]


def _fox(qkv, cum, cumt, tq=256):
    B, _, S, d = qkv.shape
    H = GROUP_HEADS
    return pl.pallas_call(
        functools.partial(_fox_kernel, tq=tq),
        grid=(B, S // tq),
        in_specs=_qkv_specs(S, tq)
        + [pl.BlockSpec((1, tq, LANES), lambda b, i: (b, i, 0)),
           pl.BlockSpec((1, S // tq, SUBLANES, tq), lambda b, i: (b, 0, 0, 0))],
        out_specs=pl.BlockSpec((1, tq, H * d), lambda b, i: (b, i, 0)),
        out_shape=jax.ShapeDtypeStruct((B, S, H * d), F32),
        compiler_params=_params(("parallel", "arbitrary")),
        name="fox_attn",
    )(qkv, qkv, qkv, cum, cumt)


EXP_UNDERFLOW = -104.0


def _sb_kernel(q_ref, k_ref, v_ref, o_ref, *, tq):
    qi = pl.program_id(1)
    scale = HEAD_DIM ** -0.5
    rowi = lax.broadcasted_iota(I32, (tq, tq), 0)
    coli = lax.broadcasted_iota(I32, (tq, tq), 1)
    strict = coli < rowi
    later = jnp.where(rowi > coli, 1.0, 0.0).astype(BF16)
    def head_step(h, ki, carry, diag):
        rest, acc = carry
        off = pl.multiple_of(ki * tq, tq)
        k = k_ref[0, h, pl.ds(off, tq), :]
        v = v_ref[0, h, pl.ds(off, tq), :]
        z = _nt_dot(q_ref[0, h], k) * scale
        ls = _log_sigmoid(z)
        l1m = ls - z
        if diag:
            l1m = jnp.where(strict, l1m, 0.0)
        hi = l1m.astype(BF16)
        lo = (l1m - hi.astype(F32)).astype(BF16)
        tail = (jnp.dot(hi, later, preferred_element_type=F32)
                + jnp.dot(lo, later, preferred_element_type=F32) + rest)
        w = jnp.exp(ls + tail)
        if diag:
            w = jnp.where(strict, w, 0.0)
        acc = acc + jnp.dot(w.astype(BF16), v, preferred_element_type=F32)
        rest = rest + jnp.sum(l1m, axis=-1, keepdims=True)
        return rest, acc

    def step(ki, carries, diag):
        return tuple(head_step(h, ki, carries[h], diag) for h in range(GROUP_HEADS))

    init = (jnp.zeros((tq, 1), F32), jnp.zeros((tq, HEAD_DIM), F32))
    carries = step(qi, (init,) * GROUP_HEADS, True)

    def more(c):
        j, carries = c
        top = carries[0][0]
        for rest, _ in carries[1:]:
            top = jnp.maximum(top, rest)
        return jnp.logical_and(j < qi, jnp.max(top) > EXP_UNDERFLOW)

    def further(c):
        j, carries = c
        return j + 1, step(qi - 1 - j, carries, False)

    _, carries = lax.while_loop(more, further, (jnp.int32(0), carries))
    o_ref[0] = jnp.concatenate([acc for _, acc in carries], axis=-1)


def _sb(qkv, tq=256):
    B, _, S, d = qkv.shape
    H = GROUP_HEADS
    return pl.pallas_call(
        functools.partial(_sb_kernel, tq=tq),
        grid=(B, S // tq),
        in_specs=_qkv_specs(S, tq),
        out_specs=pl.BlockSpec((1, tq, H * d), lambda b, i: (b, i, 0)),
        out_shape=jax.ShapeDtypeStruct((B, S, H * d), F32),
        compiler_params=_params(("parallel", "arbitrary")),
        name="sb_attn",
    )(qkv, qkv, qkv)


def _mixout_kernel(y0_ref, y1_ref, y2_ref, y3_ref, g_ref, w_ref, x_ref, lg_ref, lb_ref, o_ref):
    def rms(y):
        return y * lax.rsqrt(jnp.mean(y * y, axis=-1, keepdims=True) + 1e-6)

    y = jnp.concatenate([rms(r[...]) for r in (y0_ref, y1_ref, y2_ref, y3_ref)], axis=-1)
    y = (y * g_ref[...]).astype(BF16)
    mix = jnp.dot(y, w_ref[...], preferred_element_type=F32)
    o_ref[...] = _layer_norm(ALPHA * x_ref[...] + mix, lg_ref[...], lb_ref[...])


def _mixout(ys, g, w, x2d, lg, lb, tm=512):
    T, D = x2d.shape
    GW = GROUP_WIDTH

    def full(a):
        return pl.BlockSpec(a.shape, lambda i: (0,) * a.ndim)

    return pl.pallas_call(
        _mixout_kernel,
        grid=(T // tm,),
        in_specs=[pl.BlockSpec((tm, GW), lambda i: (i, 0))] * 4
        + [full(g), full(w), pl.BlockSpec((tm, D), lambda i: (i, 0)), full(lg), full(lb)],
        out_specs=pl.BlockSpec((tm, D), lambda i: (i, 0)),
        out_shape=jax.ShapeDtypeStruct((T, D), F32),
        compiler_params=_params(("parallel",)),
        name="mixout",
    )(*ys, g, w, x2d, lg, lb)


def _kv_kernel(m_ref, w_ref, k_ref, v_ref):
    kv = jnp.dot(m_ref[...].astype(BF16), w_ref[...], preferred_element_type=F32)
    k_ref[...] = kv[:, :D_MODEL].astype(BF16)
    v_ref[...] = kv[:, D_MODEL:].astype(BF16)


def _kv(mem2d, wkv, tm=512):
    M, D = mem2d.shape
    return pl.pallas_call(
        _kv_kernel,
        grid=(M // tm,),
        in_specs=[pl.BlockSpec((tm, D), lambda i: (i, 0)),
                  pl.BlockSpec((D, 2 * D), lambda i: (0, 0))],
        out_specs=[pl.BlockSpec((tm, D), lambda i: (i, 0))] * 2,
        out_shape=[jax.ShapeDtypeStruct((M, D), BF16)] * 2,
        compiler_params=_params(("parallel",)),
        name="mem_kv",
    )(mem2d, wkv)


PEER_TB = 32


def _chunk_major_spec(tm):
    return pl.BlockSpec((tm // PEER_TB, ROW_CHUNKS * PEER_TB, LANES), lambda i: (i, 0, 0))


def _chunk_major_shape(T):
    return jax.ShapeDtypeStruct((T // PEER_TB, ROW_CHUNKS * PEER_TB, LANES), F32)


def _to_chunk_major(y, ref):
    for g in range(y.shape[0] // PEER_TB):
        for c in range(ROW_CHUNKS):
            ref[g, c * PEER_TB:(c + 1) * PEER_TB, :] = (
                y[g * PEER_TB:(g + 1) * PEER_TB, c * LANES:(c + 1) * LANES])


def _xattn_kernel(x_ref, wq_ref, k_ref, v_ref, wo_ref, lg_ref, lb_ref, o_ref, oc_ref):
    x = x_ref[...]
    q = jnp.dot(x.astype(BF16), wq_ref[...], preferred_element_type=F32).astype(BF16)
    scale = XA_HEAD_DIM ** -0.5
    outs = []
    for h in range(XA_HEADS):
        sl = slice(h * XA_HEAD_DIM, (h + 1) * XA_HEAD_DIM)
        s = _nt_dot(q[:, sl], k_ref[0, :, sl]) * scale
        p = jnp.exp(s - jnp.max(s, axis=-1, keepdims=True))
        p = p / jnp.sum(p, axis=-1, keepdims=True)
        outs.append(jnp.dot(p.astype(BF16), v_ref[0, :, sl], preferred_element_type=F32))
    o = jnp.concatenate(outs, axis=-1).astype(BF16)
    xa = jnp.dot(o, wo_ref[...], preferred_element_type=F32)
    y = _layer_norm(ALPHA * x + xa, lg_ref[...], lb_ref[...])
    o_ref[...] = y
    _to_chunk_major(y, oc_ref)


def _xattn(x2d, wq, k, v, wo, lg, lb, seq_len, tm=512):
    T, D = x2d.shape
    M = k.shape[1]
    per_seq = seq_len // tm

    def full(a):
        return pl.BlockSpec(a.shape, lambda i: (0,) * a.ndim)

    return pl.pallas_call(
        _xattn_kernel,
        grid=(T // tm,),
        in_specs=[pl.BlockSpec((tm, D), lambda i: (i, 0)), full(wq),
                  pl.BlockSpec((1, M, D), lambda i: (i // per_seq, 0, 0)),
                  pl.BlockSpec((1, M, D), lambda i: (i // per_seq, 0, 0)),
                  full(wo), full(lg), full(lb)],
        out_specs=[pl.BlockSpec((tm, D), lambda i: (i, 0)), _chunk_major_spec(tm)],
        out_shape=[jax.ShapeDtypeStruct((T, D), F32), _chunk_major_shape(T)],
        compiler_params=_params(("parallel",)),
        name="xattn",
    )(x2d, wq, k, v, wo, lg, lb)


E_PER_TOK = PEER_HEADS * PEER_TOPK
ID_WORDS = E_PER_TOK // 2
assert (PEER_HEADS * PEER_TOPK) % 2 == 0 and N_KEYS * N_KEYS * WORD_ROWS <= 1 << 16


def _staircase():
    return [(a, b) for a in range(PEER_TOPK) for b in range(PEER_TOPK // (a + 1))]


def _route_kernel(x_ref, wq_ref, k1_ref, k2_ref, e_ref, g_ref, v_scr, i_scr, *, ts):
    xb = x_ref[...].astype(BF16)
    key_id = lax.broadcasted_iota(I32, (N_KEYS, ts), 0).astype(F32)
    qd = 2 * PEER_HALF
    for h in range(PEER_HEADS):
        qry = jnp.dot(xb, wq_ref[:, h * qd:(h + 1) * qd], preferred_element_type=F32)
        for half, kref in ((0, k1_ref), (1, k2_ref)):
            qh = qry[:, half * PEER_HALF:(half + 1) * PEER_HALF].astype(BF16)
            s0 = _nt_dot(kref[...], qh)

            def pick(it, s, h=h, half=half):
                m = jnp.max(s, axis=0, keepdims=True)
                idx = jnp.min(jnp.where(s == m, key_id, float(N_KEYS)), axis=0, keepdims=True)
                v_scr[half, it, h:h + 1, :] = m
                i_scr[half, it, h:h + 1, :] = idx.astype(I32)
                return jnp.where(key_id == idx, -jnp.inf, s)

            lax.fori_loop(0, PEER_TOPK, pick, s0)

    cells = _staircase()
    cand = tuple(v_scr[0, a] + v_scr[1, b] for a, b in cells)
    expert = [i_scr[0, a] * N_KEYS + i_scr[1, b] for a, b in cells]

    def select(it, carry):
        cand, top = carry
        best_v, best_e, best_c = cand[0], expert[0], jnp.zeros((PEER_HEADS, ts), I32)
        for c in range(1, len(cells)):
            better = cand[c] > best_v
            best_v = jnp.where(better, cand[c], best_v)
            best_e = jnp.where(better, expert[c], best_e)
            best_c = jnp.where(better, c, best_c)
        top = jnp.where(it == 0, best_v, top)
        i_scr[0, it] = best_e * WORD_ROWS
        v_scr[0, it] = best_v
        cand = tuple(jnp.where(best_c == c, -jnp.inf, cand[c]) for c in range(len(cells)))
        return cand, top

    _, top = lax.fori_loop(0, PEER_TOPK, select, (cand, jnp.zeros((PEER_HEADS, ts), F32)))
    ex = [jnp.exp(v_scr[0, it] - top) for it in range(PEER_TOPK)]
    den = ex[0]
    for it in range(1, PEER_TOPK):
        den = den + ex[it]
    inv = 1.0 / den
    g_ref[...] = jnp.concatenate([e * inv for e in ex], axis=0).T
    rows = pltpu.bitcast(i_scr[0].reshape(E_PER_TOK, ts), F32)
    rows = pltpu.bitcast(rows.T, I32)
    upper = pltpu.roll(rows, ID_WORDS, axis=1) << 16
    lane = lax.broadcasted_iota(I32, rows.shape, 1)
    e_ref[...] = jnp.where(lane < ID_WORDS, rows | upper, 0)


def _route(x2d, wq, k1, k2, ts=256):
    T, D = x2d.shape
    nb = T // ts

    def full(a):
        return pl.BlockSpec(a.shape, lambda i: (0,) * a.ndim)

    blk = pl.BlockSpec((ts, E_PER_TOK), lambda i: (i, 0))
    return pl.pallas_call(
        functools.partial(_route_kernel, ts=ts),
        grid=(nb,),
        in_specs=[pl.BlockSpec((ts, D), lambda i: (i, 0)), full(wq), full(k1), full(k2)],
        out_specs=[blk, blk],
        out_shape=[jax.ShapeDtypeStruct((T, E_PER_TOK), I32),
                   jax.ShapeDtypeStruct((T, E_PER_TOK), F32)],
        scratch_shapes=[pltpu.VMEM((2, PEER_TOPK, PEER_HEADS, ts), F32),
                        pltpu.VMEM((2, PEER_TOPK, PEER_HEADS, ts), I32)],
        compiler_params=_params(("parallel",)),
        name="peer_route",
    )(x2d, wq, k1, k2)


TOK_GROUP = SUBLANES


N_GROUPS = PEER_TB // TOK_GROUP


def _gather_rows(ids_ref, tab_ref, rows_scr, tok0):
    def fetch(tt, slot, r0):
        r0 = pl.multiple_of(r0, WORD_ROWS)
        rows_scr[tt, slot * WORD_ROWS:(slot + 1) * WORD_ROWS, :] = tab_ref[pl.ds(r0, WORD_ROWS), :]

    for tt in range(TOK_GROUP):
        for j in range(ID_WORDS):
            w = ids_ref[tok0 + tt, j]
            fetch(tt, j, w & 0xFFFF)
            fetch(tt, j + ID_WORDS, lax.shift_right_logical(w, 16))


def _token_chunks(tt):
    return pl.ds(tt, ROW_CHUNKS, stride=PEER_TB)


def _chunk_of_col():
    col = lax.broadcasted_iota(I32, (ROW_CHUNKS, E_PER_TOK * ROW_CHUNKS), 1)
    row = lax.broadcasted_iota(I32, (ROW_CHUNKS, E_PER_TOK * ROW_CHUNKS), 0)
    return (col & (ROW_CHUNKS - 1)) == row


def _peer_score_kernel(ids_ref, x_ref, tab_ref, s_ref, *scratch):
    rows_bufs, part_bufs = scratch[:N_GROUPS], scratch[N_GROUPS:]
    own_chunk = _chunk_of_col()
    c_id = lax.broadcasted_iota(I32, (E_PER_TOK * ROW_CHUNKS, E_PER_TOK), 0)
    e_id = lax.broadcasted_iota(I32, (E_PER_TOK * ROW_CHUNKS, E_PER_TOK), 1)
    fold = jnp.where(c_id >> CHUNK_SHIFT == e_id, 1.0, 0.0).astype(BF16)

    for g in range(N_GROUPS):
        tok0 = g * TOK_GROUP
        rows_scr, part_scr = rows_bufs[g], part_bufs[g]
        _gather_rows(ids_ref, tab_ref, rows_scr, tok0)
        for tt in range(TOK_GROUP):
            rows = pltpu.bitcast(rows_scr[tt], BF16)
            xt = x_ref[0, _token_chunks(tok0 + tt), :].astype(BF16)
            full = _nt_dot(xt, rows)
            part_scr[tt * ROW_CHUNKS:(tt + 1) * ROW_CHUNKS, :] = jnp.where(own_chunk, full, 0.0)
        part = part_scr[...]
        hi = part.astype(BF16)
        lo = (part - hi.astype(F32)).astype(BF16)
        sc = (jnp.dot(hi, fold, preferred_element_type=F32)
              + jnp.dot(lo, fold, preferred_element_type=F32))
        sc = jnp.sum(sc.reshape(TOK_GROUP, ROW_CHUNKS, E_PER_TOK), axis=1)
        s_ref[tok0:tok0 + TOK_GROUP, :] = sc


def _peer_mix_kernel(ids_ref, s_ref, gate_ref, tab_ref, o_ref, *rows_bufs):
    own_chunk = _chunk_of_col()
    e_id = lax.broadcasted_iota(I32, (E_PER_TOK, E_PER_TOK * ROW_CHUNKS), 0)
    c_id = lax.broadcasted_iota(I32, (E_PER_TOK, E_PER_TOK * ROW_CHUNKS), 1)
    spread = jnp.where(c_id >> CHUNK_SHIFT == e_id, 1.0, 0.0).astype(BF16)

    for g in range(N_GROUPS):
        tok0 = g * TOK_GROUP
        sl = slice(tok0, tok0 + TOK_GROUP)
        rows_scr = rows_bufs[g]
        _gather_rows(ids_ref, tab_ref, rows_scr, tok0)
        s = s_ref[sl, :]
        act = 0.5 * s * (1.0 + lax.erf(s * (2.0 ** -0.5)))
        coef = (gate_ref[sl, :] * act).astype(BF16)
        coef = jnp.dot(coef, spread, preferred_element_type=F32)
        for tt in range(TOK_GROUP):
            rows = pltpu.bitcast(rows_scr[tt], BF16)
            ct = jnp.broadcast_to(coef[tt:tt + 1], (ROW_CHUNKS, E_PER_TOK * ROW_CHUNKS))
            ct = jnp.where(own_chunk, ct, 0.0).astype(BF16)
            o_ref[0, _token_chunks(tok0 + tt), :] = jnp.dot(ct, rows, preferred_element_type=F32)


def _peer_experts(ids, gate, xc, tab_u, tab_v):
    T = ids.shape[0]
    tb = PEER_TB
    nb = T // tb
    ids_spec = pl.BlockSpec((tb, E_PER_TOK), lambda i: (i, 0), memory_space=pltpu.SMEM)
    tab_spec = pl.BlockSpec(tab_u.shape, lambda i: (0, 0), pipeline_mode=pl.Buffered(1))
    tok_spec = pl.BlockSpec((tb, E_PER_TOK), lambda i: (i, 0))
    row_spec = _chunk_major_spec(tb)
    rows_scr = [pltpu.VMEM((TOK_GROUP, E_PER_TOK * WORD_ROWS, LANES), I32)] * N_GROUPS
    part_scr = [pltpu.VMEM((TOK_GROUP * ROW_CHUNKS, E_PER_TOK * ROW_CHUNKS), F32)] * N_GROUPS
    s = pl.pallas_call(
        _peer_score_kernel,
        grid=(nb,),
        in_specs=[ids_spec, row_spec, tab_spec],
        out_specs=tok_spec,
        out_shape=jax.ShapeDtypeStruct((T, E_PER_TOK), F32),
        scratch_shapes=rows_scr + part_scr,
        compiler_params=_params(("arbitrary",)),
        name="peer_score",
    )(ids, xc, tab_u)
    return pl.pallas_call(
        _peer_mix_kernel,
        grid=(nb,),
        in_specs=[ids_spec, tok_spec, tok_spec, tab_spec],
        out_specs=row_spec,
        out_shape=_chunk_major_shape(T),
        scratch_shapes=rows_scr,
        compiler_params=_params(("arbitrary",)),
        name="peer_mix",
    )(ids, s, gate, tab_v)


def _resln_kernel(x_ref, y_ref, lg_ref, lb_ref, o_ref):
    G = x_ref.shape[0]
    z = ALPHA * x_ref[...] + y_ref[...]
    zs = [z[:, c * PEER_TB:(c + 1) * PEER_TB, :] for c in range(ROW_CHUNKS)]
    tot = zs[0]
    for c in range(1, ROW_CHUNKS):
        tot = tot + zs[c]
    mu = jnp.sum(tot, axis=-1, keepdims=True) * (1.0 / D_MODEL)
    zc = [a - mu for a in zs]
    sq = zc[0] * zc[0]
    for c in range(1, ROW_CHUNKS):
        sq = sq + zc[c] * zc[c]
    rstd = lax.rsqrt(jnp.sum(sq, axis=-1, keepdims=True) * (1.0 / D_MODEL) + LN_EPS)
    for c in range(ROW_CHUNKS):
        cols = slice(c * LANES, (c + 1) * LANES)
        y = (zc[c] * rstd).reshape(G * PEER_TB, LANES)
        o_ref[:, cols] = y * lg_ref[:, cols] + lb_ref[:, cols]


def _resln(xc, yc, lg, lb, tm=512):
    T = xc.shape[0] * PEER_TB
    blk = _chunk_major_spec(tm)
    par = pl.BlockSpec((1, D_MODEL), lambda i: (0, 0))
    return pl.pallas_call(
        _resln_kernel,
        grid=(T // tm,),
        in_specs=[blk, blk, par, par],
        out_specs=pl.BlockSpec((tm, D_MODEL), lambda i: (i, 0)),
        out_shape=jax.ShapeDtypeStruct((T, D_MODEL), F32),
        compiler_params=_params(("parallel",)),
        name="res_ln",
    )(xc, yc, lg, lb)


def _pack_kernel(t_ref, o_ref):
    def bf16_bits(v):
        return pltpu.bitcast(v.astype(BF16).astype(F32), jnp.uint32)

    te = t_ref.shape[0]
    for r in range(WORD_ROWS):
        lo = bf16_bits(t_ref[:, (2 * r) * LANES:(2 * r + 1) * LANES])
        hi = bf16_bits(t_ref[:, (2 * r + 1) * LANES:(2 * r + 2) * LANES])
        words = (hi & jnp.uint32(0xFFFF0000)) | (lo >> 16)
        o_ref[pl.ds(r, te, stride=WORD_ROWS), :] = pltpu.bitcast(words, I32)


def _pack_table(t, layer, te=512):
    _, E, D = t.shape
    return pl.pallas_call(
        _pack_kernel,
        grid=(E // te,),
        in_specs=[pl.BlockSpec((None, te, D), lambda i: (layer, i, 0))],
        out_specs=pl.BlockSpec((te * WORD_ROWS, LANES), lambda i: (i, 0)),
        out_shape=jax.ShapeDtypeStruct((E * WORD_ROWS, LANES), I32),
        compiler_params=_params(("parallel",)),
        name="pack_table",
    )(t)


def _block_diag(w):
    H, d, _ = w.shape
    out = jnp.zeros((H * d, H * d), w.dtype)
    for h in range(H):
        out = out.at[h * d:(h + 1) * d, h * d:(h + 1) * d].set(w[h])
    return out


def _cat_w_in(w):
    GW = GROUP_WIDTH
    off_f = 5 * GW
    off_sb = off_f + GROUP_HEADS
    off_sc = off_sb + 3 * GW
    pad = jnp.zeros((w.shape[0], LANES - GROUP_HEADS), w.dtype)
    return jnp.concatenate([w[:, :off_f], w[:, off_sb:off_sc], w[:, off_sc:N_IN],
                            w[:, off_f:off_sb], pad], axis=1).astype(BF16)


def _heads(t, B, S):
    t = t.reshape(B, S, 3, GROUP_HEADS, HEAD_DIM).transpose(2, 0, 3, 1, 4)
    return t[0], t[1], t[2]


def _row(v, width=None):
    v = v.reshape(1, -1).astype(F32)
    if width is not None and v.shape[1] < width:
        v = jnp.pad(v, ((0, 0), (0, width - v.shape[1])))
    return v


def _layer(x2d, kmem, vmem_, B, S, p, tables, layer, tq=256):
    T = B * S
    rg, fox, sb, sc, f = _inproj(x2d, _cat_w_in(p["w_in"]), B, S)
    yrg, ysc, cum, cumt = _rgsc(
        rg.reshape(B, S, -1), sc.reshape(B, S, -1), f.reshape(B, S, -1),
        p["rg_conv_w"], _row(p["rg_conv_b"]), _block_diag(p["rg_wa"]).astype(BF16), _row(p["rg_ba"]),
        _block_diag(p["rg_wi"]).astype(BF16), _row(p["rg_bi"]), _row(p["rg_lambda"]),
        p["sc_conv_w"], _row(p["fox_bf"], LANES), tq)
    yfox = _fox(fox, cum, cumt, tq=tq)
    ysb = _sb(sb, tq=tq)
    ys = (yrg.reshape(T, -1), yfox.reshape(T, -1), ysb.reshape(T, -1), ysc.reshape(T, -1))
    x1 = _mixout(ys, _row(p["mix_norm_g"]), p["w_out"].astype(BF16), x2d,
                 _row(p["ln1_g"]), _row(p["ln1_b"]))
    x2, x2c = _xattn(x1, p["xa_wq"].astype(BF16), kmem, vmem_, p["xa_wo"].astype(BF16),
                     _row(p["ln2_g"]), _row(p["ln2_b"]), S)
    ids, gate = _route(x2, p["peer_wq"].astype(BF16), p["peer_k1"].astype(BF16),
                       p["peer_k2"].astype(BF16))
    ffc = _peer_experts(ids, gate, x2c, _pack_table(tables[0], layer), _pack_table(tables[1], layer))
    return _resln(x2c, ffc, _row(p["ln3_g"]), _row(p["ln3_b"]))


_LAYER_PARAMS = ("w_in", "w_out", "rg_conv_w", "rg_conv_b", "rg_wa", "rg_ba", "rg_wi", "rg_bi",
                 "rg_lambda", "fox_bf", "sc_conv_w", "mix_norm_g", "ln1_g", "ln1_b", "xa_wq",
                 "xa_wkv", "xa_wo", "ln2_g", "ln2_b", "peer_wq", "peer_k1", "peer_k2", "peer_u",
                 "peer_v", "ln3_g", "ln3_b")


def kernel(x, mem, w_in, w_out, rg_conv_w, rg_conv_b, rg_wa, rg_ba, rg_wi, rg_bi, rg_lambda, fox_bf, sc_conv_w, mix_norm_g, ln1_g, ln1_b, xa_wq, xa_wkv, xa_wo, ln2_g, ln2_b, peer_wq, peer_k1, peer_k2, peer_u, peer_v, ln3_g, ln3_b):
    stacked = dict(zip(_LAYER_PARAMS, (
        w_in, w_out, rg_conv_w, rg_conv_b, rg_wa, rg_ba, rg_wi, rg_bi, rg_lambda, fox_bf,
        sc_conv_w, mix_norm_g, ln1_g, ln1_b, xa_wq, xa_wkv, xa_wo, ln2_g, ln2_b, peer_wq,
        peer_k1, peer_k2, peer_u, peer_v, ln3_g, ln3_b)))
    B, S, D = x.shape
    M = mem.shape[1]
    x2d = x.reshape(B * S, D)
    mem2d = mem.reshape(B * M, D)
    for l in range(w_in.shape[0]):
        p = {k: v[l] for k, v in stacked.items() if k not in ("peer_u", "peer_v")}
        kmem, vmem_ = _kv(mem2d, p["xa_wkv"].astype(BF16))
        x2d = _layer(x2d, kmem.reshape(B, M, D), vmem_.reshape(B, M, D), B, S, p,
                     (peer_u, peer_v), l)
    return x2d.reshape(B, S, D)
```

```python
import functools
import math

import jax
import jax.numpy as jnp
from jax import lax
from jax.experimental import pallas as pl
from jax.experimental.pallas import tpu as pltpu

F32 = jnp.float32
BF16 = jnp.bfloat16
I32 = jnp.int32

D_MODEL = 1024
GROUP_WIDTH = 256
GROUP_HEADS = 4
HEAD_DIM = 64
N_IN = 2820
RGLRU_C = 8.0
XA_HEADS = 4
XA_HEAD_DIM = D_MODEL // XA_HEADS
PEER_HEADS = 8
N_KEYS = 128
PEER_HALF = 128
PEER_TOPK = 16
DEPTH = 2
ALPHA = (2.0 * DEPTH) ** 0.25
LN_EPS = 1e-5

SUBLANES = 8
LANES = 128
WORD_ROWS = D_MODEL // (2 * LANES)
ROW_CHUNKS = D_MODEL // LANES
CHUNK_SHIFT = ROW_CHUNKS.bit_length() - 1

VMEM_LIMIT = 48 * 1024 * 1024


def _params(sem, vmem=VMEM_LIMIT):
    return pltpu.CompilerParams(dimension_semantics=sem, vmem_limit_bytes=vmem)


def _layer_norm(z, g, b):
    mu = jnp.mean(z, axis=-1, keepdims=True)
    zc = z - mu
    var = jnp.mean(zc * zc, axis=-1, keepdims=True)
    return zc * lax.rsqrt(var + LN_EPS) * g + b


def _log_sigmoid(z):
    return jnp.minimum(z, 0.0) - jnp.log1p(jnp.exp(-jnp.abs(z)))


def _nt_dot(a, b):
    return lax.dot_general(a, b, (((1,), (1,)), ((), ())), preferred_element_type=F32)


C_RG, C_FOX, C_SB, C_SC, C_F, C_END = 0, 512, 1280, 2048, 2816, 2944


def _inproj_kernel(x_ref, w_ref, rg_ref, fox_ref, sb_ref, sc_ref, f_ref):
    xb = x_ref[...].astype(BF16)

    def mm(lo, hi):
        return jnp.dot(xb, w_ref[:, lo:hi], preferred_element_type=F32)

    def heads(ref, lo, hi):
        qkv = mm(lo, hi)
        for j in range(3 * GROUP_HEADS):
            ref[0, j] = qkv[:, j * HEAD_DIM:(j + 1) * HEAD_DIM].astype(BF16)

    rg_ref[...] = mm(C_RG, C_FOX)
    heads(fox_ref, C_FOX, C_SB)
    heads(sb_ref, C_SB, C_SC)
    sc_ref[...] = mm(C_SC, C_F)
    f_ref[...] = mm(C_F, C_END)


def _inproj(x2d, w_cat, B, S, tm=512):
    T, D = x2d.shape
    per_seq = S // tm
    flat = ((C_FOX - C_RG, F32), (C_F - C_SC, F32), (C_END - C_F, F32))
    flat_specs = [pl.BlockSpec((tm, w), lambda i: (i, 0)) for w, _ in flat]
    flat_shapes = [jax.ShapeDtypeStruct((T, w), dt) for w, dt in flat]
    head_spec = pl.BlockSpec((1, 3 * GROUP_HEADS, tm, HEAD_DIM),
                             lambda i: (i // per_seq, 0, i % per_seq, 0))
    head_shape = jax.ShapeDtypeStruct((B, 3 * GROUP_HEADS, S, HEAD_DIM), BF16)
    return pl.pallas_call(
        _inproj_kernel,
        grid=(T // tm,),
        in_specs=[pl.BlockSpec((tm, D), lambda i: (i, 0)),
                  pl.BlockSpec((D, C_END), lambda i: (0, 0))],
        out_specs=[flat_specs[0], head_spec, head_spec, flat_specs[1], flat_specs[2]],
        out_shape=[flat_shapes[0], head_shape, head_shape, flat_shapes[1], flat_shapes[2]],
        compiler_params=_params(("parallel",)),
        name="inproj",
    )(x2d, w_cat)


def _rgsc_kernel(rg_ref, sc_ref, f_ref, cw_ref, cb_ref, wa_ref, ba_ref, wi_ref, bi_ref, lam_ref,
                 scw_ref, fb_ref, yrg_ref, ysc_ref, cum_ref, cumt_ref, xprev, chprev, hprev, cprev,
                 *, ts, tk):
    GW = GROUP_WIDTH

    @pl.when(pl.program_id(1) == 0)
    def _():
        xprev[...] = jnp.zeros_like(xprev)
        chprev[...] = jnp.zeros_like(chprev)
        hprev[...] = jnp.zeros_like(hprev)
        cprev[...] = jnp.zeros_like(cprev)

    row = lax.broadcasted_iota(I32, (ts, GW), 0)

    def delayed(prev, cur, d):
        ext = jnp.concatenate([prev, cur], axis=0)
        return pltpu.roll(ext, d, 0)[SUBLANES:]

    xr = rg_ref[0, :, :GW]
    gate = rg_ref[0, :, GW:]
    xp = xprev[...]
    cw = cw_ref[...]
    xc = (delayed(xp, xr, 3) * cw[0:1] + delayed(xp, xr, 2) * cw[1:2]
          + delayed(xp, xr, 1) * cw[2:3] + xr * cw[3:4] + cb_ref[...])
    xprev[...] = xr[ts - SUBLANES:]
    xcb = xc.astype(BF16)
    r = jax.nn.sigmoid(jnp.dot(xcb, wa_ref[...], preferred_element_type=F32) + ba_ref[...])
    ig = jax.nn.sigmoid(jnp.dot(xcb, wi_ref[...], preferred_element_type=F32) + bi_ref[...])
    z = -lam_ref[...]
    softplus = jnp.maximum(z, 0.0) + jnp.log1p(jnp.exp(-jnp.abs(z)))
    log_a = -RGLRU_C * r * softplus
    a = jnp.exp(log_a)
    u = jnp.sqrt(-jnp.tanh(log_a) * (a * a + 1.0)) * (ig * xc)
    acc_a, acc_b = a, u
    d = 1
    while d < ts:
        keep = row >= d
        a_s = jnp.where(keep, pltpu.roll(acc_a, d, 0), 1.0)
        b_s = jnp.where(keep, pltpu.roll(acc_b, d, 0), 0.0)
        acc_b = acc_a * b_s + acc_b
        acc_a = acc_a * a_s
        d *= 2
    h = acc_b + acc_a * hprev[...]
    hprev[...] = h[ts - 1:]
    c0 = math.sqrt(2.0 / math.pi)
    gelu = 0.5 * gate * (1.0 + jnp.tanh(c0 * (gate + 0.044715 * gate * gate * gate)))
    yrg_ref[0] = h * gelu

    bg = sc_ref[0, :, :GW]
    ch = sc_ref[0, :, GW:2 * GW] * sc_ref[0, :, 2 * GW:]
    cp = chprev[...]
    sw = scw_ref[...]
    ysc_ref[0] = bg * (delayed(cp, ch, 2) * sw[0:1] + delayed(cp, ch, 1) * sw[1:2] + ch * sw[2:3])
    chprev[...] = ch[ts - SUBLANES:]

    rowf = lax.broadcasted_iota(I32, (ts, LANES), 0)
    c = _log_sigmoid(f_ref[0] + fb_ref[...])
    d = 1
    while d < ts:
        c = c + jnp.where(rowf >= d, pltpu.roll(c, d, 0), 0.0)
        d *= 2
    c = c + cprev[...]
    cprev[...] = c[ts - 1:]
    cum_ref[0] = c
    ct = c.T[:SUBLANES]
    for j in range(ts // tk):
        cumt_ref[0, j] = ct[:, j * tk:(j + 1) * tk]


def _rgsc(rg, sc, f, cw, cb, wa, ba, wi, bi, lam, scw, fb, tk, ts=512):
    B, S, _ = rg.shape
    GW = GROUP_WIDTH
    per = ts // tk

    def full(a):
        return pl.BlockSpec(a.shape, lambda b, t: (0,) * a.ndim)

    def seq(w):
        return pl.BlockSpec((1, ts, w), lambda b, t: (b, t, 0))

    params = (cw, cb, wa, ba, wi, bi, lam, scw, fb)
    return pl.pallas_call(
        functools.partial(_rgsc_kernel, ts=ts, tk=tk),
        grid=(B, S // ts),
        in_specs=[seq(2 * GW), seq(3 * GW), seq(LANES)] + [full(p) for p in params],
        out_specs=[seq(GW), seq(GW), seq(LANES),
                   pl.BlockSpec((1, per, SUBLANES, tk), lambda b, t: (b, t, 0, 0))],
        out_shape=[jax.ShapeDtypeStruct((B, S, GW), F32), jax.ShapeDtypeStruct((B, S, GW), F32),
                   jax.ShapeDtypeStruct((B, S, LANES), F32),
                   jax.ShapeDtypeStruct((B, S // tk, SUBLANES, tk), F32)],
        scratch_shapes=[pltpu.VMEM((SUBLANES, GW), F32), pltpu.VMEM((SUBLANES, GW), F32),
                        pltpu.VMEM((1, GW), F32), pltpu.VMEM((1, LANES), F32)],
        compiler_params=_params(("parallel", "arbitrary")),
        name="rgsc",
    )(rg, sc, f, *params)


def _fox_kernel(q_ref, k_ref, v_ref, cc_ref, cr_ref, o_ref, *, tq):
    qi = pl.program_id(1)
    scale = HEAD_DIM ** -0.5
    rowi = lax.broadcasted_iota(I32, (tq, tq), 0)
    coli = lax.broadcasted_iota(I32, (tq, tq), 1)
    causal = coli <= rowi
    def head_step(h, ki, carry, diag):
        m, l, acc = carry
        off = pl.multiple_of(ki * tq, tq)
        k = k_ref[0, h, pl.ds(off, tq), :]
        v = v_ref[0, h, pl.ds(off, tq), :]
        s = (_nt_dot(q_ref[0, h], k) * scale + cc_ref[0, :, h:h + 1]
             - cr_ref[0, ki, h:h + 1, :])
        if diag:
            s = jnp.where(causal, s, -jnp.inf)
        m_new = jnp.maximum(m, jnp.max(s, axis=-1, keepdims=True))
        alpha = jnp.exp(m - m_new)
        p = jnp.exp(s - m_new)
        l = alpha * l + jnp.sum(p, axis=-1, keepdims=True)
        acc = alpha * acc + jnp.dot(p.astype(BF16), v, preferred_element_type=F32)
        return m_new, l, acc

    def step(ki, carries, diag):
        return tuple(head_step(h, ki, carries[h], diag) for h in range(GROUP_HEADS))

    init = (jnp.full((tq, 1), -jnp.inf, F32), jnp.zeros((tq, 1), F32),
            jnp.zeros((tq, HEAD_DIM), F32))
    carries = lax.fori_loop(0, qi, functools.partial(step, diag=False), (init,) * GROUP_HEADS)
    carries = step(qi, carries, True)
    o_ref[0] = jnp.concatenate([acc / l for _, l, acc in carries], axis=-1)


def _qkv_specs(S, tq):
    H, d = GROUP_HEADS, HEAD_DIM
    return [pl.BlockSpec((1, H, tq, d), lambda b, i: (b, 0, i, 0)),
            pl.BlockSpec((1, H, S, d), lambda b, i: (b, 1, 0, 0)),
            pl.BlockSpec((1, H, S, d), lambda b, i: (b, 2, 0, 0))]


def _fox(qkv, cum, cumt, tq=256):
    B, _, S, d = qkv.shape
    H = GROUP_HEADS
    return pl.pallas_call(
        functools.partial(_fox_kernel, tq=tq),
        grid=(B, S // tq),
        in_specs=_qkv_specs(S, tq)
        + [pl.BlockSpec((1, tq, LANES), lambda b, i: (b, i, 0)),
           pl.BlockSpec((1, S // tq, SUBLANES, tq), lambda b, i: (b, 0, 0, 0))],
        out_specs=pl.BlockSpec((1, tq, H * d), lambda b, i: (b, i, 0)),
        out_shape=jax.ShapeDtypeStruct((B, S, H * d), F32),
        compiler_params=_params(("parallel", "arbitrary")),
        name="fox_attn",
    )(qkv, qkv, qkv, cum, cumt)


EXP_UNDERFLOW = -104.0


def _sb_kernel(q_ref, k_ref, v_ref, o_ref, *, tq):
    qi = pl.program_id(1)
    scale = HEAD_DIM ** -0.5
    rowi = lax.broadcasted_iota(I32, (tq, tq), 0)
    coli = lax.broadcasted_iota(I32, (tq, tq), 1)
    strict = coli < rowi
    later = jnp.where(rowi > coli, 1.0, 0.0).astype(BF16)
    def head_step(h, ki, carry, diag):
        rest, acc = carry
        off = pl.multiple_of(ki * tq, tq)
        k = k_ref[0, h, pl.ds(off, tq), :]
        v = v_ref[0, h, pl.ds(off, tq), :]
        z = _nt_dot(q_ref[0, h], k) * scale
        ls = _log_sigmoid(z)
        l1m = ls - z
        if diag:
            l1m = jnp.where(strict, l1m, 0.0)
        hi = l1m.astype(BF16)
        lo = (l1m - hi.astype(F32)).astype(BF16)
        tail = (jnp.dot(hi, later, preferred_element_type=F32)
                + jnp.dot(lo, later, preferred_element_type=F32) + rest)
        w = jnp.exp(ls + tail)
        if diag:
            w = jnp.where(strict, w, 0.0)
        acc = acc + jnp.dot(w.astype(BF16), v, preferred_element_type=F32)
        rest = rest + jnp.sum(l1m, axis=-1, keepdims=True)
        return rest, acc

    def step(ki, carries, diag):
        return tuple(head_step(h, ki, carries[h], diag) for h in range(GROUP_HEADS))

    init = (jnp.zeros((tq, 1), F32), jnp.zeros((tq, HEAD_DIM), F32))
    carries = step(qi, (init,) * GROUP_HEADS, True)

    def more(c):
        j, carries = c
        top = carries[0][0]
        for rest, _ in carries[1:]:
            top = jnp.maximum(top, rest)
        return jnp.logical_and(j < qi, jnp.max(top) > EXP_UNDERFLOW)

    def further(c):
        j, carries = c
        return j + 1, step(qi - 1 - j, carries, False)

    _, carries = lax.while_loop(more, further, (jnp.int32(0), carries))
    o_ref[0] = jnp.concatenate([acc for _, acc in carries], axis=-1)


def _sb(qkv, tq=256):
    B, _, S, d = qkv.shape
    H = GROUP_HEADS
    return pl.pallas_call(
        functools.partial(_sb_kernel, tq=tq),
        grid=(B, S // tq),
        in_specs=_qkv_specs(S, tq),
        out_specs=pl.BlockSpec((1, tq, H * d), lambda b, i: (b, i, 0)),
        out_shape=jax.ShapeDtypeStruct((B, S, H * d), F32),
        compiler_params=_params(("parallel", "arbitrary")),
        name="sb_attn",
    )(qkv, qkv, qkv)


def _mixout_kernel(y0_ref, y1_ref, y2_ref, y3_ref, g_ref, w_ref, x_ref, lg_ref, lb_ref, o_ref):
    def rms(y):
        return y * lax.rsqrt(jnp.mean(y * y, axis=-1, keepdims=True) + 1e-6)

    y = jnp.concatenate([rms(r[...]) for r in (y0_ref, y1_ref, y2_ref, y3_ref)], axis=-1)
    y = (y * g_ref[...]).astype(BF16)
    mix = jnp.dot(y, w_ref[...], preferred_element_type=F32)
    o_ref[...] = _layer_norm(ALPHA * x_ref[...] + mix, lg_ref[...], lb_ref[...])


def _mixout(ys, g, w, x2d, lg, lb, tm=512):
    T, D = x2d.shape
    GW = GROUP_WIDTH

    def full(a):
        return pl.BlockSpec(a.shape, lambda i: (0,) * a.ndim)

    return pl.pallas_call(
        _mixout_kernel,
        grid=(T // tm,),
        in_specs=[pl.BlockSpec((tm, GW), lambda i: (i, 0))] * 4
        + [full(g), full(w), pl.BlockSpec((tm, D), lambda i: (i, 0)), full(lg), full(lb)],
        out_specs=pl.BlockSpec((tm, D), lambda i: (i, 0)),
        out_shape=jax.ShapeDtypeStruct((T, D), F32),
        compiler_params=_params(("parallel",)),
        name="mixout",
    )(*ys, g, w, x2d, lg, lb)


def _kv_kernel(m_ref, w_ref, k_ref, v_ref):
    kv = jnp.dot(m_ref[...].astype(BF16), w_ref[...], preferred_element_type=F32)
    k_ref[...] = kv[:, :D_MODEL].astype(BF16)
    v_ref[...] = kv[:, D_MODEL:].astype(BF16)


def _kv(mem2d, wkv, tm=512):
    M, D = mem2d.shape
    return pl.pallas_call(
        _kv_kernel,
        grid=(M // tm,),
        in_specs=[pl.BlockSpec((tm, D), lambda i: (i, 0)),
                  pl.BlockSpec((D, 2 * D), lambda i: (0, 0))],
        out_specs=[pl.BlockSpec((tm, D), lambda i: (i, 0))] * 2,
        out_shape=[jax.ShapeDtypeStruct((M, D), BF16)] * 2,
        compiler_params=_params(("parallel",)),
        name="mem_kv",
    )(mem2d, wkv)


PEER_TB = 32


def _chunk_major_spec(tm):
    return pl.BlockSpec((tm // PEER_TB, ROW_CHUNKS * PEER_TB, LANES), lambda i: (i, 0, 0))


def _chunk_major_shape(T):
    return jax.ShapeDtypeStruct((T // PEER_TB, ROW_CHUNKS * PEER_TB, LANES), F32)


def _to_chunk_major(y, ref):
    for g in range(y.shape[0] // PEER_TB):
        for c in range(ROW_CHUNKS):
            ref[g, c * PEER_TB:(c + 1) * PEER_TB, :] = (
                y[g * PEER_TB:(g + 1) * PEER_TB, c * LANES:(c + 1) * LANES])


def _xattn_kernel(x_ref, wq_ref, k_ref, v_ref, wo_ref, lg_ref, lb_ref, o_ref, oc_ref):
    x = x_ref[...]
    q = jnp.dot(x.astype(BF16), wq_ref[...], preferred_element_type=F32).astype(BF16)
    scale = XA_HEAD_DIM ** -0.5
    outs = []
    for h in range(XA_HEADS):
        sl = slice(h * XA_HEAD_DIM, (h + 1) * XA_HEAD_DIM)
        s = _nt_dot(q[:, sl], k_ref[0, :, sl]) * scale
        p = jnp.exp(s - jnp.max(s, axis=-1, keepdims=True))
        p = p / jnp.sum(p, axis=-1, keepdims=True)
        outs.append(jnp.dot(p.astype(BF16), v_ref[0, :, sl], preferred_element_type=F32))
    o = jnp.concatenate(outs, axis=-1).astype(BF16)
    xa = jnp.dot(o, wo_ref[...], preferred_element_type=F32)
    y = _layer_norm(ALPHA * x + xa, lg_ref[...], lb_ref[...])
    o_ref[...] = y
    _to_chunk_major(y, oc_ref)


def _xattn(x2d, wq, k, v, wo, lg, lb, seq_len, tm=512):
    T, D = x2d.shape
    M = k.shape[1]
    per_seq = seq_len // tm

    def full(a):
        return pl.BlockSpec(a.shape, lambda i: (0,) * a.ndim)

    return pl.pallas_call(
        _xattn_kernel,
        grid=(T // tm,),
        in_specs=[pl.BlockSpec((tm, D), lambda i: (i, 0)), full(wq),
                  pl.BlockSpec((1, M, D), lambda i: (i // per_seq, 0, 0)),
                  pl.BlockSpec((1, M, D), lambda i: (i // per_seq, 0, 0)),
                  full(wo), full(lg), full(lb)],
        out_specs=[pl.BlockSpec((tm, D), lambda i: (i, 0)), _chunk_major_spec(tm)],
        out_shape=[jax.ShapeDtypeStruct((T, D), F32), _chunk_major_shape(T)],
        compiler_params=_params(("parallel",)),
        name="xattn",
    )(x2d, wq, k, v, wo, lg, lb)


E_PER_TOK = PEER_HEADS * PEER_TOPK


def _staircase():
    return [(a, b) for a in range(PEER_TOPK) for b in range(PEER_TOPK // (a + 1))]


def _route_kernel(x_ref, wq_ref, k1_ref, k2_ref, e_ref, g_ref, v_scr, i_scr, *, ts):
    xb = x_ref[...].astype(BF16)
    key_id = lax.broadcasted_iota(I32, (N_KEYS, ts), 0).astype(F32)
    qd = 2 * PEER_HALF
    for h in range(PEER_HEADS):
        qry = jnp.dot(xb, wq_ref[:, h * qd:(h + 1) * qd], preferred_element_type=F32)
        for half, kref in ((0, k1_ref), (1, k2_ref)):
            qh = qry[:, half * PEER_HALF:(half + 1) * PEER_HALF].astype(BF16)
            s0 = _nt_dot(kref[...], qh)

            def pick(it, s, h=h, half=half):
                m = jnp.max(s, axis=0, keepdims=True)
                idx = jnp.min(jnp.where(s == m, key_id, float(N_KEYS)), axis=0, keepdims=True)
                v_scr[half, it, h:h + 1, :] = m
                i_scr[half, it, h:h + 1, :] = idx.astype(I32)
                return jnp.where(key_id == idx, -jnp.inf, s)

            lax.fori_loop(0, PEER_TOPK, pick, s0)

    cells = _staircase()
    cand = tuple(v_scr[0, a] + v_scr[1, b] for a, b in cells)
    expert = [i_scr[0, a] * N_KEYS + i_scr[1, b] for a, b in cells]

    def select(it, carry):
        cand, top = carry
        best_v, best_e, best_c = cand[0], expert[0], jnp.zeros((PEER_HEADS, ts), I32)
        for c in range(1, len(cells)):
            better = cand[c] > best_v
            best_v = jnp.where(better, cand[c], best_v)
            best_e = jnp.where(better, expert[c], best_e)
            best_c = jnp.where(better, c, best_c)
        top = jnp.where(it == 0, best_v, top)
        i_scr[0, it] = best_e * WORD_ROWS
        v_scr[0, it] = best_v
        cand = tuple(jnp.where(best_c == c, -jnp.inf, cand[c]) for c in range(len(cells)))
        return cand, top

    _, top = lax.fori_loop(0, PEER_TOPK, select, (cand, jnp.zeros((PEER_HEADS, ts), F32)))
    ex = [jnp.exp(v_scr[0, it] - top) for it in range(PEER_TOPK)]
    den = ex[0]
    for it in range(1, PEER_TOPK):
        den = den + ex[it]
    inv = 1.0 / den
    g_ref[...] = jnp.concatenate([e * inv for e in ex], axis=0).T
    rows = pltpu.bitcast(i_scr[0].reshape(E_PER_TOK, ts), F32)
    e_ref[...] = pltpu.bitcast(rows.T, I32)


def _route(x2d, wq, k1, k2, ts=256):
    T, D = x2d.shape
    nb = T // ts

    def full(a):
        return pl.BlockSpec(a.shape, lambda i: (0,) * a.ndim)

    blk = pl.BlockSpec((ts, E_PER_TOK), lambda i: (i, 0))
    return pl.pallas_call(
        functools.partial(_route_kernel, ts=ts),
        grid=(nb,),
        in_specs=[pl.BlockSpec((ts, D), lambda i: (i, 0)), full(wq), full(k1), full(k2)],
        out_specs=[blk, blk],
        out_shape=[jax.ShapeDtypeStruct((T, E_PER_TOK), I32),
                   jax.ShapeDtypeStruct((T, E_PER_TOK), F32)],
        scratch_shapes=[pltpu.VMEM((2, PEER_TOPK, PEER_HEADS, ts), F32),
                        pltpu.VMEM((2, PEER_TOPK, PEER_HEADS, ts), I32)],
        compiler_params=_params(("parallel",)),
        name="peer_route",
    )(x2d, wq, k1, k2)


TOK_GROUP = SUBLANES


N_GROUPS = PEER_TB // TOK_GROUP


def _gather_rows(ids_ref, tab_ref, rows_scr, tok0):
    for tt in range(TOK_GROUP):
        for j in range(E_PER_TOK):
            r0 = pl.multiple_of(ids_ref[tok0 + tt, j], WORD_ROWS)
            rows_scr[tt, j * WORD_ROWS:(j + 1) * WORD_ROWS, :] = tab_ref[pl.ds(r0, WORD_ROWS), :]


def _token_chunks(tt):
    return pl.ds(tt, ROW_CHUNKS, stride=PEER_TB)


def _chunk_of_col():
    col = lax.broadcasted_iota(I32, (ROW_CHUNKS, E_PER_TOK * ROW_CHUNKS), 1)
    row = lax.broadcasted_iota(I32, (ROW_CHUNKS, E_PER_TOK * ROW_CHUNKS), 0)
    return (col & (ROW_CHUNKS - 1)) == row


def _peer_score_kernel(ids_ref, x_ref, tab_ref, s_ref, *scratch):
    rows_bufs, part_bufs = scratch[:N_GROUPS], scratch[N_GROUPS:]
    own_chunk = _chunk_of_col()
    c_id = lax.broadcasted_iota(I32, (E_PER_TOK * ROW_CHUNKS, E_PER_TOK), 0)
    e_id = lax.broadcasted_iota(I32, (E_PER_TOK * ROW_CHUNKS, E_PER_TOK), 1)
    fold = jnp.where(c_id >> CHUNK_SHIFT == e_id, 1.0, 0.0).astype(BF16)

    for g in range(N_GROUPS):
        tok0 = g * TOK_GROUP
        rows_scr, part_scr = rows_bufs[g], part_bufs[g]
        _gather_rows(ids_ref, tab_ref, rows_scr, tok0)
        for tt in range(TOK_GROUP):
            rows = pltpu.bitcast(rows_scr[tt], BF16)
            xt = x_ref[0, _token_chunks(tok0 + tt), :].astype(BF16)
            full = _nt_dot(xt, rows)
            part_scr[tt * ROW_CHUNKS:(tt + 1) * ROW_CHUNKS, :] = jnp.where(own_chunk, full, 0.0)
        part = part_scr[...]
        hi = part.astype(BF16)
        lo = (part - hi.astype(F32)).astype(BF16)
        sc = (jnp.dot(hi, fold, preferred_element_type=F32)
              + jnp.dot(lo, fold, preferred_element_type=F32))
        sc = jnp.sum(sc.reshape(TOK_GROUP, ROW_CHUNKS, E_PER_TOK), axis=1)
        s_ref[tok0:tok0 + TOK_GROUP, :] = sc


def _peer_mix_kernel(ids_ref, s_ref, gate_ref, tab_ref, o_ref, *rows_bufs):
    own_chunk = _chunk_of_col()
    e_id = lax.broadcasted_iota(I32, (E_PER_TOK, E_PER_TOK * ROW_CHUNKS), 0)
    c_id = lax.broadcasted_iota(I32, (E_PER_TOK, E_PER_TOK * ROW_CHUNKS), 1)
    spread = jnp.where(c_id >> CHUNK_SHIFT == e_id, 1.0, 0.0).astype(BF16)

    for g in range(N_GROUPS):
        tok0 = g * TOK_GROUP
        sl = slice(tok0, tok0 + TOK_GROUP)
        rows_scr = rows_bufs[g % len(rows_bufs)]
        _gather_rows(ids_ref, tab_ref, rows_scr, tok0)
        s = s_ref[sl, :]
        act = 0.5 * s * (1.0 + lax.erf(s * (2.0 ** -0.5)))
        coef = (gate_ref[sl, :] * act).astype(BF16)
        coef = jnp.dot(coef, spread, preferred_element_type=F32)
        for tt in range(TOK_GROUP):
            rows = pltpu.bitcast(rows_scr[tt], BF16)
            ct = jnp.broadcast_to(coef[tt:tt + 1], (ROW_CHUNKS, E_PER_TOK * ROW_CHUNKS))
            ct = jnp.where(own_chunk, ct, 0.0).astype(BF16)
            o_ref[0, _token_chunks(tok0 + tt), :] = jnp.dot(ct, rows, preferred_element_type=F32)


def _peer_experts(ids, gate, xc, tab_u, tab_v):
    T = ids.shape[0]
    tb = PEER_TB
    nb = T // tb
    ids_spec = pl.BlockSpec((tb, E_PER_TOK), lambda i: (i, 0), memory_space=pltpu.SMEM)
    tab_spec = pl.BlockSpec(tab_u.shape, lambda i: (0, 0), pipeline_mode=pl.Buffered(1))
    tok_spec = pl.BlockSpec((tb, E_PER_TOK), lambda i: (i, 0))
    row_spec = _chunk_major_spec(tb)
    rows_scr = [pltpu.VMEM((TOK_GROUP, E_PER_TOK * WORD_ROWS, LANES), I32)] * N_GROUPS
    part_scr = [pltpu.VMEM((TOK_GROUP * ROW_CHUNKS, E_PER_TOK * ROW_CHUNKS), F32)] * N_GROUPS
    s = pl.pallas_call(
        _peer_score_kernel,
        grid=(nb,),
        in_specs=[ids_spec, row_spec, tab_spec],
        out_specs=tok_spec,
        out_shape=jax.ShapeDtypeStruct((T, E_PER_TOK), F32),
        scratch_shapes=rows_scr + part_scr,
        compiler_params=_params(("arbitrary",)),
        name="peer_score",
    )(ids, xc, tab_u)
    return pl.pallas_call(
        _peer_mix_kernel,
        grid=(nb,),
        in_specs=[ids_spec, tok_spec, tok_spec, tab_spec],
        out_specs=row_spec,
        out_shape=_chunk_major_shape(T),
        scratch_shapes=rows_scr[:1],
        compiler_params=_params(("arbitrary",)),
        name="peer_mix",
    )(ids, s, gate, tab_v)


def _resln_kernel(x_ref, y_ref, lg_ref, lb_ref, o_ref):
    G = x_ref.shape[0]
    z = ALPHA * x_ref[...] + y_ref[...]
    zs = [z[:, c * PEER_TB:(c + 1) * PEER_TB, :] for c in range(ROW_CHUNKS)]
    tot = zs[0]
    for c in range(1, ROW_CHUNKS):
        tot = tot + zs[c]
    mu = jnp.sum(tot, axis=-1, keepdims=True) * (1.0 / D_MODEL)
    zc = [a - mu for a in zs]
    sq = zc[0] * zc[0]
    for c in range(1, ROW_CHUNKS):
        sq = sq + zc[c] * zc[c]
    rstd = lax.rsqrt(jnp.sum(sq, axis=-1, keepdims=True) * (1.0 / D_MODEL) + LN_EPS)
    for c in range(ROW_CHUNKS):
        cols = slice(c * LANES, (c + 1) * LANES)
        y = (zc[c] * rstd).reshape(G * PEER_TB, LANES)
        o_ref[:, cols] = y * lg_ref[:, cols] + lb_ref[:, cols]


def _resln(xc, yc, lg, lb, tm=512):
    T = xc.shape[0] * PEER_TB
    blk = _chunk_major_spec(tm)
    par = pl.BlockSpec((1, D_MODEL), lambda i: (0, 0))
    return pl.pallas_call(
        _resln_kernel,
        grid=(T // tm,),
        in_specs=[blk, blk, par, par],
        out_specs=pl.BlockSpec((tm, D_MODEL), lambda i: (i, 0)),
        out_shape=jax.ShapeDtypeStruct((T, D_MODEL), F32),
        compiler_params=_params(("parallel",)),
        name="res_ln",
    )(xc, yc, lg, lb)


def _pack_kernel(t_ref, o_ref):
    def bf16_bits(v):
        return pltpu.bitcast(v.astype(BF16).astype(F32), jnp.uint32)

    te = t_ref.shape[0]
    for r in range(WORD_ROWS):
        lo = bf16_bits(t_ref[:, (2 * r) * LANES:(2 * r + 1) * LANES])
        hi = bf16_bits(t_ref[:, (2 * r + 1) * LANES:(2 * r + 2) * LANES])
        words = (hi & jnp.uint32(0xFFFF0000)) | (lo >> 16)
        o_ref[pl.ds(r, te, stride=WORD_ROWS), :] = pltpu.bitcast(words, I32)


def _pack_table(t, layer, te=512):
    _, E, D = t.shape
    return pl.pallas_call(
        _pack_kernel,
        grid=(E // te,),
        in_specs=[pl.BlockSpec((None, te, D), lambda i: (layer, i, 0))],
        out_specs=pl.BlockSpec((te * WORD_ROWS, LANES), lambda i: (i, 0)),
        out_shape=jax.ShapeDtypeStruct((E * WORD_ROWS, LANES), I32),
        compiler_params=_params(("parallel",)),
        name="pack_table",
    )(t)


def _block_diag(w):
    H, d, _ = w.shape
    out = jnp.zeros((H * d, H * d), w.dtype)
    for h in range(H):
        out = out.at[h * d:(h + 1) * d, h * d:(h + 1) * d].set(w[h])
    return out


def _cat_w_in(w):
    GW = GROUP_WIDTH
    off_f = 5 * GW
    off_sb = off_f + GROUP_HEADS
    off_sc = off_sb + 3 * GW
    pad = jnp.zeros((w.shape[0], LANES - GROUP_HEADS), w.dtype)
    return jnp.concatenate([w[:, :off_f], w[:, off_sb:off_sc], w[:, off_sc:N_IN],
                            w[:, off_f:off_sb], pad], axis=1).astype(BF16)


def _heads(t, B, S):
    t = t.reshape(B, S, 3, GROUP_HEADS, HEAD_DIM).transpose(2, 0, 3, 1, 4)
    return t[0], t[1], t[2]


def _row(v, width=None):
    v = v.reshape(1, -1).astype(F32)
    if width is not None and v.shape[1] < width:
        v = jnp.pad(v, ((0, 0), (0, width - v.shape[1])))
    return v


def _layer(x2d, kmem, vmem_, B, S, p, tables, layer, tq=256):
    T = B * S
    rg, fox, sb, sc, f = _inproj(x2d, _cat_w_in(p["w_in"]), B, S)
    yrg, ysc, cum, cumt = _rgsc(
        rg.reshape(B, S, -1), sc.reshape(B, S, -1), f.reshape(B, S, -1),
        p["rg_conv_w"], _row(p["rg_conv_b"]), _block_diag(p["rg_wa"]).astype(BF16), _row(p["rg_ba"]),
        _block_diag(p["rg_wi"]).astype(BF16), _row(p["rg_bi"]), _row(p["rg_lambda"]),
        p["sc_conv_w"], _row(p["fox_bf"], LANES), tq)
    yfox = _fox(fox, cum, cumt, tq=tq)
    ysb = _sb(sb, tq=tq)
    ys = (yrg.reshape(T, -1), yfox.reshape(T, -1), ysb.reshape(T, -1), ysc.reshape(T, -1))
    x1 = _mixout(ys, _row(p["mix_norm_g"]), p["w_out"].astype(BF16), x2d,
                 _row(p["ln1_g"]), _row(p["ln1_b"]))
    x2, x2c = _xattn(x1, p["xa_wq"].astype(BF16), kmem, vmem_, p["xa_wo"].astype(BF16),
                     _row(p["ln2_g"]), _row(p["ln2_b"]), S)
    ids, gate = _route(x2, p["peer_wq"].astype(BF16), p["peer_k1"].astype(BF16),
                       p["peer_k2"].astype(BF16))
    ffc = _peer_experts(ids, gate, x2c, _pack_table(tables[0], layer), _pack_table(tables[1], layer))
    return _resln(x2c, ffc, _row(p["ln3_g"]), _row(p["ln3_b"]))


_LAYER_PARAMS = ("w_in", "w_out", "rg_conv_w", "rg_conv_b", "rg_wa", "rg_ba", "rg_wi", "rg_bi",
                 "rg_lambda", "fox_bf", "sc_conv_w", "mix_norm_g", "ln1_g", "ln1_b", "xa_wq",
                 "xa_wkv", "xa_wo", "ln2_g", "ln2_b", "peer_wq", "peer_k1", "peer_k2", "peer_u",
                 "peer_v", "ln3_g", "ln3_b")


def kernel(x, mem, w_in, w_out, rg_conv_w, rg_conv_b, rg_wa, rg_ba, rg_wi, rg_bi, rg_lambda, fox_bf, sc_conv_w, mix_norm_g, ln1_g, ln1_b, xa_wq, xa_wkv, xa_wo, ln2_g, ln2_b, peer_wq, peer_k1, peer_k2, peer_u, peer_v, ln3_g, ln3_b):
    stacked = dict(zip(_LAYER_PARAMS, (
        w_in, w_out, rg_conv_w, rg_conv_b, rg_wa, rg_ba, rg_wi, rg_bi, rg_lambda, fox_bf,
        sc_conv_w, mix_norm_g, ln1_g, ln1_b, xa_wq, xa_wkv, xa_wo, ln2_g, ln2_b, peer_wq,
        peer_k1, peer_k2, peer_u, peer_v, ln3_g, ln3_b)))
    B, S, D = x.shape
    M = mem.shape[1]
    x2d = x.reshape(B * S, D)
    mem2d = mem.reshape(B * M, D)
    for l in range(w_in.shape[0]):
        p = {k: v[l] for k, v in stacked.items() if k not in ("peer_u", "peer_v")}
        kmem, vmem_ = _kv(mem2d, p["xa_wkv"].astype(BF16))
        x2d = _layer(x2d, kmem.reshape(B, M, D), vmem_.reshape(B, M, D), B, S, p,
                     (peer_u, peer_v), l)
    return x2d.reshape(B, S, D)
```

```python
import functools
import math

import jax
import jax.numpy as jnp
from jax import lax
from jax.experimental import pallas as pl
from jax.experimental.pallas import tpu as pltpu

F32 = jnp.float32
BF16 = jnp.bfloat16
I32 = jnp.int32

D_MODEL = 1024
GROUP_WIDTH = 256
GROUP_HEADS = 4
HEAD_DIM = 64
N_IN = 2820
RGLRU_C = 8.0
XA_HEADS = 4
XA_HEAD_DIM = D_MODEL // XA_HEADS
PEER_HEADS = 8
N_KEYS = 128
PEER_HALF = 128
PEER_TOPK = 16
DEPTH = 2
ALPHA = (2.0 * DEPTH) ** 0.25
LN_EPS = 1e-5

SUBLANES = 8
LANES = 128
WORD_ROWS = D_MODEL // (2 * LANES)
ROW_CHUNKS = D_MODEL // LANES
CHUNK_SHIFT = ROW_CHUNKS.bit_length() - 1

VMEM_LIMIT = 48 * 1024 * 1024


def _params(sem, vmem=VMEM_LIMIT):
    return pltpu.CompilerParams(dimension_semantics=sem, vmem_limit_bytes=vmem)


def _layer_norm(z, g, b):
    mu = jnp.mean(z, axis=-1, keepdims=True)
    zc = z - mu
    var = jnp.mean(zc * zc, axis=-1, keepdims=True)
    return zc * lax.rsqrt(var + LN_EPS) * g + b


def _log_sigmoid(z):
    return jnp.minimum(z, 0.0) - jnp.log1p(jnp.exp(-jnp.abs(z)))


def _nt_dot(a, b):
    return lax.dot_general(a, b, (((1,), (1,)), ((), ())), preferred_element_type=F32)


C_RG, C_FOX, C_SB, C_SC, C_F, C_END = 0, 512, 1280, 2048, 2816, 2944


def _inproj_kernel(x_ref, w_ref, rg_ref, fox_ref, sb_ref, sc_ref, f_ref):
    xb = x_ref[...].astype(BF16)

    def mm(lo, hi):
        return jnp.dot(xb, w_ref[:, lo:hi], preferred_element_type=F32)

    def heads(ref, lo, hi):
        qkv = mm(lo, hi)
        for j in range(3 * GROUP_HEADS):
            ref[0, j] = qkv[:, j * HEAD_DIM:(j + 1) * HEAD_DIM].astype(BF16)

    rg_ref[...] = mm(C_RG, C_FOX)
    heads(fox_ref, C_FOX, C_SB)
    heads(sb_ref, C_SB, C_SC)
    sc_ref[...] = mm(C_SC, C_F)
    f_ref[...] = mm(C_F, C_END)


def _inproj(x2d, w_cat, B, S, tm=512):
    T, D = x2d.shape
    per_seq = S // tm
    flat = ((C_FOX - C_RG, F32), (C_F - C_SC, F32), (C_END - C_F, F32))
    flat_specs = [pl.BlockSpec((tm, w), lambda i: (i, 0)) for w, _ in flat]
    flat_shapes = [jax.ShapeDtypeStruct((T, w), dt) for w, dt in flat]
    head_spec = pl.BlockSpec((1, 3 * GROUP_HEADS, tm, HEAD_DIM),
                             lambda i: (i // per_seq, 0, i % per_seq, 0))
    head_shape = jax.ShapeDtypeStruct((B, 3 * GROUP_HEADS, S, HEAD_DIM), BF16)
    return pl.pallas_call(
        _inproj_kernel,
        grid=(T // tm,),
        in_specs=[pl.BlockSpec((tm, D), lambda i: (i, 0)),
                  pl.BlockSpec((D, C_END), lambda i: (0, 0))],
        out_specs=[flat_specs[0], head_spec, head_spec, flat_specs[1], flat_specs[2]],
        out_shape=[flat_shapes[0], head_shape, head_shape, flat_shapes[1], flat_shapes[2]],
        compiler_params=_params(("parallel",)),
        name="inproj",
    )(x2d, w_cat)


def _rgsc_kernel(rg_ref, sc_ref, f_ref, cw_ref, cb_ref, wa_ref, ba_ref, wi_ref, bi_ref, lam_ref,
                 scw_ref, fb_ref, yrg_ref, ysc_ref, cum_ref, cumt_ref, xprev, chprev, hprev, cprev,
                 *, ts, tk):
    GW = GROUP_WIDTH

    @pl.when(pl.program_id(1) == 0)
    def _():
        xprev[...] = jnp.zeros_like(xprev)
        chprev[...] = jnp.zeros_like(chprev)
        hprev[...] = jnp.zeros_like(hprev)
        cprev[...] = jnp.zeros_like(cprev)

    row = lax.broadcasted_iota(I32, (ts, GW), 0)

    def delayed(prev, cur, d):
        ext = jnp.concatenate([prev, cur], axis=0)
        return pltpu.roll(ext, d, 0)[SUBLANES:]

    xr = rg_ref[0, :, :GW]
    gate = rg_ref[0, :, GW:]
    xp = xprev[...]
    cw = cw_ref[...]
    xc = (delayed(xp, xr, 3) * cw[0:1] + delayed(xp, xr, 2) * cw[1:2]
          + delayed(xp, xr, 1) * cw[2:3] + xr * cw[3:4] + cb_ref[...])
    xprev[...] = xr[ts - SUBLANES:]
    xcb = xc.astype(BF16)
    r = jax.nn.sigmoid(jnp.dot(xcb, wa_ref[...], preferred_element_type=F32) + ba_ref[...])
    ig = jax.nn.sigmoid(jnp.dot(xcb, wi_ref[...], preferred_element_type=F32) + bi_ref[...])
    z = -lam_ref[...]
    softplus = jnp.maximum(z, 0.0) + jnp.log1p(jnp.exp(-jnp.abs(z)))
    log_a = -RGLRU_C * r * softplus
    a = jnp.exp(log_a)
    u = jnp.sqrt(-jnp.tanh(log_a) * (a * a + 1.0)) * (ig * xc)
    acc_a, acc_b = a, u
    d = 1
    while d < ts:
        keep = row >= d
        a_s = jnp.where(keep, pltpu.roll(acc_a, d, 0), 1.0)
        b_s = jnp.where(keep, pltpu.roll(acc_b, d, 0), 0.0)
        acc_b = acc_a * b_s + acc_b
        acc_a = acc_a * a_s
        d *= 2
    h = acc_b + acc_a * hprev[...]
    hprev[...] = h[ts - 1:]
    c0 = math.sqrt(2.0 / math.pi)
    gelu = 0.5 * gate * (1.0 + jnp.tanh(c0 * (gate + 0.044715 * gate * gate * gate)))
    yrg_ref[0] = h * gelu

    bg = sc_ref[0, :, :GW]
    ch = sc_ref[0, :, GW:2 * GW] * sc_ref[0, :, 2 * GW:]
    cp = chprev[...]
    sw = scw_ref[...]
    ysc_ref[0] = bg * (delayed(cp, ch, 2) * sw[0:1] + delayed(cp, ch, 1) * sw[1:2] + ch * sw[2:3])
    chprev[...] = ch[ts - SUBLANES:]

    rowf = lax.broadcasted_iota(I32, (ts, LANES), 0)
    c = _log_sigmoid(f_ref[0] + fb_ref[...])
    d = 1
    while d < ts:
        c = c + jnp.where(rowf >= d, pltpu.roll(c, d, 0), 0.0)
        d *= 2
    c = c + cprev[...]
    cprev[...] = c[ts - 1:]
    cum_ref[0] = c
    ct = c.T[:SUBLANES]
    for j in range(ts // tk):
        cumt_ref[0, j] = ct[:, j * tk:(j + 1) * tk]


def _rgsc(rg, sc, f, cw, cb, wa, ba, wi, bi, lam, scw, fb, tk, ts=512):
    B, S, _ = rg.shape
    GW = GROUP_WIDTH
    per = ts // tk

    def full(a):
        return pl.BlockSpec(a.shape, lambda b, t: (0,) * a.ndim)

    def seq(w):
        return pl.BlockSpec((1, ts, w), lambda b, t: (b, t, 0))

    params = (cw, cb, wa, ba, wi, bi, lam, scw, fb)
    return pl.pallas_call(
        functools.partial(_rgsc_kernel, ts=ts, tk=tk),
        grid=(B, S // ts),
        in_specs=[seq(2 * GW), seq(3 * GW), seq(LANES)] + [full(p) for p in params],
        out_specs=[seq(GW), seq(GW), seq(LANES),
                   pl.BlockSpec((1, per, SUBLANES, tk), lambda b, t: (b, t, 0, 0))],
        out_shape=[jax.ShapeDtypeStruct((B, S, GW), F32), jax.ShapeDtypeStruct((B, S, GW), F32),
                   jax.ShapeDtypeStruct((B, S, LANES), F32),
                   jax.ShapeDtypeStruct((B, S // tk, SUBLANES, tk), F32)],
        scratch_shapes=[pltpu.VMEM((SUBLANES, GW), F32), pltpu.VMEM((SUBLANES, GW), F32),
                        pltpu.VMEM((1, GW), F32), pltpu.VMEM((1, LANES), F32)],
        compiler_params=_params(("parallel", "arbitrary")),
        name="rgsc",
    )(rg, sc, f, *params)


def _fox_kernel(q_ref, k_ref, v_ref, cc_ref, cr_ref, o_ref, *, tq):
    qi = pl.program_id(1)
    scale = HEAD_DIM ** -0.5
    rowi = lax.broadcasted_iota(I32, (tq, tq), 0)
    coli = lax.broadcasted_iota(I32, (tq, tq), 1)
    causal = coli <= rowi
    def head_step(h, ki, carry, diag):
        m, l, acc = carry
        off = pl.multiple_of(ki * tq, tq)
        k = k_ref[0, h, pl.ds(off, tq), :]
        v = v_ref[0, h, pl.ds(off, tq), :]
        s = (_nt_dot(q_ref[0, h], k) * scale + cc_ref[0, :, h:h + 1]
             - cr_ref[0, ki, h:h + 1, :])
        if diag:
            s = jnp.where(causal, s, -jnp.inf)
        m_new = jnp.maximum(m, jnp.max(s, axis=-1, keepdims=True))
        alpha = jnp.exp(m - m_new)
        p = jnp.exp(s - m_new)
        l = alpha * l + jnp.sum(p, axis=-1, keepdims=True)
        acc = alpha * acc + jnp.dot(p.astype(BF16), v, preferred_element_type=F32)
        return m_new, l, acc

    def step(ki, carries, diag):
        return tuple(head_step(h, ki, carries[h], diag) for h in range(GROUP_HEADS))

    init = (jnp.full((tq, 1), -jnp.inf, F32), jnp.zeros((tq, 1), F32),
            jnp.zeros((tq, HEAD_DIM), F32))
    carries = lax.fori_loop(0, qi, functools.partial(step, diag=False), (init,) * GROUP_HEADS)
    carries = step(qi, carries, True)
    o_ref[0] = jnp.concatenate([acc / l for _, l, acc in carries], axis=-1)


def _qkv_specs(S, tq):
    H, d = GROUP_HEADS, HEAD_DIM
    return [pl.BlockSpec((1, H, tq, d), lambda b, i: (b, 0, i, 0)),
            pl.BlockSpec((1, H, S, d), lambda b, i: (b, 1, 0, 0)),
            pl.BlockSpec((1, H, S, d), lambda b, i: (b, 2, 0, 0))]


def _fox(qkv, cum, cumt, tq=256):
    B, _, S, d = qkv.shape
    H = GROUP_HEADS
    return pl.pallas_call(
        functools.partial(_fox_kernel, tq=tq),
        grid=(B, S // tq),
        in_specs=_qkv_specs(S, tq)
        + [pl.BlockSpec((1, tq, LANES), lambda b, i: (b, i, 0)),
           pl.BlockSpec((1, S // tq, SUBLANES, tq), lambda b, i: (b, 0, 0, 0))],
        out_specs=pl.BlockSpec((1, tq, H * d), lambda b, i: (b, i, 0)),
        out_shape=jax.ShapeDtypeStruct((B, S, H * d), F32),
        compiler_params=_params(("parallel", "arbitrary")),
        name="fox_attn",
    )(qkv, qkv, qkv, cum, cumt)


EXP_UNDERFLOW = -104.0


def _sb_kernel(q_ref, k_ref, v_ref, o_ref, *, tq):
    qi = pl.program_id(1)
    scale = HEAD_DIM ** -0.5
    rowi = lax.broadcasted_iota(I32, (tq, tq), 0)
    coli = lax.broadcasted_iota(I32, (tq, tq), 1)
    strict = coli < rowi
    later = jnp.where(rowi > coli, 1.0, 0.0).astype(BF16)
    def head_step(h, ki, carry, diag):
        rest, acc = carry
        off = pl.multiple_of(ki * tq, tq)
        k = k_ref[0, h, pl.ds(off, tq), :]
        v = v_ref[0, h, pl.ds(off, tq), :]
        z = _nt_dot(q_ref[0, h], k) * scale
        ls = _log_sigmoid(z)
        l1m = ls - z
        if diag:
            l1m = jnp.where(strict, l1m, 0.0)
        hi = l1m.astype(BF16)
        lo = (l1m - hi.astype(F32)).astype(BF16)
        tail = (jnp.dot(hi, later, preferred_element_type=F32)
                + jnp.dot(lo, later, preferred_element_type=F32) + rest)
        w = jnp.exp(ls + tail)
        if diag:
            w = jnp.where(strict, w, 0.0)
        acc = acc + jnp.dot(w.astype(BF16), v, preferred_element_type=F32)
        rest = rest + jnp.sum(l1m, axis=-1, keepdims=True)
        return rest, acc

    def step(ki, carries, diag):
        return tuple(head_step(h, ki, carries[h], diag) for h in range(GROUP_HEADS))

    init = (jnp.zeros((tq, 1), F32), jnp.zeros((tq, HEAD_DIM), F32))
    carries = step(qi, (init,) * GROUP_HEADS, True)

    def more(c):
        j, carries = c
        top = carries[0][0]
        for rest, _ in carries[1:]:
            top = jnp.maximum(top, rest)
        return jnp.logical_and(j < qi, jnp.max(top) > EXP_UNDERFLOW)

    def further(c):
        j, carries = c
        return j + 1, step(qi - 1 - j, carries, False)

    _, carries = lax.while_loop(more, further, (jnp.int32(0), carries))
    o_ref[0] = jnp.concatenate([acc for _, acc in carries], axis=-1)


def _sb(qkv, tq=256):
    B, _, S, d = qkv.shape
    H = GROUP_HEADS
    return pl.pallas_call(
        functools.partial(_sb_kernel, tq=tq),
        grid=(B, S // tq),
        in_specs=_qkv_specs(S, tq),
        out_specs=pl.BlockSpec((1, tq, H * d), lambda b, i: (b, i, 0)),
        out_shape=jax.ShapeDtypeStruct((B, S, H * d), F32),
        compiler_params=_params(("parallel", "arbitrary")),
        name="sb_attn",
    )(qkv, qkv, qkv)


def _mixout_kernel(y0_ref, y1_ref, y2_ref, y3_ref, g_ref, w_ref, x_ref, lg_ref, lb_ref, o_ref):
    def rms(y):
        return y * lax.rsqrt(jnp.mean(y * y, axis=-1, keepdims=True) + 1e-6)

    y = jnp.concatenate([rms(r[...]) for r in (y0_ref, y1_ref, y2_ref, y3_ref)], axis=-1)
    y = (y * g_ref[...]).astype(BF16)
    mix = jnp.dot(y, w_ref[...], preferred_element_type=F32)
    o_ref[...] = _layer_norm(ALPHA * x_ref[...] + mix, lg_ref[...], lb_ref[...])


def _mixout(ys, g, w, x2d, lg, lb, tm=512):
    T, D = x2d.shape
    GW = GROUP_WIDTH

    def full(a):
        return pl.BlockSpec(a.shape, lambda i: (0,) * a.ndim)

    return pl.pallas_call(
        _mixout_kernel,
        grid=(T // tm,),
        in_specs=[pl.BlockSpec((tm, GW), lambda i: (i, 0))] * 4
        + [full(g), full(w), pl.BlockSpec((tm, D), lambda i: (i, 0)), full(lg), full(lb)],
        out_specs=pl.BlockSpec((tm, D), lambda i: (i, 0)),
        out_shape=jax.ShapeDtypeStruct((T, D), F32),
        compiler_params=_params(("parallel",)),
        name="mixout",
    )(*ys, g, w, x2d, lg, lb)


def _kv_kernel(m_ref, w_ref, k_ref, v_ref):
    kv = jnp.dot(m_ref[...].astype(BF16), w_ref[...], preferred_element_type=F32)
    k_ref[...] = kv[:, :D_MODEL].astype(BF16)
    v_ref[...] = kv[:, D_MODEL:].astype(BF16)


def _kv(mem2d, wkv, tm=512):
    M, D = mem2d.shape
    return pl.pallas_call(
        _kv_kernel,
        grid=(M // tm,),
        in_specs=[pl.BlockSpec((tm, D), lambda i: (i, 0)),
                  pl.BlockSpec((D, 2 * D), lambda i: (0, 0))],
        out_specs=[pl.BlockSpec((tm, D), lambda i: (i, 0))] * 2,
        out_shape=[jax.ShapeDtypeStruct((M, D), BF16)] * 2,
        compiler_params=_params(("parallel",)),
        name="mem_kv",
    )(mem2d, wkv)


PEER_TB = 64


def _chunk_major_spec(tm):
    return pl.BlockSpec((tm // PEER_TB, ROW_CHUNKS * PEER_TB, LANES), lambda i: (i, 0, 0))


def _chunk_major_shape(T):
    return jax.ShapeDtypeStruct((T // PEER_TB, ROW_CHUNKS * PEER_TB, LANES), F32)


def _to_chunk_major(y, ref):
    for g in range(y.shape[0] // PEER_TB):
        for c in range(ROW_CHUNKS):
            ref[g, c * PEER_TB:(c + 1) * PEER_TB, :] = (
                y[g * PEER_TB:(g + 1) * PEER_TB, c * LANES:(c + 1) * LANES])


def _xattn_kernel(x_ref, wq_ref, k_ref, v_ref, wo_ref, lg_ref, lb_ref, o_ref, oc_ref):
    x = x_ref[...]
    q = jnp.dot(x.astype(BF16), wq_ref[...], preferred_element_type=F32).astype(BF16)
    scale = XA_HEAD_DIM ** -0.5
    outs = []
    for h in range(XA_HEADS):
        sl = slice(h * XA_HEAD_DIM, (h + 1) * XA_HEAD_DIM)
        s = _nt_dot(q[:, sl], k_ref[0, :, sl]) * scale
        p = jnp.exp(s - jnp.max(s, axis=-1, keepdims=True))
        p = p / jnp.sum(p, axis=-1, keepdims=True)
        outs.append(jnp.dot(p.astype(BF16), v_ref[0, :, sl], preferred_element_type=F32))
    o = jnp.concatenate(outs, axis=-1).astype(BF16)
    xa = jnp.dot(o, wo_ref[...], preferred_element_type=F32)
    y = _layer_norm(ALPHA * x + xa, lg_ref[...], lb_ref[...])
    o_ref[...] = y
    _to_chunk_major(y, oc_ref)


def _xattn(x2d, wq, k, v, wo, lg, lb, seq_len, tm=512):
    T, D = x2d.shape
    M = k.shape[1]
    per_seq = seq_len // tm

    def full(a):
        return pl.BlockSpec(a.shape, lambda i: (0,) * a.ndim)

    return pl.pallas_call(
        _xattn_kernel,
        grid=(T // tm,),
        in_specs=[pl.BlockSpec((tm, D), lambda i: (i, 0)), full(wq),
                  pl.BlockSpec((1, M, D), lambda i: (i // per_seq, 0, 0)),
                  pl.BlockSpec((1, M, D), lambda i: (i // per_seq, 0, 0)),
                  full(wo), full(lg), full(lb)],
        out_specs=[pl.BlockSpec((tm, D), lambda i: (i, 0)), _chunk_major_spec(tm)],
        out_shape=[jax.ShapeDtypeStruct((T, D), F32), _chunk_major_shape(T)],
        compiler_params=_params(("parallel",)),
        name="xattn",
    )(x2d, wq, k, v, wo, lg, lb)


E_PER_TOK = PEER_HEADS * PEER_TOPK


def _staircase():
    return [(a, b) for a in range(PEER_TOPK) for b in range(PEER_TOPK // (a + 1))]


def _route_kernel(x_ref, wq_ref, k1_ref, k2_ref, e_ref, g_ref, v_scr, i_scr, *, ts):
    xb = x_ref[...].astype(BF16)
    key_id = lax.broadcasted_iota(I32, (N_KEYS, ts), 0).astype(F32)
    qd = 2 * PEER_HALF
    for h in range(PEER_HEADS):
        qry = jnp.dot(xb, wq_ref[:, h * qd:(h + 1) * qd], preferred_element_type=F32)
        for half, kref in ((0, k1_ref), (1, k2_ref)):
            qh = qry[:, half * PEER_HALF:(half + 1) * PEER_HALF].astype(BF16)
            s0 = _nt_dot(kref[...], qh)

            def pick(it, s, h=h, half=half):
                m = jnp.max(s, axis=0, keepdims=True)
                idx = jnp.min(jnp.where(s == m, key_id, float(N_KEYS)), axis=0, keepdims=True)
                v_scr[half, it, h:h + 1, :] = m
                i_scr[half, it, h:h + 1, :] = idx.astype(I32)
                return jnp.where(key_id == idx, -jnp.inf, s)

            lax.fori_loop(0, PEER_TOPK, pick, s0)

    cells = _staircase()
    cand = tuple(v_scr[0, a] + v_scr[1, b] for a, b in cells)
    expert = [i_scr[0, a] * N_KEYS + i_scr[1, b] for a, b in cells]

    def select(it, carry):
        cand, top = carry
        best_v, best_e, best_c = cand[0], expert[0], jnp.zeros((PEER_HEADS, ts), I32)
        for c in range(1, len(cells)):
            better = cand[c] > best_v
            best_v = jnp.where(better, cand[c], best_v)
            best_e = jnp.where(better, expert[c], best_e)
            best_c = jnp.where(better, c, best_c)
        top = jnp.where(it == 0, best_v, top)
        i_scr[0, it] = best_e * WORD_ROWS
        v_scr[0, it] = best_v
        cand = tuple(jnp.where(best_c == c, -jnp.inf, cand[c]) for c in range(len(cells)))
        return cand, top

    _, top = lax.fori_loop(0, PEER_TOPK, select, (cand, jnp.zeros((PEER_HEADS, ts), F32)))
    ex = [jnp.exp(v_scr[0, it] - top) for it in range(PEER_TOPK)]
    den = ex[0]
    for it in range(1, PEER_TOPK):
        den = den + ex[it]
    inv = 1.0 / den
    g_ref[...] = jnp.concatenate([e * inv for e in ex], axis=0).T
    rows = pltpu.bitcast(i_scr[0].reshape(E_PER_TOK, ts), F32)
    e_ref[...] = pltpu.bitcast(rows.T, I32)


def _route(x2d, wq, k1, k2, ts=256):
    T, D = x2d.shape
    nb = T // ts

    def full(a):
        return pl.BlockSpec(a.shape, lambda i: (0,) * a.ndim)

    blk = pl.BlockSpec((ts, E_PER_TOK), lambda i: (i, 0))
    return pl.pallas_call(
        functools.partial(_route_kernel, ts=ts),
        grid=(nb,),
        in_specs=[pl.BlockSpec((ts, D), lambda i: (i, 0)), full(wq), full(k1), full(k2)],
        out_specs=[blk, blk],
        out_shape=[jax.ShapeDtypeStruct((T, E_PER_TOK), I32),
                   jax.ShapeDtypeStruct((T, E_PER_TOK), F32)],
        scratch_shapes=[pltpu.VMEM((2, PEER_TOPK, PEER_HEADS, ts), F32),
                        pltpu.VMEM((2, PEER_TOPK, PEER_HEADS, ts), I32)],
        compiler_params=_params(("parallel",)),
        name="peer_route",
    )(x2d, wq, k1, k2)


TOK_GROUP = SUBLANES


N_GROUPS = PEER_TB // TOK_GROUP
SCORE_BUFS = 4


def _gather_rows(ids_ref, tab_ref, rows_scr, tok0):
    for tt in range(TOK_GROUP):
        for j in range(E_PER_TOK):
            r0 = pl.multiple_of(ids_ref[tok0 + tt, j], WORD_ROWS)
            rows_scr[tt, j * WORD_ROWS:(j + 1) * WORD_ROWS, :] = tab_ref[pl.ds(r0, WORD_ROWS), :]


def _token_chunks(tt):
    return pl.ds(tt, ROW_CHUNKS, stride=PEER_TB)


def _chunk_of_col():
    col = lax.broadcasted_iota(I32, (ROW_CHUNKS, E_PER_TOK * ROW_CHUNKS), 1)
    row = lax.broadcasted_iota(I32, (ROW_CHUNKS, E_PER_TOK * ROW_CHUNKS), 0)
    return (col & (ROW_CHUNKS - 1)) == row


def _peer_score_kernel(ids_ref, x_ref, tab_ref, s_ref, *scratch):
    rows_bufs, part_bufs = scratch[:SCORE_BUFS], scratch[SCORE_BUFS:]
    own_chunk = _chunk_of_col()
    c_id = lax.broadcasted_iota(I32, (E_PER_TOK * ROW_CHUNKS, E_PER_TOK), 0)
    e_id = lax.broadcasted_iota(I32, (E_PER_TOK * ROW_CHUNKS, E_PER_TOK), 1)
    fold = jnp.where(c_id >> CHUNK_SHIFT == e_id, 1.0, 0.0).astype(BF16)

    for g in range(N_GROUPS):
        tok0 = g * TOK_GROUP
        rows_scr, part_scr = rows_bufs[g % SCORE_BUFS], part_bufs[g % SCORE_BUFS]
        _gather_rows(ids_ref, tab_ref, rows_scr, tok0)
        for tt in range(TOK_GROUP):
            rows = pltpu.bitcast(rows_scr[tt], BF16)
            xt = x_ref[0, _token_chunks(tok0 + tt), :].astype(BF16)
            full = _nt_dot(xt, rows)
            part_scr[tt * ROW_CHUNKS:(tt + 1) * ROW_CHUNKS, :] = jnp.where(own_chunk, full, 0.0)
        part = part_scr[...]
        hi = part.astype(BF16)
        lo = (part - hi.astype(F32)).astype(BF16)
        sc = (jnp.dot(hi, fold, preferred_element_type=F32)
              + jnp.dot(lo, fold, preferred_element_type=F32))
        sc = jnp.sum(sc.reshape(TOK_GROUP, ROW_CHUNKS, E_PER_TOK), axis=1)
        s_ref[tok0:tok0 + TOK_GROUP, :] = sc


def _peer_mix_kernel(ids_ref, s_ref, gate_ref, tab_ref, o_ref, *rows_bufs):
    own_chunk = _chunk_of_col()
    e_id = lax.broadcasted_iota(I32, (E_PER_TOK, E_PER_TOK * ROW_CHUNKS), 0)
    c_id = lax.broadcasted_iota(I32, (E_PER_TOK, E_PER_TOK * ROW_CHUNKS), 1)
    spread = jnp.where(c_id >> CHUNK_SHIFT == e_id, 1.0, 0.0).astype(BF16)

    for g in range(N_GROUPS):
        tok0 = g * TOK_GROUP
        sl = slice(tok0, tok0 + TOK_GROUP)
        rows_scr = rows_bufs[g % len(rows_bufs)]
        _gather_rows(ids_ref, tab_ref, rows_scr, tok0)
        s = s_ref[sl, :]
        act = 0.5 * s * (1.0 + lax.erf(s * (2.0 ** -0.5)))
        coef = (gate_ref[sl, :] * act).astype(BF16)
        coef = jnp.dot(coef, spread, preferred_element_type=F32)
        for tt in range(TOK_GROUP):
            rows = pltpu.bitcast(rows_scr[tt], BF16)
            ct = jnp.broadcast_to(coef[tt:tt + 1], (ROW_CHUNKS, E_PER_TOK * ROW_CHUNKS))
            ct = jnp.where(own_chunk, ct, 0.0).astype(BF16)
            o_ref[0, _token_chunks(tok0 + tt), :] = jnp.dot(ct, rows, preferred_element_type=F32)


def _peer_experts(ids, gate, xc, tab_u, tab_v):
    T = ids.shape[0]
    tb = PEER_TB
    nb = T // tb
    ids_spec = pl.BlockSpec((tb, E_PER_TOK), lambda i: (i, 0), memory_space=pltpu.SMEM)
    tab_spec = pl.BlockSpec(tab_u.shape, lambda i: (0, 0), pipeline_mode=pl.Buffered(1))
    tok_spec = pl.BlockSpec((tb, E_PER_TOK), lambda i: (i, 0))
    row_spec = _chunk_major_spec(tb)
    rows_scr = [pltpu.VMEM((TOK_GROUP, E_PER_TOK * WORD_ROWS, LANES), I32)] * SCORE_BUFS
    part_scr = [pltpu.VMEM((TOK_GROUP * ROW_CHUNKS, E_PER_TOK * ROW_CHUNKS), F32)] * SCORE_BUFS
    s = pl.pallas_call(
        _peer_score_kernel,
        grid=(nb,),
        in_specs=[ids_spec, row_spec, tab_spec],
        out_specs=tok_spec,
        out_shape=jax.ShapeDtypeStruct((T, E_PER_TOK), F32),
        scratch_shapes=rows_scr + part_scr,
        compiler_params=_params(("arbitrary",)),
        name="peer_score",
    )(ids, xc, tab_u)
    return pl.pallas_call(
        _peer_mix_kernel,
        grid=(nb,),
        in_specs=[ids_spec, tok_spec, tok_spec, tab_spec],
        out_specs=row_spec,
        out_shape=_chunk_major_shape(T),
        scratch_shapes=rows_scr[:1],
        compiler_params=_params(("arbitrary",)),
        name="peer_mix",
    )(ids, s, gate, tab_v)


def _resln_kernel(x_ref, y_ref, lg_ref, lb_ref, o_ref):
    G = x_ref.shape[0]
    z = ALPHA * x_ref[...] + y_ref[...]
    zs = [z[:, c * PEER_TB:(c + 1) * PEER_TB, :] for c in range(ROW_CHUNKS)]
    tot = zs[0]
    for c in range(1, ROW_CHUNKS):
        tot = tot + zs[c]
    mu = jnp.sum(tot, axis=-1, keepdims=True) * (1.0 / D_MODEL)
    zc = [a - mu for a in zs]
    sq = zc[0] * zc[0]
    for c in range(1, ROW_CHUNKS):
        sq = sq + zc[c] * zc[c]
    rstd = lax.rsqrt(jnp.sum(sq, axis=-1, keepdims=True) * (1.0 / D_MODEL) + LN_EPS)
    for c in range(ROW_CHUNKS):
        cols = slice(c * LANES, (c + 1) * LANES)
        y = (zc[c] * rstd).reshape(G * PEER_TB, LANES)
        o_ref[:, cols] = y * lg_ref[:, cols] + lb_ref[:, cols]


def _resln(xc, yc, lg, lb, tm=512):
    T = xc.shape[0] * PEER_TB
    blk = _chunk_major_spec(tm)
    par = pl.BlockSpec((1, D_MODEL), lambda i: (0, 0))
    return pl.pallas_call(
        _resln_kernel,
        grid=(T // tm,),
        in_specs=[blk, blk, par, par],
        out_specs=pl.BlockSpec((tm, D_MODEL), lambda i: (i, 0)),
        out_shape=jax.ShapeDtypeStruct((T, D_MODEL), F32),
        compiler_params=_params(("parallel",)),
        name="res_ln",
    )(xc, yc, lg, lb)


def _pack_kernel(t_ref, o_ref):
    def bf16_bits(v):
        return pltpu.bitcast(v.astype(BF16).astype(F32), jnp.uint32)

    te = t_ref.shape[0]
    for r in range(WORD_ROWS):
        lo = bf16_bits(t_ref[:, (2 * r) * LANES:(2 * r + 1) * LANES])
        hi = bf16_bits(t_ref[:, (2 * r + 1) * LANES:(2 * r + 2) * LANES])
        words = (hi & jnp.uint32(0xFFFF0000)) | (lo >> 16)
        o_ref[pl.ds(r, te, stride=WORD_ROWS), :] = pltpu.bitcast(words, I32)


def _pack_table(t, layer, te=512):
    _, E, D = t.shape
    return pl.pallas_call(
        _pack_kernel,
        grid=(E // te,),
        in_specs=[pl.BlockSpec((None, te, D), lambda i: (layer, i, 0))],
        out_specs=pl.BlockSpec((te * WORD_ROWS, LANES), lambda i: (i, 0)),
        out_shape=jax.ShapeDtypeStruct((E * WORD_ROWS, LANES), I32),
        compiler_params=_params(("parallel",)),
        name="pack_table",
    )(t)


def _block_diag(w):
    H, d, _ = w.shape
    out = jnp.zeros((H * d, H * d), w.dtype)
    for h in range(H):
        out = out.at[h * d:(h + 1) * d, h * d:(h + 1) * d].set(w[h])
    return out


def _cat_w_in_kernel(w_ref, o_ref):
    GW = GROUP_WIDTH
    off_f = 5 * GW
    off_sb = off_f + GROUP_HEADS
    w = w_ref[...].astype(BF16)
    o_ref[:, :off_f] = w[:, :off_f]
    o_ref[:, off_f:C_F] = w[:, off_sb:N_IN]
    o_ref[:, C_F:C_END] = jnp.zeros((w.shape[0], C_END - C_F), BF16)
    o_ref[:, C_F:C_F + GROUP_HEADS] = w[:, off_f:off_sb]


def _cat_w_in(w_in, layer, tr=256):
    _, D, N = w_in.shape
    return pl.pallas_call(
        _cat_w_in_kernel,
        grid=(D // tr,),
        in_specs=[pl.BlockSpec((None, tr, N), lambda i: (layer, i, 0))],
        out_specs=pl.BlockSpec((tr, C_END), lambda i: (i, 0)),
        out_shape=jax.ShapeDtypeStruct((D, C_END), BF16),
        compiler_params=_params(("parallel",)),
        name="regroup_w_in",
    )(w_in)


def _heads(t, B, S):
    t = t.reshape(B, S, 3, GROUP_HEADS, HEAD_DIM).transpose(2, 0, 3, 1, 4)
    return t[0], t[1], t[2]


def _row(v, width=None):
    v = v.reshape(1, -1).astype(F32)
    if width is not None and v.shape[1] < width:
        v = jnp.pad(v, ((0, 0), (0, width - v.shape[1])))
    return v


def _layer(x2d, kmem, vmem_, B, S, p, tables, layer, tq=256):
    T = B * S
    rg, fox, sb, sc, f = _inproj(x2d, _cat_w_in(tables[2], layer), B, S)
    yrg, ysc, cum, cumt = _rgsc(
        rg.reshape(B, S, -1), sc.reshape(B, S, -1), f.reshape(B, S, -1),
        p["rg_conv_w"], _row(p["rg_conv_b"]), _block_diag(p["rg_wa"]).astype(BF16), _row(p["rg_ba"]),
        _block_diag(p["rg_wi"]).astype(BF16), _row(p["rg_bi"]), _row(p["rg_lambda"]),
        p["sc_conv_w"], _row(p["fox_bf"], LANES), tq)
    yfox = _fox(fox, cum, cumt, tq=tq)
    ysb = _sb(sb, tq=tq)
    ys = (yrg.reshape(T, -1), yfox.reshape(T, -1), ysb.reshape(T, -1), ysc.reshape(T, -1))
    x1 = _mixout(ys, _row(p["mix_norm_g"]), p["w_out"].astype(BF16), x2d,
                 _row(p["ln1_g"]), _row(p["ln1_b"]))
    x2, x2c = _xattn(x1, p["xa_wq"].astype(BF16), kmem, vmem_, p["xa_wo"].astype(BF16),
                     _row(p["ln2_g"]), _row(p["ln2_b"]), S)
    ids, gate = _route(x2, p["peer_wq"].astype(BF16), p["peer_k1"].astype(BF16),
                       p["peer_k2"].astype(BF16))
    ffc = _peer_experts(ids, gate, x2c, _pack_table(tables[0], layer), _pack_table(tables[1], layer))
    return _resln(x2c, ffc, _row(p["ln3_g"]), _row(p["ln3_b"]))


_LAYER_PARAMS = ("w_in", "w_out", "rg_conv_w", "rg_conv_b", "rg_wa", "rg_ba", "rg_wi", "rg_bi",
                 "rg_lambda", "fox_bf", "sc_conv_w", "mix_norm_g", "ln1_g", "ln1_b", "xa_wq",
                 "xa_wkv", "xa_wo", "ln2_g", "ln2_b", "peer_wq", "peer_k1", "peer_k2", "peer_u",
                 "peer_v", "ln3_g", "ln3_b")


def kernel(x, mem, w_in, w_out, rg_conv_w, rg_conv_b, rg_wa, rg_ba, rg_wi, rg_bi, rg_lambda, fox_bf, sc_conv_w, mix_norm_g, ln1_g, ln1_b, xa_wq, xa_wkv, xa_wo, ln2_g, ln2_b, peer_wq, peer_k1, peer_k2, peer_u, peer_v, ln3_g, ln3_b):
    stacked = dict(zip(_LAYER_PARAMS, (
        w_in, w_out, rg_conv_w, rg_conv_b, rg_wa, rg_ba, rg_wi, rg_bi, rg_lambda, fox_bf,
        sc_conv_w, mix_norm_g, ln1_g, ln1_b, xa_wq, xa_wkv, xa_wo, ln2_g, ln2_b, peer_wq,
        peer_k1, peer_k2, peer_u, peer_v, ln3_g, ln3_b)))
    B, S, D = x.shape
    M = mem.shape[1]
    x2d = x.reshape(B * S, D)
    mem2d = mem.reshape(B * M, D)
    for l in range(w_in.shape[0]):
        p = {k: v[l] for k, v in stacked.items() if k not in ("peer_u", "peer_v", "w_in")}
        kmem, vmem_ = _kv(mem2d, p["xa_wkv"].astype(BF16))
        x2d = _layer(x2d, kmem.reshape(B, M, D), vmem_.reshape(B, M, D), B, S, p,
                     (peer_u, peer_v, w_in), l)
    return x2d.reshape(B, S, D)
```

```python
import functools
import math

import jax
import jax.numpy as jnp
from jax import lax
from jax.experimental import pallas as pl
from jax.experimental.pallas import tpu as pltpu

F32 = jnp.float32
BF16 = jnp.bfloat16
I32 = jnp.int32

D_MODEL = 1024
GROUP_WIDTH = 256
GROUP_HEADS = 4
HEAD_DIM = 64
N_IN = 2820
RGLRU_C = 8.0
XA_HEADS = 4
XA_HEAD_DIM = D_MODEL // XA_HEADS
PEER_HEADS = 8
N_KEYS = 128
PEER_HALF = 128
PEER_TOPK = 16
DEPTH = 2
ALPHA = (2.0 * DEPTH) ** 0.25
LN_EPS = 1e-5

SUBLANES = 8
LANES = 128
WORD_ROWS = D_MODEL // (2 * LANES)
ROW_CHUNKS = D_MODEL // LANES
CHUNK_SHIFT = ROW_CHUNKS.bit_length() - 1

VMEM_LIMIT = 48 * 1024 * 1024


def _params(sem, vmem=VMEM_LIMIT):
    return pltpu.CompilerParams(dimension_semantics=sem, vmem_limit_bytes=vmem)


def _layer_norm(z, g, b):
    mu = jnp.mean(z, axis=-1, keepdims=True)
    zc = z - mu
    var = jnp.mean(zc * zc, axis=-1, keepdims=True)
    return zc * lax.rsqrt(var + LN_EPS) * g + b


def _log_sigmoid(z):
    return jnp.minimum(z, 0.0) - jnp.log1p(jnp.exp(-jnp.abs(z)))


def _nt_dot(a, b):
    return lax.dot_general(a, b, (((1,), (1,)), ((), ())), preferred_element_type=F32)


C_RG, C_FOX, C_SB, C_SC, C_F, C_END = 0, 512, 1280, 2048, 2816, 2944


def _inproj_kernel(x_ref, w_ref, rg_ref, fox_ref, sb_ref, sc_ref, f_ref):
    xb = x_ref[...].astype(BF16)

    def mm(lo, hi):
        return jnp.dot(xb, w_ref[:, lo:hi], preferred_element_type=F32)

    def heads(ref, lo, hi):
        qkv = mm(lo, hi)
        for j in range(3 * GROUP_HEADS):
            ref[0, j] = qkv[:, j * HEAD_DIM:(j + 1) * HEAD_DIM].astype(BF16)

    rg_ref[...] = mm(C_RG, C_FOX)
    heads(fox_ref, C_FOX, C_SB)
    heads(sb_ref, C_SB, C_SC)
    sc_ref[...] = mm(C_SC, C_F)
    f_ref[...] = mm(C_F, C_END)


def _inproj(x2d, w_cat, B, S, tm=512):
    T, D = x2d.shape
    per_seq = S // tm
    flat = ((C_FOX - C_RG, F32), (C_F - C_SC, F32), (C_END - C_F, F32))
    flat_specs = [pl.BlockSpec((tm, w), lambda i: (i, 0)) for w, _ in flat]
    flat_shapes = [jax.ShapeDtypeStruct((T, w), dt) for w, dt in flat]
    head_spec = pl.BlockSpec((1, 3 * GROUP_HEADS, tm, HEAD_DIM),
                             lambda i: (i // per_seq, 0, i % per_seq, 0))
    head_shape = jax.ShapeDtypeStruct((B, 3 * GROUP_HEADS, S, HEAD_DIM), BF16)
    return pl.pallas_call(
        _inproj_kernel,
        grid=(T // tm,),
        in_specs=[pl.BlockSpec((tm, D), lambda i: (i, 0)),
                  pl.BlockSpec((D, C_END), lambda i: (0, 0))],
        out_specs=[flat_specs[0], head_spec, head_spec, flat_specs[1], flat_specs[2]],
        out_shape=[flat_shapes[0], head_shape, head_shape, flat_shapes[1], flat_shapes[2]],
        compiler_params=_params(("parallel",)),
        name="inproj",
    )(x2d, w_cat)


def _rgsc_kernel(rg_ref, sc_ref, f_ref, cw_ref, cb_ref, wa_ref, ba_ref, wi_ref, bi_ref, lam_ref,
                 scw_ref, fb_ref, yrg_ref, ysc_ref, cumt_ref, xprev, chprev, hprev, cprev,
                 *, ts, tk):
    GW = GROUP_WIDTH

    @pl.when(pl.program_id(1) == 0)
    def _():
        xprev[...] = jnp.zeros_like(xprev)
        chprev[...] = jnp.zeros_like(chprev)
        hprev[...] = jnp.zeros_like(hprev)
        cprev[...] = jnp.zeros_like(cprev)

    row = lax.broadcasted_iota(I32, (ts, GW), 0)

    def delayed(prev, cur, d):
        ext = jnp.concatenate([prev, cur], axis=0)
        return pltpu.roll(ext, d, 0)[SUBLANES:]

    xr = rg_ref[0, :, :GW]
    gate = rg_ref[0, :, GW:]
    xp = xprev[...]
    cw = cw_ref[...]
    xc = (delayed(xp, xr, 3) * cw[0:1] + delayed(xp, xr, 2) * cw[1:2]
          + delayed(xp, xr, 1) * cw[2:3] + xr * cw[3:4] + cb_ref[...])
    xprev[...] = xr[ts - SUBLANES:]
    xcb = xc.astype(BF16)
    r = jax.nn.sigmoid(jnp.dot(xcb, wa_ref[...], preferred_element_type=F32) + ba_ref[...])
    ig = jax.nn.sigmoid(jnp.dot(xcb, wi_ref[...], preferred_element_type=F32) + bi_ref[...])
    z = -lam_ref[...]
    softplus = jnp.maximum(z, 0.0) + jnp.log1p(jnp.exp(-jnp.abs(z)))
    log_a = -RGLRU_C * r * softplus
    a = jnp.exp(log_a)
    u = jnp.sqrt(-jnp.tanh(log_a) * (a * a + 1.0)) * (ig * xc)
    acc_a, acc_b = a, u
    d = 1
    while d < ts:
        keep = row >= d
        a_s = jnp.where(keep, pltpu.roll(acc_a, d, 0), 1.0)
        b_s = jnp.where(keep, pltpu.roll(acc_b, d, 0), 0.0)
        acc_b = acc_a * b_s + acc_b
        acc_a = acc_a * a_s
        d *= 2
    h = acc_b + acc_a * hprev[...]
    hprev[...] = h[ts - 1:]
    c0 = math.sqrt(2.0 / math.pi)
    gelu = 0.5 * gate * (1.0 + jnp.tanh(c0 * (gate + 0.044715 * gate * gate * gate)))
    yrg_ref[0] = h * gelu

    bg = sc_ref[0, :, :GW]
    ch = sc_ref[0, :, GW:2 * GW] * sc_ref[0, :, 2 * GW:]
    cp = chprev[...]
    sw = scw_ref[...]
    ysc_ref[0] = bg * (delayed(cp, ch, 2) * sw[0:1] + delayed(cp, ch, 1) * sw[1:2] + ch * sw[2:3])
    chprev[...] = ch[ts - SUBLANES:]

    rowf = lax.broadcasted_iota(I32, (ts, LANES), 0)
    c = _log_sigmoid(f_ref[0] + fb_ref[...])
    d = 1
    while d < ts:
        c = c + jnp.where(rowf >= d, pltpu.roll(c, d, 0), 0.0)
        d *= 2
    c = c + cprev[...]
    cprev[...] = c[ts - 1:]
    ct = c.T[:SUBLANES]
    for j in range(ts // tk):
        cumt_ref[0, j] = ct[:, j * tk:(j + 1) * tk]


def _rgsc(rg, sc, f, cw, cb, wa, ba, wi, bi, lam, scw, fb, tk, ts=512):
    B, S, _ = rg.shape
    GW = GROUP_WIDTH
    per = ts // tk

    def full(a):
        return pl.BlockSpec(a.shape, lambda b, t: (0,) * a.ndim)

    def seq(w):
        return pl.BlockSpec((1, ts, w), lambda b, t: (b, t, 0))

    params = (cw, cb, wa, ba, wi, bi, lam, scw, fb)
    return pl.pallas_call(
        functools.partial(_rgsc_kernel, ts=ts, tk=tk),
        grid=(B, S // ts),
        in_specs=[seq(2 * GW), seq(3 * GW), seq(LANES)] + [full(p) for p in params],
        out_specs=[seq(GW), seq(GW),
                   pl.BlockSpec((1, per, SUBLANES, tk), lambda b, t: (b, t, 0, 0))],
        out_shape=[jax.ShapeDtypeStruct((B, S, GW), F32), jax.ShapeDtypeStruct((B, S, GW), F32),
                   jax.ShapeDtypeStruct((B, S // tk, SUBLANES, tk), F32)],
        scratch_shapes=[pltpu.VMEM((SUBLANES, GW), F32), pltpu.VMEM((SUBLANES, GW), F32),
                        pltpu.VMEM((1, GW), F32), pltpu.VMEM((1, LANES), F32)],
        compiler_params=_params(("parallel", "arbitrary")),
        name="rgsc",
    )(rg, sc, f, *params)


HEADS_TOGETHER = GROUP_HEADS


assert math.log2(HEAD_DIM) % 2 == 0


def _fox_kernel(q_ref, k_ref, v_ref, cr_ref, o_ref, *, tq):
    qi = pl.program_id(1)
    scale = HEAD_DIM ** -0.5
    rowi = lax.broadcasted_iota(I32, (tq, tq), 0)
    coli = lax.broadcasted_iota(I32, (tq, tq), 1)
    causal = coli <= rowi
    qs = [q_ref[0, h] * scale for h in range(GROUP_HEADS)]

    def head_step(h, ki, carry, diag):
        m, l, acc = carry
        off = pl.multiple_of(ki * tq, tq)
        k = k_ref[0, h, pl.ds(off, tq), :]
        v = v_ref[0, h, pl.ds(off, tq), :]
        s = _nt_dot(qs[h], k) - cr_ref[0, ki, h:h + 1, :]
        if diag:
            s = jnp.where(causal, s, -jnp.inf)
        m_new = jnp.maximum(m, jnp.max(s, axis=-1, keepdims=True))
        alpha = jnp.exp(m - m_new)
        p = jnp.exp(s - m_new)
        l = alpha * l + jnp.sum(p, axis=-1, keepdims=True)
        acc = alpha * acc + jnp.dot(p.astype(BF16), v, preferred_element_type=F32)
        return m_new, l, acc

    def step(ki, carries, diag, heads):
        return tuple(head_step(h, ki, c, diag) for h, c in zip(heads, carries))

    init = (jnp.full((tq, 1), -jnp.inf, F32), jnp.zeros((tq, 1), F32),
            jnp.zeros((tq, HEAD_DIM), F32))
    outs = []
    for h0 in range(0, GROUP_HEADS, HEADS_TOGETHER):
        heads = range(h0, h0 + HEADS_TOGETHER)
        carries = lax.fori_loop(0, qi, functools.partial(step, diag=False, heads=heads),
                                (init,) * HEADS_TOGETHER)
        carries = step(qi, carries, True, heads)
        outs += [acc / l for _, l, acc in carries]
    o_ref[0] = jnp.concatenate(outs, axis=-1)


def _qkv_specs(S, tq):
    H, d = GROUP_HEADS, HEAD_DIM
    return [pl.BlockSpec((1, H, tq, d), lambda b, i: (b, 0, i, 0)),
            pl.BlockSpec((1, H, S, d), lambda b, i: (b, 1, 0, 0)),
            pl.BlockSpec((1, H, S, d), lambda b, i: (b, 2, 0, 0))]


def _fox(qkv, cumt, tq=256):
    B, _, S, d = qkv.shape
    H = GROUP_HEADS
    return pl.pallas_call(
        functools.partial(_fox_kernel, tq=tq),
        grid=(B, S // tq),
        in_specs=_qkv_specs(S, tq)
        + [pl.BlockSpec((1, S // tq, SUBLANES, tq), lambda b, i: (b, 0, 0, 0))],
        out_specs=pl.BlockSpec((1, tq, H * d), lambda b, i: (b, i, 0)),
        out_shape=jax.ShapeDtypeStruct((B, S, H * d), F32),
        compiler_params=_params(("parallel", "arbitrary")),
        name="fox_attn",
    )(qkv, qkv, qkv, cumt)


EXP_UNDERFLOW = -104.0


def _sb_kernel(q_ref, k_ref, v_ref, o_ref, *, tq):
    qi = pl.program_id(1)
    scale = HEAD_DIM ** -0.5
    rowi = lax.broadcasted_iota(I32, (tq, tq), 0)
    coli = lax.broadcasted_iota(I32, (tq, tq), 1)
    strict = coli < rowi
    later = jnp.where(rowi > coli, 1.0, 0.0).astype(BF16)
    qs = [q_ref[0, h] * scale for h in range(GROUP_HEADS)]
    def head_step(h, ki, carry, diag):
        rest, acc = carry
        off = pl.multiple_of(ki * tq, tq)
        k = k_ref[0, h, pl.ds(off, tq), :]
        v = v_ref[0, h, pl.ds(off, tq), :]
        z = _nt_dot(qs[h], k)
        ls = _log_sigmoid(z)
        l1m = ls - z
        if diag:
            l1m = jnp.where(strict, l1m, 0.0)
        hi = l1m.astype(BF16)
        lo = (l1m - hi.astype(F32)).astype(BF16)
        tail = (jnp.dot(hi, later, preferred_element_type=F32)
                + jnp.dot(lo, later, preferred_element_type=F32) + rest)
        w = jnp.exp(ls + tail)
        if diag:
            w = jnp.where(strict, w, 0.0)
        acc = acc + jnp.dot(w.astype(BF16), v, preferred_element_type=F32)
        rest = rest + jnp.sum(l1m, axis=-1, keepdims=True)
        return rest, acc

    def step(ki, carries, diag):
        return tuple(head_step(h, ki, carries[h], diag) for h in range(GROUP_HEADS))

    init = (jnp.zeros((tq, 1), F32), jnp.zeros((tq, HEAD_DIM), F32))
    carries = step(qi, (init,) * GROUP_HEADS, True)

    def more(c):
        j, carries = c
        top = carries[0][0]
        for rest, _ in carries[1:]:
            top = jnp.maximum(top, rest)
        return jnp.logical_and(j < qi, jnp.max(top) > EXP_UNDERFLOW)

    def further(c):
        j, carries = c
        return j + 1, step(qi - 1 - j, carries, False)

    _, carries = lax.while_loop(more, further, (jnp.int32(0), carries))
    o_ref[0] = jnp.concatenate([acc for _, acc in carries], axis=-1)


def _sb(qkv, tq=256):
    B, _, S, d = qkv.shape
    H = GROUP_HEADS
    return pl.pallas_call(
        functools.partial(_sb_kernel, tq=tq),
        grid=(B, S // tq),
        in_specs=_qkv_specs(S, tq),
        out_specs=pl.BlockSpec((1, tq, H * d), lambda b, i: (b, i, 0)),
        out_shape=jax.ShapeDtypeStruct((B, S, H * d), F32),
        compiler_params=_params(("parallel", "arbitrary")),
        name="sb_attn",
    )(qkv, qkv, qkv)


def _mixout_kernel(y0_ref, y1_ref, y2_ref, y3_ref, g_ref, w_ref, x_ref, lg_ref, lb_ref, o_ref):
    def rms(y):
        return y * lax.rsqrt(jnp.mean(y * y, axis=-1, keepdims=True) + 1e-6)

    y = jnp.concatenate([rms(r[...]) for r in (y0_ref, y1_ref, y2_ref, y3_ref)], axis=-1)
    y = (y * g_ref[...]).astype(BF16)
    mix = jnp.dot(y, w_ref[...], preferred_element_type=F32)
    o_ref[...] = _layer_norm(ALPHA * x_ref[...] + mix, lg_ref[...], lb_ref[...])


def _mixout(ys, g, w, x2d, lg, lb, tm=512):
    T, D = x2d.shape
    GW = GROUP_WIDTH

    def full(a):
        return pl.BlockSpec(a.shape, lambda i: (0,) * a.ndim)

    return pl.pallas_call(
        _mixout_kernel,
        grid=(T // tm,),
        in_specs=[pl.BlockSpec((tm, GW), lambda i: (i, 0))] * 4
        + [full(g), full(w), pl.BlockSpec((tm, D), lambda i: (i, 0)), full(lg), full(lb)],
        out_specs=pl.BlockSpec((tm, D), lambda i: (i, 0)),
        out_shape=jax.ShapeDtypeStruct((T, D), F32),
        compiler_params=_params(("parallel",)),
        name="mixout",
    )(*ys, g, w, x2d, lg, lb)


def _kv_kernel(m_ref, w_ref, k_ref, v_ref):
    kv = jnp.dot(m_ref[...].astype(BF16), w_ref[...], preferred_element_type=F32)
    k_ref[...] = kv[:, :D_MODEL].astype(BF16)
    v_ref[...] = kv[:, D_MODEL:].astype(BF16)


def _kv(mem2d, wkv, tm=512):
    M, D = mem2d.shape
    return pl.pallas_call(
        _kv_kernel,
        grid=(M // tm,),
        in_specs=[pl.BlockSpec((tm, D), lambda i: (i, 0)),
                  pl.BlockSpec((D, 2 * D), lambda i: (0, 0))],
        out_specs=[pl.BlockSpec((tm, D), lambda i: (i, 0))] * 2,
        out_shape=[jax.ShapeDtypeStruct((M, D), BF16)] * 2,
        compiler_params=_params(("parallel",)),
        name="mem_kv",
    )(mem2d, wkv)


PEER_TB = 32


def _chunk_major_spec(tm):
    return pl.BlockSpec((tm // PEER_TB, ROW_CHUNKS * PEER_TB, LANES), lambda i: (i, 0, 0))


def _chunk_major_shape(T):
    return jax.ShapeDtypeStruct((T // PEER_TB, ROW_CHUNKS * PEER_TB, LANES), F32)


def _to_chunk_major(y, ref):
    for g in range(y.shape[0] // PEER_TB):
        for c in range(ROW_CHUNKS):
            ref[g, c * PEER_TB:(c + 1) * PEER_TB, :] = (
                y[g * PEER_TB:(g + 1) * PEER_TB, c * LANES:(c + 1) * LANES])


def _xattn_kernel(x_ref, wq_ref, k_ref, v_ref, wo_ref, lg_ref, lb_ref, o_ref, oc_ref):
    x = x_ref[...]
    q = jnp.dot(x.astype(BF16), wq_ref[...], preferred_element_type=F32).astype(BF16)
    scale = XA_HEAD_DIM ** -0.5
    outs = []
    for h in range(XA_HEADS):
        sl = slice(h * XA_HEAD_DIM, (h + 1) * XA_HEAD_DIM)
        s = _nt_dot(q[:, sl], k_ref[0, :, sl]) * scale
        p = jnp.exp(s - jnp.max(s, axis=-1, keepdims=True))
        p = p / jnp.sum(p, axis=-1, keepdims=True)
        outs.append(jnp.dot(p.astype(BF16), v_ref[0, :, sl], preferred_element_type=F32))
    o = jnp.concatenate(outs, axis=-1).astype(BF16)
    xa = jnp.dot(o, wo_ref[...], preferred_element_type=F32)
    y = _layer_norm(ALPHA * x + xa, lg_ref[...], lb_ref[...])
    o_ref[...] = y
    _to_chunk_major(y, oc_ref)


def _xattn(x2d, wq, k, v, wo, lg, lb, seq_len, tm=512):
    T, D = x2d.shape
    M = k.shape[1]
    per_seq = seq_len // tm

    def full(a):
        return pl.BlockSpec(a.shape, lambda i: (0,) * a.ndim)

    return pl.pallas_call(
        _xattn_kernel,
        grid=(T // tm,),
        in_specs=[pl.BlockSpec((tm, D), lambda i: (i, 0)), full(wq),
                  pl.BlockSpec((1, M, D), lambda i: (i // per_seq, 0, 0)),
                  pl.BlockSpec((1, M, D), lambda i: (i // per_seq, 0, 0)),
                  full(wo), full(lg), full(lb)],
        out_specs=[pl.BlockSpec((tm, D), lambda i: (i, 0)), _chunk_major_spec(tm)],
        out_shape=[jax.ShapeDtypeStruct((T, D), F32), _chunk_major_shape(T)],
        compiler_params=_params(("parallel",)),
        name="xattn",
    )(x2d, wq, k, v, wo, lg, lb)


E_PER_TOK = PEER_HEADS * PEER_TOPK


def _staircase():
    return [(a, b) for a in range(PEER_TOPK) for b in range(PEER_TOPK // (a + 1))]


def _route_kernel(x_ref, wq_ref, k1_ref, k2_ref, e_ref, g_ref, v_scr, i_scr, *, ts):
    xb = x_ref[...].astype(BF16)
    key_id = lax.broadcasted_iota(I32, (N_KEYS, ts), 0).astype(F32)
    qd = 2 * PEER_HALF
    for h in range(PEER_HEADS):
        qry = jnp.dot(xb, wq_ref[:, h * qd:(h + 1) * qd], preferred_element_type=F32)
        for half, kref in ((0, k1_ref), (1, k2_ref)):
            qh = qry[:, half * PEER_HALF:(half + 1) * PEER_HALF].astype(BF16)
            s0 = _nt_dot(kref[...], qh)

            def pick(it, s, h=h, half=half):
                m = jnp.max(s, axis=0, keepdims=True)
                idx = jnp.min(jnp.where(s == m, key_id, float(N_KEYS)), axis=0, keepdims=True)
                v_scr[half, it, h:h + 1, :] = m
                i_scr[half, it, h:h + 1, :] = idx.astype(I32)
                return jnp.where(key_id == idx, -jnp.inf, s)

            lax.fori_loop(0, PEER_TOPK, pick, s0)

    cells = _staircase()
    cand = tuple(v_scr[0, a] + v_scr[1, b] for a, b in cells)
    expert = [i_scr[0, a] * N_KEYS + i_scr[1, b] for a, b in cells]

    def select(it, carry):
        cand, top = carry
        best_v, best_e, best_c = cand[0], expert[0], jnp.zeros((PEER_HEADS, ts), I32)
        for c in range(1, len(cells)):
            better = cand[c] > best_v
            best_v = jnp.where(better, cand[c], best_v)
            best_e = jnp.where(better, expert[c], best_e)
            best_c = jnp.where(better, c, best_c)
        top = jnp.where(it == 0, best_v, top)
        i_scr[0, it] = best_e * WORD_ROWS
        v_scr[0, it] = best_v
        cand = tuple(jnp.where(best_c == c, -jnp.inf, cand[c]) for c in range(len(cells)))
        return cand, top

    _, top = lax.fori_loop(0, PEER_TOPK, select, (cand, jnp.zeros((PEER_HEADS, ts), F32)))
    ex = [jnp.exp(v_scr[0, it] - top) for it in range(PEER_TOPK)]
    den = ex[0]
    for it in range(1, PEER_TOPK):
        den = den + ex[it]
    inv = 1.0 / den
    g_ref[...] = jnp.concatenate([e * inv for e in ex], axis=0).T
    rows = pltpu.bitcast(i_scr[0].reshape(E_PER_TOK, ts), F32)
    e_ref[...] = pltpu.bitcast(rows.T, I32)


def _route(x2d, wq, k1, k2, ts=256):
    T, D = x2d.shape
    nb = T // ts

    def full(a):
        return pl.BlockSpec(a.shape, lambda i: (0,) * a.ndim)

    blk = pl.BlockSpec((ts, E_PER_TOK), lambda i: (i, 0))
    return pl.pallas_call(
        functools.partial(_route_kernel, ts=ts),
        grid=(nb,),
        in_specs=[pl.BlockSpec((ts, D), lambda i: (i, 0)), full(wq), full(k1), full(k2)],
        out_specs=[blk, blk],
        out_shape=[jax.ShapeDtypeStruct((T, E_PER_TOK), I32),
                   jax.ShapeDtypeStruct((T, E_PER_TOK), F32)],
        scratch_shapes=[pltpu.VMEM((2, PEER_TOPK, PEER_HEADS, ts), F32),
                        pltpu.VMEM((2, PEER_TOPK, PEER_HEADS, ts), I32)],
        compiler_params=_params(("parallel",)),
        name="peer_route",
    )(x2d, wq, k1, k2)


TOK_GROUP = SUBLANES


N_GROUPS = PEER_TB // TOK_GROUP
SCORE_BUFS = 4


def _gather_rows(ids_ref, tab_ref, rows_scr, tok0):
    for tt in range(TOK_GROUP):
        for j in range(E_PER_TOK):
            r0 = pl.multiple_of(ids_ref[tok0 + tt, j], WORD_ROWS)
            rows_scr[tt, j * WORD_ROWS:(j + 1) * WORD_ROWS, :] = tab_ref[pl.ds(r0, WORD_ROWS), :]


def _token_chunks(tt):
    return pl.ds(tt, ROW_CHUNKS, stride=PEER_TB)


def _chunk_of_col():
    col = lax.broadcasted_iota(I32, (ROW_CHUNKS, E_PER_TOK * ROW_CHUNKS), 1)
    row = lax.broadcasted_iota(I32, (ROW_CHUNKS, E_PER_TOK * ROW_CHUNKS), 0)
    return (col & (ROW_CHUNKS - 1)) == row


def _peer_score_kernel(ids_ref, x_ref, tab_ref, s_ref, *scratch):
    rows_bufs, part_bufs = scratch[:SCORE_BUFS], scratch[SCORE_BUFS:]
    own_chunk = _chunk_of_col()
    c_id = lax.broadcasted_iota(I32, (E_PER_TOK * ROW_CHUNKS, E_PER_TOK), 0)
    e_id = lax.broadcasted_iota(I32, (E_PER_TOK * ROW_CHUNKS, E_PER_TOK), 1)
    fold = jnp.where(c_id >> CHUNK_SHIFT == e_id, 1.0, 0.0).astype(BF16)

    for g in range(N_GROUPS):
        tok0 = g * TOK_GROUP
        rows_scr, part_scr = rows_bufs[g % SCORE_BUFS], part_bufs[g % SCORE_BUFS]
        _gather_rows(ids_ref, tab_ref, rows_scr, tok0)
        for tt in range(TOK_GROUP):
            rows = pltpu.bitcast(rows_scr[tt], BF16)
            xt = x_ref[0, _token_chunks(tok0 + tt), :].astype(BF16)
            full = _nt_dot(xt, rows)
            part_scr[tt * ROW_CHUNKS:(tt + 1) * ROW_CHUNKS, :] = jnp.where(own_chunk, full, 0.0)
        part = part_scr[...]
        hi = part.astype(BF16)
        lo = (part - hi.astype(F32)).astype(BF16)
        sc = (jnp.dot(hi, fold, preferred_element_type=F32)
              + jnp.dot(lo, fold, preferred_element_type=F32))
        sc = jnp.sum(sc.reshape(TOK_GROUP, ROW_CHUNKS, E_PER_TOK), axis=1)
        s_ref[tok0:tok0 + TOK_GROUP, :] = sc


def _peer_mix_kernel(ids_ref, s_ref, gate_ref, tab_ref, o_ref, *rows_bufs):
    own_chunk = _chunk_of_col()
    e_id = lax.broadcasted_iota(I32, (E_PER_TOK, E_PER_TOK * ROW_CHUNKS), 0)
    c_id = lax.broadcasted_iota(I32, (E_PER_TOK, E_PER_TOK * ROW_CHUNKS), 1)
    spread = jnp.where(c_id >> CHUNK_SHIFT == e_id, 1.0, 0.0).astype(BF16)

    for g in range(N_GROUPS):
        tok0 = g * TOK_GROUP
        sl = slice(tok0, tok0 + TOK_GROUP)
        rows_scr = rows_bufs[g % len(rows_bufs)]
        _gather_rows(ids_ref, tab_ref, rows_scr, tok0)
        s = s_ref[sl, :]
        act = 0.5 * s * (1.0 + lax.erf(s * (2.0 ** -0.5)))
        coef = (gate_ref[sl, :] * act).astype(BF16)
        coef = jnp.dot(coef, spread, preferred_element_type=F32)
        for tt in range(TOK_GROUP):
            rows = pltpu.bitcast(rows_scr[tt], BF16)
            ct = jnp.broadcast_to(coef[tt:tt + 1], (ROW_CHUNKS, E_PER_TOK * ROW_CHUNKS))
            ct = jnp.where(own_chunk, ct, 0.0).astype(BF16)
            o_ref[0, _token_chunks(tok0 + tt), :] = jnp.dot(ct, rows, preferred_element_type=F32)


def _peer_experts(ids, gate, xc, tab_u, tab_v):
    T = ids.shape[0]
    tb = PEER_TB
    nb = T // tb
    ids_spec = pl.BlockSpec((tb, E_PER_TOK), lambda i: (i, 0), memory_space=pltpu.SMEM)
    tab_spec = pl.BlockSpec(tab_u.shape, lambda i: (0, 0), pipeline_mode=pl.Buffered(1))
    tok_spec = pl.BlockSpec((tb, E_PER_TOK), lambda i: (i, 0))
    row_spec = _chunk_major_spec(tb)
    rows_scr = [pltpu.VMEM((TOK_GROUP, E_PER_TOK * WORD_ROWS, LANES), I32)] * SCORE_BUFS
    part_scr = [pltpu.VMEM((TOK_GROUP * ROW_CHUNKS, E_PER_TOK * ROW_CHUNKS), F32)] * SCORE_BUFS
    s = pl.pallas_call(
        _peer_score_kernel,
        grid=(nb,),
        in_specs=[ids_spec, row_spec, tab_spec],
        out_specs=tok_spec,
        out_shape=jax.ShapeDtypeStruct((T, E_PER_TOK), F32),
        scratch_shapes=rows_scr + part_scr,
        compiler_params=_params(("arbitrary",)),
        name="peer_score",
    )(ids, xc, tab_u)
    return pl.pallas_call(
        _peer_mix_kernel,
        grid=(nb,),
        in_specs=[ids_spec, tok_spec, tok_spec, tab_spec],
        out_specs=row_spec,
        out_shape=_chunk_major_shape(T),
        scratch_shapes=rows_scr[:1],
        compiler_params=_params(("arbitrary",)),
        name="peer_mix",
    )(ids, s, gate, tab_v)


def _resln_kernel(x_ref, y_ref, lg_ref, lb_ref, o_ref):
    G = x_ref.shape[0]
    z = ALPHA * x_ref[...] + y_ref[...]
    zs = [z[:, c * PEER_TB:(c + 1) * PEER_TB, :] for c in range(ROW_CHUNKS)]
    tot = zs[0]
    for c in range(1, ROW_CHUNKS):
        tot = tot + zs[c]
    mu = jnp.sum(tot, axis=-1, keepdims=True) * (1.0 / D_MODEL)
    zc = [a - mu for a in zs]
    sq = zc[0] * zc[0]
    for c in range(1, ROW_CHUNKS):
        sq = sq + zc[c] * zc[c]
    rstd = lax.rsqrt(jnp.sum(sq, axis=-1, keepdims=True) * (1.0 / D_MODEL) + LN_EPS)
    for c in range(ROW_CHUNKS):
        cols = slice(c * LANES, (c + 1) * LANES)
        y = (zc[c] * rstd).reshape(G * PEER_TB, LANES)
        o_ref[:, cols] = y * lg_ref[:, cols] + lb_ref[:, cols]


def _resln(xc, yc, lg, lb, tm=512):
    T = xc.shape[0] * PEER_TB
    blk = _chunk_major_spec(tm)
    par = pl.BlockSpec((1, D_MODEL), lambda i: (0, 0))
    return pl.pallas_call(
        _resln_kernel,
        grid=(T // tm,),
        in_specs=[blk, blk, par, par],
        out_specs=pl.BlockSpec((tm, D_MODEL), lambda i: (i, 0)),
        out_shape=jax.ShapeDtypeStruct((T, D_MODEL), F32),
        compiler_params=_params(("parallel",)),
        name="res_ln",
    )(xc, yc, lg, lb)


def _pack_kernel(t_ref, o_ref):
    def bf16_bits(v):
        return pltpu.bitcast(v.astype(BF16).astype(F32), jnp.uint32)

    te = t_ref.shape[0]
    for r in range(WORD_ROWS):
        lo = bf16_bits(t_ref[:, (2 * r) * LANES:(2 * r + 1) * LANES])
        hi = bf16_bits(t_ref[:, (2 * r + 1) * LANES:(2 * r + 2) * LANES])
        words = (hi & jnp.uint32(0xFFFF0000)) | (lo >> 16)
        o_ref[pl.ds(r, te, stride=WORD_ROWS), :] = pltpu.bitcast(words, I32)


def _pack_table(t, layer, te=512):
    _, E, D = t.shape
    return pl.pallas_call(
        _pack_kernel,
        grid=(E // te,),
        in_specs=[pl.BlockSpec((None, te, D), lambda i: (layer, i, 0))],
        out_specs=pl.BlockSpec((te * WORD_ROWS, LANES), lambda i: (i, 0)),
        out_shape=jax.ShapeDtypeStruct((E * WORD_ROWS, LANES), I32),
        compiler_params=_params(("parallel",)),
        name="pack_table",
    )(t)


def _block_diag(w):
    H, d, _ = w.shape
    out = jnp.zeros((H * d, H * d), w.dtype)
    for h in range(H):
        out = out.at[h * d:(h + 1) * d, h * d:(h + 1) * d].set(w[h])
    return out


def _cat_w_in_kernel(w_ref, o_ref):
    GW = GROUP_WIDTH
    off_f = 5 * GW
    off_sb = off_f + GROUP_HEADS
    w = w_ref[...].astype(BF16)
    o_ref[:, :off_f] = w[:, :off_f]
    o_ref[:, off_f:C_F] = w[:, off_sb:N_IN]
    o_ref[:, C_F:C_END] = jnp.zeros((w.shape[0], C_END - C_F), BF16)
    o_ref[:, C_F:C_F + GROUP_HEADS] = w[:, off_f:off_sb]


def _cat_w_in(w_in, layer, tr=256):
    _, D, N = w_in.shape
    return pl.pallas_call(
        _cat_w_in_kernel,
        grid=(D // tr,),
        in_specs=[pl.BlockSpec((None, tr, N), lambda i: (layer, i, 0))],
        out_specs=pl.BlockSpec((tr, C_END), lambda i: (i, 0)),
        out_shape=jax.ShapeDtypeStruct((D, C_END), BF16),
        compiler_params=_params(("parallel",)),
        name="regroup_w_in",
    )(w_in)


def _heads(t, B, S):
    t = t.reshape(B, S, 3, GROUP_HEADS, HEAD_DIM).transpose(2, 0, 3, 1, 4)
    return t[0], t[1], t[2]


def _row(v, width=None):
    v = v.reshape(1, -1).astype(F32)
    if width is not None and v.shape[1] < width:
        v = jnp.pad(v, ((0, 0), (0, width - v.shape[1])))
    return v


def _layer(x2d, kmem, vmem_, B, S, p, tables, layer, tq=256):
    T = B * S
    rg, fox, sb, sc, f = _inproj(x2d, _cat_w_in(tables[2], layer), B, S)
    yrg, ysc, cumt = _rgsc(
        rg.reshape(B, S, -1), sc.reshape(B, S, -1), f.reshape(B, S, -1),
        p["rg_conv_w"], _row(p["rg_conv_b"]), _block_diag(p["rg_wa"]).astype(BF16), _row(p["rg_ba"]),
        _block_diag(p["rg_wi"]).astype(BF16), _row(p["rg_bi"]), _row(p["rg_lambda"]),
        p["sc_conv_w"], _row(p["fox_bf"], LANES), tq)
    yfox = _fox(fox, cumt, tq=tq)
    ysb = _sb(sb, tq=tq)
    ys = (yrg.reshape(T, -1), yfox.reshape(T, -1), ysb.reshape(T, -1), ysc.reshape(T, -1))
    x1 = _mixout(ys, _row(p["mix_norm_g"]), p["w_out"].astype(BF16), x2d,
                 _row(p["ln1_g"]), _row(p["ln1_b"]))
    x2, x2c = _xattn(x1, p["xa_wq"].astype(BF16), kmem, vmem_, p["xa_wo"].astype(BF16),
                     _row(p["ln2_g"]), _row(p["ln2_b"]), S)
    ids, gate = _route(x2, p["peer_wq"].astype(BF16), p["peer_k1"].astype(BF16),
                       p["peer_k2"].astype(BF16))
    ffc = _peer_experts(ids, gate, x2c, _pack_table(tables[0], layer), _pack_table(tables[1], layer))
    return _resln(x2c, ffc, _row(p["ln3_g"]), _row(p["ln3_b"]))


_LAYER_PARAMS = ("w_in", "w_out", "rg_conv_w", "rg_conv_b", "rg_wa", "rg_ba", "rg_wi", "rg_bi",
                 "rg_lambda", "fox_bf", "sc_conv_w", "mix_norm_g", "ln1_g", "ln1_b", "xa_wq",
                 "xa_wkv", "xa_wo", "ln2_g", "ln2_b", "peer_wq", "peer_k1", "peer_k2", "peer_u",
                 "peer_v", "ln3_g", "ln3_b")


def kernel(x, mem, w_in, w_out, rg_conv_w, rg_conv_b, rg_wa, rg_ba, rg_wi, rg_bi, rg_lambda, fox_bf, sc_conv_w, mix_norm_g, ln1_g, ln1_b, xa_wq, xa_wkv, xa_wo, ln2_g, ln2_b, peer_wq, peer_k1, peer_k2, peer_u, peer_v, ln3_g, ln3_b):
    stacked = dict(zip(_LAYER_PARAMS, (
        w_in, w_out, rg_conv_w, rg_conv_b, rg_wa, rg_ba, rg_wi, rg_bi, rg_lambda, fox_bf,
        sc_conv_w, mix_norm_g, ln1_g, ln1_b, xa_wq, xa_wkv, xa_wo, ln2_g, ln2_b, peer_wq,
        peer_k1, peer_k2, peer_u, peer_v, ln3_g, ln3_b)))
    B, S, D = x.shape
    M = mem.shape[1]
    x2d = x.reshape(B * S, D)
    mem2d = mem.reshape(B * M, D)
    for l in range(w_in.shape[0]):
        p = {k: v[l] for k, v in stacked.items() if k not in ("peer_u", "peer_v", "w_in")}
        kmem, vmem_ = _kv(mem2d, p["xa_wkv"].astype(BF16))
        x2d = _layer(x2d, kmem.reshape(B, M, D), vmem_.reshape(B, M, D), B, S, p,
                     (peer_u, peer_v, w_in), l)
    return x2d.reshape(B, S, D)
```

```python
import functools
import math

import jax
import jax.numpy as jnp
from jax import lax
from jax.experimental import pallas as pl
from jax.experimental.pallas import tpu as pltpu

F32 = jnp.float32
BF16 = jnp.bfloat16
I32 = jnp.int32

D_MODEL = 1024
GROUP_WIDTH = 256
GROUP_HEADS = 4
HEAD_DIM = 64
N_IN = 2820
RGLRU_C = 8.0
XA_HEADS = 4
XA_HEAD_DIM = D_MODEL // XA_HEADS
PEER_HEADS = 8
N_KEYS = 128
PEER_HALF = 128
PEER_TOPK = 16
DEPTH = 2
ALPHA = (2.0 * DEPTH) ** 0.25
LN_EPS = 1e-5

SUBLANES = 8
LANES = 128
WORD_ROWS = D_MODEL // (2 * LANES)
ROW_CHUNKS = D_MODEL // LANES
CHUNK_SHIFT = ROW_CHUNKS.bit_length() - 1

V7X_VMEM_BYTES = 64 * 1024 * 1024
VMEM_LIMIT = V7X_VMEM_BYTES * 3 // 4
GROUP_RMS_EPS = 1e-6
GELU_TANH_CUBIC = 0.044715


def _params(sem, vmem=VMEM_LIMIT):
    return pltpu.CompilerParams(dimension_semantics=sem, vmem_limit_bytes=vmem)


def _layer_norm(z, g, b):
    mu = jnp.mean(z, axis=-1, keepdims=True)
    zc = z - mu
    var = jnp.mean(zc * zc, axis=-1, keepdims=True)
    return zc * lax.rsqrt(var + LN_EPS) * g + b


def _log_sigmoid(z):
    return jnp.minimum(z, 0.0) - jnp.log1p(jnp.exp(-jnp.abs(z)))


def _nt_dot(a, b):
    return lax.dot_general(a, b, (((1,), (1,)), ((), ())), preferred_element_type=F32)


C_RG, C_FOX, C_SB, C_SC, C_F, C_END = 0, 512, 1280, 2048, 2816, 2944


def _inproj_kernel(x_ref, w_ref, rg_ref, fox_ref, sb_ref, sc_ref, f_ref):
    xb = x_ref[...].astype(BF16)

    def mm(lo, hi):
        return jnp.dot(xb, w_ref[:, lo:hi], preferred_element_type=F32)

    def heads(ref, lo, hi):
        qkv = mm(lo, hi)
        for j in range(3 * GROUP_HEADS):
            ref[0, j] = qkv[:, j * HEAD_DIM:(j + 1) * HEAD_DIM].astype(BF16)

    rg_ref[...] = mm(C_RG, C_FOX)
    heads(fox_ref, C_FOX, C_SB)
    heads(sb_ref, C_SB, C_SC)
    sc_ref[...] = mm(C_SC, C_F)
    f_ref[...] = mm(C_F, C_END)


def _inproj(x2d, w_cat, B, S, tm=512):
    T, D = x2d.shape
    per_seq = S // tm
    flat = ((C_FOX - C_RG, F32), (C_F - C_SC, F32), (C_END - C_F, F32))
    flat_specs = [pl.BlockSpec((tm, w), lambda i: (i, 0)) for w, _ in flat]
    flat_shapes = [jax.ShapeDtypeStruct((T, w), dt) for w, dt in flat]
    head_spec = pl.BlockSpec((1, 3 * GROUP_HEADS, tm, HEAD_DIM),
                             lambda i: (i // per_seq, 0, i % per_seq, 0))
    head_shape = jax.ShapeDtypeStruct((B, 3 * GROUP_HEADS, S, HEAD_DIM), BF16)
    return pl.pallas_call(
        _inproj_kernel,
        grid=(T // tm,),
        in_specs=[pl.BlockSpec((tm, D), lambda i: (i, 0)),
                  pl.BlockSpec((D, C_END), lambda i: (0, 0))],
        out_specs=[flat_specs[0], head_spec, head_spec, flat_specs[1], flat_specs[2]],
        out_shape=[flat_shapes[0], head_shape, head_shape, flat_shapes[1], flat_shapes[2]],
        compiler_params=_params(("parallel",)),
        name="inproj",
    )(x2d, w_cat)


def _rgsc_kernel(rg_ref, sc_ref, f_ref, cw_ref, cb_ref, wa_ref, ba_ref, wi_ref, bi_ref, lam_ref,
                 scw_ref, fb_ref, yrg_ref, ysc_ref, cumt_ref, xprev, chprev, hprev, cprev,
                 *, ts, tk):
    GW = GROUP_WIDTH

    @pl.when(pl.program_id(1) == 0)
    def _():
        xprev[...] = jnp.zeros_like(xprev)
        chprev[...] = jnp.zeros_like(chprev)
        hprev[...] = jnp.zeros_like(hprev)
        cprev[...] = jnp.zeros_like(cprev)

    row = lax.broadcasted_iota(I32, (ts, GW), 0)

    def delayed(prev, cur, d):
        ext = jnp.concatenate([prev, cur], axis=0)
        return pltpu.roll(ext, d, 0)[SUBLANES:]

    xr = rg_ref[0, :, :GW]
    gate = rg_ref[0, :, GW:]
    xp = xprev[...]
    cw = cw_ref[...]
    xc = (delayed(xp, xr, 3) * cw[0:1] + delayed(xp, xr, 2) * cw[1:2]
          + delayed(xp, xr, 1) * cw[2:3] + xr * cw[3:4] + cb_ref[...])
    xprev[...] = xr[ts - SUBLANES:]
    xcb = xc.astype(BF16)
    r = jax.nn.sigmoid(jnp.dot(xcb, wa_ref[...], preferred_element_type=F32) + ba_ref[...])
    ig = jax.nn.sigmoid(jnp.dot(xcb, wi_ref[...], preferred_element_type=F32) + bi_ref[...])
    z = -lam_ref[...]
    softplus = jnp.maximum(z, 0.0) + jnp.log1p(jnp.exp(-jnp.abs(z)))
    log_a = -RGLRU_C * r * softplus
    a = jnp.exp(log_a)
    u = jnp.sqrt(-jnp.tanh(log_a) * (a * a + 1.0)) * (ig * xc)
    acc_a, acc_b = a, u
    d = 1
    while d < ts:
        keep = row >= d
        a_s = jnp.where(keep, pltpu.roll(acc_a, d, 0), 1.0)
        b_s = jnp.where(keep, pltpu.roll(acc_b, d, 0), 0.0)
        acc_b = acc_a * b_s + acc_b
        acc_a = acc_a * a_s
        d *= 2
    h = acc_b + acc_a * hprev[...]
    hprev[...] = h[ts - 1:]
    c0 = math.sqrt(2.0 / math.pi)
    gelu = 0.5 * gate * (1.0 + jnp.tanh(c0 * (gate + GELU_TANH_CUBIC * gate * gate * gate)))
    yrg_ref[0] = h * gelu

    bg = sc_ref[0, :, :GW]
    ch = sc_ref[0, :, GW:2 * GW] * sc_ref[0, :, 2 * GW:]
    cp = chprev[...]
    sw = scw_ref[...]
    ysc_ref[0] = bg * (delayed(cp, ch, 2) * sw[0:1] + delayed(cp, ch, 1) * sw[1:2] + ch * sw[2:3])
    chprev[...] = ch[ts - SUBLANES:]

    rowf = lax.broadcasted_iota(I32, (ts, LANES), 0)
    c = _log_sigmoid(f_ref[0] + fb_ref[...])
    d = 1
    while d < ts:
        c = c + jnp.where(rowf >= d, pltpu.roll(c, d, 0), 0.0)
        d *= 2
    c = c + cprev[...]
    cprev[...] = c[ts - 1:]
    ct = c.T[:SUBLANES]
    for j in range(ts // tk):
        cumt_ref[0, j] = ct[:, j * tk:(j + 1) * tk]


def _rgsc(rg, sc, f, cw, cb, wa, ba, wi, bi, lam, scw, fb, tk, ts=512):
    B, S, _ = rg.shape
    GW = GROUP_WIDTH
    per = ts // tk

    def full(a):
        return pl.BlockSpec(a.shape, lambda b, t: (0,) * a.ndim)

    def seq(w):
        return pl.BlockSpec((1, ts, w), lambda b, t: (b, t, 0))

    params = (cw, cb, wa, ba, wi, bi, lam, scw, fb)
    return pl.pallas_call(
        functools.partial(_rgsc_kernel, ts=ts, tk=tk),
        grid=(B, S // ts),
        in_specs=[seq(2 * GW), seq(3 * GW), seq(LANES)] + [full(p) for p in params],
        out_specs=[seq(GW), seq(GW),
                   pl.BlockSpec((1, per, SUBLANES, tk), lambda b, t: (b, t, 0, 0))],
        out_shape=[jax.ShapeDtypeStruct((B, S, GW), F32), jax.ShapeDtypeStruct((B, S, GW), F32),
                   jax.ShapeDtypeStruct((B, S // tk, SUBLANES, tk), F32)],
        scratch_shapes=[pltpu.VMEM((SUBLANES, GW), F32), pltpu.VMEM((SUBLANES, GW), F32),
                        pltpu.VMEM((1, GW), F32), pltpu.VMEM((1, LANES), F32)],
        compiler_params=_params(("parallel", "arbitrary")),
        name="rgsc",
    )(rg, sc, f, *params)


HEADS_TOGETHER = GROUP_HEADS


assert math.log2(HEAD_DIM) % 2 == 0


def _fox_kernel(q_ref, k_ref, v_ref, cr_ref, o_ref, *, tq):
    qi = pl.program_id(1)
    scale = HEAD_DIM ** -0.5
    rowi = lax.broadcasted_iota(I32, (tq, tq), 0)
    coli = lax.broadcasted_iota(I32, (tq, tq), 1)
    causal = coli <= rowi
    qs = [q_ref[0, h] * scale for h in range(GROUP_HEADS)]

    def head_step(h, ki, carry, diag):
        m, l, acc = carry
        off = pl.multiple_of(ki * tq, tq)
        k = k_ref[0, h, pl.ds(off, tq), :]
        v = v_ref[0, h, pl.ds(off, tq), :]
        s = _nt_dot(qs[h], k) - cr_ref[0, ki, h:h + 1, :]
        if diag:
            s = jnp.where(causal, s, -jnp.inf)
        m_new = jnp.maximum(m, jnp.max(s, axis=-1, keepdims=True))
        alpha = jnp.exp(m - m_new)
        p = jnp.exp(s - m_new)
        l = alpha * l + jnp.sum(p, axis=-1, keepdims=True)
        acc = alpha * acc + jnp.dot(p.astype(BF16), v, preferred_element_type=F32)
        return m_new, l, acc

    def step(ki, carries, diag, heads):
        return tuple(head_step(h, ki, c, diag) for h, c in zip(heads, carries))

    init = (jnp.full((tq, 1), -jnp.inf, F32), jnp.zeros((tq, 1), F32),
            jnp.zeros((tq, HEAD_DIM), F32))
    outs = []
    for h0 in range(0, GROUP_HEADS, HEADS_TOGETHER):
        heads = range(h0, h0 + HEADS_TOGETHER)
        carries = lax.fori_loop(0, qi, functools.partial(step, diag=False, heads=heads),
                                (init,) * HEADS_TOGETHER)
        carries = step(qi, carries, True, heads)
        outs += [acc / l for _, l, acc in carries]
    o_ref[0] = jnp.concatenate(outs, axis=-1)


def _qkv_specs(S, tq):
    H, d = GROUP_HEADS, HEAD_DIM
    return [pl.BlockSpec((1, H, tq, d), lambda b, i: (b, 0, i, 0)),
            pl.BlockSpec((1, H, S, d), lambda b, i: (b, 1, 0, 0)),
            pl.BlockSpec((1, H, S, d), lambda b, i: (b, 2, 0, 0))]


def _fox(qkv, cumt, tq=256):
    B, _, S, d = qkv.shape
    H = GROUP_HEADS
    return pl.pallas_call(
        functools.partial(_fox_kernel, tq=tq),
        grid=(B, S // tq),
        in_specs=_qkv_specs(S, tq)
        + [pl.BlockSpec((1, S // tq, SUBLANES, tq), lambda b, i: (b, 0, 0, 0))],
        out_specs=pl.BlockSpec((1, tq, H * d), lambda b, i: (b, i, 0)),
        out_shape=jax.ShapeDtypeStruct((B, S, H * d), F32),
        compiler_params=_params(("parallel", "arbitrary")),
        name="fox_attn",
    )(qkv, qkv, qkv, cumt)


EXP_UNDERFLOW = -104.0


def _sb_kernel(q_ref, k_ref, v_ref, o_ref, *, tq):
    qi = pl.program_id(1)
    scale = HEAD_DIM ** -0.5
    rowi = lax.broadcasted_iota(I32, (tq, tq), 0)
    coli = lax.broadcasted_iota(I32, (tq, tq), 1)
    strict = coli < rowi
    later = jnp.where(rowi > coli, 1.0, 0.0).astype(BF16)
    qs = [q_ref[0, h] * scale for h in range(GROUP_HEADS)]
    def head_step(h, ki, carry, diag):
        rest, acc = carry
        off = pl.multiple_of(ki * tq, tq)
        k = k_ref[0, h, pl.ds(off, tq), :]
        v = v_ref[0, h, pl.ds(off, tq), :]
        z = _nt_dot(qs[h], k)
        ls = _log_sigmoid(z)
        l1m = ls - z
        if diag:
            l1m = jnp.where(strict, l1m, 0.0)
        hi = l1m.astype(BF16)
        lo = (l1m - hi.astype(F32)).astype(BF16)
        tail = (jnp.dot(hi, later, preferred_element_type=F32)
                + jnp.dot(lo, later, preferred_element_type=F32) + rest)
        w = jnp.exp(ls + tail)
        if diag:
            w = jnp.where(strict, w, 0.0)
        acc = acc + jnp.dot(w.astype(BF16), v, preferred_element_type=F32)
        rest = rest + jnp.sum(l1m, axis=-1, keepdims=True)
        return rest, acc

    def step(ki, carries, diag):
        return tuple(head_step(h, ki, carries[h], diag) for h in range(GROUP_HEADS))

    init = (jnp.zeros((tq, 1), F32), jnp.zeros((tq, HEAD_DIM), F32))
    carries = step(qi, (init,) * GROUP_HEADS, True)

    def more(c):
        j, carries = c
        top = carries[0][0]
        for rest, _ in carries[1:]:
            top = jnp.maximum(top, rest)
        return jnp.logical_and(j < qi, jnp.max(top) > EXP_UNDERFLOW)

    def further(c):
        j, carries = c
        return j + 1, step(qi - 1 - j, carries, False)

    _, carries = lax.while_loop(more, further, (jnp.int32(0), carries))
    o_ref[0] = jnp.concatenate([acc for _, acc in carries], axis=-1)


def _sb(qkv, tq=256):
    B, _, S, d = qkv.shape
    H = GROUP_HEADS
    return pl.pallas_call(
        functools.partial(_sb_kernel, tq=tq),
        grid=(B, S // tq),
        in_specs=_qkv_specs(S, tq),
        out_specs=pl.BlockSpec((1, tq, H * d), lambda b, i: (b, i, 0)),
        out_shape=jax.ShapeDtypeStruct((B, S, H * d), F32),
        compiler_params=_params(("parallel", "arbitrary")),
        name="sb_attn",
    )(qkv, qkv, qkv)


def _mixout_kernel(y0_ref, y1_ref, y2_ref, y3_ref, g_ref, w_ref, x_ref, lg_ref, lb_ref, o_ref):
    def rms(y):
        return y * lax.rsqrt(jnp.mean(y * y, axis=-1, keepdims=True) + GROUP_RMS_EPS)

    y = jnp.concatenate([rms(r[...]) for r in (y0_ref, y1_ref, y2_ref, y3_ref)], axis=-1)
    y = (y * g_ref[...]).astype(BF16)
    mix = jnp.dot(y, w_ref[...], preferred_element_type=F32)
    o_ref[...] = _layer_norm(ALPHA * x_ref[...] + mix, lg_ref[...], lb_ref[...])


def _mixout(ys, g, w, x2d, lg, lb, tm=512):
    T, D = x2d.shape
    GW = GROUP_WIDTH

    def full(a):
        return pl.BlockSpec(a.shape, lambda i: (0,) * a.ndim)

    return pl.pallas_call(
        _mixout_kernel,
        grid=(T // tm,),
        in_specs=[pl.BlockSpec((tm, GW), lambda i: (i, 0))] * 4
        + [full(g), full(w), pl.BlockSpec((tm, D), lambda i: (i, 0)), full(lg), full(lb)],
        out_specs=pl.BlockSpec((tm, D), lambda i: (i, 0)),
        out_shape=jax.ShapeDtypeStruct((T, D), F32),
        compiler_params=_params(("parallel",)),
        name="mixout",
    )(*ys, g, w, x2d, lg, lb)


def _kv_kernel(m_ref, w_ref, k_ref, v_ref):
    kv = jnp.dot(m_ref[...].astype(BF16), w_ref[...], preferred_element_type=F32)
    k_ref[...] = kv[:, :D_MODEL].astype(BF16)
    v_ref[...] = kv[:, D_MODEL:].astype(BF16)


def _kv(mem2d, wkv, tm=512):
    M, D = mem2d.shape
    return pl.pallas_call(
        _kv_kernel,
        grid=(M // tm,),
        in_specs=[pl.BlockSpec((tm, D), lambda i: (i, 0)),
                  pl.BlockSpec((D, 2 * D), lambda i: (0, 0))],
        out_specs=[pl.BlockSpec((tm, D), lambda i: (i, 0))] * 2,
        out_shape=[jax.ShapeDtypeStruct((M, D), BF16)] * 2,
        compiler_params=_params(("parallel",)),
        name="mem_kv",
    )(mem2d, wkv)


PEER_TB = 32


def _chunk_major_spec(tm):
    return pl.BlockSpec((tm // PEER_TB, ROW_CHUNKS * PEER_TB, LANES), lambda i: (i, 0, 0))


def _chunk_major_shape(T):
    return jax.ShapeDtypeStruct((T // PEER_TB, ROW_CHUNKS * PEER_TB, LANES), F32)


def _to_chunk_major(y, ref):
    for g in range(y.shape[0] // PEER_TB):
        for c in range(ROW_CHUNKS):
            ref[g, c * PEER_TB:(c + 1) * PEER_TB, :] = (
                y[g * PEER_TB:(g + 1) * PEER_TB, c * LANES:(c + 1) * LANES])


def _xattn_kernel(x_ref, wq_ref, k_ref, v_ref, wo_ref, lg_ref, lb_ref, o_ref, oc_ref):
    x = x_ref[...]
    q = jnp.dot(x.astype(BF16), wq_ref[...], preferred_element_type=F32).astype(BF16)
    scale = XA_HEAD_DIM ** -0.5
    outs = []
    for h in range(XA_HEADS):
        sl = slice(h * XA_HEAD_DIM, (h + 1) * XA_HEAD_DIM)
        s = _nt_dot(q[:, sl], k_ref[0, :, sl]) * scale
        p = jnp.exp(s - jnp.max(s, axis=-1, keepdims=True))
        p = p / jnp.sum(p, axis=-1, keepdims=True)
        outs.append(jnp.dot(p.astype(BF16), v_ref[0, :, sl], preferred_element_type=F32))
    o = jnp.concatenate(outs, axis=-1).astype(BF16)
    xa = jnp.dot(o, wo_ref[...], preferred_element_type=F32)
    y = _layer_norm(ALPHA * x + xa, lg_ref[...], lb_ref[...])
    o_ref[...] = y
    _to_chunk_major(y, oc_ref)


def _xattn(x2d, wq, k, v, wo, lg, lb, seq_len, tm=512):
    T, D = x2d.shape
    M = k.shape[1]
    per_seq = seq_len // tm

    def full(a):
        return pl.BlockSpec(a.shape, lambda i: (0,) * a.ndim)

    return pl.pallas_call(
        _xattn_kernel,
        grid=(T // tm,),
        in_specs=[pl.BlockSpec((tm, D), lambda i: (i, 0)), full(wq),
                  pl.BlockSpec((1, M, D), lambda i: (i // per_seq, 0, 0)),
                  pl.BlockSpec((1, M, D), lambda i: (i // per_seq, 0, 0)),
                  full(wo), full(lg), full(lb)],
        out_specs=[pl.BlockSpec((tm, D), lambda i: (i, 0)), _chunk_major_spec(tm)],
        out_shape=[jax.ShapeDtypeStruct((T, D), F32), _chunk_major_shape(T)],
        compiler_params=_params(("parallel",)),
        name="xattn",
    )(x2d, wq, k, v, wo, lg, lb)


E_PER_TOK = PEER_HEADS * PEER_TOPK


def _staircase():
    return [(a, b) for a in range(PEER_TOPK) for b in range(PEER_TOPK // (a + 1))]


def _route_kernel(x_ref, wq_ref, k1_ref, k2_ref, e_ref, g_ref, v_scr, i_scr, *, ts):
    xb = x_ref[...].astype(BF16)
    key_id = lax.broadcasted_iota(I32, (N_KEYS, ts), 0).astype(F32)
    qd = 2 * PEER_HALF
    for h in range(PEER_HEADS):
        qry = jnp.dot(xb, wq_ref[:, h * qd:(h + 1) * qd], preferred_element_type=F32)
        for half, kref in ((0, k1_ref), (1, k2_ref)):
            qh = qry[:, half * PEER_HALF:(half + 1) * PEER_HALF].astype(BF16)
            s0 = _nt_dot(kref[...], qh)

            def pick(it, s, h=h, half=half):
                m = jnp.max(s, axis=0, keepdims=True)
                idx = jnp.min(jnp.where(s == m, key_id, float(N_KEYS)), axis=0, keepdims=True)
                v_scr[half, it, h:h + 1, :] = m
                i_scr[half, it, h:h + 1, :] = idx.astype(I32)
                return jnp.where(key_id == idx, -jnp.inf, s)

            lax.fori_loop(0, PEER_TOPK, pick, s0)

    cells = _staircase()
    cand = tuple(v_scr[0, a] + v_scr[1, b] for a, b in cells)
    expert = [i_scr[0, a] * N_KEYS + i_scr[1, b] for a, b in cells]

    def select(it, carry):
        cand, top = carry
        best_v, best_e, best_c = cand[0], expert[0], jnp.zeros((PEER_HEADS, ts), I32)
        for c in range(1, len(cells)):
            better = cand[c] > best_v
            best_v = jnp.where(better, cand[c], best_v)
            best_e = jnp.where(better, expert[c], best_e)
            best_c = jnp.where(better, c, best_c)
        top = jnp.where(it == 0, best_v, top)
        i_scr[0, it] = best_e * WORD_ROWS
        v_scr[0, it] = best_v
        cand = tuple(jnp.where(best_c == c, -jnp.inf, cand[c]) for c in range(len(cells)))
        return cand, top

    _, top = lax.fori_loop(0, PEER_TOPK, select, (cand, jnp.zeros((PEER_HEADS, ts), F32)))
    ex = [jnp.exp(v_scr[0, it] - top) for it in range(PEER_TOPK)]
    den = ex[0]
    for it in range(1, PEER_TOPK):
        den = den + ex[it]
    inv = 1.0 / den
    g_ref[...] = jnp.concatenate([e * inv for e in ex], axis=0).T
    rows = pltpu.bitcast(i_scr[0].reshape(E_PER_TOK, ts), F32)
    e_ref[...] = pltpu.bitcast(rows.T, I32)


def _route(x2d, wq, k1, k2, ts=256):
    T, D = x2d.shape
    nb = T // ts

    def full(a):
        return pl.BlockSpec(a.shape, lambda i: (0,) * a.ndim)

    blk = pl.BlockSpec((ts, E_PER_TOK), lambda i: (i, 0))
    return pl.pallas_call(
        functools.partial(_route_kernel, ts=ts),
        grid=(nb,),
        in_specs=[pl.BlockSpec((ts, D), lambda i: (i, 0)), full(wq), full(k1), full(k2)],
        out_specs=[blk, blk],
        out_shape=[jax.ShapeDtypeStruct((T, E_PER_TOK), I32),
                   jax.ShapeDtypeStruct((T, E_PER_TOK), F32)],
        scratch_shapes=[pltpu.VMEM((2, PEER_TOPK, PEER_HEADS, ts), F32),
                        pltpu.VMEM((2, PEER_TOPK, PEER_HEADS, ts), I32)],
        compiler_params=_params(("parallel",)),
        name="peer_route",
    )(x2d, wq, k1, k2)


TOK_GROUP = SUBLANES


N_GROUPS = PEER_TB // TOK_GROUP
SCORE_BUFS = 4


def _gather_rows(ids_ref, tab_ref, rows_scr, tok0):
    for tt in range(TOK_GROUP):
        for j in range(E_PER_TOK):
            r0 = pl.multiple_of(ids_ref[tok0 + tt, j], WORD_ROWS)
            rows_scr[tt, j * WORD_ROWS:(j + 1) * WORD_ROWS, :] = tab_ref[pl.ds(r0, WORD_ROWS), :]


def _gathered_rows(ids_ref, tab_ref, t):
    pieces = []
    for j in range(E_PER_TOK):
        r0 = pl.multiple_of(ids_ref[t, j], WORD_ROWS)
        pieces.append(tab_ref[pl.ds(r0, WORD_ROWS), :])
    return jnp.concatenate(pieces, axis=0)


def _token_chunks(tt):
    return pl.ds(tt, ROW_CHUNKS, stride=PEER_TB)


def _chunk_of_col():
    col = lax.broadcasted_iota(I32, (ROW_CHUNKS, E_PER_TOK * ROW_CHUNKS), 1)
    row = lax.broadcasted_iota(I32, (ROW_CHUNKS, E_PER_TOK * ROW_CHUNKS), 0)
    return (col & (ROW_CHUNKS - 1)) == row


def _peer_score_kernel(ids_ref, x_ref, tab_ref, s_ref, *scratch):
    rows_bufs, part_bufs = scratch[:SCORE_BUFS], scratch[SCORE_BUFS:]
    own_chunk = _chunk_of_col()
    c_id = lax.broadcasted_iota(I32, (E_PER_TOK * ROW_CHUNKS, E_PER_TOK), 0)
    e_id = lax.broadcasted_iota(I32, (E_PER_TOK * ROW_CHUNKS, E_PER_TOK), 1)
    fold = jnp.where(c_id >> CHUNK_SHIFT == e_id, 1.0, 0.0).astype(BF16)

    for g in range(N_GROUPS):
        tok0 = g * TOK_GROUP
        rows_scr, part_scr = rows_bufs[g % SCORE_BUFS], part_bufs[g % SCORE_BUFS]
        _gather_rows(ids_ref, tab_ref, rows_scr, tok0)
        for tt in range(TOK_GROUP):
            rows = pltpu.bitcast(rows_scr[tt], BF16)
            xt = x_ref[0, _token_chunks(tok0 + tt), :].astype(BF16)
            full = _nt_dot(xt, rows)
            part_scr[tt * ROW_CHUNKS:(tt + 1) * ROW_CHUNKS, :] = jnp.where(own_chunk, full, 0.0)
        part = part_scr[...]
        hi = part.astype(BF16)
        lo = (part - hi.astype(F32)).astype(BF16)
        sc = (jnp.dot(hi, fold, preferred_element_type=F32)
              + jnp.dot(lo, fold, preferred_element_type=F32))
        sc = jnp.sum(sc.reshape(TOK_GROUP, ROW_CHUNKS, E_PER_TOK), axis=1)
        s_ref[tok0:tok0 + TOK_GROUP, :] = sc


def _peer_mix_kernel(ids_ref, s_ref, gate_ref, tab_ref, o_ref, *rows_bufs):
    own_chunk = _chunk_of_col()
    e_id = lax.broadcasted_iota(I32, (E_PER_TOK, E_PER_TOK * ROW_CHUNKS), 0)
    c_id = lax.broadcasted_iota(I32, (E_PER_TOK, E_PER_TOK * ROW_CHUNKS), 1)
    spread = jnp.where(c_id >> CHUNK_SHIFT == e_id, 1.0, 0.0).astype(BF16)

    for g in range(N_GROUPS):
        tok0 = g * TOK_GROUP
        sl = slice(tok0, tok0 + TOK_GROUP)
        s = s_ref[sl, :]
        act = 0.5 * s * (1.0 + lax.erf(s * (2.0 ** -0.5)))
        coef = (gate_ref[sl, :] * act).astype(BF16)
        coef = jnp.dot(coef, spread, preferred_element_type=F32)
        for tt in range(TOK_GROUP):
            rows = pltpu.bitcast(_gathered_rows(ids_ref, tab_ref, tok0 + tt), BF16)
            ct = jnp.broadcast_to(coef[tt:tt + 1], (ROW_CHUNKS, E_PER_TOK * ROW_CHUNKS))
            ct = jnp.where(own_chunk, ct, 0.0).astype(BF16)
            o_ref[0, _token_chunks(tok0 + tt), :] = jnp.dot(ct, rows, preferred_element_type=F32)


def _peer_experts(ids, gate, xc, tab_u, tab_v):
    T = ids.shape[0]
    tb = PEER_TB
    nb = T // tb
    ids_spec = pl.BlockSpec((tb, E_PER_TOK), lambda i: (i, 0), memory_space=pltpu.SMEM)
    tab_spec = pl.BlockSpec(tab_u.shape, lambda i: (0, 0), pipeline_mode=pl.Buffered(1))
    tok_spec = pl.BlockSpec((tb, E_PER_TOK), lambda i: (i, 0))
    row_spec = _chunk_major_spec(tb)
    rows_scr = [pltpu.VMEM((TOK_GROUP, E_PER_TOK * WORD_ROWS, LANES), I32)] * SCORE_BUFS
    part_scr = [pltpu.VMEM((TOK_GROUP * ROW_CHUNKS, E_PER_TOK * ROW_CHUNKS), F32)] * SCORE_BUFS
    s = pl.pallas_call(
        _peer_score_kernel,
        grid=(nb,),
        in_specs=[ids_spec, row_spec, tab_spec],
        out_specs=tok_spec,
        out_shape=jax.ShapeDtypeStruct((T, E_PER_TOK), F32),
        scratch_shapes=rows_scr + part_scr,
        compiler_params=_params(("arbitrary",)),
        name="peer_score",
    )(ids, xc, tab_u)
    return pl.pallas_call(
        _peer_mix_kernel,
        grid=(nb,),
        in_specs=[ids_spec, tok_spec, tok_spec, tab_spec],
        out_specs=row_spec,
        out_shape=_chunk_major_shape(T),
        scratch_shapes=rows_scr[:1],
        compiler_params=_params(("arbitrary",)),
        name="peer_mix",
    )(ids, s, gate, tab_v)


def _resln_kernel(x_ref, y_ref, lg_ref, lb_ref, o_ref):
    G = x_ref.shape[0]
    z = ALPHA * x_ref[...] + y_ref[...]
    zs = [z[:, c * PEER_TB:(c + 1) * PEER_TB, :] for c in range(ROW_CHUNKS)]
    tot = zs[0]
    for c in range(1, ROW_CHUNKS):
        tot = tot + zs[c]
    mu = jnp.sum(tot, axis=-1, keepdims=True) * (1.0 / D_MODEL)
    zc = [a - mu for a in zs]
    sq = zc[0] * zc[0]
    for c in range(1, ROW_CHUNKS):
        sq = sq + zc[c] * zc[c]
    rstd = lax.rsqrt(jnp.sum(sq, axis=-1, keepdims=True) * (1.0 / D_MODEL) + LN_EPS)
    for c in range(ROW_CHUNKS):
        cols = slice(c * LANES, (c + 1) * LANES)
        y = (zc[c] * rstd).reshape(G * PEER_TB, LANES)
        o_ref[:, cols] = y * lg_ref[:, cols] + lb_ref[:, cols]


def _resln(xc, yc, lg, lb, tm=512):
    T = xc.shape[0] * PEER_TB
    blk = _chunk_major_spec(tm)
    par = pl.BlockSpec((1, D_MODEL), lambda i: (0, 0))
    return pl.pallas_call(
        _resln_kernel,
        grid=(T // tm,),
        in_specs=[blk, blk, par, par],
        out_specs=pl.BlockSpec((tm, D_MODEL), lambda i: (i, 0)),
        out_shape=jax.ShapeDtypeStruct((T, D_MODEL), F32),
        compiler_params=_params(("parallel",)),
        name="res_ln",
    )(xc, yc, lg, lb)


def _pack_kernel(t_ref, o_ref):
    def bf16_bits(v):
        return pltpu.bitcast(v.astype(BF16).astype(F32), jnp.uint32)

    te = t_ref.shape[0]
    for r in range(WORD_ROWS):
        lo = bf16_bits(t_ref[:, (2 * r) * LANES:(2 * r + 1) * LANES])
        hi = bf16_bits(t_ref[:, (2 * r + 1) * LANES:(2 * r + 2) * LANES])
        words = (hi & jnp.uint32(0xFFFF0000)) | (lo >> 16)
        o_ref[pl.ds(r, te, stride=WORD_ROWS), :] = pltpu.bitcast(words, I32)


def _pack_table(t, layer, te=512):
    _, E, D = t.shape
    return pl.pallas_call(
        _pack_kernel,
        grid=(E // te,),
        in_specs=[pl.BlockSpec((None, te, D), lambda i: (layer, i, 0))],
        out_specs=pl.BlockSpec((te * WORD_ROWS, LANES), lambda i: (i, 0)),
        out_shape=jax.ShapeDtypeStruct((E * WORD_ROWS, LANES), I32),
        compiler_params=_params(("parallel",)),
        name="pack_table",
    )(t)


def _block_diag(w):
    H, d, _ = w.shape
    out = jnp.zeros((H * d, H * d), w.dtype)
    for h in range(H):
        out = out.at[h * d:(h + 1) * d, h * d:(h + 1) * d].set(w[h])
    return out


def _cat_w_in_kernel(w_ref, o_ref):
    GW = GROUP_WIDTH
    off_f = 5 * GW
    off_sb = off_f + GROUP_HEADS
    w = w_ref[...].astype(BF16)
    o_ref[:, :off_f] = w[:, :off_f]
    o_ref[:, off_f:C_F] = w[:, off_sb:N_IN]
    o_ref[:, C_F:C_END] = jnp.zeros((w.shape[0], C_END - C_F), BF16)
    o_ref[:, C_F:C_F + GROUP_HEADS] = w[:, off_f:off_sb]


def _cat_w_in(w_in, layer, tr=256):
    _, D, N = w_in.shape
    return pl.pallas_call(
        _cat_w_in_kernel,
        grid=(D // tr,),
        in_specs=[pl.BlockSpec((None, tr, N), lambda i: (layer, i, 0))],
        out_specs=pl.BlockSpec((tr, C_END), lambda i: (i, 0)),
        out_shape=jax.ShapeDtypeStruct((D, C_END), BF16),
        compiler_params=_params(("parallel",)),
        name="regroup_w_in",
    )(w_in)


def _row(v, width=None):
    v = v.reshape(1, -1).astype(F32)
    if width is not None and v.shape[1] < width:
        v = jnp.pad(v, ((0, 0), (0, width - v.shape[1])))
    return v


def _layer(x2d, kmem, vmem_, B, S, p, tables, layer, tq=256):
    T = B * S
    rg, fox, sb, sc, f = _inproj(x2d, _cat_w_in(tables[2], layer), B, S)
    yrg, ysc, cumt = _rgsc(
        rg.reshape(B, S, -1), sc.reshape(B, S, -1), f.reshape(B, S, -1),
        p["rg_conv_w"], _row(p["rg_conv_b"]), _block_diag(p["rg_wa"]).astype(BF16), _row(p["rg_ba"]),
        _block_diag(p["rg_wi"]).astype(BF16), _row(p["rg_bi"]), _row(p["rg_lambda"]),
        p["sc_conv_w"], _row(p["fox_bf"], LANES), tq)
    yfox = _fox(fox, cumt, tq=tq)
    ysb = _sb(sb, tq=tq)
    ys = (yrg.reshape(T, -1), yfox.reshape(T, -1), ysb.reshape(T, -1), ysc.reshape(T, -1))
    x1 = _mixout(ys, _row(p["mix_norm_g"]), p["w_out"].astype(BF16), x2d,
                 _row(p["ln1_g"]), _row(p["ln1_b"]))
    x2, x2c = _xattn(x1, p["xa_wq"].astype(BF16), kmem, vmem_, p["xa_wo"].astype(BF16),
                     _row(p["ln2_g"]), _row(p["ln2_b"]), S)
    ids, gate = _route(x2, p["peer_wq"].astype(BF16), p["peer_k1"].astype(BF16),
                       p["peer_k2"].astype(BF16))
    ffc = _peer_experts(ids, gate, x2c, _pack_table(tables[0], layer), _pack_table(tables[1], layer))
    return _resln(x2c, ffc, _row(p["ln3_g"]), _row(p["ln3_b"]))


_LAYER_PARAMS = ("w_in", "w_out", "rg_conv_w", "rg_conv_b", "rg_wa", "rg_ba", "rg_wi", "rg_bi",
                 "rg_lambda", "fox_bf", "sc_conv_w", "mix_norm_g", "ln1_g", "ln1_b", "xa_wq",
                 "xa_wkv", "xa_wo", "ln2_g", "ln2_b", "peer_wq", "peer_k1", "peer_k2", "peer_u",
                 "peer_v", "ln3_g", "ln3_b")


def kernel(x, mem, w_in, w_out, rg_conv_w, rg_conv_b, rg_wa, rg_ba, rg_wi, rg_bi, rg_lambda, fox_bf, sc_conv_w, mix_norm_g, ln1_g, ln1_b, xa_wq, xa_wkv, xa_wo, ln2_g, ln2_b, peer_wq, peer_k1, peer_k2, peer_u, peer_v, ln3_g, ln3_b):
    stacked = dict(zip(_LAYER_PARAMS, (
        w_in, w_out, rg_conv_w, rg_conv_b, rg_wa, rg_ba, rg_wi, rg_bi, rg_lambda, fox_bf,
        sc_conv_w, mix_norm_g, ln1_g, ln1_b, xa_wq, xa_wkv, xa_wo, ln2_g, ln2_b, peer_wq,
        peer_k1, peer_k2, peer_u, peer_v, ln3_g, ln3_b)))
    B, S, D = x.shape
    M = mem.shape[1]
    x2d = x.reshape(B * S, D)
    mem2d = mem.reshape(B * M, D)
    for l in range(w_in.shape[0]):
        p = {k: v[l] for k, v in stacked.items() if k not in ("peer_u", "peer_v", "w_in")}
        kmem, vmem_ = _kv(mem2d, p["xa_wkv"].astype(BF16))
        x2d = _layer(x2d, kmem.reshape(B, M, D), vmem_.reshape(B, M, D), B, S, p,
                     (peer_u, peer_v, w_in), l)
    return x2d.reshape(B, S, D)
```

```python
import functools
import math

import jax
import jax.numpy as jnp
from jax import lax
from jax.experimental import pallas as pl
from jax.experimental.pallas import tpu as pltpu

F32 = jnp.float32
BF16 = jnp.bfloat16
I32 = jnp.int32

D_MODEL = 1024
GROUP_WIDTH = 256
GROUP_HEADS = 4
HEAD_DIM = 64
N_IN = 2820
RGLRU_C = 8.0
XA_HEADS = 4
XA_HEAD_DIM = D_MODEL // XA_HEADS
PEER_HEADS = 8
N_KEYS = 128
PEER_HALF = 128
PEER_TOPK = 16
DEPTH = 2
ALPHA = (2.0 * DEPTH) ** 0.25
LN_EPS = 1e-5

SUBLANES = 8
LANES = 128
WORD_ROWS = D_MODEL // (2 * LANES)
ROW_CHUNKS = D_MODEL // LANES
CHUNK_SHIFT = ROW_CHUNKS.bit_length() - 1

V7X_VMEM_BYTES = 64 * 1024 * 1024
VMEM_LIMIT = V7X_VMEM_BYTES * 3 // 4
GROUP_RMS_EPS = 1e-6
GELU_TANH_CUBIC = 0.044715


def _params(sem, vmem=VMEM_LIMIT):
    return pltpu.CompilerParams(dimension_semantics=sem, vmem_limit_bytes=vmem)


def _layer_norm(z, g, b):
    mu = jnp.mean(z, axis=-1, keepdims=True)
    zc = z - mu
    var = jnp.mean(zc * zc, axis=-1, keepdims=True)
    return zc * lax.rsqrt(var + LN_EPS) * g + b


def _log_sigmoid(z):
    return jnp.minimum(z, 0.0) - jnp.log1p(jnp.exp(-jnp.abs(z)))


def _nt_dot(a, b):
    return lax.dot_general(a, b, (((1,), (1,)), ((), ())), preferred_element_type=F32)


C_RG, C_FOX, C_SB, C_SC, C_F, C_END = 0, 512, 1280, 2048, 2816, 2944


def _inproj_kernel(x_ref, w_ref, rg_ref, fox_ref, sb_ref, sc_ref, f_ref):
    xb = x_ref[...].astype(BF16)

    def mm(lo, hi):
        return jnp.dot(xb, w_ref[:, lo:hi], preferred_element_type=F32)

    def heads(ref, lo, hi):
        qkv = mm(lo, hi)
        for j in range(3 * GROUP_HEADS):
            ref[0, j] = qkv[:, j * HEAD_DIM:(j + 1) * HEAD_DIM].astype(BF16)

    rg_ref[...] = mm(C_RG, C_FOX)
    heads(fox_ref, C_FOX, C_SB)
    heads(sb_ref, C_SB, C_SC)
    sc_ref[...] = mm(C_SC, C_F)
    f_ref[...] = mm(C_F, C_END)


def _inproj(x2d, w_cat, B, S, tm=512):
    T, D = x2d.shape
    per_seq = S // tm
    flat = ((C_FOX - C_RG, F32), (C_F - C_SC, F32), (C_END - C_F, F32))
    flat_specs = [pl.BlockSpec((tm, w), lambda i: (i, 0)) for w, _ in flat]
    flat_shapes = [jax.ShapeDtypeStruct((T, w), dt) for w, dt in flat]
    head_spec = pl.BlockSpec((1, 3 * GROUP_HEADS, tm, HEAD_DIM),
                             lambda i: (i // per_seq, 0, i % per_seq, 0))
    head_shape = jax.ShapeDtypeStruct((B, 3 * GROUP_HEADS, S, HEAD_DIM), BF16)
    return pl.pallas_call(
        _inproj_kernel,
        grid=(T // tm,),
        in_specs=[pl.BlockSpec((tm, D), lambda i: (i, 0)),
                  pl.BlockSpec((D, C_END), lambda i: (0, 0))],
        out_specs=[flat_specs[0], head_spec, head_spec, flat_specs[1], flat_specs[2]],
        out_shape=[flat_shapes[0], head_shape, head_shape, flat_shapes[1], flat_shapes[2]],
        compiler_params=_params(("parallel",)),
        name="inproj",
    )(x2d, w_cat)


def _rgsc_kernel(rg_ref, sc_ref, f_ref, cw_ref, cb_ref, wa_ref, ba_ref, wi_ref, bi_ref, lam_ref,
                 scw_ref, fb_ref, yrg_ref, ysc_ref, cumt_ref, xprev, chprev, hprev, cprev,
                 *, ts, tk):
    GW = GROUP_WIDTH

    @pl.when(pl.program_id(1) == 0)
    def _():
        xprev[...] = jnp.zeros_like(xprev)
        chprev[...] = jnp.zeros_like(chprev)
        hprev[...] = jnp.zeros_like(hprev)
        cprev[...] = jnp.zeros_like(cprev)

    row = lax.broadcasted_iota(I32, (ts, GW), 0)

    def delayed(prev, cur, d):
        ext = jnp.concatenate([prev, cur], axis=0)
        return pltpu.roll(ext, d, 0)[SUBLANES:]

    xr = rg_ref[0, :, :GW]
    gate = rg_ref[0, :, GW:]
    xp = xprev[...]
    cw = cw_ref[...]
    xc = (delayed(xp, xr, 3) * cw[0:1] + delayed(xp, xr, 2) * cw[1:2]
          + delayed(xp, xr, 1) * cw[2:3] + xr * cw[3:4] + cb_ref[...])
    xprev[...] = xr[ts - SUBLANES:]
    xcb = xc.astype(BF16)
    r = jax.nn.sigmoid(jnp.dot(xcb, wa_ref[...], preferred_element_type=F32) + ba_ref[...])
    ig = jax.nn.sigmoid(jnp.dot(xcb, wi_ref[...], preferred_element_type=F32) + bi_ref[...])
    z = -lam_ref[...]
    softplus = jnp.maximum(z, 0.0) + jnp.log1p(jnp.exp(-jnp.abs(z)))
    log_a = -RGLRU_C * r * softplus
    a = jnp.exp(log_a)
    u = jnp.sqrt(-jnp.tanh(log_a) * (a * a + 1.0)) * (ig * xc)
    acc_a, acc_b = a, u
    d = 1
    while d < ts:
        keep = row >= d
        a_s = jnp.where(keep, pltpu.roll(acc_a, d, 0), 1.0)
        b_s = jnp.where(keep, pltpu.roll(acc_b, d, 0), 0.0)
        acc_b = acc_a * b_s + acc_b
        acc_a = acc_a * a_s
        d *= 2
    h = acc_b + acc_a * hprev[...]
    hprev[...] = h[ts - 1:]
    c0 = math.sqrt(2.0 / math.pi)
    gelu = 0.5 * gate * (1.0 + jnp.tanh(c0 * (gate + GELU_TANH_CUBIC * gate * gate * gate)))
    yrg_ref[0] = h * gelu

    bg = sc_ref[0, :, :GW]
    ch = sc_ref[0, :, GW:2 * GW] * sc_ref[0, :, 2 * GW:]
    cp = chprev[...]
    sw = scw_ref[...]
    ysc_ref[0] = bg * (delayed(cp, ch, 2) * sw[0:1] + delayed(cp, ch, 1) * sw[1:2] + ch * sw[2:3])
    chprev[...] = ch[ts - SUBLANES:]

    rowf = lax.broadcasted_iota(I32, (ts, LANES), 0)
    c = _log_sigmoid(f_ref[0] + fb_ref[...])
    d = 1
    while d < ts:
        c = c + jnp.where(rowf >= d, pltpu.roll(c, d, 0), 0.0)
        d *= 2
    c = c + cprev[...]
    cprev[...] = c[ts - 1:]
    ct = c.T[:SUBLANES]
    for j in range(ts // tk):
        cumt_ref[0, j] = ct[:, j * tk:(j + 1) * tk]


def _rgsc(rg, sc, f, cw, cb, wa, ba, wi, bi, lam, scw, fb, tk, ts=512):
    B, S, _ = rg.shape
    GW = GROUP_WIDTH
    per = ts // tk

    def full(a):
        return pl.BlockSpec(a.shape, lambda b, t: (0,) * a.ndim)

    def seq(w):
        return pl.BlockSpec((1, ts, w), lambda b, t: (b, t, 0))

    params = (cw, cb, wa, ba, wi, bi, lam, scw, fb)
    return pl.pallas_call(
        functools.partial(_rgsc_kernel, ts=ts, tk=tk),
        grid=(B, S // ts),
        in_specs=[seq(2 * GW), seq(3 * GW), seq(LANES)] + [full(p) for p in params],
        out_specs=[seq(GW), seq(GW),
                   pl.BlockSpec((1, per, SUBLANES, tk), lambda b, t: (b, t, 0, 0))],
        out_shape=[jax.ShapeDtypeStruct((B, S, GW), F32), jax.ShapeDtypeStruct((B, S, GW), F32),
                   jax.ShapeDtypeStruct((B, S // tk, SUBLANES, tk), F32)],
        scratch_shapes=[pltpu.VMEM((SUBLANES, GW), F32), pltpu.VMEM((SUBLANES, GW), F32),
                        pltpu.VMEM((1, GW), F32), pltpu.VMEM((1, LANES), F32)],
        compiler_params=_params(("parallel", "arbitrary")),
        name="rgsc",
    )(rg, sc, f, *params)


HEADS_TOGETHER = GROUP_HEADS


assert math.log2(HEAD_DIM) % 2 == 0


def _fox_kernel(q_ref, k_ref, v_ref, cr_ref, o_ref, *, tq):
    qi = pl.program_id(1)
    scale = HEAD_DIM ** -0.5
    rowi = lax.broadcasted_iota(I32, (tq, tq), 0)
    coli = lax.broadcasted_iota(I32, (tq, tq), 1)
    causal = coli <= rowi
    qs = [q_ref[0, h] * scale for h in range(GROUP_HEADS)]

    def head_step(h, ki, carry, diag):
        m, l, acc = carry
        off = pl.multiple_of(ki * tq, tq)
        k = k_ref[0, h, pl.ds(off, tq), :]
        v = v_ref[0, h, pl.ds(off, tq), :]
        s = _nt_dot(qs[h], k) - cr_ref[0, ki, h:h + 1, :]
        if diag:
            s = jnp.where(causal, s, -jnp.inf)
        m_new = jnp.maximum(m, jnp.max(s, axis=-1, keepdims=True))
        alpha = jnp.exp(m - m_new)
        p = jnp.exp(s - m_new)
        l = alpha * l + jnp.sum(p, axis=-1, keepdims=True)
        acc = alpha * acc + jnp.dot(p.astype(BF16), v, preferred_element_type=F32)
        return m_new, l, acc

    def step(ki, carries, diag, heads):
        return tuple(head_step(h, ki, c, diag) for h, c in zip(heads, carries))

    init = (jnp.full((tq, 1), -jnp.inf, F32), jnp.zeros((tq, 1), F32),
            jnp.zeros((tq, HEAD_DIM), F32))
    outs = []
    for h0 in range(0, GROUP_HEADS, HEADS_TOGETHER):
        heads = range(h0, h0 + HEADS_TOGETHER)
        carries = lax.fori_loop(0, qi, functools.partial(step, diag=False, heads=heads),
                                (init,) * HEADS_TOGETHER)
        carries = step(qi, carries, True, heads)
        outs += [acc / l for _, l, acc in carries]
    o_ref[0] = jnp.concatenate(outs, axis=-1)


def _qkv_specs(S, tq):
    H, d = GROUP_HEADS, HEAD_DIM
    return [pl.BlockSpec((1, H, tq, d), lambda b, i: (b, 0, i, 0)),
            pl.BlockSpec((1, H, S, d), lambda b, i: (b, 1, 0, 0)),
            pl.BlockSpec((1, H, S, d), lambda b, i: (b, 2, 0, 0))]


def _fox(qkv, cumt, tq=256):
    B, _, S, d = qkv.shape
    H = GROUP_HEADS
    return pl.pallas_call(
        functools.partial(_fox_kernel, tq=tq),
        grid=(B, S // tq),
        in_specs=_qkv_specs(S, tq)
        + [pl.BlockSpec((1, S // tq, SUBLANES, tq), lambda b, i: (b, 0, 0, 0))],
        out_specs=pl.BlockSpec((1, tq, H * d), lambda b, i: (b, i, 0)),
        out_shape=jax.ShapeDtypeStruct((B, S, H * d), F32),
        compiler_params=_params(("parallel", "arbitrary")),
        name="fox_attn",
    )(qkv, qkv, qkv, cumt)


EXP_UNDERFLOW = -104.0


def _sb_kernel(q_ref, k_ref, v_ref, o_ref, *, tq):
    qi = pl.program_id(1)
    scale = HEAD_DIM ** -0.5
    rowi = lax.broadcasted_iota(I32, (tq, tq), 0)
    coli = lax.broadcasted_iota(I32, (tq, tq), 1)
    strict = coli < rowi
    later = jnp.where(rowi > coli, 1.0, 0.0).astype(BF16)
    qs = [q_ref[0, h] * scale for h in range(GROUP_HEADS)]
    def head_step(h, ki, carry, diag):
        rest, acc = carry
        off = pl.multiple_of(ki * tq, tq)
        k = k_ref[0, h, pl.ds(off, tq), :]
        v = v_ref[0, h, pl.ds(off, tq), :]
        z = _nt_dot(qs[h], k)
        ls = _log_sigmoid(z)
        l1m = ls - z
        if diag:
            l1m = jnp.where(strict, l1m, 0.0)
        hi = l1m.astype(BF16)
        lo = (l1m - hi.astype(F32)).astype(BF16)
        tail = (jnp.dot(hi, later, preferred_element_type=F32)
                + jnp.dot(lo, later, preferred_element_type=F32) + rest)
        w = jnp.exp(ls + tail)
        if diag:
            w = jnp.where(strict, w, 0.0)
        acc = acc + jnp.dot(w.astype(BF16), v, preferred_element_type=F32)
        rest = rest + jnp.sum(l1m, axis=-1, keepdims=True)
        return rest, acc

    def step(ki, carries, diag):
        return tuple(head_step(h, ki, carries[h], diag) for h in range(GROUP_HEADS))

    init = (jnp.zeros((tq, 1), F32), jnp.zeros((tq, HEAD_DIM), F32))
    carries = step(qi, (init,) * GROUP_HEADS, True)

    def more(c):
        j, carries = c
        top = carries[0][0]
        for rest, _ in carries[1:]:
            top = jnp.maximum(top, rest)
        return jnp.logical_and(j < qi, jnp.max(top) > EXP_UNDERFLOW)

    def further(c):
        j, carries = c
        return j + 1, step(qi - 1 - j, carries, False)

    _, carries = lax.while_loop(more, further, (jnp.int32(0), carries))
    o_ref[0] = jnp.concatenate([acc for _, acc in carries], axis=-1)


def _sb(qkv, tq=256):
    B, _, S, d = qkv.shape
    H = GROUP_HEADS
    return pl.pallas_call(
        functools.partial(_sb_kernel, tq=tq),
        grid=(B, S // tq),
        in_specs=_qkv_specs(S, tq),
        out_specs=pl.BlockSpec((1, tq, H * d), lambda b, i: (b, i, 0)),
        out_shape=jax.ShapeDtypeStruct((B, S, H * d), F32),
        compiler_params=_params(("parallel", "arbitrary")),
        name="sb_attn",
    )(qkv, qkv, qkv)


def _mixout_kernel(y0_ref, y1_ref, y2_ref, y3_ref, g_ref, w_ref, x_ref, lg_ref, lb_ref, o_ref):
    def rms(y):
        return y * lax.rsqrt(jnp.mean(y * y, axis=-1, keepdims=True) + GROUP_RMS_EPS)

    y = jnp.concatenate([rms(r[...]) for r in (y0_ref, y1_ref, y2_ref, y3_ref)], axis=-1)
    y = (y * g_ref[...]).astype(BF16)
    mix = jnp.dot(y, w_ref[...], preferred_element_type=F32)
    o_ref[...] = _layer_norm(ALPHA * x_ref[...] + mix, lg_ref[...], lb_ref[...])


def _mixout(ys, g, w, x2d, lg, lb, tm=512):
    T, D = x2d.shape
    GW = GROUP_WIDTH

    def full(a):
        return pl.BlockSpec(a.shape, lambda i: (0,) * a.ndim)

    return pl.pallas_call(
        _mixout_kernel,
        grid=(T // tm,),
        in_specs=[pl.BlockSpec((tm, GW), lambda i: (i, 0))] * 4
        + [full(g), full(w), pl.BlockSpec((tm, D), lambda i: (i, 0)), full(lg), full(lb)],
        out_specs=pl.BlockSpec((tm, D), lambda i: (i, 0)),
        out_shape=jax.ShapeDtypeStruct((T, D), F32),
        compiler_params=_params(("parallel",)),
        name="mixout",
    )(*ys, g, w, x2d, lg, lb)


def _kv_kernel(m_ref, w_ref, k_ref, v_ref):
    kv = jnp.dot(m_ref[...].astype(BF16), w_ref[...], preferred_element_type=F32)
    k_ref[...] = kv[:, :D_MODEL].astype(BF16)
    v_ref[...] = kv[:, D_MODEL:].astype(BF16)


def _kv(mem2d, wkv, tm=512):
    M, D = mem2d.shape
    return pl.pallas_call(
        _kv_kernel,
        grid=(M // tm,),
        in_specs=[pl.BlockSpec((tm, D), lambda i: (i, 0)),
                  pl.BlockSpec((D, 2 * D), lambda i: (0, 0))],
        out_specs=[pl.BlockSpec((tm, D), lambda i: (i, 0))] * 2,
        out_shape=[jax.ShapeDtypeStruct((M, D), BF16)] * 2,
        compiler_params=_params(("parallel",)),
        name="mem_kv",
    )(mem2d, wkv)


PEER_TB = 32


def _chunk_major_spec(tm):
    return pl.BlockSpec((tm // PEER_TB, ROW_CHUNKS * PEER_TB, LANES), lambda i: (i, 0, 0))


def _chunk_major_shape(T):
    return jax.ShapeDtypeStruct((T // PEER_TB, ROW_CHUNKS * PEER_TB, LANES), F32)


def _to_chunk_major(y, ref):
    for g in range(y.shape[0] // PEER_TB):
        for c in range(ROW_CHUNKS):
            ref[g, c * PEER_TB:(c + 1) * PEER_TB, :] = (
                y[g * PEER_TB:(g + 1) * PEER_TB, c * LANES:(c + 1) * LANES])


def _xattn_kernel(x_ref, wq_ref, k_ref, v_ref, wo_ref, lg_ref, lb_ref, o_ref, oc_ref):
    x = x_ref[...]
    q = jnp.dot(x.astype(BF16), wq_ref[...], preferred_element_type=F32).astype(BF16)
    scale = XA_HEAD_DIM ** -0.5
    outs = []
    for h in range(XA_HEADS):
        sl = slice(h * XA_HEAD_DIM, (h + 1) * XA_HEAD_DIM)
        s = _nt_dot(q[:, sl], k_ref[0, :, sl]) * scale
        p = jnp.exp(s - jnp.max(s, axis=-1, keepdims=True))
        p = p / jnp.sum(p, axis=-1, keepdims=True)
        outs.append(jnp.dot(p.astype(BF16), v_ref[0, :, sl], preferred_element_type=F32))
    o = jnp.concatenate(outs, axis=-1).astype(BF16)
    xa = jnp.dot(o, wo_ref[...], preferred_element_type=F32)
    y = _layer_norm(ALPHA * x + xa, lg_ref[...], lb_ref[...])
    o_ref[...] = y
    _to_chunk_major(y, oc_ref)


def _xattn(x2d, wq, k, v, wo, lg, lb, seq_len, tm=512):
    T, D = x2d.shape
    M = k.shape[1]
    per_seq = seq_len // tm

    def full(a):
        return pl.BlockSpec(a.shape, lambda i: (0,) * a.ndim)

    return pl.pallas_call(
        _xattn_kernel,
        grid=(T // tm,),
        in_specs=[pl.BlockSpec((tm, D), lambda i: (i, 0)), full(wq),
                  pl.BlockSpec((1, M, D), lambda i: (i // per_seq, 0, 0)),
                  pl.BlockSpec((1, M, D), lambda i: (i // per_seq, 0, 0)),
                  full(wo), full(lg), full(lb)],
        out_specs=[pl.BlockSpec((tm, D), lambda i: (i, 0)), _chunk_major_spec(tm)],
        out_shape=[jax.ShapeDtypeStruct((T, D), F32), _chunk_major_shape(T)],
        compiler_params=_params(("parallel",)),
        name="xattn",
    )(x2d, wq, k, v, wo, lg, lb)


E_PER_TOK = PEER_HEADS * PEER_TOPK


def _staircase():
    return [(a, b) for a in range(PEER_TOPK) for b in range(PEER_TOPK // (a + 1))]


def _route_kernel(x_ref, wq_ref, k1_ref, k2_ref, e_ref, g_ref, v_scr, i_scr, *, ts):
    xb = x_ref[...].astype(BF16)
    key_id = lax.broadcasted_iota(I32, (N_KEYS, ts), 0).astype(F32)
    qd = 2 * PEER_HALF
    for h in range(PEER_HEADS):
        qry = jnp.dot(xb, wq_ref[:, h * qd:(h + 1) * qd], preferred_element_type=F32)
        for half, kref in ((0, k1_ref), (1, k2_ref)):
            qh = qry[:, half * PEER_HALF:(half + 1) * PEER_HALF].astype(BF16)
            s0 = _nt_dot(kref[...], qh)

            def pick(it, s, h=h, half=half):
                m = jnp.max(s, axis=0, keepdims=True)
                idx = jnp.min(jnp.where(s == m, key_id, float(N_KEYS)), axis=0, keepdims=True)
                v_scr[half, it, h:h + 1, :] = m
                i_scr[half, it, h:h + 1, :] = idx.astype(I32)
                return jnp.where(key_id == idx, -jnp.inf, s)

            lax.fori_loop(0, PEER_TOPK, pick, s0)

    cells = _staircase()
    cand = tuple(v_scr[0, a] + v_scr[1, b] for a, b in cells)
    expert = [i_scr[0, a] * N_KEYS + i_scr[1, b] for a, b in cells]

    def select(it, carry):
        cand, top = carry
        best_v, best_e, best_c = cand[0], expert[0], jnp.zeros((PEER_HEADS, ts), I32)
        for c in range(1, len(cells)):
            better = cand[c] > best_v
            best_v = jnp.where(better, cand[c], best_v)
            best_e = jnp.where(better, expert[c], best_e)
            best_c = jnp.where(better, c, best_c)
        top = jnp.where(it == 0, best_v, top)
        i_scr[0, it] = best_e * WORD_ROWS
        v_scr[0, it] = best_v
        cand = tuple(jnp.where(best_c == c, -jnp.inf, cand[c]) for c in range(len(cells)))
        return cand, top

    _, top = lax.fori_loop(0, PEER_TOPK, select, (cand, jnp.zeros((PEER_HEADS, ts), F32)))
    ex = [jnp.exp(v_scr[0, it] - top) for it in range(PEER_TOPK)]
    den = ex[0]
    for it in range(1, PEER_TOPK):
        den = den + ex[it]
    inv = 1.0 / den
    g_ref[...] = jnp.concatenate([e * inv for e in ex], axis=0).T
    rows = pltpu.bitcast(i_scr[0].reshape(E_PER_TOK, ts), F32)
    e_ref[...] = pltpu.bitcast(rows.T, I32)


def _route(x2d, wq, k1, k2, ts=256):
    T, D = x2d.shape
    nb = T // ts

    def full(a):
        return pl.BlockSpec(a.shape, lambda i: (0,) * a.ndim)

    blk = pl.BlockSpec((ts, E_PER_TOK), lambda i: (i, 0))
    return pl.pallas_call(
        functools.partial(_route_kernel, ts=ts),
        grid=(nb,),
        in_specs=[pl.BlockSpec((ts, D), lambda i: (i, 0)), full(wq), full(k1), full(k2)],
        out_specs=[blk, blk],
        out_shape=[jax.ShapeDtypeStruct((T, E_PER_TOK), I32),
                   jax.ShapeDtypeStruct((T, E_PER_TOK), F32)],
        scratch_shapes=[pltpu.VMEM((2, PEER_TOPK, PEER_HEADS, ts), F32),
                        pltpu.VMEM((2, PEER_TOPK, PEER_HEADS, ts), I32)],
        compiler_params=_params(("parallel",)),
        name="peer_route",
    )(x2d, wq, k1, k2)


TOK_GROUP = SUBLANES


N_GROUPS = PEER_TB // TOK_GROUP
SCORE_BUFS = 4


def _gathered_rows(ids_ref, tab_ref, t):
    pieces = []
    for j in range(E_PER_TOK):
        r0 = pl.multiple_of(ids_ref[t, j], WORD_ROWS)
        pieces.append(tab_ref[pl.ds(r0, WORD_ROWS), :])
    return jnp.concatenate(pieces, axis=0)


def _token_chunks(tt):
    return pl.ds(tt, ROW_CHUNKS, stride=PEER_TB)


def _chunk_of_col():
    col = lax.broadcasted_iota(I32, (ROW_CHUNKS, E_PER_TOK * ROW_CHUNKS), 1)
    row = lax.broadcasted_iota(I32, (ROW_CHUNKS, E_PER_TOK * ROW_CHUNKS), 0)
    return (col & (ROW_CHUNKS - 1)) == row


def _peer_score_kernel(ids_ref, x_ref, tab_ref, s_ref, *scratch):
    part_bufs = scratch
    own_chunk = _chunk_of_col()
    c_id = lax.broadcasted_iota(I32, (E_PER_TOK * ROW_CHUNKS, E_PER_TOK), 0)
    e_id = lax.broadcasted_iota(I32, (E_PER_TOK * ROW_CHUNKS, E_PER_TOK), 1)
    fold = jnp.where(c_id >> CHUNK_SHIFT == e_id, 1.0, 0.0).astype(BF16)

    for g in range(N_GROUPS):
        tok0 = g * TOK_GROUP
        part_scr = part_bufs[g % SCORE_BUFS]
        for tt in range(TOK_GROUP):
            rows = pltpu.bitcast(_gathered_rows(ids_ref, tab_ref, tok0 + tt), BF16)
            xt = x_ref[0, _token_chunks(tok0 + tt), :].astype(BF16)
            full = _nt_dot(xt, rows)
            part_scr[tt * ROW_CHUNKS:(tt + 1) * ROW_CHUNKS, :] = jnp.where(own_chunk, full, 0.0)
        part = part_scr[...]
        hi = part.astype(BF16)
        lo = (part - hi.astype(F32)).astype(BF16)
        sc = (jnp.dot(hi, fold, preferred_element_type=F32)
              + jnp.dot(lo, fold, preferred_element_type=F32))
        sc = jnp.sum(sc.reshape(TOK_GROUP, ROW_CHUNKS, E_PER_TOK), axis=1)
        s_ref[tok0:tok0 + TOK_GROUP, :] = sc


def _peer_mix_kernel(ids_ref, s_ref, gate_ref, tab_ref, o_ref):
    own_chunk = _chunk_of_col()
    e_id = lax.broadcasted_iota(I32, (E_PER_TOK, E_PER_TOK * ROW_CHUNKS), 0)
    c_id = lax.broadcasted_iota(I32, (E_PER_TOK, E_PER_TOK * ROW_CHUNKS), 1)
    spread = jnp.where(c_id >> CHUNK_SHIFT == e_id, 1.0, 0.0).astype(BF16)

    for g in range(N_GROUPS):
        tok0 = g * TOK_GROUP
        sl = slice(tok0, tok0 + TOK_GROUP)
        s = s_ref[sl, :]
        act = 0.5 * s * (1.0 + lax.erf(s * (2.0 ** -0.5)))
        coef = (gate_ref[sl, :] * act).astype(BF16)
        coef = jnp.dot(coef, spread, preferred_element_type=F32)
        for tt in range(TOK_GROUP):
            rows = pltpu.bitcast(_gathered_rows(ids_ref, tab_ref, tok0 + tt), BF16)
            ct = jnp.broadcast_to(coef[tt:tt + 1], (ROW_CHUNKS, E_PER_TOK * ROW_CHUNKS))
            ct = jnp.where(own_chunk, ct, 0.0).astype(BF16)
            o_ref[0, _token_chunks(tok0 + tt), :] = jnp.dot(ct, rows, preferred_element_type=F32)


def _peer_experts(ids, gate, xc, tab_u, tab_v):
    T = ids.shape[0]
    tb = PEER_TB
    nb = T // tb
    ids_spec = pl.BlockSpec((tb, E_PER_TOK), lambda i: (i, 0), memory_space=pltpu.SMEM)
    tab_spec = pl.BlockSpec(tab_u.shape, lambda i: (0, 0), pipeline_mode=pl.Buffered(1))
    tok_spec = pl.BlockSpec((tb, E_PER_TOK), lambda i: (i, 0))
    row_spec = _chunk_major_spec(tb)
    part_scr = [pltpu.VMEM((TOK_GROUP * ROW_CHUNKS, E_PER_TOK * ROW_CHUNKS), F32)] * SCORE_BUFS
    s = pl.pallas_call(
        _peer_score_kernel,
        grid=(nb,),
        in_specs=[ids_spec, row_spec, tab_spec],
        out_specs=tok_spec,
        out_shape=jax.ShapeDtypeStruct((T, E_PER_TOK), F32),
        scratch_shapes=part_scr,
        compiler_params=_params(("arbitrary",)),
        name="peer_score",
    )(ids, xc, tab_u)
    return pl.pallas_call(
        _peer_mix_kernel,
        grid=(nb,),
        in_specs=[ids_spec, tok_spec, tok_spec, tab_spec],
        out_specs=row_spec,
        out_shape=_chunk_major_shape(T),
        compiler_params=_params(("arbitrary",)),
        name="peer_mix",
    )(ids, s, gate, tab_v)


def _resln_kernel(x_ref, y_ref, lg_ref, lb_ref, o_ref):
    G = x_ref.shape[0]
    z = ALPHA * x_ref[...] + y_ref[...]
    zs = [z[:, c * PEER_TB:(c + 1) * PEER_TB, :] for c in range(ROW_CHUNKS)]
    tot = zs[0]
    for c in range(1, ROW_CHUNKS):
        tot = tot + zs[c]
    mu = jnp.sum(tot, axis=-1, keepdims=True) * (1.0 / D_MODEL)
    zc = [a - mu for a in zs]
    sq = zc[0] * zc[0]
    for c in range(1, ROW_CHUNKS):
        sq = sq + zc[c] * zc[c]
    rstd = lax.rsqrt(jnp.sum(sq, axis=-1, keepdims=True) * (1.0 / D_MODEL) + LN_EPS)
    for c in range(ROW_CHUNKS):
        cols = slice(c * LANES, (c + 1) * LANES)
        y = (zc[c] * rstd).reshape(G * PEER_TB, LANES)
        o_ref[:, cols] = y * lg_ref[:, cols] + lb_ref[:, cols]


def _resln(xc, yc, lg, lb, tm=512):
    T = xc.shape[0] * PEER_TB
    blk = _chunk_major_spec(tm)
    par = pl.BlockSpec((1, D_MODEL), lambda i: (0, 0))
    return pl.pallas_call(
        _resln_kernel,
        grid=(T // tm,),
        in_specs=[blk, blk, par, par],
        out_specs=pl.BlockSpec((tm, D_MODEL), lambda i: (i, 0)),
        out_shape=jax.ShapeDtypeStruct((T, D_MODEL), F32),
        compiler_params=_params(("parallel",)),
        name="res_ln",
    )(xc, yc, lg, lb)


def _pack_kernel(t_ref, o_ref):
    def bf16_bits(v):
        return pltpu.bitcast(v.astype(BF16).astype(F32), jnp.uint32)

    te = t_ref.shape[0]
    for r in range(WORD_ROWS):
        lo = bf16_bits(t_ref[:, (2 * r) * LANES:(2 * r + 1) * LANES])
        hi = bf16_bits(t_ref[:, (2 * r + 1) * LANES:(2 * r + 2) * LANES])
        words = (hi & jnp.uint32(0xFFFF0000)) | (lo >> 16)
        o_ref[pl.ds(r, te, stride=WORD_ROWS), :] = pltpu.bitcast(words, I32)


def _pack_table(t, layer, te=512):
    _, E, D = t.shape
    return pl.pallas_call(
        _pack_kernel,
        grid=(E // te,),
        in_specs=[pl.BlockSpec((None, te, D), lambda i: (layer, i, 0))],
        out_specs=pl.BlockSpec((te * WORD_ROWS, LANES), lambda i: (i, 0)),
        out_shape=jax.ShapeDtypeStruct((E * WORD_ROWS, LANES), I32),
        compiler_params=_params(("parallel",)),
        name="pack_table",
    )(t)


def _block_diag(w):
    H, d, _ = w.shape
    out = jnp.zeros((H * d, H * d), w.dtype)
    for h in range(H):
        out = out.at[h * d:(h + 1) * d, h * d:(h + 1) * d].set(w[h])
    return out


def _cat_w_in_kernel(w_ref, o_ref):
    GW = GROUP_WIDTH
    off_f = 5 * GW
    off_sb = off_f + GROUP_HEADS
    w = w_ref[...].astype(BF16)
    o_ref[:, :off_f] = w[:, :off_f]
    o_ref[:, off_f:C_F] = w[:, off_sb:N_IN]
    o_ref[:, C_F:C_END] = jnp.zeros((w.shape[0], C_END - C_F), BF16)
    o_ref[:, C_F:C_F + GROUP_HEADS] = w[:, off_f:off_sb]


def _cat_w_in(w_in, layer, tr=256):
    _, D, N = w_in.shape
    return pl.pallas_call(
        _cat_w_in_kernel,
        grid=(D // tr,),
        in_specs=[pl.BlockSpec((None, tr, N), lambda i: (layer, i, 0))],
        out_specs=pl.BlockSpec((tr, C_END), lambda i: (i, 0)),
        out_shape=jax.ShapeDtypeStruct((D, C_END), BF16),
        compiler_params=_params(("parallel",)),
        name="regroup_w_in",
    )(w_in)


def _row(v, width=None):
    v = v.reshape(1, -1).astype(F32)
    if width is not None and v.shape[1] < width:
        v = jnp.pad(v, ((0, 0), (0, width - v.shape[1])))
    return v


def _layer(x2d, kmem, vmem_, B, S, p, tables, layer, tq=256):
    T = B * S
    rg, fox, sb, sc, f = _inproj(x2d, _cat_w_in(tables[2], layer), B, S)
    yrg, ysc, cumt = _rgsc(
        rg.reshape(B, S, -1), sc.reshape(B, S, -1), f.reshape(B, S, -1),
        p["rg_conv_w"], _row(p["rg_conv_b"]), _block_diag(p["rg_wa"]).astype(BF16), _row(p["rg_ba"]),
        _block_diag(p["rg_wi"]).astype(BF16), _row(p["rg_bi"]), _row(p["rg_lambda"]),
        p["sc_conv_w"], _row(p["fox_bf"], LANES), tq)
    yfox = _fox(fox, cumt, tq=tq)
    ysb = _sb(sb, tq=tq)
    ys = (yrg.reshape(T, -1), yfox.reshape(T, -1), ysb.reshape(T, -1), ysc.reshape(T, -1))
    x1 = _mixout(ys, _row(p["mix_norm_g"]), p["w_out"].astype(BF16), x2d,
                 _row(p["ln1_g"]), _row(p["ln1_b"]))
    x2, x2c = _xattn(x1, p["xa_wq"].astype(BF16), kmem, vmem_, p["xa_wo"].astype(BF16),
                     _row(p["ln2_g"]), _row(p["ln2_b"]), S)
    ids, gate = _route(x2, p["peer_wq"].astype(BF16), p["peer_k1"].astype(BF16),
                       p["peer_k2"].astype(BF16))
    ffc = _peer_experts(ids, gate, x2c, _pack_table(tables[0], layer), _pack_table(tables[1], layer))
    return _resln(x2c, ffc, _row(p["ln3_g"]), _row(p["ln3_b"]))


_LAYER_PARAMS = ("w_in", "w_out", "rg_conv_w", "rg_conv_b", "rg_wa", "rg_ba", "rg_wi", "rg_bi",
                 "rg_lambda", "fox_bf", "sc_conv_w", "mix_norm_g", "ln1_g", "ln1_b", "xa_wq",
                 "xa_wkv", "xa_wo", "ln2_g", "ln2_b", "peer_wq", "peer_k1", "peer_k2", "peer_u",
                 "peer_v", "ln3_g", "ln3_b")


def kernel(x, mem, w_in, w_out, rg_conv_w, rg_conv_b, rg_wa, rg_ba, rg_wi, rg_bi, rg_lambda, fox_bf, sc_conv_w, mix_norm_g, ln1_g, ln1_b, xa_wq, xa_wkv, xa_wo, ln2_g, ln2_b, peer_wq, peer_k1, peer_k2, peer_u, peer_v, ln3_g, ln3_b):
    stacked = dict(zip(_LAYER_PARAMS, (
        w_in, w_out, rg_conv_w, rg_conv_b, rg_wa, rg_ba, rg_wi, rg_bi, rg_lambda, fox_bf,
        sc_conv_w, mix_norm_g, ln1_g, ln1_b, xa_wq, xa_wkv, xa_wo, ln2_g, ln2_b, peer_wq,
        peer_k1, peer_k2, peer_u, peer_v, ln3_g, ln3_b)))
    B, S, D = x.shape
    M = mem.shape[1]
    x2d = x.reshape(B * S, D)
    mem2d = mem.reshape(B * M, D)
    for l in range(w_in.shape[0]):
        p = {k: v[l] for k, v in stacked.items() if k not in ("peer_u", "peer_v", "w_in")}
        kmem, vmem_ = _kv(mem2d, p["xa_wkv"].astype(BF16))
        x2d = _layer(x2d, kmem.reshape(B, M, D), vmem_.reshape(B, M, D), B, S, p,
                     (peer_u, peer_v, w_in), l)
    return x2d.reshape(B, S, D)
```

```python
import functools
import math

import jax
import jax.numpy as jnp
from jax import lax
from jax.experimental import pallas as pl
from jax.experimental.pallas import tpu as pltpu

F32 = jnp.float32
BF16 = jnp.bfloat16
I32 = jnp.int32

D_MODEL = 1024
GROUP_WIDTH = 256
GROUP_HEADS = 4
HEAD_DIM = 64
N_IN = 2820
RGLRU_C = 8.0
XA_HEADS = 4
XA_HEAD_DIM = D_MODEL // XA_HEADS
PEER_HEADS = 8
N_KEYS = 128
PEER_HALF = 128
PEER_TOPK = 16
DEPTH = 2
ALPHA = (2.0 * DEPTH) ** 0.25
LN_EPS = 1e-5

SUBLANES = 8
LANES = 128
WORD_ROWS = D_MODEL // (2 * LANES)
ROW_CHUNKS = D_MODEL // LANES
CHUNK_SHIFT = ROW_CHUNKS.bit_length() - 1

V7X_VMEM_BYTES = 64 * 1024 * 1024
VMEM_LIMIT = V7X_VMEM_BYTES * 3 // 4
GROUP_RMS_EPS = 1e-6
GELU_TANH_CUBIC = 0.044715


def _params(sem, vmem=VMEM_LIMIT):
    return pltpu.CompilerParams(dimension_semantics=sem, vmem_limit_bytes=vmem)


def _layer_norm(z, g, b):
    mu = jnp.mean(z, axis=-1, keepdims=True)
    zc = z - mu
    var = jnp.mean(zc * zc, axis=-1, keepdims=True)
    return zc * lax.rsqrt(var + LN_EPS) * g + b


def _log_sigmoid(z):
    return jnp.minimum(z, 0.0) - jnp.log1p(jnp.exp(-jnp.abs(z)))


def _nt_dot(a, b):
    return lax.dot_general(a, b, (((1,), (1,)), ((), ())), preferred_element_type=F32)


C_RG, C_FOX, C_SB, C_SC, C_F, C_END = 0, 512, 1280, 2048, 2816, 2944


def _inproj_kernel(x_ref, w_ref, rg_ref, fox_ref, sb_ref, sc_ref, f_ref):
    xb = x_ref[...].astype(BF16)

    def mm(lo, hi):
        return jnp.dot(xb, w_ref[:, lo:hi], preferred_element_type=F32)

    def heads(ref, lo, hi):
        qkv = mm(lo, hi)
        for j in range(3 * GROUP_HEADS):
            ref[0, j] = qkv[:, j * HEAD_DIM:(j + 1) * HEAD_DIM].astype(BF16)

    rg_ref[...] = mm(C_RG, C_FOX)
    heads(fox_ref, C_FOX, C_SB)
    heads(sb_ref, C_SB, C_SC)
    sc_ref[...] = mm(C_SC, C_F)
    f_ref[...] = mm(C_F, C_END)


def _inproj(x2d, w_cat, B, S, tm=512):
    T, D = x2d.shape
    per_seq = S // tm
    flat = ((C_FOX - C_RG, F32), (C_F - C_SC, F32), (C_END - C_F, F32))
    flat_specs = [pl.BlockSpec((tm, w), lambda i: (i, 0)) for w, _ in flat]
    flat_shapes = [jax.ShapeDtypeStruct((T, w), dt) for w, dt in flat]
    head_spec = pl.BlockSpec((1, 3 * GROUP_HEADS, tm, HEAD_DIM),
                             lambda i: (i // per_seq, 0, i % per_seq, 0))
    head_shape = jax.ShapeDtypeStruct((B, 3 * GROUP_HEADS, S, HEAD_DIM), BF16)
    return pl.pallas_call(
        _inproj_kernel,
        grid=(T // tm,),
        in_specs=[pl.BlockSpec((tm, D), lambda i: (i, 0)),
                  pl.BlockSpec((D, C_END), lambda i: (0, 0))],
        out_specs=[flat_specs[0], head_spec, head_spec, flat_specs[1], flat_specs[2]],
        out_shape=[flat_shapes[0], head_shape, head_shape, flat_shapes[1], flat_shapes[2]],
        compiler_params=_params(("parallel",)),
        name="inproj",
    )(x2d, w_cat)


def _rgsc_kernel(rg_ref, sc_ref, f_ref, cw_ref, cb_ref, wa_ref, ba_ref, wi_ref, bi_ref, lam_ref,
                 scw_ref, fb_ref, yrg_ref, ysc_ref, cumt_ref, xprev, chprev, hprev, cprev,
                 *, ts, tk):
    GW = GROUP_WIDTH

    @pl.when(pl.program_id(1) == 0)
    def _():
        xprev[...] = jnp.zeros_like(xprev)
        chprev[...] = jnp.zeros_like(chprev)
        hprev[...] = jnp.zeros_like(hprev)
        cprev[...] = jnp.zeros_like(cprev)

    row = lax.broadcasted_iota(I32, (ts, GW), 0)

    def delayed(prev, cur, d):
        ext = jnp.concatenate([prev, cur], axis=0)
        return pltpu.roll(ext, d, 0)[SUBLANES:]

    xr = rg_ref[0, :, :GW]
    gate = rg_ref[0, :, GW:]
    xp = xprev[...]
    cw = cw_ref[...]
    xc = (delayed(xp, xr, 3) * cw[0:1] + delayed(xp, xr, 2) * cw[1:2]
          + delayed(xp, xr, 1) * cw[2:3] + xr * cw[3:4] + cb_ref[...])
    xprev[...] = xr[ts - SUBLANES:]
    xcb = xc.astype(BF16)
    r = jax.nn.sigmoid(jnp.dot(xcb, wa_ref[...], preferred_element_type=F32) + ba_ref[...])
    ig = jax.nn.sigmoid(jnp.dot(xcb, wi_ref[...], preferred_element_type=F32) + bi_ref[...])
    z = -lam_ref[...]
    softplus = jnp.maximum(z, 0.0) + jnp.log1p(jnp.exp(-jnp.abs(z)))
    log_a = -RGLRU_C * r * softplus
    a = jnp.exp(log_a)
    u = jnp.sqrt(-jnp.tanh(log_a) * (a * a + 1.0)) * (ig * xc)
    acc_a, acc_b = a, u
    d = 1
    while d < ts:
        keep = row >= d
        a_s = jnp.where(keep, pltpu.roll(acc_a, d, 0), 1.0)
        b_s = jnp.where(keep, pltpu.roll(acc_b, d, 0), 0.0)
        acc_b = acc_a * b_s + acc_b
        acc_a = acc_a * a_s
        d *= 2
    h = acc_b + acc_a * hprev[...]
    hprev[...] = h[ts - 1:]
    c0 = math.sqrt(2.0 / math.pi)
    gelu = 0.5 * gate * (1.0 + jnp.tanh(c0 * (gate + GELU_TANH_CUBIC * gate * gate * gate)))
    yrg_ref[0] = h * gelu

    bg = sc_ref[0, :, :GW]
    ch = sc_ref[0, :, GW:2 * GW] * sc_ref[0, :, 2 * GW:]
    cp = chprev[...]
    sw = scw_ref[...]
    ysc_ref[0] = bg * (delayed(cp, ch, 2) * sw[0:1] + delayed(cp, ch, 1) * sw[1:2] + ch * sw[2:3])
    chprev[...] = ch[ts - SUBLANES:]

    rowf = lax.broadcasted_iota(I32, (ts, LANES), 0)
    c = _log_sigmoid(f_ref[0] + fb_ref[...])
    d = 1
    while d < ts:
        c = c + jnp.where(rowf >= d, pltpu.roll(c, d, 0), 0.0)
        d *= 2
    c = c + cprev[...]
    cprev[...] = c[ts - 1:]
    ct = c.T[:SUBLANES]
    for j in range(ts // tk):
        cumt_ref[0, j] = ct[:, j * tk:(j + 1) * tk]


def _rgsc(rg, sc, f, cw, cb, wa, ba, wi, bi, lam, scw, fb, tk, ts=512):
    B, S, _ = rg.shape
    GW = GROUP_WIDTH
    per = ts // tk

    def full(a):
        return pl.BlockSpec(a.shape, lambda b, t: (0,) * a.ndim)

    def seq(w):
        return pl.BlockSpec((1, ts, w), lambda b, t: (b, t, 0))

    params = (cw, cb, wa, ba, wi, bi, lam, scw, fb)
    return pl.pallas_call(
        functools.partial(_rgsc_kernel, ts=ts, tk=tk),
        grid=(B, S // ts),
        in_specs=[seq(2 * GW), seq(3 * GW), seq(LANES)] + [full(p) for p in params],
        out_specs=[seq(GW), seq(GW),
                   pl.BlockSpec((1, per, SUBLANES, tk), lambda b, t: (b, t, 0, 0))],
        out_shape=[jax.ShapeDtypeStruct((B, S, GW), F32), jax.ShapeDtypeStruct((B, S, GW), F32),
                   jax.ShapeDtypeStruct((B, S // tk, SUBLANES, tk), F32)],
        scratch_shapes=[pltpu.VMEM((SUBLANES, GW), F32), pltpu.VMEM((SUBLANES, GW), F32),
                        pltpu.VMEM((1, GW), F32), pltpu.VMEM((1, LANES), F32)],
        compiler_params=_params(("parallel", "arbitrary")),
        name="rgsc",
    )(rg, sc, f, *params)


HEADS_TOGETHER = GROUP_HEADS


assert math.log2(HEAD_DIM) % 2 == 0


def _fox_kernel(q_ref, k_ref, v_ref, cr_ref, o_ref, *, tq):
    qi = pl.program_id(1)
    scale = HEAD_DIM ** -0.5
    rowi = lax.broadcasted_iota(I32, (tq, tq), 0)
    coli = lax.broadcasted_iota(I32, (tq, tq), 1)
    causal = coli <= rowi
    qs = [q_ref[0, h] * scale for h in range(GROUP_HEADS)]

    def head_step(h, ki, carry, diag):
        m, l, acc = carry
        off = pl.multiple_of(ki * tq, tq)
        k = k_ref[0, h, pl.ds(off, tq), :]
        v = v_ref[0, h, pl.ds(off, tq), :]
        s = _nt_dot(qs[h], k) - cr_ref[0, ki, h:h + 1, :]
        if diag:
            s = jnp.where(causal, s, -jnp.inf)
        m_new = jnp.maximum(m, jnp.max(s, axis=-1, keepdims=True))
        alpha = jnp.exp(m - m_new)
        p = jnp.exp(s - m_new)
        l = alpha * l + jnp.sum(p, axis=-1, keepdims=True)
        acc = alpha * acc + jnp.dot(p.astype(BF16), v, preferred_element_type=F32)
        return m_new, l, acc

    def step(ki, carries, diag, heads):
        return tuple(head_step(h, ki, c, diag) for h, c in zip(heads, carries))

    init = (jnp.full((tq, 1), -jnp.inf, F32), jnp.zeros((tq, 1), F32),
            jnp.zeros((tq, HEAD_DIM), F32))
    outs = []
    for h0 in range(0, GROUP_HEADS, HEADS_TOGETHER):
        heads = range(h0, h0 + HEADS_TOGETHER)
        carries = lax.fori_loop(0, qi, functools.partial(step, diag=False, heads=heads),
                                (init,) * HEADS_TOGETHER)
        carries = step(qi, carries, True, heads)
        outs += [acc / l for _, l, acc in carries]
    o_ref[0] = jnp.concatenate(outs, axis=-1)


def _qkv_specs(S, tq):
    H, d = GROUP_HEADS, HEAD_DIM
    return [pl.BlockSpec((1, H, tq, d), lambda b, i: (b, 0, i, 0)),
            pl.BlockSpec((1, H, S, d), lambda b, i: (b, 1, 0, 0)),
            pl.BlockSpec((1, H, S, d), lambda b, i: (b, 2, 0, 0))]


def _fox(qkv, cumt, tq=256):
    B, _, S, d = qkv.shape
    H = GROUP_HEADS
    return pl.pallas_call(
        functools.partial(_fox_kernel, tq=tq),
        grid=(B, S // tq),
        in_specs=_qkv_specs(S, tq)
        + [pl.BlockSpec((1, S // tq, SUBLANES, tq), lambda b, i: (b, 0, 0, 0))],
        out_specs=pl.BlockSpec((1, tq, H * d), lambda b, i: (b, i, 0)),
        out_shape=jax.ShapeDtypeStruct((B, S, H * d), F32),
        compiler_params=_params(("parallel", "arbitrary")),
        name="fox_attn",
    )(qkv, qkv, qkv, cumt)


EXP_UNDERFLOW = -104.0


def _sb_kernel(q_ref, k_ref, v_ref, o_ref, *, tq):
    qi = pl.program_id(1)
    scale = HEAD_DIM ** -0.5
    rowi = lax.broadcasted_iota(I32, (tq, tq), 0)
    coli = lax.broadcasted_iota(I32, (tq, tq), 1)
    strict = coli < rowi
    later = jnp.where(rowi > coli, 1.0, 0.0).astype(BF16)
    qs = [q_ref[0, h] * scale for h in range(GROUP_HEADS)]
    def head_step(h, ki, carry, diag):
        rest, acc = carry
        off = pl.multiple_of(ki * tq, tq)
        k = k_ref[0, h, pl.ds(off, tq), :]
        v = v_ref[0, h, pl.ds(off, tq), :]
        z = _nt_dot(qs[h], k)
        ls = _log_sigmoid(z)
        l1m = ls - z
        if diag:
            l1m = jnp.where(strict, l1m, 0.0)
        hi = l1m.astype(BF16)
        lo = (l1m - hi.astype(F32)).astype(BF16)
        tail = (jnp.dot(hi, later, preferred_element_type=F32)
                + jnp.dot(lo, later, preferred_element_type=F32) + rest)
        w = jnp.exp(ls + tail)
        if diag:
            w = jnp.where(strict, w, 0.0)
        acc = acc + jnp.dot(w.astype(BF16), v, preferred_element_type=F32)
        rest = rest + jnp.sum(l1m, axis=-1, keepdims=True)
        return rest, acc

    def step(ki, carries, diag):
        return tuple(head_step(h, ki, carries[h], diag) for h in range(GROUP_HEADS))

    init = (jnp.zeros((tq, 1), F32), jnp.zeros((tq, HEAD_DIM), F32))
    carries = step(qi, (init,) * GROUP_HEADS, True)

    def more(c):
        j, carries = c
        top = carries[0][0]
        for rest, _ in carries[1:]:
            top = jnp.maximum(top, rest)
        return jnp.logical_and(j < qi, jnp.max(top) > EXP_UNDERFLOW)

    def further(c):
        j, carries = c
        return j + 1, step(qi - 1 - j, carries, False)

    _, carries = lax.while_loop(more, further, (jnp.int32(0), carries))
    o_ref[0] = jnp.concatenate([acc for _, acc in carries], axis=-1)


def _sb(qkv, tq=256):
    B, _, S, d = qkv.shape
    H = GROUP_HEADS
    return pl.pallas_call(
        functools.partial(_sb_kernel, tq=tq),
        grid=(B, S // tq),
        in_specs=_qkv_specs(S, tq),
        out_specs=pl.BlockSpec((1, tq, H * d), lambda b, i: (b, i, 0)),
        out_shape=jax.ShapeDtypeStruct((B, S, H * d), F32),
        compiler_params=_params(("parallel", "arbitrary")),
        name="sb_attn",
    )(qkv, qkv, qkv)


def _mixout_kernel(y0_ref, y1_ref, y2_ref, y3_ref, g_ref, w_ref, x_ref, lg_ref, lb_ref, o_ref):
    def rms(y):
        return y * lax.rsqrt(jnp.mean(y * y, axis=-1, keepdims=True) + GROUP_RMS_EPS)

    y = jnp.concatenate([rms(r[...]) for r in (y0_ref, y1_ref, y2_ref, y3_ref)], axis=-1)
    y = (y * g_ref[...]).astype(BF16)
    mix = jnp.dot(y, w_ref[...], preferred_element_type=F32)
    o_ref[...] = _layer_norm(ALPHA * x_ref[...] + mix, lg_ref[...], lb_ref[...])


def _mixout(ys, g, w, x2d, lg, lb, tm=512):
    T, D = x2d.shape
    GW = GROUP_WIDTH

    def full(a):
        return pl.BlockSpec(a.shape, lambda i: (0,) * a.ndim)

    return pl.pallas_call(
        _mixout_kernel,
        grid=(T // tm,),
        in_specs=[pl.BlockSpec((tm, GW), lambda i: (i, 0))] * 4
        + [full(g), full(w), pl.BlockSpec((tm, D), lambda i: (i, 0)), full(lg), full(lb)],
        out_specs=pl.BlockSpec((tm, D), lambda i: (i, 0)),
        out_shape=jax.ShapeDtypeStruct((T, D), F32),
        compiler_params=_params(("parallel",)),
        name="mixout",
    )(*ys, g, w, x2d, lg, lb)


def _kv_kernel(m_ref, w_ref, k_ref, v_ref):
    kv = jnp.dot(m_ref[...].astype(BF16), w_ref[...], preferred_element_type=F32)
    k_ref[...] = kv[:, :D_MODEL].astype(BF16)
    v_ref[...] = kv[:, D_MODEL:].astype(BF16)


def _kv(mem2d, wkv, tm=512):
    M, D = mem2d.shape
    return pl.pallas_call(
        _kv_kernel,
        grid=(M // tm,),
        in_specs=[pl.BlockSpec((tm, D), lambda i: (i, 0)),
                  pl.BlockSpec((D, 2 * D), lambda i: (0, 0))],
        out_specs=[pl.BlockSpec((tm, D), lambda i: (i, 0))] * 2,
        out_shape=[jax.ShapeDtypeStruct((M, D), BF16)] * 2,
        compiler_params=_params(("parallel",)),
        name="mem_kv",
    )(mem2d, wkv)


PEER_TB = 32


def _chunk_major_spec(tm):
    return pl.BlockSpec((tm // PEER_TB, ROW_CHUNKS * PEER_TB, LANES), lambda i: (i, 0, 0))


def _chunk_major_shape(T):
    return jax.ShapeDtypeStruct((T // PEER_TB, ROW_CHUNKS * PEER_TB, LANES), F32)


def _to_chunk_major(y, ref):
    for g in range(y.shape[0] // PEER_TB):
        for c in range(ROW_CHUNKS):
            ref[g, c * PEER_TB:(c + 1) * PEER_TB, :] = (
                y[g * PEER_TB:(g + 1) * PEER_TB, c * LANES:(c + 1) * LANES])


def _xattn_kernel(x_ref, wq_ref, k_ref, v_ref, wo_ref, lg_ref, lb_ref, o_ref, oc_ref):
    x = x_ref[...]
    q = jnp.dot(x.astype(BF16), wq_ref[...], preferred_element_type=F32).astype(BF16)
    scale = XA_HEAD_DIM ** -0.5
    outs = []
    for h in range(XA_HEADS):
        sl = slice(h * XA_HEAD_DIM, (h + 1) * XA_HEAD_DIM)
        s = _nt_dot(q[:, sl], k_ref[0, :, sl]) * scale
        p = jnp.exp(s - jnp.max(s, axis=-1, keepdims=True))
        p = p / jnp.sum(p, axis=-1, keepdims=True)
        outs.append(jnp.dot(p.astype(BF16), v_ref[0, :, sl], preferred_element_type=F32))
    o = jnp.concatenate(outs, axis=-1).astype(BF16)
    xa = jnp.dot(o, wo_ref[...], preferred_element_type=F32)
    y = _layer_norm(ALPHA * x + xa, lg_ref[...], lb_ref[...])
    o_ref[...] = y
    _to_chunk_major(y, oc_ref)


def _xattn(x2d, wq, k, v, wo, lg, lb, seq_len, tm=512):
    T, D = x2d.shape
    M = k.shape[1]
    per_seq = seq_len // tm

    def full(a):
        return pl.BlockSpec(a.shape, lambda i: (0,) * a.ndim)

    return pl.pallas_call(
        _xattn_kernel,
        grid=(T // tm,),
        in_specs=[pl.BlockSpec((tm, D), lambda i: (i, 0)), full(wq),
                  pl.BlockSpec((1, M, D), lambda i: (i // per_seq, 0, 0)),
                  pl.BlockSpec((1, M, D), lambda i: (i // per_seq, 0, 0)),
                  full(wo), full(lg), full(lb)],
        out_specs=[pl.BlockSpec((tm, D), lambda i: (i, 0)), _chunk_major_spec(tm)],
        out_shape=[jax.ShapeDtypeStruct((T, D), F32), _chunk_major_shape(T)],
        compiler_params=_params(("parallel",)),
        name="xattn",
    )(x2d, wq, k, v, wo, lg, lb)


E_PER_TOK = PEER_HEADS * PEER_TOPK


def _staircase():
    return [(a, b) for a in range(PEER_TOPK) for b in range(PEER_TOPK // (a + 1))]


def _route_kernel(x_ref, wq_ref, k1_ref, k2_ref, e_ref, g_ref, v_scr, i_scr, *, ts):
    xb = x_ref[...].astype(BF16)
    key_id = lax.broadcasted_iota(I32, (N_KEYS, ts), 0).astype(F32)
    qd = 2 * PEER_HALF
    for h in range(PEER_HEADS):
        qry = jnp.dot(xb, wq_ref[:, h * qd:(h + 1) * qd], preferred_element_type=F32)
        for half, kref in ((0, k1_ref), (1, k2_ref)):
            qh = qry[:, half * PEER_HALF:(half + 1) * PEER_HALF].astype(BF16)
            s0 = _nt_dot(kref[...], qh)

            s = s0
            for it in range(PEER_TOPK):
                m = jnp.max(s, axis=0, keepdims=True)
                idx = jnp.min(jnp.where(s == m, key_id, float(N_KEYS)), axis=0, keepdims=True)
                v_scr[half, it, h:h + 1, :] = m
                i_scr[half, it, h:h + 1, :] = idx.astype(I32)
                s = jnp.where(key_id == idx, -jnp.inf, s)

    cells = _staircase()
    cand = tuple(v_scr[0, a] + v_scr[1, b] for a, b in cells)
    expert = [i_scr[0, a] * N_KEYS + i_scr[1, b] for a, b in cells]

    picked = []
    for it in range(PEER_TOPK):
        best_v, best_e, best_c = cand[0], expert[0], jnp.zeros((PEER_HEADS, ts), I32)
        for c in range(1, len(cells)):
            better = cand[c] > best_v
            best_v = jnp.where(better, cand[c], best_v)
            best_e = jnp.where(better, expert[c], best_e)
            best_c = jnp.where(better, c, best_c)
        i_scr[0, it] = best_e * WORD_ROWS
        picked.append(best_v)
        cand = tuple(jnp.where(best_c == c, -jnp.inf, cand[c]) for c in range(len(cells)))
    ex = [jnp.exp(v - picked[0]) for v in picked]
    den = ex[0]
    for it in range(1, PEER_TOPK):
        den = den + ex[it]
    inv = 1.0 / den
    g_ref[...] = jnp.concatenate([e * inv for e in ex], axis=0).T
    rows = pltpu.bitcast(i_scr[0].reshape(E_PER_TOK, ts), F32)
    e_ref[...] = pltpu.bitcast(rows.T, I32)


def _route(x2d, wq, k1, k2, ts=256):
    T, D = x2d.shape
    nb = T // ts

    def full(a):
        return pl.BlockSpec(a.shape, lambda i: (0,) * a.ndim)

    blk = pl.BlockSpec((ts, E_PER_TOK), lambda i: (i, 0))
    return pl.pallas_call(
        functools.partial(_route_kernel, ts=ts),
        grid=(nb,),
        in_specs=[pl.BlockSpec((ts, D), lambda i: (i, 0)), full(wq), full(k1), full(k2)],
        out_specs=[blk, blk],
        out_shape=[jax.ShapeDtypeStruct((T, E_PER_TOK), I32),
                   jax.ShapeDtypeStruct((T, E_PER_TOK), F32)],
        scratch_shapes=[pltpu.VMEM((2, PEER_TOPK, PEER_HEADS, ts), F32),
                        pltpu.VMEM((2, PEER_TOPK, PEER_HEADS, ts), I32)],
        compiler_params=_params(("parallel",)),
        name="peer_route",
    )(x2d, wq, k1, k2)


TOK_GROUP = SUBLANES


N_GROUPS = PEER_TB // TOK_GROUP


def _gathered_rows(ids_ref, tab_ref, t):
    pieces = []
    for j in range(E_PER_TOK):
        r0 = pl.multiple_of(ids_ref[t, j], WORD_ROWS)
        pieces.append(tab_ref[pl.ds(r0, WORD_ROWS), :])
    return jnp.concatenate(pieces, axis=0)


def _token_chunks(tt):
    return pl.ds(tt, ROW_CHUNKS, stride=PEER_TB)


def _chunk_of_col():
    col = lax.broadcasted_iota(I32, (ROW_CHUNKS, E_PER_TOK * ROW_CHUNKS), 1)
    row = lax.broadcasted_iota(I32, (ROW_CHUNKS, E_PER_TOK * ROW_CHUNKS), 0)
    return (col & (ROW_CHUNKS - 1)) == row


def _peer_score_kernel(ids_ref, x_ref, tab_ref, s_ref):
    own_chunk = _chunk_of_col()
    c_id = lax.broadcasted_iota(I32, (E_PER_TOK * ROW_CHUNKS, E_PER_TOK), 0)
    e_id = lax.broadcasted_iota(I32, (E_PER_TOK * ROW_CHUNKS, E_PER_TOK), 1)
    fold = jnp.where(c_id >> CHUNK_SHIFT == e_id, 1.0, 0.0).astype(BF16)

    for g in range(N_GROUPS):
        tok0 = g * TOK_GROUP
        parts = []
        for tt in range(TOK_GROUP):
            rows = pltpu.bitcast(_gathered_rows(ids_ref, tab_ref, tok0 + tt), BF16)
            xt = x_ref[0, _token_chunks(tok0 + tt), :].astype(BF16)
            full = _nt_dot(xt, rows)
            parts.append(jnp.where(own_chunk, full, 0.0))
        part = jnp.concatenate(parts, axis=0)
        hi = part.astype(BF16)
        lo = (part - hi.astype(F32)).astype(BF16)
        sc = (jnp.dot(hi, fold, preferred_element_type=F32)
              + jnp.dot(lo, fold, preferred_element_type=F32))
        sc = jnp.sum(sc.reshape(TOK_GROUP, ROW_CHUNKS, E_PER_TOK), axis=1)
        s_ref[tok0:tok0 + TOK_GROUP, :] = sc


def _peer_mix_kernel(ids_ref, s_ref, gate_ref, tab_ref, o_ref):
    own_chunk = _chunk_of_col()
    e_id = lax.broadcasted_iota(I32, (E_PER_TOK, E_PER_TOK * ROW_CHUNKS), 0)
    c_id = lax.broadcasted_iota(I32, (E_PER_TOK, E_PER_TOK * ROW_CHUNKS), 1)
    spread = jnp.where(c_id >> CHUNK_SHIFT == e_id, 1.0, 0.0).astype(BF16)

    for g in range(N_GROUPS):
        tok0 = g * TOK_GROUP
        sl = slice(tok0, tok0 + TOK_GROUP)
        s = s_ref[sl, :]
        act = 0.5 * s * (1.0 + lax.erf(s * (2.0 ** -0.5)))
        coef = (gate_ref[sl, :] * act).astype(BF16)
        coef = jnp.dot(coef, spread, preferred_element_type=F32)
        for tt in range(TOK_GROUP):
            rows = pltpu.bitcast(_gathered_rows(ids_ref, tab_ref, tok0 + tt), BF16)
            ct = jnp.broadcast_to(coef[tt:tt + 1], (ROW_CHUNKS, E_PER_TOK * ROW_CHUNKS))
            ct = jnp.where(own_chunk, ct, 0.0).astype(BF16)
            o_ref[0, _token_chunks(tok0 + tt), :] = jnp.dot(ct, rows, preferred_element_type=F32)


def _peer_experts(ids, gate, xc, tab_u, tab_v):
    T = ids.shape[0]
    tb = PEER_TB
    nb = T // tb
    ids_spec = pl.BlockSpec((tb, E_PER_TOK), lambda i: (i, 0), memory_space=pltpu.SMEM)
    tab_spec = pl.BlockSpec(tab_u.shape, lambda i: (0, 0), pipeline_mode=pl.Buffered(1))
    tok_spec = pl.BlockSpec((tb, E_PER_TOK), lambda i: (i, 0))
    row_spec = _chunk_major_spec(tb)
    s = pl.pallas_call(
        _peer_score_kernel,
        grid=(nb,),
        in_specs=[ids_spec, row_spec, tab_spec],
        out_specs=tok_spec,
        out_shape=jax.ShapeDtypeStruct((T, E_PER_TOK), F32),
        compiler_params=_params(("arbitrary",)),
        name="peer_score",
    )(ids, xc, tab_u)
    return pl.pallas_call(
        _peer_mix_kernel,
        grid=(nb,),
        in_specs=[ids_spec, tok_spec, tok_spec, tab_spec],
        out_specs=row_spec,
        out_shape=_chunk_major_shape(T),
        compiler_params=_params(("arbitrary",)),
        name="peer_mix",
    )(ids, s, gate, tab_v)


def _resln_kernel(x_ref, y_ref, lg_ref, lb_ref, o_ref):
    G = x_ref.shape[0]
    z = ALPHA * x_ref[...] + y_ref[...]
    zs = [z[:, c * PEER_TB:(c + 1) * PEER_TB, :] for c in range(ROW_CHUNKS)]
    tot = zs[0]
    for c in range(1, ROW_CHUNKS):
        tot = tot + zs[c]
    mu = jnp.sum(tot, axis=-1, keepdims=True) * (1.0 / D_MODEL)
    zc = [a - mu for a in zs]
    sq = zc[0] * zc[0]
    for c in range(1, ROW_CHUNKS):
        sq = sq + zc[c] * zc[c]
    rstd = lax.rsqrt(jnp.sum(sq, axis=-1, keepdims=True) * (1.0 / D_MODEL) + LN_EPS)
    for c in range(ROW_CHUNKS):
        cols = slice(c * LANES, (c + 1) * LANES)
        y = (zc[c] * rstd).reshape(G * PEER_TB, LANES)
        o_ref[:, cols] = y * lg_ref[:, cols] + lb_ref[:, cols]


def _resln(xc, yc, lg, lb, tm=512):
    T = xc.shape[0] * PEER_TB
    blk = _chunk_major_spec(tm)
    par = pl.BlockSpec((1, D_MODEL), lambda i: (0, 0))
    return pl.pallas_call(
        _resln_kernel,
        grid=(T // tm,),
        in_specs=[blk, blk, par, par],
        out_specs=pl.BlockSpec((tm, D_MODEL), lambda i: (i, 0)),
        out_shape=jax.ShapeDtypeStruct((T, D_MODEL), F32),
        compiler_params=_params(("parallel",)),
        name="res_ln",
    )(xc, yc, lg, lb)


def _pack_kernel(t_ref, o_ref):
    def bf16_bits(v):
        return pltpu.bitcast(v.astype(BF16).astype(F32), jnp.uint32)

    te = t_ref.shape[0]
    for r in range(WORD_ROWS):
        lo = bf16_bits(t_ref[:, (2 * r) * LANES:(2 * r + 1) * LANES])
        hi = bf16_bits(t_ref[:, (2 * r + 1) * LANES:(2 * r + 2) * LANES])
        words = (hi & jnp.uint32(0xFFFF0000)) | (lo >> 16)
        o_ref[pl.ds(r, te, stride=WORD_ROWS), :] = pltpu.bitcast(words, I32)


def _pack_table(t, layer, te=512):
    _, E, D = t.shape
    return pl.pallas_call(
        _pack_kernel,
        grid=(E // te,),
        in_specs=[pl.BlockSpec((None, te, D), lambda i: (layer, i, 0))],
        out_specs=pl.BlockSpec((te * WORD_ROWS, LANES), lambda i: (i, 0)),
        out_shape=jax.ShapeDtypeStruct((E * WORD_ROWS, LANES), I32),
        compiler_params=_params(("parallel",)),
        name="pack_table",
    )(t)


def _block_diag(w):
    H, d, _ = w.shape
    out = jnp.zeros((H * d, H * d), w.dtype)
    for h in range(H):
        out = out.at[h * d:(h + 1) * d, h * d:(h + 1) * d].set(w[h])
    return out


def _cat_w_in_kernel(w_ref, o_ref):
    GW = GROUP_WIDTH
    off_f = 5 * GW
    off_sb = off_f + GROUP_HEADS
    w = w_ref[...].astype(BF16)
    o_ref[:, :off_f] = w[:, :off_f]
    o_ref[:, off_f:C_F] = w[:, off_sb:N_IN]
    o_ref[:, C_F:C_END] = jnp.zeros((w.shape[0], C_END - C_F), BF16)
    o_ref[:, C_F:C_F + GROUP_HEADS] = w[:, off_f:off_sb]


def _cat_w_in(w_in, layer, tr=256):
    _, D, N = w_in.shape
    return pl.pallas_call(
        _cat_w_in_kernel,
        grid=(D // tr,),
        in_specs=[pl.BlockSpec((None, tr, N), lambda i: (layer, i, 0))],
        out_specs=pl.BlockSpec((tr, C_END), lambda i: (i, 0)),
        out_shape=jax.ShapeDtypeStruct((D, C_END), BF16),
        compiler_params=_params(("parallel",)),
        name="regroup_w_in",
    )(w_in)


def _row(v, width=None):
    v = v.reshape(1, -1).astype(F32)
    if width is not None and v.shape[1] < width:
        v = jnp.pad(v, ((0, 0), (0, width - v.shape[1])))
    return v


def _layer(x2d, kmem, vmem_, B, S, p, tables, layer, tq=256):
    T = B * S
    rg, fox, sb, sc, f = _inproj(x2d, _cat_w_in(tables[2], layer), B, S)
    yrg, ysc, cumt = _rgsc(
        rg.reshape(B, S, -1), sc.reshape(B, S, -1), f.reshape(B, S, -1),
        p["rg_conv_w"], _row(p["rg_conv_b"]), _block_diag(p["rg_wa"]).astype(BF16), _row(p["rg_ba"]),
        _block_diag(p["rg_wi"]).astype(BF16), _row(p["rg_bi"]), _row(p["rg_lambda"]),
        p["sc_conv_w"], _row(p["fox_bf"], LANES), tq)
    yfox = _fox(fox, cumt, tq=tq)
    ysb = _sb(sb, tq=tq)
    ys = (yrg.reshape(T, -1), yfox.reshape(T, -1), ysb.reshape(T, -1), ysc.reshape(T, -1))
    x1 = _mixout(ys, _row(p["mix_norm_g"]), p["w_out"].astype(BF16), x2d,
                 _row(p["ln1_g"]), _row(p["ln1_b"]))
    x2, x2c = _xattn(x1, p["xa_wq"].astype(BF16), kmem, vmem_, p["xa_wo"].astype(BF16),
                     _row(p["ln2_g"]), _row(p["ln2_b"]), S)
    ids, gate = _route(x2, p["peer_wq"].astype(BF16), p["peer_k1"].astype(BF16),
                       p["peer_k2"].astype(BF16))
    ffc = _peer_experts(ids, gate, x2c, _pack_table(tables[0], layer), _pack_table(tables[1], layer))
    return _resln(x2c, ffc, _row(p["ln3_g"]), _row(p["ln3_b"]))


_LAYER_PARAMS = ("w_in", "w_out", "rg_conv_w", "rg_conv_b", "rg_wa", "rg_ba", "rg_wi", "rg_bi",
                 "rg_lambda", "fox_bf", "sc_conv_w", "mix_norm_g", "ln1_g", "ln1_b", "xa_wq",
                 "xa_wkv", "xa_wo", "ln2_g", "ln2_b", "peer_wq", "peer_k1", "peer_k2", "peer_u",
                 "peer_v", "ln3_g", "ln3_b")


def kernel(x, mem, w_in, w_out, rg_conv_w, rg_conv_b, rg_wa, rg_ba, rg_wi, rg_bi, rg_lambda, fox_bf, sc_conv_w, mix_norm_g, ln1_g, ln1_b, xa_wq, xa_wkv, xa_wo, ln2_g, ln2_b, peer_wq, peer_k1, peer_k2, peer_u, peer_v, ln3_g, ln3_b):
    stacked = dict(zip(_LAYER_PARAMS, (
        w_in, w_out, rg_conv_w, rg_conv_b, rg_wa, rg_ba, rg_wi, rg_bi, rg_lambda, fox_bf,
        sc_conv_w, mix_norm_g, ln1_g, ln1_b, xa_wq, xa_wkv, xa_wo, ln2_g, ln2_b, peer_wq,
        peer_k1, peer_k2, peer_u, peer_v, ln3_g, ln3_b)))
    B, S, D = x.shape
    M = mem.shape[1]
    x2d = x.reshape(B * S, D)
    mem2d = mem.reshape(B * M, D)
    for l in range(w_in.shape[0]):
        p = {k: v[l] for k, v in stacked.items() if k not in ("peer_u", "peer_v", "w_in")}
        kmem, vmem_ = _kv(mem2d, p["xa_wkv"].astype(BF16))
        x2d = _layer(x2d, kmem.reshape(B, M, D), vmem_.reshape(B, M, D), B, S, p,
                     (peer_u, peer_v, w_in), l)
    return x2d.reshape(B, S, D)
```

```python
import functools
import math

import jax
import jax.numpy as jnp
from jax import lax
from jax.experimental import pallas as pl
from jax.experimental.pallas import tpu as pltpu

F32 = jnp.float32
BF16 = jnp.bfloat16
I32 = jnp.int32

D_MODEL = 1024
GROUP_WIDTH = 256
GROUP_HEADS = 4
HEAD_DIM = 64
N_IN = 2820
RGLRU_C = 8.0
XA_HEADS = 4
XA_HEAD_DIM = D_MODEL // XA_HEADS
PEER_HEADS = 8
N_KEYS = 128
PEER_HALF = 128
PEER_TOPK = 16
DEPTH = 2
ALPHA = (2.0 * DEPTH) ** 0.25
LN_EPS = 1e-5

SUBLANES = 8
LANES = 128
WORD_ROWS = D_MODEL // (2 * LANES)
ROW_CHUNKS = D_MODEL // LANES
CHUNK_SHIFT = ROW_CHUNKS.bit_length() - 1

V7X_VMEM_BYTES = 64 * 1024 * 1024
VMEM_LIMIT = V7X_VMEM_BYTES * 3 // 4
GROUP_RMS_EPS = 1e-6
GELU_TANH_CUBIC = 0.044715


def _params(sem, vmem=VMEM_LIMIT):
    return pltpu.CompilerParams(dimension_semantics=sem, vmem_limit_bytes=vmem)


def _layer_norm(z, g, b):
    mu = jnp.mean(z, axis=-1, keepdims=True)
    zc = z - mu
    var = jnp.mean(zc * zc, axis=-1, keepdims=True)
    return zc * lax.rsqrt(var + LN_EPS) * g + b


def _log_sigmoid(z):
    return jnp.minimum(z, 0.0) - jnp.log1p(jnp.exp(-jnp.abs(z)))


def _nt_dot(a, b):
    return lax.dot_general(a, b, (((1,), (1,)), ((), ())), preferred_element_type=F32)


C_RG, C_FOX, C_SB, C_SC, C_F, C_END = 0, 512, 1280, 2048, 2816, 2944


def _inproj_kernel(x_ref, w_ref, rg_ref, fox_ref, sb_ref, sc_ref, f_ref):
    xb = x_ref[...].astype(BF16)

    def mm(lo, hi):
        return jnp.dot(xb, w_ref[:, lo:hi], preferred_element_type=F32)

    def heads(ref, lo, hi):
        qkv = mm(lo, hi)
        for j in range(3 * GROUP_HEADS):
            ref[0, j] = qkv[:, j * HEAD_DIM:(j + 1) * HEAD_DIM].astype(BF16)

    rg_ref[...] = mm(C_RG, C_FOX)
    heads(fox_ref, C_FOX, C_SB)
    heads(sb_ref, C_SB, C_SC)
    sc_ref[...] = mm(C_SC, C_F)
    f_ref[...] = mm(C_F, C_END)


def _inproj(x2d, w_cat, B, S, tm=512):
    T, D = x2d.shape
    per_seq = S // tm
    flat = ((C_FOX - C_RG, F32), (C_F - C_SC, F32), (C_END - C_F, F32))
    flat_specs = [pl.BlockSpec((tm, w), lambda i: (i, 0)) for w, _ in flat]
    flat_shapes = [jax.ShapeDtypeStruct((T, w), dt) for w, dt in flat]
    head_spec = pl.BlockSpec((1, 3 * GROUP_HEADS, tm, HEAD_DIM),
                             lambda i: (i // per_seq, 0, i % per_seq, 0))
    head_shape = jax.ShapeDtypeStruct((B, 3 * GROUP_HEADS, S, HEAD_DIM), BF16)
    return pl.pallas_call(
        _inproj_kernel,
        grid=(T // tm,),
        in_specs=[pl.BlockSpec((tm, D), lambda i: (i, 0)),
                  pl.BlockSpec((D, C_END), lambda i: (0, 0))],
        out_specs=[flat_specs[0], head_spec, head_spec, flat_specs[1], flat_specs[2]],
        out_shape=[flat_shapes[0], head_shape, head_shape, flat_shapes[1], flat_shapes[2]],
        compiler_params=_params(("parallel",)),
        name="inproj",
    )(x2d, w_cat)


def _rgsc_kernel(rg_ref, sc_ref, f_ref, cw_ref, cb_ref, wa_ref, ba_ref, wi_ref, bi_ref, lam_ref,
                 scw_ref, fb_ref, yrg_ref, ysc_ref, cumt_ref, xprev, chprev, hprev, cprev,
                 *, ts, tk):
    GW = GROUP_WIDTH

    @pl.when(pl.program_id(1) == 0)
    def _():
        xprev[...] = jnp.zeros_like(xprev)
        chprev[...] = jnp.zeros_like(chprev)
        hprev[...] = jnp.zeros_like(hprev)
        cprev[...] = jnp.zeros_like(cprev)

    row = lax.broadcasted_iota(I32, (ts, GW), 0)

    def delayed(prev, cur, d):
        ext = jnp.concatenate([prev, cur], axis=0)
        return pltpu.roll(ext, d, 0)[SUBLANES:]

    xr = rg_ref[0, :, :GW]
    gate = rg_ref[0, :, GW:]
    xp = xprev[...]
    cw = cw_ref[...]
    xc = (delayed(xp, xr, 3) * cw[0:1] + delayed(xp, xr, 2) * cw[1:2]
          + delayed(xp, xr, 1) * cw[2:3] + xr * cw[3:4] + cb_ref[...])
    xprev[...] = xr[ts - SUBLANES:]
    xcb = xc.astype(BF16)
    r = jax.nn.sigmoid(jnp.dot(xcb, wa_ref[...], preferred_element_type=F32) + ba_ref[...])
    ig = jax.nn.sigmoid(jnp.dot(xcb, wi_ref[...], preferred_element_type=F32) + bi_ref[...])
    z = -lam_ref[...]
    softplus = jnp.maximum(z, 0.0) + jnp.log1p(jnp.exp(-jnp.abs(z)))
    log_a = -RGLRU_C * r * softplus
    a = jnp.exp(log_a)
    u = jnp.sqrt(-jnp.tanh(log_a) * (a * a + 1.0)) * (ig * xc)
    acc_a, acc_b = a, u
    d = 1
    while d < ts:
        keep = row >= d
        a_s = jnp.where(keep, pltpu.roll(acc_a, d, 0), 1.0)
        b_s = jnp.where(keep, pltpu.roll(acc_b, d, 0), 0.0)
        acc_b = acc_a * b_s + acc_b
        acc_a = acc_a * a_s
        d *= 2
    h = acc_b + acc_a * hprev[...]
    hprev[...] = h[ts - 1:]
    c0 = math.sqrt(2.0 / math.pi)
    gelu = 0.5 * gate * (1.0 + jnp.tanh(c0 * (gate + GELU_TANH_CUBIC * gate * gate * gate)))
    yrg_ref[0] = h * gelu

    bg = sc_ref[0, :, :GW]
    ch = sc_ref[0, :, GW:2 * GW] * sc_ref[0, :, 2 * GW:]
    cp = chprev[...]
    sw = scw_ref[...]
    ysc_ref[0] = bg * (delayed(cp, ch, 2) * sw[0:1] + delayed(cp, ch, 1) * sw[1:2] + ch * sw[2:3])
    chprev[...] = ch[ts - SUBLANES:]

    rowf = lax.broadcasted_iota(I32, (ts, LANES), 0)
    c = _log_sigmoid(f_ref[0] + fb_ref[...])
    d = 1
    while d < ts:
        c = c + jnp.where(rowf >= d, pltpu.roll(c, d, 0), 0.0)
        d *= 2
    c = c + cprev[...]
    cprev[...] = c[ts - 1:]
    ct = c.T[:SUBLANES]
    for j in range(ts // tk):
        cumt_ref[0, j] = ct[:, j * tk:(j + 1) * tk]


def _rgsc(rg, sc, f, cw, cb, wa, ba, wi, bi, lam, scw, fb, tk, ts=512):
    B, S, _ = rg.shape
    GW = GROUP_WIDTH
    per = ts // tk

    def full(a):
        return pl.BlockSpec(a.shape, lambda b, t: (0,) * a.ndim)

    def seq(w):
        return pl.BlockSpec((1, ts, w), lambda b, t: (b, t, 0))

    params = (cw, cb, wa, ba, wi, bi, lam, scw, fb)
    return pl.pallas_call(
        functools.partial(_rgsc_kernel, ts=ts, tk=tk),
        grid=(B, S // ts),
        in_specs=[seq(2 * GW), seq(3 * GW), seq(LANES)] + [full(p) for p in params],
        out_specs=[seq(GW), seq(GW),
                   pl.BlockSpec((1, per, SUBLANES, tk), lambda b, t: (b, t, 0, 0))],
        out_shape=[jax.ShapeDtypeStruct((B, S, GW), F32), jax.ShapeDtypeStruct((B, S, GW), F32),
                   jax.ShapeDtypeStruct((B, S // tk, SUBLANES, tk), F32)],
        scratch_shapes=[pltpu.VMEM((SUBLANES, GW), F32), pltpu.VMEM((SUBLANES, GW), F32),
                        pltpu.VMEM((1, GW), F32), pltpu.VMEM((1, LANES), F32)],
        compiler_params=_params(("parallel", "arbitrary")),
        name="rgsc",
    )(rg, sc, f, *params)


HEADS_TOGETHER = GROUP_HEADS


assert math.log2(HEAD_DIM) % 2 == 0


def _fox_kernel(q_ref, k_ref, v_ref, cr_ref, o_ref, *, tq):
    qi = pl.program_id(1)
    scale = HEAD_DIM ** -0.5
    rowi = lax.broadcasted_iota(I32, (tq, tq), 0)
    coli = lax.broadcasted_iota(I32, (tq, tq), 1)
    causal = coli <= rowi
    qs = [q_ref[0, h] * scale for h in range(GROUP_HEADS)]

    def head_step(h, ki, carry, diag):
        m, l, acc = carry
        off = pl.multiple_of(ki * tq, tq)
        k = k_ref[0, h, pl.ds(off, tq), :]
        v = v_ref[0, h, pl.ds(off, tq), :]
        s = _nt_dot(qs[h], k) - cr_ref[0, ki, h:h + 1, :]
        if diag:
            s = jnp.where(causal, s, -jnp.inf)
        m_new = jnp.maximum(m, jnp.max(s, axis=-1, keepdims=True))
        alpha = jnp.exp(m - m_new)
        p = jnp.exp(s - m_new)
        l = alpha * l + jnp.sum(p, axis=-1, keepdims=True)
        acc = alpha * acc + jnp.dot(p.astype(BF16), v, preferred_element_type=F32)
        return m_new, l, acc

    def step(ki, carries, diag, heads):
        return tuple(head_step(h, ki, c, diag) for h, c in zip(heads, carries))

    init = (jnp.full((tq, 1), -jnp.inf, F32), jnp.zeros((tq, 1), F32),
            jnp.zeros((tq, HEAD_DIM), F32))
    outs = []
    for h0 in range(0, GROUP_HEADS, HEADS_TOGETHER):
        heads = range(h0, h0 + HEADS_TOGETHER)
        carries = lax.fori_loop(0, qi, functools.partial(step, diag=False, heads=heads),
                                (init,) * HEADS_TOGETHER)
        carries = step(qi, carries, True, heads)
        outs += [acc / l for _, l, acc in carries]
    o_ref[0] = jnp.concatenate(outs, axis=-1)


def _qkv_specs(S, tq):
    H, d = GROUP_HEADS, HEAD_DIM
    return [pl.BlockSpec((1, H, tq, d), lambda b, i: (b, 0, i, 0)),
            pl.BlockSpec((1, H, S, d), lambda b, i: (b, 1, 0, 0)),
            pl.BlockSpec((1, H, S, d), lambda b, i: (b, 2, 0, 0))]


def _fox(qkv, cumt, tq=256):
    B, _, S, d = qkv.shape
    H = GROUP_HEADS
    return pl.pallas_call(
        functools.partial(_fox_kernel, tq=tq),
        grid=(B, S // tq),
        in_specs=_qkv_specs(S, tq)
        + [pl.BlockSpec((1, S // tq, SUBLANES, tq), lambda b, i: (b, 0, 0, 0))],
        out_specs=pl.BlockSpec((1, tq, H * d), lambda b, i: (b, i, 0)),
        out_shape=jax.ShapeDtypeStruct((B, S, H * d), F32),
        compiler_params=_params(("parallel", "arbitrary")),
        name="fox_attn",
    )(qkv, qkv, qkv, cumt)


EXP_UNDERFLOW = -104.0


def _sb_kernel(q_ref, k_ref, v_ref, o_ref, *, tq):
    qi = pl.program_id(1)
    scale = HEAD_DIM ** -0.5
    rowi = lax.broadcasted_iota(I32, (tq, tq), 0)
    coli = lax.broadcasted_iota(I32, (tq, tq), 1)
    strict = coli < rowi
    later = jnp.where(rowi > coli, 1.0, 0.0).astype(BF16)
    qs = [q_ref[0, h] * scale for h in range(GROUP_HEADS)]
    def head_step(h, ki, carry, diag):
        rest, acc = carry
        off = pl.multiple_of(ki * tq, tq)
        k = k_ref[0, h, pl.ds(off, tq), :]
        v = v_ref[0, h, pl.ds(off, tq), :]
        z = _nt_dot(qs[h], k)
        ls = _log_sigmoid(z)
        l1m = ls - z
        if diag:
            l1m = jnp.where(strict, l1m, 0.0)
        hi = l1m.astype(BF16)
        lo = (l1m - hi.astype(F32)).astype(BF16)
        tail = (jnp.dot(hi, later, preferred_element_type=F32)
                + jnp.dot(lo, later, preferred_element_type=F32) + rest)
        w = jnp.exp(ls + tail)
        if diag:
            w = jnp.where(strict, w, 0.0)
        acc = acc + jnp.dot(w.astype(BF16), v, preferred_element_type=F32)
        rest = rest + jnp.sum(l1m, axis=-1, keepdims=True)
        return rest, acc

    def step(ki, carries, diag):
        return tuple(head_step(h, ki, carries[h], diag) for h in range(GROUP_HEADS))

    init = (jnp.zeros((tq, 1), F32), jnp.zeros((tq, HEAD_DIM), F32))
    carries = step(qi, (init,) * GROUP_HEADS, True)

    def more(c):
        j, carries = c
        top = carries[0][0]
        for rest, _ in carries[1:]:
            top = jnp.maximum(top, rest)
        return jnp.logical_and(j < qi, jnp.max(top) > EXP_UNDERFLOW)

    def further(c):
        j, carries = c
        return j + 1, step(qi - 1 - j, carries, False)

    _, carries = lax.while_loop(more, further, (jnp.int32(0), carries))
    o_ref[0] = jnp.concatenate([acc for _, acc in carries], axis=-1)


def _sb(qkv, tq=256):
    B, _, S, d = qkv.shape
    H = GROUP_HEADS
    return pl.pallas_call(
        functools.partial(_sb_kernel, tq=tq),
        grid=(B, S // tq),
        in_specs=_qkv_specs(S, tq),
        out_specs=pl.BlockSpec((1, tq, H * d), lambda b, i: (b, i, 0)),
        out_shape=jax.ShapeDtypeStruct((B, S, H * d), F32),
        compiler_params=_params(("parallel", "arbitrary")),
        name="sb_attn",
    )(qkv, qkv, qkv)


def _mixout_kernel(y0_ref, y1_ref, y2_ref, y3_ref, g_ref, w_ref, x_ref, lg_ref, lb_ref, o_ref):
    def rms(y):
        return y * lax.rsqrt(jnp.mean(y * y, axis=-1, keepdims=True) + GROUP_RMS_EPS)

    y = jnp.concatenate([rms(r[...]) for r in (y0_ref, y1_ref, y2_ref, y3_ref)], axis=-1)
    y = (y * g_ref[...]).astype(BF16)
    mix = jnp.dot(y, w_ref[...], preferred_element_type=F32)
    o_ref[...] = _layer_norm(ALPHA * x_ref[...] + mix, lg_ref[...], lb_ref[...])


def _mixout(ys, g, w, x2d, lg, lb, tm=512):
    T, D = x2d.shape
    GW = GROUP_WIDTH

    def full(a):
        return pl.BlockSpec(a.shape, lambda i: (0,) * a.ndim)

    return pl.pallas_call(
        _mixout_kernel,
        grid=(T // tm,),
        in_specs=[pl.BlockSpec((tm, GW), lambda i: (i, 0))] * 4
        + [full(g), full(w), pl.BlockSpec((tm, D), lambda i: (i, 0)), full(lg), full(lb)],
        out_specs=pl.BlockSpec((tm, D), lambda i: (i, 0)),
        out_shape=jax.ShapeDtypeStruct((T, D), F32),
        compiler_params=_params(("parallel",)),
        name="mixout",
    )(*ys, g, w, x2d, lg, lb)


def _kv_kernel(m_ref, w_ref, k_ref, v_ref):
    kv = jnp.dot(m_ref[...].astype(BF16), w_ref[...], preferred_element_type=F32)
    k_ref[...] = kv[:, :D_MODEL].astype(BF16)
    v_ref[...] = kv[:, D_MODEL:].astype(BF16)


def _kv(mem2d, wkv, tm=512):
    M, D = mem2d.shape
    return pl.pallas_call(
        _kv_kernel,
        grid=(M // tm,),
        in_specs=[pl.BlockSpec((tm, D), lambda i: (i, 0)),
                  pl.BlockSpec((D, 2 * D), lambda i: (0, 0))],
        out_specs=[pl.BlockSpec((tm, D), lambda i: (i, 0))] * 2,
        out_shape=[jax.ShapeDtypeStruct((M, D), BF16)] * 2,
        compiler_params=_params(("parallel",)),
        name="mem_kv",
    )(mem2d, wkv)


PEER_TB = 32


def _chunk_major_spec(tm):
    return pl.BlockSpec((tm // PEER_TB, ROW_CHUNKS * PEER_TB, LANES), lambda i: (i, 0, 0))


def _chunk_major_shape(T):
    return jax.ShapeDtypeStruct((T // PEER_TB, ROW_CHUNKS * PEER_TB, LANES), F32)


def _to_chunk_major(y, ref):
    for g in range(y.shape[0] // PEER_TB):
        for c in range(ROW_CHUNKS):
            ref[g, c * PEER_TB:(c + 1) * PEER_TB, :] = (
                y[g * PEER_TB:(g + 1) * PEER_TB, c * LANES:(c + 1) * LANES])


def _xattn_kernel(x_ref, wq_ref, k_ref, v_ref, wo_ref, lg_ref, lb_ref, o_ref, oc_ref):
    x = x_ref[...]
    q = jnp.dot(x.astype(BF16), wq_ref[...], preferred_element_type=F32).astype(BF16)
    scale = XA_HEAD_DIM ** -0.5
    outs = []
    for h in range(XA_HEADS):
        sl = slice(h * XA_HEAD_DIM, (h + 1) * XA_HEAD_DIM)
        s = _nt_dot(q[:, sl], k_ref[0, :, sl]) * scale
        p = jnp.exp(s - jnp.max(s, axis=-1, keepdims=True))
        p = p / jnp.sum(p, axis=-1, keepdims=True)
        outs.append(jnp.dot(p.astype(BF16), v_ref[0, :, sl], preferred_element_type=F32))
    o = jnp.concatenate(outs, axis=-1).astype(BF16)
    xa = jnp.dot(o, wo_ref[...], preferred_element_type=F32)
    y = _layer_norm(ALPHA * x + xa, lg_ref[...], lb_ref[...])
    o_ref[...] = y
    _to_chunk_major(y, oc_ref)


def _xattn(x2d, wq, k, v, wo, lg, lb, seq_len, tm=512):
    T, D = x2d.shape
    M = k.shape[1]
    per_seq = seq_len // tm

    def full(a):
        return pl.BlockSpec(a.shape, lambda i: (0,) * a.ndim)

    return pl.pallas_call(
        _xattn_kernel,
        grid=(T // tm,),
        in_specs=[pl.BlockSpec((tm, D), lambda i: (i, 0)), full(wq),
                  pl.BlockSpec((1, M, D), lambda i: (i // per_seq, 0, 0)),
                  pl.BlockSpec((1, M, D), lambda i: (i // per_seq, 0, 0)),
                  full(wo), full(lg), full(lb)],
        out_specs=[pl.BlockSpec((tm, D), lambda i: (i, 0)), _chunk_major_spec(tm)],
        out_shape=[jax.ShapeDtypeStruct((T, D), F32), _chunk_major_shape(T)],
        compiler_params=_params(("parallel",)),
        name="xattn",
    )(x2d, wq, k, v, wo, lg, lb)


E_PER_TOK = PEER_HEADS * PEER_TOPK


def _staircase():
    return [(a, b) for a in range(PEER_TOPK) for b in range(PEER_TOPK // (a + 1))]


def _route_kernel(x_ref, wq_ref, k1_ref, k2_ref, e_ref, g_ref, v_scr, i_scr, *, ts):
    xb = x_ref[...].astype(BF16)
    key_id = lax.broadcasted_iota(I32, (N_KEYS, ts), 0).astype(F32)
    qd = 2 * PEER_HALF
    for h in range(PEER_HEADS):
        qry = jnp.dot(xb, wq_ref[:, h * qd:(h + 1) * qd], preferred_element_type=F32)
        for half, kref in ((0, k1_ref), (1, k2_ref)):
            qh = qry[:, half * PEER_HALF:(half + 1) * PEER_HALF].astype(BF16)
            s0 = _nt_dot(kref[...], qh)

            s = s0
            for it in range(PEER_TOPK):
                m = jnp.max(s, axis=0, keepdims=True)
                idx = jnp.min(jnp.where(s == m, key_id, float(N_KEYS)), axis=0, keepdims=True)
                v_scr[half, it, h:h + 1, :] = m
                i_scr[half, it, h:h + 1, :] = idx.astype(I32)
                s = jnp.where(key_id == idx, -jnp.inf, s)

    cells = _staircase()
    cand = tuple(v_scr[0, a] + v_scr[1, b] for a, b in cells)
    expert = [i_scr[0, a] * N_KEYS + i_scr[1, b] for a, b in cells]

    picked = []
    for it in range(PEER_TOPK):
        best_v, best_e, best_c = cand[0], expert[0], jnp.zeros((PEER_HEADS, ts), I32)
        for c in range(1, len(cells)):
            better = cand[c] > best_v
            best_v = jnp.where(better, cand[c], best_v)
            best_e = jnp.where(better, expert[c], best_e)
            best_c = jnp.where(better, c, best_c)
        i_scr[0, it] = best_e * WORD_ROWS
        picked.append(best_v)
        cand = tuple(jnp.where(best_c == c, -jnp.inf, cand[c]) for c in range(len(cells)))
    ex = [jnp.exp(v - picked[0]) for v in picked]
    den = ex[0]
    for it in range(1, PEER_TOPK):
        den = den + ex[it]
    inv = 1.0 / den
    g_ref[...] = jnp.concatenate([e * inv for e in ex], axis=0).T
    rows = pltpu.bitcast(i_scr[0].reshape(E_PER_TOK, ts), F32)
    e_ref[...] = pltpu.bitcast(rows.T, I32)


def _route(x2d, wq, k1, k2, ts=256):
    T, D = x2d.shape
    nb = T // ts

    def full(a):
        return pl.BlockSpec(a.shape, lambda i: (0,) * a.ndim)

    blk = pl.BlockSpec((ts, E_PER_TOK), lambda i: (i, 0))
    return pl.pallas_call(
        functools.partial(_route_kernel, ts=ts),
        grid=(nb,),
        in_specs=[pl.BlockSpec((ts, D), lambda i: (i, 0)), full(wq), full(k1), full(k2)],
        out_specs=[blk, blk],
        out_shape=[jax.ShapeDtypeStruct((T, E_PER_TOK), I32),
                   jax.ShapeDtypeStruct((T, E_PER_TOK), F32)],
        scratch_shapes=[pltpu.VMEM((2, PEER_TOPK, PEER_HEADS, ts), F32),
                        pltpu.VMEM((2, PEER_TOPK, PEER_HEADS, ts), I32)],
        compiler_params=_params(("parallel",)),
        name="peer_route",
    )(x2d, wq, k1, k2)


TOK_GROUP = SUBLANES


N_GROUPS = PEER_TB // TOK_GROUP


def _gathered_rows(ids_ref, tab_ref, t):
    pieces = []
    for j in range(E_PER_TOK):
        r0 = pl.multiple_of(ids_ref[t, j], WORD_ROWS)
        pieces.append(tab_ref[pl.ds(r0, WORD_ROWS), :])
    return jnp.concatenate(pieces, axis=0)


def _token_chunks(tt):
    return pl.ds(tt, ROW_CHUNKS, stride=PEER_TB)


def _chunk_of_col():
    col = lax.broadcasted_iota(I32, (ROW_CHUNKS, E_PER_TOK * ROW_CHUNKS), 1)
    row = lax.broadcasted_iota(I32, (ROW_CHUNKS, E_PER_TOK * ROW_CHUNKS), 0)
    return (col & (ROW_CHUNKS - 1)) == row


def _peer_score_kernel(ids_ref, x_ref, tab_ref, s_ref):
    own_chunk = _chunk_of_col()
    c_id = lax.broadcasted_iota(I32, (E_PER_TOK * ROW_CHUNKS, E_PER_TOK), 0)
    e_id = lax.broadcasted_iota(I32, (E_PER_TOK * ROW_CHUNKS, E_PER_TOK), 1)
    fold = jnp.where(c_id >> CHUNK_SHIFT == e_id, 1.0, 0.0).astype(BF16)

    for g in range(N_GROUPS):
        tok0 = g * TOK_GROUP
        parts = []
        for tt in range(TOK_GROUP):
            rows = pltpu.bitcast(_gathered_rows(ids_ref, tab_ref, tok0 + tt), BF16)
            xt = x_ref[0, _token_chunks(tok0 + tt), :].astype(BF16)
            full = _nt_dot(xt, rows)
            parts.append(jnp.where(own_chunk, full, 0.0))
        part = jnp.concatenate(parts, axis=0)
        hi = part.astype(BF16)
        lo = (part - hi.astype(F32)).astype(BF16)
        sc = (jnp.dot(hi, fold, preferred_element_type=F32)
              + jnp.dot(lo, fold, preferred_element_type=F32))
        sc = jnp.sum(sc.reshape(TOK_GROUP, ROW_CHUNKS, E_PER_TOK), axis=1)
        s_ref[tok0:tok0 + TOK_GROUP, :] = sc


def _peer_mix_kernel(ids_ref, s_ref, gate_ref, tab_ref, o_ref):
    own_chunk = _chunk_of_col()
    e_id = lax.broadcasted_iota(I32, (E_PER_TOK, E_PER_TOK * ROW_CHUNKS), 0)
    c_id = lax.broadcasted_iota(I32, (E_PER_TOK, E_PER_TOK * ROW_CHUNKS), 1)
    spread = jnp.where(c_id >> CHUNK_SHIFT == e_id, 1.0, 0.0).astype(BF16)

    for g in range(N_GROUPS):
        tok0 = g * TOK_GROUP
        sl = slice(tok0, tok0 + TOK_GROUP)
        s = s_ref[sl, :]
        act = 0.5 * s * (1.0 + lax.erf(s * (2.0 ** -0.5)))
        coef = (gate_ref[sl, :] * act).astype(BF16)
        coef = jnp.dot(coef, spread, preferred_element_type=F32)
        for tt in range(TOK_GROUP):
            rows = pltpu.bitcast(_gathered_rows(ids_ref, tab_ref, tok0 + tt), BF16)
            ct = jnp.broadcast_to(coef[tt:tt + 1], (ROW_CHUNKS, E_PER_TOK * ROW_CHUNKS))
            ct = jnp.where(own_chunk, ct, 0.0).astype(BF16)
            o_ref[0, _token_chunks(tok0 + tt), :] = jnp.dot(ct, rows, preferred_element_type=F32)


def _peer_experts(ids, gate, xc, tab_u, tab_v):
    T = ids.shape[0]
    tb = PEER_TB
    nb = T // tb
    ids_spec = pl.BlockSpec((tb, E_PER_TOK), lambda i: (i, 0), memory_space=pltpu.SMEM,
                            pipeline_mode=pl.Buffered(1))
    tab_spec = pl.BlockSpec(tab_u.shape, lambda i: (0, 0), pipeline_mode=pl.Buffered(1))
    tok_spec = pl.BlockSpec((tb, E_PER_TOK), lambda i: (i, 0))
    row_spec = _chunk_major_spec(tb)
    s = pl.pallas_call(
        _peer_score_kernel,
        grid=(nb,),
        in_specs=[ids_spec, row_spec, tab_spec],
        out_specs=tok_spec,
        out_shape=jax.ShapeDtypeStruct((T, E_PER_TOK), F32),
        compiler_params=_params(("arbitrary",)),
        name="peer_score",
    )(ids, xc, tab_u)
    return pl.pallas_call(
        _peer_mix_kernel,
        grid=(nb,),
        in_specs=[ids_spec, tok_spec, tok_spec, tab_spec],
        out_specs=row_spec,
        out_shape=_chunk_major_shape(T),
        compiler_params=_params(("arbitrary",)),
        name="peer_mix",
    )(ids, s, gate, tab_v)


def _resln_kernel(x_ref, y_ref, lg_ref, lb_ref, o_ref):
    G = x_ref.shape[0]
    z = ALPHA * x_ref[...] + y_ref[...]
    zs = [z[:, c * PEER_TB:(c + 1) * PEER_TB, :] for c in range(ROW_CHUNKS)]
    tot = zs[0]
    for c in range(1, ROW_CHUNKS):
        tot = tot + zs[c]
    mu = jnp.sum(tot, axis=-1, keepdims=True) * (1.0 / D_MODEL)
    zc = [a - mu for a in zs]
    sq = zc[0] * zc[0]
    for c in range(1, ROW_CHUNKS):
        sq = sq + zc[c] * zc[c]
    rstd = lax.rsqrt(jnp.sum(sq, axis=-1, keepdims=True) * (1.0 / D_MODEL) + LN_EPS)
    for c in range(ROW_CHUNKS):
        cols = slice(c * LANES, (c + 1) * LANES)
        y = (zc[c] * rstd).reshape(G * PEER_TB, LANES)
        o_ref[:, cols] = y * lg_ref[:, cols] + lb_ref[:, cols]


def _resln(xc, yc, lg, lb, tm=512):
    T = xc.shape[0] * PEER_TB
    blk = _chunk_major_spec(tm)
    par = pl.BlockSpec((1, D_MODEL), lambda i: (0, 0))
    return pl.pallas_call(
        _resln_kernel,
        grid=(T // tm,),
        in_specs=[blk, blk, par, par],
        out_specs=pl.BlockSpec((tm, D_MODEL), lambda i: (i, 0)),
        out_shape=jax.ShapeDtypeStruct((T, D_MODEL), F32),
        compiler_params=_params(("parallel",)),
        name="res_ln",
    )(xc, yc, lg, lb)


def _pack_kernel(t_ref, o_ref):
    def bf16_bits(v):
        return pltpu.bitcast(v.astype(BF16).astype(F32), jnp.uint32)

    te = t_ref.shape[0]
    for r in range(WORD_ROWS):
        lo = bf16_bits(t_ref[:, (2 * r) * LANES:(2 * r + 1) * LANES])
        hi = bf16_bits(t_ref[:, (2 * r + 1) * LANES:(2 * r + 2) * LANES])
        words = (hi & jnp.uint32(0xFFFF0000)) | (lo >> 16)
        o_ref[pl.ds(r, te, stride=WORD_ROWS), :] = pltpu.bitcast(words, I32)


def _pack_table(t, layer, te=512):
    _, E, D = t.shape
    return pl.pallas_call(
        _pack_kernel,
        grid=(E // te,),
        in_specs=[pl.BlockSpec((None, te, D), lambda i: (layer, i, 0))],
        out_specs=pl.BlockSpec((te * WORD_ROWS, LANES), lambda i: (i, 0)),
        out_shape=jax.ShapeDtypeStruct((E * WORD_ROWS, LANES), I32),
        compiler_params=_params(("parallel",)),
        name="pack_table",
    )(t)


def _block_diag(w):
    H, d, _ = w.shape
    out = jnp.zeros((H * d, H * d), w.dtype)
    for h in range(H):
        out = out.at[h * d:(h + 1) * d, h * d:(h + 1) * d].set(w[h])
    return out


def _cat_w_in_kernel(w_ref, o_ref):
    GW = GROUP_WIDTH
    off_f = 5 * GW
    off_sb = off_f + GROUP_HEADS
    w = w_ref[...].astype(BF16)
    o_ref[:, :off_f] = w[:, :off_f]
    o_ref[:, off_f:C_F] = w[:, off_sb:N_IN]
    o_ref[:, C_F:C_END] = jnp.zeros((w.shape[0], C_END - C_F), BF16)
    o_ref[:, C_F:C_F + GROUP_HEADS] = w[:, off_f:off_sb]


def _cat_w_in(w_in, layer, tr=256):
    _, D, N = w_in.shape
    return pl.pallas_call(
        _cat_w_in_kernel,
        grid=(D // tr,),
        in_specs=[pl.BlockSpec((None, tr, N), lambda i: (layer, i, 0))],
        out_specs=pl.BlockSpec((tr, C_END), lambda i: (i, 0)),
        out_shape=jax.ShapeDtypeStruct((D, C_END), BF16),
        compiler_params=_params(("parallel",)),
        name="regroup_w_in",
    )(w_in)


def _row(v, width=None):
    v = v.reshape(1, -1).astype(F32)
    if width is not None and v.shape[1] < width:
        v = jnp.pad(v, ((0, 0), (0, width - v.shape[1])))
    return v


def _layer(x2d, kmem, vmem_, B, S, p, tables, layer, tq=256):
    T = B * S
    rg, fox, sb, sc, f = _inproj(x2d, _cat_w_in(tables[2], layer), B, S)
    yrg, ysc, cumt = _rgsc(
        rg.reshape(B, S, -1), sc.reshape(B, S, -1), f.reshape(B, S, -1),
        p["rg_conv_w"], _row(p["rg_conv_b"]), _block_diag(p["rg_wa"]).astype(BF16), _row(p["rg_ba"]),
        _block_diag(p["rg_wi"]).astype(BF16), _row(p["rg_bi"]), _row(p["rg_lambda"]),
        p["sc_conv_w"], _row(p["fox_bf"], LANES), tq)
    yfox = _fox(fox, cumt, tq=tq)
    ysb = _sb(sb, tq=tq)
    ys = (yrg.reshape(T, -1), yfox.reshape(T, -1), ysb.reshape(T, -1), ysc.reshape(T, -1))
    x1 = _mixout(ys, _row(p["mix_norm_g"]), p["w_out"].astype(BF16), x2d,
                 _row(p["ln1_g"]), _row(p["ln1_b"]))
    x2, x2c = _xattn(x1, p["xa_wq"].astype(BF16), kmem, vmem_, p["xa_wo"].astype(BF16),
                     _row(p["ln2_g"]), _row(p["ln2_b"]), S)
    ids, gate = _route(x2, p["peer_wq"].astype(BF16), p["peer_k1"].astype(BF16),
                       p["peer_k2"].astype(BF16))
    ffc = _peer_experts(ids, gate, x2c, _pack_table(tables[0], layer), _pack_table(tables[1], layer))
    return _resln(x2c, ffc, _row(p["ln3_g"]), _row(p["ln3_b"]))


_LAYER_PARAMS = ("w_in", "w_out", "rg_conv_w", "rg_conv_b", "rg_wa", "rg_ba", "rg_wi", "rg_bi",
                 "rg_lambda", "fox_bf", "sc_conv_w", "mix_norm_g", "ln1_g", "ln1_b", "xa_wq",
                 "xa_wkv", "xa_wo", "ln2_g", "ln2_b", "peer_wq", "peer_k1", "peer_k2", "peer_u",
                 "peer_v", "ln3_g", "ln3_b")


def kernel(x, mem, w_in, w_out, rg_conv_w, rg_conv_b, rg_wa, rg_ba, rg_wi, rg_bi, rg_lambda, fox_bf, sc_conv_w, mix_norm_g, ln1_g, ln1_b, xa_wq, xa_wkv, xa_wo, ln2_g, ln2_b, peer_wq, peer_k1, peer_k2, peer_u, peer_v, ln3_g, ln3_b):
    stacked = dict(zip(_LAYER_PARAMS, (
        w_in, w_out, rg_conv_w, rg_conv_b, rg_wa, rg_ba, rg_wi, rg_bi, rg_lambda, fox_bf,
        sc_conv_w, mix_norm_g, ln1_g, ln1_b, xa_wq, xa_wkv, xa_wo, ln2_g, ln2_b, peer_wq,
        peer_k1, peer_k2, peer_u, peer_v, ln3_g, ln3_b)))
    B, S, D = x.shape
    M = mem.shape[1]
    x2d = x.reshape(B * S, D)
    mem2d = mem.reshape(B * M, D)
    for l in range(w_in.shape[0]):
        p = {k: v[l] for k, v in stacked.items() if k not in ("peer_u", "peer_v", "w_in")}
        kmem, vmem_ = _kv(mem2d, p["xa_wkv"].astype(BF16))
        x2d = _layer(x2d, kmem.reshape(B, M, D), vmem_.reshape(B, M, D), B, S, p,
                     (peer_u, peer_v, w_in), l)
    return x2d.reshape(B, S, D)
```

```python
import functools
import math

import jax
import jax.numpy as jnp
from jax import lax
from jax.experimental import pallas as pl
from jax.experimental.pallas import tpu as pltpu

F32 = jnp.float32
BF16 = jnp.bfloat16
I32 = jnp.int32

D_MODEL = 1024
GROUP_WIDTH = 256
GROUP_HEADS = 4
HEAD_DIM = 64
N_IN = 2820
RGLRU_C = 8.0
XA_HEADS = 4
XA_HEAD_DIM = D_MODEL // XA_HEADS
PEER_HEADS = 8
N_KEYS = 128
PEER_HALF = 128
PEER_TOPK = 16
DEPTH = 2
ALPHA = (2.0 * DEPTH) ** 0.25
LN_EPS = 1e-5

SUBLANES = 8
LANES = 128
WORD_ROWS = D_MODEL // (2 * LANES)
ROW_CHUNKS = D_MODEL // LANES
CHUNK_SHIFT = ROW_CHUNKS.bit_length() - 1

V7X_VMEM_BYTES = 64 * 1024 * 1024
VMEM_LIMIT = V7X_VMEM_BYTES * 3 // 4
GROUP_RMS_EPS = 1e-6
GELU_TANH_CUBIC = 0.044715


def _params(sem, vmem=VMEM_LIMIT):
    return pltpu.CompilerParams(dimension_semantics=sem, vmem_limit_bytes=vmem)


def _layer_norm(z, g, b):
    mu = jnp.mean(z, axis=-1, keepdims=True)
    zc = z - mu
    var = jnp.mean(zc * zc, axis=-1, keepdims=True)
    return zc * lax.rsqrt(var + LN_EPS) * g + b


def _log_sigmoid(z):
    return jnp.minimum(z, 0.0) - jnp.log1p(jnp.exp(-jnp.abs(z)))


def _nt_dot(a, b):
    return lax.dot_general(a, b, (((1,), (1,)), ((), ())), preferred_element_type=F32)


C_RG, C_FOX, C_SB, C_SC, C_F, C_END = 0, 512, 1280, 2048, 2816, 2944


def _inproj_kernel(x_ref, w_ref, rg_ref, fox_ref, sb_ref, sc_ref, f_ref):
    xb = x_ref[...].astype(BF16)

    def mm(lo, hi):
        return jnp.dot(xb, w_ref[:, lo:hi], preferred_element_type=F32)

    def heads(ref, lo, hi):
        qkv = mm(lo, hi)
        for j in range(3 * GROUP_HEADS):
            ref[0, j] = qkv[:, j * HEAD_DIM:(j + 1) * HEAD_DIM].astype(BF16)

    rg_ref[...] = mm(C_RG, C_FOX)
    heads(fox_ref, C_FOX, C_SB)
    heads(sb_ref, C_SB, C_SC)
    sc_ref[...] = mm(C_SC, C_F)
    f_ref[...] = mm(C_F, C_END)


def _inproj(x2d, w_cat, B, S, tm=512):
    T, D = x2d.shape
    per_seq = S // tm
    flat = ((C_FOX - C_RG, F32), (C_F - C_SC, F32), (C_END - C_F, F32))
    flat_specs = [pl.BlockSpec((tm, w), lambda i: (i, 0)) for w, _ in flat]
    flat_shapes = [jax.ShapeDtypeStruct((T, w), dt) for w, dt in flat]
    head_spec = pl.BlockSpec((1, 3 * GROUP_HEADS, tm, HEAD_DIM),
                             lambda i: (i // per_seq, 0, i % per_seq, 0))
    head_shape = jax.ShapeDtypeStruct((B, 3 * GROUP_HEADS, S, HEAD_DIM), BF16)
    return pl.pallas_call(
        _inproj_kernel,
        grid=(T // tm,),
        in_specs=[pl.BlockSpec((tm, D), lambda i: (i, 0)),
                  pl.BlockSpec((D, C_END), lambda i: (0, 0))],
        out_specs=[flat_specs[0], head_spec, head_spec, flat_specs[1], flat_specs[2]],
        out_shape=[flat_shapes[0], head_shape, head_shape, flat_shapes[1], flat_shapes[2]],
        compiler_params=_params(("parallel",)),
        name="inproj",
    )(x2d, w_cat)


def _rgsc_kernel(rg_ref, sc_ref, f_ref, cw_ref, cb_ref, wa_ref, ba_ref, wi_ref, bi_ref, lam_ref,
                 scw_ref, fb_ref, yrg_ref, ysc_ref, cumt_ref, xprev, chprev, hprev, cprev,
                 *, ts, tk):
    GW = GROUP_WIDTH

    @pl.when(pl.program_id(1) == 0)
    def _():
        xprev[...] = jnp.zeros_like(xprev)
        chprev[...] = jnp.zeros_like(chprev)
        hprev[...] = jnp.zeros_like(hprev)
        cprev[...] = jnp.zeros_like(cprev)

    row = lax.broadcasted_iota(I32, (ts, GW), 0)

    def delayed(prev, cur, d):
        ext = jnp.concatenate([prev, cur], axis=0)
        return pltpu.roll(ext, d, 0)[SUBLANES:]

    xr = rg_ref[0, :, :GW]
    gate = rg_ref[0, :, GW:]
    xp = xprev[...]
    cw = cw_ref[...]
    xc = (delayed(xp, xr, 3) * cw[0:1] + delayed(xp, xr, 2) * cw[1:2]
          + delayed(xp, xr, 1) * cw[2:3] + xr * cw[3:4] + cb_ref[...])
    xprev[...] = xr[ts - SUBLANES:]
    xcb = xc.astype(BF16)
    r = jax.nn.sigmoid(jnp.dot(xcb, wa_ref[...], preferred_element_type=F32) + ba_ref[...])
    ig = jax.nn.sigmoid(jnp.dot(xcb, wi_ref[...], preferred_element_type=F32) + bi_ref[...])
    z = -lam_ref[...]
    softplus = jnp.maximum(z, 0.0) + jnp.log1p(jnp.exp(-jnp.abs(z)))
    log_a = -RGLRU_C * r * softplus
    a = jnp.exp(log_a)
    u = jnp.sqrt(-jnp.tanh(log_a) * (a * a + 1.0)) * (ig * xc)
    acc_a, acc_b = a, u
    d = 1
    while d < ts:
        keep = row >= d
        a_s = jnp.where(keep, pltpu.roll(acc_a, d, 0), 1.0)
        b_s = jnp.where(keep, pltpu.roll(acc_b, d, 0), 0.0)
        acc_b = acc_a * b_s + acc_b
        acc_a = acc_a * a_s
        d *= 2
    h = acc_b + acc_a * hprev[...]
    hprev[...] = h[ts - 1:]
    c0 = math.sqrt(2.0 / math.pi)
    gelu = 0.5 * gate * (1.0 + jnp.tanh(c0 * (gate + GELU_TANH_CUBIC * gate * gate * gate)))
    yrg_ref[0] = h * gelu

    bg = sc_ref[0, :, :GW]
    ch = sc_ref[0, :, GW:2 * GW] * sc_ref[0, :, 2 * GW:]
    cp = chprev[...]
    sw = scw_ref[...]
    ysc_ref[0] = bg * (delayed(cp, ch, 2) * sw[0:1] + delayed(cp, ch, 1) * sw[1:2] + ch * sw[2:3])
    chprev[...] = ch[ts - SUBLANES:]

    rowf = lax.broadcasted_iota(I32, (ts, LANES), 0)
    c = _log_sigmoid(f_ref[0] + fb_ref[...])
    d = 1
    while d < ts:
        c = c + jnp.where(rowf >= d, pltpu.roll(c, d, 0), 0.0)
        d *= 2
    c = c + cprev[...]
    cprev[...] = c[ts - 1:]
    ct = c.T[:SUBLANES]
    for j in range(ts // tk):
        cumt_ref[0, j] = ct[:, j * tk:(j + 1) * tk]


def _rgsc(rg, sc, f, cw, cb, wa, ba, wi, bi, lam, scw, fb, tk, ts=512):
    B, S, _ = rg.shape
    GW = GROUP_WIDTH
    per = ts // tk

    def full(a):
        return pl.BlockSpec(a.shape, lambda b, t: (0,) * a.ndim)

    def seq(w):
        return pl.BlockSpec((1, ts, w), lambda b, t: (b, t, 0))

    params = (cw, cb, wa, ba, wi, bi, lam, scw, fb)
    return pl.pallas_call(
        functools.partial(_rgsc_kernel, ts=ts, tk=tk),
        grid=(B, S // ts),
        in_specs=[seq(2 * GW), seq(3 * GW), seq(LANES)] + [full(p) for p in params],
        out_specs=[seq(GW), seq(GW),
                   pl.BlockSpec((1, per, SUBLANES, tk), lambda b, t: (b, t, 0, 0))],
        out_shape=[jax.ShapeDtypeStruct((B, S, GW), F32), jax.ShapeDtypeStruct((B, S, GW), F32),
                   jax.ShapeDtypeStruct((B, S // tk, SUBLANES, tk), F32)],
        scratch_shapes=[pltpu.VMEM((SUBLANES, GW), F32), pltpu.VMEM((SUBLANES, GW), F32),
                        pltpu.VMEM((1, GW), F32), pltpu.VMEM((1, LANES), F32)],
        compiler_params=_params(("parallel", "arbitrary")),
        name="rgsc",
    )(rg, sc, f, *params)


HEADS_TOGETHER = GROUP_HEADS


assert math.log2(HEAD_DIM) % 2 == 0


def _fox_kernel(q_ref, k_ref, v_ref, cr_ref, o_ref, *, tq):
    qi = pl.program_id(1)
    scale = HEAD_DIM ** -0.5
    rowi = lax.broadcasted_iota(I32, (tq, tq), 0)
    coli = lax.broadcasted_iota(I32, (tq, tq), 1)
    causal = coli <= rowi
    qs = [q_ref[0, h] * scale for h in range(GROUP_HEADS)]

    def head_step(h, ki, carry, diag):
        m, l, acc = carry
        off = pl.multiple_of(ki * tq, tq)
        k = k_ref[0, h, pl.ds(off, tq), :]
        v = v_ref[0, h, pl.ds(off, tq), :]
        s = _nt_dot(qs[h], k) - cr_ref[0, ki, h:h + 1, :]
        if diag:
            s = jnp.where(causal, s, -jnp.inf)
        m_new = jnp.maximum(m, jnp.max(s, axis=-1, keepdims=True))
        alpha = jnp.exp(m - m_new)
        p = jnp.exp(s - m_new)
        l = alpha * l + jnp.sum(p, axis=-1, keepdims=True)
        acc = alpha * acc + jnp.dot(p.astype(BF16), v, preferred_element_type=F32)
        return m_new, l, acc

    def step(ki, carries, diag, heads):
        return tuple(head_step(h, ki, c, diag) for h, c in zip(heads, carries))

    init = (jnp.full((tq, 1), -jnp.inf, F32), jnp.zeros((tq, 1), F32),
            jnp.zeros((tq, HEAD_DIM), F32))
    outs = []
    for h0 in range(0, GROUP_HEADS, HEADS_TOGETHER):
        heads = range(h0, h0 + HEADS_TOGETHER)
        carries = lax.fori_loop(0, qi, functools.partial(step, diag=False, heads=heads),
                                (init,) * HEADS_TOGETHER)
        carries = step(qi, carries, True, heads)
        outs += [acc / l for _, l, acc in carries]
    o_ref[0] = jnp.concatenate(outs, axis=-1)


def _qkv_specs(S, tq):
    H, d = GROUP_HEADS, HEAD_DIM
    return [pl.BlockSpec((1, H, tq, d), lambda b, i: (b, 0, i, 0)),
            pl.BlockSpec((1, H, S, d), lambda b, i: (b, 1, 0, 0)),
            pl.BlockSpec((1, H, S, d), lambda b, i: (b, 2, 0, 0))]


def _fox(qkv, cumt, tq=256):
    B, _, S, d = qkv.shape
    H = GROUP_HEADS
    return pl.pallas_call(
        functools.partial(_fox_kernel, tq=tq),
        grid=(B, S // tq),
        in_specs=_qkv_specs(S, tq)
        + [pl.BlockSpec((1, S // tq, SUBLANES, tq), lambda b, i: (b, 0, 0, 0))],
        out_specs=pl.BlockSpec((1, tq, H * d), lambda b, i: (b, i, 0)),
        out_shape=jax.ShapeDtypeStruct((B, S, H * d), F32),
        compiler_params=_params(("parallel", "arbitrary")),
        name="fox_attn",
    )(qkv, qkv, qkv, cumt)


EXP_UNDERFLOW = -104.0


def _sb_kernel(q_ref, k_ref, v_ref, o_ref, *, tq):
    qi = pl.program_id(1)
    scale = HEAD_DIM ** -0.5
    rowi = lax.broadcasted_iota(I32, (tq, tq), 0)
    coli = lax.broadcasted_iota(I32, (tq, tq), 1)
    strict = coli < rowi
    later = jnp.where(rowi > coli, 1.0, 0.0).astype(BF16)
    qs = [q_ref[0, h] * scale for h in range(GROUP_HEADS)]
    def head_step(h, ki, carry, diag):
        rest, acc = carry
        off = pl.multiple_of(ki * tq, tq)
        k = k_ref[0, h, pl.ds(off, tq), :]
        v = v_ref[0, h, pl.ds(off, tq), :]
        z = _nt_dot(qs[h], k)
        ls = _log_sigmoid(z)
        l1m = ls - z
        if diag:
            l1m = jnp.where(strict, l1m, 0.0)
        hi = l1m.astype(BF16)
        lo = (l1m - hi.astype(F32)).astype(BF16)
        tail = (jnp.dot(hi, later, preferred_element_type=F32)
                + jnp.dot(lo, later, preferred_element_type=F32) + rest)
        w = jnp.exp(ls + tail)
        if diag:
            w = jnp.where(strict, w, 0.0)
        acc = acc + jnp.dot(w.astype(BF16), v, preferred_element_type=F32)
        rest = rest + jnp.sum(l1m, axis=-1, keepdims=True)
        return rest, acc

    def step(ki, carries, diag):
        return tuple(head_step(h, ki, carries[h], diag) for h in range(GROUP_HEADS))

    init = (jnp.zeros((tq, 1), F32), jnp.zeros((tq, HEAD_DIM), F32))
    carries = step(qi, (init,) * GROUP_HEADS, True)

    def more(c):
        j, carries = c
        top = carries[0][0]
        for rest, _ in carries[1:]:
            top = jnp.maximum(top, rest)
        return jnp.logical_and(j < qi, jnp.max(top) > EXP_UNDERFLOW)

    def further(c):
        j, carries = c
        return j + 1, step(qi - 1 - j, carries, False)

    _, carries = lax.while_loop(more, further, (jnp.int32(0), carries))
    o_ref[0] = jnp.concatenate([acc for _, acc in carries], axis=-1)


def _sb(qkv, tq=256):
    B, _, S, d = qkv.shape
    H = GROUP_HEADS
    return pl.pallas_call(
        functools.partial(_sb_kernel, tq=tq),
        grid=(B, S // tq),
        in_specs=_qkv_specs(S, tq),
        out_specs=pl.BlockSpec((1, tq, H * d), lambda b, i: (b, i, 0)),
        out_shape=jax.ShapeDtypeStruct((B, S, H * d), F32),
        compiler_params=_params(("parallel", "arbitrary")),
        name="sb_attn",
    )(qkv, qkv, qkv)


def _mixout_kernel(y0_ref, y1_ref, y2_ref, y3_ref, g_ref, w_ref, x_ref, lg_ref, lb_ref, o_ref):
    def rms(y):
        return y * lax.rsqrt(jnp.mean(y * y, axis=-1, keepdims=True) + GROUP_RMS_EPS)

    y = jnp.concatenate([rms(r[...]) for r in (y0_ref, y1_ref, y2_ref, y3_ref)], axis=-1)
    y = (y * g_ref[...]).astype(BF16)
    mix = jnp.dot(y, w_ref[...], preferred_element_type=F32)
    o_ref[...] = _layer_norm(ALPHA * x_ref[...] + mix, lg_ref[...], lb_ref[...])


def _mixout(ys, g, w, x2d, lg, lb, tm=512):
    T, D = x2d.shape
    GW = GROUP_WIDTH

    def full(a):
        return pl.BlockSpec(a.shape, lambda i: (0,) * a.ndim)

    return pl.pallas_call(
        _mixout_kernel,
        grid=(T // tm,),
        in_specs=[pl.BlockSpec((tm, GW), lambda i: (i, 0))] * 4
        + [full(g), full(w), pl.BlockSpec((tm, D), lambda i: (i, 0)), full(lg), full(lb)],
        out_specs=pl.BlockSpec((tm, D), lambda i: (i, 0)),
        out_shape=jax.ShapeDtypeStruct((T, D), F32),
        compiler_params=_params(("parallel",)),
        name="mixout",
    )(*ys, g, w, x2d, lg, lb)


def _kv_kernel(m_ref, w_ref, k_ref, v_ref):
    kv = jnp.dot(m_ref[...].astype(BF16), w_ref[...], preferred_element_type=F32)
    k_ref[...] = kv[:, :D_MODEL].astype(BF16)
    v_ref[...] = kv[:, D_MODEL:].astype(BF16)


def _kv(mem2d, wkv, tm=512):
    M, D = mem2d.shape
    return pl.pallas_call(
        _kv_kernel,
        grid=(M // tm,),
        in_specs=[pl.BlockSpec((tm, D), lambda i: (i, 0)),
                  pl.BlockSpec((D, 2 * D), lambda i: (0, 0))],
        out_specs=[pl.BlockSpec((tm, D), lambda i: (i, 0))] * 2,
        out_shape=[jax.ShapeDtypeStruct((M, D), BF16)] * 2,
        compiler_params=_params(("parallel",)),
        name="mem_kv",
    )(mem2d, wkv)


PEER_TB = 32


def _chunk_major_spec(tm):
    return pl.BlockSpec((tm // PEER_TB, ROW_CHUNKS * PEER_TB, LANES), lambda i: (i, 0, 0))


def _chunk_major_shape(T):
    return jax.ShapeDtypeStruct((T // PEER_TB, ROW_CHUNKS * PEER_TB, LANES), F32)


def _to_chunk_major(y, ref):
    for g in range(y.shape[0] // PEER_TB):
        for c in range(ROW_CHUNKS):
            ref[g, c * PEER_TB:(c + 1) * PEER_TB, :] = (
                y[g * PEER_TB:(g + 1) * PEER_TB, c * LANES:(c + 1) * LANES])


def _xattn_kernel(x_ref, wq_ref, k_ref, v_ref, wo_ref, lg_ref, lb_ref, o_ref, oc_ref):
    x = x_ref[...]
    q = jnp.dot(x.astype(BF16), wq_ref[...], preferred_element_type=F32).astype(BF16)
    scale = XA_HEAD_DIM ** -0.5
    outs = []
    for h in range(XA_HEADS):
        sl = slice(h * XA_HEAD_DIM, (h + 1) * XA_HEAD_DIM)
        s = _nt_dot(q[:, sl], k_ref[0, :, sl]) * scale
        p = jnp.exp(s - jnp.max(s, axis=-1, keepdims=True))
        p = p / jnp.sum(p, axis=-1, keepdims=True)
        outs.append(jnp.dot(p.astype(BF16), v_ref[0, :, sl], preferred_element_type=F32))
    o = jnp.concatenate(outs, axis=-1).astype(BF16)
    xa = jnp.dot(o, wo_ref[...], preferred_element_type=F32)
    y = _layer_norm(ALPHA * x + xa, lg_ref[...], lb_ref[...])
    o_ref[...] = y
    _to_chunk_major(y, oc_ref)


def _xattn(x2d, wq, k, v, wo, lg, lb, seq_len, tm=512):
    T, D = x2d.shape
    M = k.shape[1]
    per_seq = seq_len // tm

    def full(a):
        return pl.BlockSpec(a.shape, lambda i: (0,) * a.ndim)

    return pl.pallas_call(
        _xattn_kernel,
        grid=(T // tm,),
        in_specs=[pl.BlockSpec((tm, D), lambda i: (i, 0)), full(wq),
                  pl.BlockSpec((1, M, D), lambda i: (i // per_seq, 0, 0)),
                  pl.BlockSpec((1, M, D), lambda i: (i // per_seq, 0, 0)),
                  full(wo), full(lg), full(lb)],
        out_specs=[pl.BlockSpec((tm, D), lambda i: (i, 0)), _chunk_major_spec(tm)],
        out_shape=[jax.ShapeDtypeStruct((T, D), F32), _chunk_major_shape(T)],
        compiler_params=_params(("parallel",)),
        name="xattn",
    )(x2d, wq, k, v, wo, lg, lb)


E_PER_TOK = PEER_HEADS * PEER_TOPK


def _staircase():
    return [(a, b) for a in range(PEER_TOPK) for b in range(PEER_TOPK // (a + 1))]


def _route_kernel(x_ref, wq_ref, k1_ref, k2_ref, e_ref, g_ref, v_scr, i_scr, *, ts):
    xb = x_ref[...].astype(BF16)
    key_id = lax.broadcasted_iota(I32, (N_KEYS, ts), 0).astype(F32)
    qd = 2 * PEER_HALF
    for h in range(PEER_HEADS):
        qry = jnp.dot(xb, wq_ref[:, h * qd:(h + 1) * qd], preferred_element_type=F32)
        for half, kref in ((0, k1_ref), (1, k2_ref)):
            qh = qry[:, half * PEER_HALF:(half + 1) * PEER_HALF].astype(BF16)
            s0 = _nt_dot(kref[...], qh)

            s = s0
            for it in range(PEER_TOPK):
                m = jnp.max(s, axis=0, keepdims=True)
                idx = jnp.min(jnp.where(s == m, key_id, float(N_KEYS)), axis=0, keepdims=True)
                v_scr[half, it, h:h + 1, :] = m
                i_scr[half, it, h:h + 1, :] = idx.astype(I32)
                s = jnp.where(key_id == idx, -jnp.inf, s)

    cells = _staircase()
    cand = tuple(v_scr[0, a] + v_scr[1, b] for a, b in cells)
    expert = [i_scr[0, a] * N_KEYS + i_scr[1, b] for a, b in cells]

    picked = []
    for it in range(PEER_TOPK):
        best_v, best_e, best_c = cand[0], expert[0], jnp.zeros((PEER_HEADS, ts), I32)
        for c in range(1, len(cells)):
            better = cand[c] > best_v
            best_v = jnp.where(better, cand[c], best_v)
            best_e = jnp.where(better, expert[c], best_e)
            best_c = jnp.where(better, c, best_c)
        i_scr[0, it] = best_e * WORD_ROWS
        picked.append(best_v)
        cand = tuple(jnp.where(best_c == c, -jnp.inf, cand[c]) for c in range(len(cells)))
    ex = [jnp.exp(v - picked[0]) for v in picked]
    den = ex[0]
    for it in range(1, PEER_TOPK):
        den = den + ex[it]
    inv = 1.0 / den
    g_ref[...] = jnp.concatenate([e * inv for e in ex], axis=0).T
    rows = pltpu.bitcast(i_scr[0].reshape(E_PER_TOK, ts), F32)
    e_ref[...] = pltpu.bitcast(rows.T, I32)


def _route(x2d, wq, k1, k2, ts=256):
    T, D = x2d.shape
    nb = T // ts

    def full(a):
        return pl.BlockSpec(a.shape, lambda i: (0,) * a.ndim)

    blk = pl.BlockSpec((ts, E_PER_TOK), lambda i: (i, 0))
    return pl.pallas_call(
        functools.partial(_route_kernel, ts=ts),
        grid=(nb,),
        in_specs=[pl.BlockSpec((ts, D), lambda i: (i, 0)), full(wq), full(k1), full(k2)],
        out_specs=[blk, blk],
        out_shape=[jax.ShapeDtypeStruct((T, E_PER_TOK), I32),
                   jax.ShapeDtypeStruct((T, E_PER_TOK), F32)],
        scratch_shapes=[pltpu.VMEM((2, PEER_TOPK, PEER_HEADS, ts), F32),
                        pltpu.VMEM((2, PEER_TOPK, PEER_HEADS, ts), I32)],
        compiler_params=_params(("parallel",)),
        name="peer_route",
    )(x2d, wq, k1, k2)


TOK_GROUP = SUBLANES


N_GROUPS = PEER_TB // TOK_GROUP


def _gathered_rows(ids_ref, tab_ref, t):
    pieces = []
    for j in range(E_PER_TOK):
        r0 = pl.multiple_of(ids_ref[t, j], WORD_ROWS)
        pieces.append(tab_ref[pl.ds(r0, WORD_ROWS), :])
    return jnp.concatenate(pieces, axis=0)


def _token_chunks(tt):
    return pl.ds(tt, ROW_CHUNKS, stride=PEER_TB)


def _chunk_of_col():
    col = lax.broadcasted_iota(I32, (ROW_CHUNKS, E_PER_TOK * ROW_CHUNKS), 1)
    row = lax.broadcasted_iota(I32, (ROW_CHUNKS, E_PER_TOK * ROW_CHUNKS), 0)
    return (col & (ROW_CHUNKS - 1)) == row


def _with_ids(ids_hbm, bufs, sems, body):
    i = pl.program_id(0)

    def block_copy(step, slot):
        rows = pl.ds(pl.multiple_of(step * PEER_TB, PEER_TB), PEER_TB)
        return pltpu.make_async_copy(ids_hbm.at[rows], bufs[slot], sems.at[slot])

    @pl.when(i == 0)
    def _():
        block_copy(0, 0).start()

    for slot in range(2):
        @pl.when(i % 2 == slot)
        def _(slot=slot):
            block_copy(i, slot).wait()

            @pl.when(i + 1 < pl.num_programs(0))
            def _():
                block_copy(i + 1, 1 - slot).start()

            body(bufs[slot])


def _peer_score_kernel(ids_hbm, x_ref, tab_ref, s_ref, ids_a, ids_b, sems):
    _with_ids(ids_hbm, (ids_a, ids_b), sems,
              functools.partial(_peer_score_body, x_ref, tab_ref, s_ref))


def _peer_score_body(x_ref, tab_ref, s_ref, ids_ref):
    own_chunk = _chunk_of_col()
    c_id = lax.broadcasted_iota(I32, (E_PER_TOK * ROW_CHUNKS, E_PER_TOK), 0)
    e_id = lax.broadcasted_iota(I32, (E_PER_TOK * ROW_CHUNKS, E_PER_TOK), 1)
    fold = jnp.where(c_id >> CHUNK_SHIFT == e_id, 1.0, 0.0).astype(BF16)

    for g in range(N_GROUPS):
        tok0 = g * TOK_GROUP
        parts = []
        for tt in range(TOK_GROUP):
            rows = pltpu.bitcast(_gathered_rows(ids_ref, tab_ref, tok0 + tt), BF16)
            xt = x_ref[0, _token_chunks(tok0 + tt), :].astype(BF16)
            full = _nt_dot(xt, rows)
            parts.append(jnp.where(own_chunk, full, 0.0))
        part = jnp.concatenate(parts, axis=0)
        hi = part.astype(BF16)
        lo = (part - hi.astype(F32)).astype(BF16)
        sc = (jnp.dot(hi, fold, preferred_element_type=F32)
              + jnp.dot(lo, fold, preferred_element_type=F32))
        sc = jnp.sum(sc.reshape(TOK_GROUP, ROW_CHUNKS, E_PER_TOK), axis=1)
        s_ref[tok0:tok0 + TOK_GROUP, :] = sc


def _peer_mix_kernel(ids_hbm, s_ref, gate_ref, tab_ref, o_ref, ids_a, ids_b, sems):
    _with_ids(ids_hbm, (ids_a, ids_b), sems,
              functools.partial(_peer_mix_body, s_ref, gate_ref, tab_ref, o_ref))


def _peer_mix_body(s_ref, gate_ref, tab_ref, o_ref, ids_ref):
    own_chunk = _chunk_of_col()
    e_id = lax.broadcasted_iota(I32, (E_PER_TOK, E_PER_TOK * ROW_CHUNKS), 0)
    c_id = lax.broadcasted_iota(I32, (E_PER_TOK, E_PER_TOK * ROW_CHUNKS), 1)
    spread = jnp.where(c_id >> CHUNK_SHIFT == e_id, 1.0, 0.0).astype(BF16)

    for g in range(N_GROUPS):
        tok0 = g * TOK_GROUP
        sl = slice(tok0, tok0 + TOK_GROUP)
        s = s_ref[sl, :]
        act = 0.5 * s * (1.0 + lax.erf(s * (2.0 ** -0.5)))
        coef = (gate_ref[sl, :] * act).astype(BF16)
        coef = jnp.dot(coef, spread, preferred_element_type=F32)
        for tt in range(TOK_GROUP):
            rows = pltpu.bitcast(_gathered_rows(ids_ref, tab_ref, tok0 + tt), BF16)
            ct = jnp.broadcast_to(coef[tt:tt + 1], (ROW_CHUNKS, E_PER_TOK * ROW_CHUNKS))
            ct = jnp.where(own_chunk, ct, 0.0).astype(BF16)
            o_ref[0, _token_chunks(tok0 + tt), :] = jnp.dot(ct, rows, preferred_element_type=F32)


def _peer_experts(ids, gate, xc, tab_u, tab_v):
    T = ids.shape[0]
    tb = PEER_TB
    nb = T // tb
    ids_spec = pl.BlockSpec(memory_space=pl.ANY)
    ids_scratch = [pltpu.SMEM((tb, E_PER_TOK), I32), pltpu.SMEM((tb, E_PER_TOK), I32),
                   pltpu.SemaphoreType.DMA((2,))]
    tab_spec = pl.BlockSpec(tab_u.shape, lambda i: (0, 0), pipeline_mode=pl.Buffered(1))
    tok_spec = pl.BlockSpec((tb, E_PER_TOK), lambda i: (i, 0))
    row_spec = _chunk_major_spec(tb)
    s = pl.pallas_call(
        _peer_score_kernel,
        grid=(nb,),
        in_specs=[ids_spec, row_spec, tab_spec],
        out_specs=tok_spec,
        out_shape=jax.ShapeDtypeStruct((T, E_PER_TOK), F32),
        scratch_shapes=ids_scratch,
        compiler_params=_params(("arbitrary",)),
        name="peer_score",
    )(ids, xc, tab_u)
    return pl.pallas_call(
        _peer_mix_kernel,
        grid=(nb,),
        in_specs=[ids_spec, tok_spec, tok_spec, tab_spec],
        out_specs=row_spec,
        out_shape=_chunk_major_shape(T),
        scratch_shapes=ids_scratch,
        compiler_params=_params(("arbitrary",)),
        name="peer_mix",
    )(ids, s, gate, tab_v)


def _resln_kernel(x_ref, y_ref, lg_ref, lb_ref, o_ref):
    G = x_ref.shape[0]
    z = ALPHA * x_ref[...] + y_ref[...]
    zs = [z[:, c * PEER_TB:(c + 1) * PEER_TB, :] for c in range(ROW_CHUNKS)]
    tot = zs[0]
    for c in range(1, ROW_CHUNKS):
        tot = tot + zs[c]
    mu = jnp.sum(tot, axis=-1, keepdims=True) * (1.0 / D_MODEL)
    zc = [a - mu for a in zs]
    sq = zc[0] * zc[0]
    for c in range(1, ROW_CHUNKS):
        sq = sq + zc[c] * zc[c]
    rstd = lax.rsqrt(jnp.sum(sq, axis=-1, keepdims=True) * (1.0 / D_MODEL) + LN_EPS)
    for c in range(ROW_CHUNKS):
        cols = slice(c * LANES, (c + 1) * LANES)
        y = (zc[c] * rstd).reshape(G * PEER_TB, LANES)
        o_ref[:, cols] = y * lg_ref[:, cols] + lb_ref[:, cols]


def _resln(xc, yc, lg, lb, tm=512):
    T = xc.shape[0] * PEER_TB
    blk = _chunk_major_spec(tm)
    par = pl.BlockSpec((1, D_MODEL), lambda i: (0, 0))
    return pl.pallas_call(
        _resln_kernel,
        grid=(T // tm,),
        in_specs=[blk, blk, par, par],
        out_specs=pl.BlockSpec((tm, D_MODEL), lambda i: (i, 0)),
        out_shape=jax.ShapeDtypeStruct((T, D_MODEL), F32),
        compiler_params=_params(("parallel",)),
        name="res_ln",
    )(xc, yc, lg, lb)


def _pack_kernel(t_ref, o_ref):
    def bf16_bits(v):
        return pltpu.bitcast(v.astype(BF16).astype(F32), jnp.uint32)

    te = t_ref.shape[0]
    for r in range(WORD_ROWS):
        lo = bf16_bits(t_ref[:, (2 * r) * LANES:(2 * r + 1) * LANES])
        hi = bf16_bits(t_ref[:, (2 * r + 1) * LANES:(2 * r + 2) * LANES])
        words = (hi & jnp.uint32(0xFFFF0000)) | (lo >> 16)
        o_ref[pl.ds(r, te, stride=WORD_ROWS), :] = pltpu.bitcast(words, I32)


def _pack_table(t, layer, te=512):
    _, E, D = t.shape
    return pl.pallas_call(
        _pack_kernel,
        grid=(E // te,),
        in_specs=[pl.BlockSpec((None, te, D), lambda i: (layer, i, 0))],
        out_specs=pl.BlockSpec((te * WORD_ROWS, LANES), lambda i: (i, 0)),
        out_shape=jax.ShapeDtypeStruct((E * WORD_ROWS, LANES), I32),
        compiler_params=_params(("parallel",)),
        name="pack_table",
    )(t)


def _block_diag(w):
    H, d, _ = w.shape
    out = jnp.zeros((H * d, H * d), w.dtype)
    for h in range(H):
        out = out.at[h * d:(h + 1) * d, h * d:(h + 1) * d].set(w[h])
    return out


def _cat_w_in_kernel(w_ref, o_ref):
    GW = GROUP_WIDTH
    off_f = 5 * GW
    off_sb = off_f + GROUP_HEADS
    w = w_ref[...].astype(BF16)
    o_ref[:, :off_f] = w[:, :off_f]
    o_ref[:, off_f:C_F] = w[:, off_sb:N_IN]
    o_ref[:, C_F:C_END] = jnp.zeros((w.shape[0], C_END - C_F), BF16)
    o_ref[:, C_F:C_F + GROUP_HEADS] = w[:, off_f:off_sb]


def _cat_w_in(w_in, layer, tr=256):
    _, D, N = w_in.shape
    return pl.pallas_call(
        _cat_w_in_kernel,
        grid=(D // tr,),
        in_specs=[pl.BlockSpec((None, tr, N), lambda i: (layer, i, 0))],
        out_specs=pl.BlockSpec((tr, C_END), lambda i: (i, 0)),
        out_shape=jax.ShapeDtypeStruct((D, C_END), BF16),
        compiler_params=_params(("parallel",)),
        name="regroup_w_in",
    )(w_in)


def _row(v, width=None):
    v = v.reshape(1, -1).astype(F32)
    if width is not None and v.shape[1] < width:
        v = jnp.pad(v, ((0, 0), (0, width - v.shape[1])))
    return v


def _layer(x2d, kmem, vmem_, B, S, p, tables, layer, tq=256):
    T = B * S
    rg, fox, sb, sc, f = _inproj(x2d, _cat_w_in(tables[2], layer), B, S)
    yrg, ysc, cumt = _rgsc(
        rg.reshape(B, S, -1), sc.reshape(B, S, -1), f.reshape(B, S, -1),
        p["rg_conv_w"], _row(p["rg_conv_b"]), _block_diag(p["rg_wa"]).astype(BF16), _row(p["rg_ba"]),
        _block_diag(p["rg_wi"]).astype(BF16), _row(p["rg_bi"]), _row(p["rg_lambda"]),
        p["sc_conv_w"], _row(p["fox_bf"], LANES), tq)
    yfox = _fox(fox, cumt, tq=tq)
    ysb = _sb(sb, tq=tq)
    ys = (yrg.reshape(T, -1), yfox.reshape(T, -1), ysb.reshape(T, -1), ysc.reshape(T, -1))
    x1 = _mixout(ys, _row(p["mix_norm_g"]), p["w_out"].astype(BF16), x2d,
                 _row(p["ln1_g"]), _row(p["ln1_b"]))
    x2, x2c = _xattn(x1, p["xa_wq"].astype(BF16), kmem, vmem_, p["xa_wo"].astype(BF16),
                     _row(p["ln2_g"]), _row(p["ln2_b"]), S)
    ids, gate = _route(x2, p["peer_wq"].astype(BF16), p["peer_k1"].astype(BF16),
                       p["peer_k2"].astype(BF16))
    ffc = _peer_experts(ids, gate, x2c, _pack_table(tables[0], layer), _pack_table(tables[1], layer))
    return _resln(x2c, ffc, _row(p["ln3_g"]), _row(p["ln3_b"]))


_LAYER_PARAMS = ("w_in", "w_out", "rg_conv_w", "rg_conv_b", "rg_wa", "rg_ba", "rg_wi", "rg_bi",
                 "rg_lambda", "fox_bf", "sc_conv_w", "mix_norm_g", "ln1_g", "ln1_b", "xa_wq",
                 "xa_wkv", "xa_wo", "ln2_g", "ln2_b", "peer_wq", "peer_k1", "peer_k2", "peer_u",
                 "peer_v", "ln3_g", "ln3_b")


def kernel(x, mem, w_in, w_out, rg_conv_w, rg_conv_b, rg_wa, rg_ba, rg_wi, rg_bi, rg_lambda, fox_bf, sc_conv_w, mix_norm_g, ln1_g, ln1_b, xa_wq, xa_wkv, xa_wo, ln2_g, ln2_b, peer_wq, peer_k1, peer_k2, peer_u, peer_v, ln3_g, ln3_b):
    stacked = dict(zip(_LAYER_PARAMS, (
        w_in, w_out, rg_conv_w, rg_conv_b, rg_wa, rg_ba, rg_wi, rg_bi, rg_lambda, fox_bf,
        sc_conv_w, mix_norm_g, ln1_g, ln1_b, xa_wq, xa_wkv, xa_wo, ln2_g, ln2_b, peer_wq,
        peer_k1, peer_k2, peer_u, peer_v, ln3_g, ln3_b)))
    B, S, D = x.shape
    M = mem.shape[1]
    x2d = x.reshape(B * S, D)
    mem2d = mem.reshape(B * M, D)
    for l in range(w_in.shape[0]):
        p = {k: v[l] for k, v in stacked.items() if k not in ("peer_u", "peer_v", "w_in")}
        kmem, vmem_ = _kv(mem2d, p["xa_wkv"].astype(BF16))
        x2d = _layer(x2d, kmem.reshape(B, M, D), vmem_.reshape(B, M, D), B, S, p,
                     (peer_u, peer_v, w_in), l)
    return x2d.reshape(B, S, D)
```

```python
import functools
import math

import jax
import jax.numpy as jnp
from jax import lax
from jax.experimental import pallas as pl
from jax.experimental.pallas import tpu as pltpu

F32 = jnp.float32
BF16 = jnp.bfloat16
I32 = jnp.int32

D_MODEL = 1024
GROUP_WIDTH = 256
GROUP_HEADS = 4
HEAD_DIM = 64
N_IN = 2820
RGLRU_C = 8.0
XA_HEADS = 4
XA_HEAD_DIM = D_MODEL // XA_HEADS
PEER_HEADS = 8
N_KEYS = 128
PEER_HALF = 128
PEER_TOPK = 16
DEPTH = 2
ALPHA = (2.0 * DEPTH) ** 0.25
LN_EPS = 1e-5

SUBLANES = 8
LANES = 128
WORD_ROWS = D_MODEL // (2 * LANES)
ROW_CHUNKS = D_MODEL // LANES
CHUNK_SHIFT = ROW_CHUNKS.bit_length() - 1

V7X_VMEM_BYTES = 64 * 1024 * 1024
VMEM_LIMIT = V7X_VMEM_BYTES * 3 // 4
GROUP_RMS_EPS = 1e-6
GELU_TANH_CUBIC = 0.044715


def _params(sem, vmem=VMEM_LIMIT):
    return pltpu.CompilerParams(dimension_semantics=sem, vmem_limit_bytes=vmem)


def _layer_norm(z, g, b):
    mu = jnp.mean(z, axis=-1, keepdims=True)
    zc = z - mu
    var = jnp.mean(zc * zc, axis=-1, keepdims=True)
    return zc * lax.rsqrt(var + LN_EPS) * g + b


def _log_sigmoid(z):
    return jnp.minimum(z, 0.0) - jnp.log1p(jnp.exp(-jnp.abs(z)))


def _nt_dot(a, b):
    return lax.dot_general(a, b, (((1,), (1,)), ((), ())), preferred_element_type=F32)


C_RG, C_FOX, C_SB, C_SC, C_F, C_END = 0, 512, 1280, 2048, 2816, 2944


def _inproj_kernel(x_ref, w_ref, rg_ref, fox_ref, sb_ref, sc_ref, f_ref):
    xb = x_ref[...].astype(BF16)

    def mm(lo, hi):
        return jnp.dot(xb, w_ref[:, lo:hi], preferred_element_type=F32)

    def heads(ref, lo, hi):
        qkv = mm(lo, hi)
        for j in range(3 * GROUP_HEADS):
            ref[0, j] = qkv[:, j * HEAD_DIM:(j + 1) * HEAD_DIM].astype(BF16)

    rg_ref[...] = mm(C_RG, C_FOX)
    heads(fox_ref, C_FOX, C_SB)
    heads(sb_ref, C_SB, C_SC)
    sc_ref[...] = mm(C_SC, C_F)
    f_ref[...] = mm(C_F, C_END)


def _inproj(x2d, w_cat, B, S, tm=512):
    T, D = x2d.shape
    per_seq = S // tm
    flat = ((C_FOX - C_RG, F32), (C_F - C_SC, F32), (C_END - C_F, F32))
    flat_specs = [pl.BlockSpec((tm, w), lambda i: (i, 0)) for w, _ in flat]
    flat_shapes = [jax.ShapeDtypeStruct((T, w), dt) for w, dt in flat]
    head_spec = pl.BlockSpec((1, 3 * GROUP_HEADS, tm, HEAD_DIM),
                             lambda i: (i // per_seq, 0, i % per_seq, 0))
    head_shape = jax.ShapeDtypeStruct((B, 3 * GROUP_HEADS, S, HEAD_DIM), BF16)
    return pl.pallas_call(
        _inproj_kernel,
        grid=(T // tm,),
        in_specs=[pl.BlockSpec((tm, D), lambda i: (i, 0)),
                  pl.BlockSpec((D, C_END), lambda i: (0, 0))],
        out_specs=[flat_specs[0], head_spec, head_spec, flat_specs[1], flat_specs[2]],
        out_shape=[flat_shapes[0], head_shape, head_shape, flat_shapes[1], flat_shapes[2]],
        compiler_params=_params(("parallel",)),
        name="inproj",
    )(x2d, w_cat)


def _rgsc_kernel(rg_ref, sc_ref, f_ref, cw_ref, cb_ref, wa_ref, ba_ref, wi_ref, bi_ref, lam_ref,
                 scw_ref, fb_ref, yrg_ref, ysc_ref, cumt_ref, xprev, chprev, hprev, cprev,
                 *, ts, tk):
    GW = GROUP_WIDTH

    @pl.when(pl.program_id(1) == 0)
    def _():
        xprev[...] = jnp.zeros_like(xprev)
        chprev[...] = jnp.zeros_like(chprev)
        hprev[...] = jnp.zeros_like(hprev)
        cprev[...] = jnp.zeros_like(cprev)

    row = lax.broadcasted_iota(I32, (ts, GW), 0)

    def delayed(prev, cur, d):
        ext = jnp.concatenate([prev, cur], axis=0)
        return pltpu.roll(ext, d, 0)[SUBLANES:]

    xr = rg_ref[0, :, :GW]
    gate = rg_ref[0, :, GW:]
    xp = xprev[...]
    cw = cw_ref[...]
    xc = (delayed(xp, xr, 3) * cw[0:1] + delayed(xp, xr, 2) * cw[1:2]
          + delayed(xp, xr, 1) * cw[2:3] + xr * cw[3:4] + cb_ref[...])
    xprev[...] = xr[ts - SUBLANES:]
    xcb = xc.astype(BF16)
    r = jax.nn.sigmoid(jnp.dot(xcb, wa_ref[...], preferred_element_type=F32) + ba_ref[...])
    ig = jax.nn.sigmoid(jnp.dot(xcb, wi_ref[...], preferred_element_type=F32) + bi_ref[...])
    z = -lam_ref[...]
    softplus = jnp.maximum(z, 0.0) + jnp.log1p(jnp.exp(-jnp.abs(z)))
    log_a = -RGLRU_C * r * softplus
    a = jnp.exp(log_a)
    u = jnp.sqrt(-jnp.tanh(log_a) * (a * a + 1.0)) * (ig * xc)
    acc_a, acc_b = a, u
    d = 1
    while d < ts:
        keep = row >= d
        a_s = jnp.where(keep, pltpu.roll(acc_a, d, 0), 1.0)
        b_s = jnp.where(keep, pltpu.roll(acc_b, d, 0), 0.0)
        acc_b = acc_a * b_s + acc_b
        acc_a = acc_a * a_s
        d *= 2
    h = acc_b + acc_a * hprev[...]
    hprev[...] = h[ts - 1:]
    c0 = math.sqrt(2.0 / math.pi)
    gelu = 0.5 * gate * (1.0 + jnp.tanh(c0 * (gate + GELU_TANH_CUBIC * gate * gate * gate)))
    yrg_ref[0] = h * gelu

    bg = sc_ref[0, :, :GW]
    ch = sc_ref[0, :, GW:2 * GW] * sc_ref[0, :, 2 * GW:]
    cp = chprev[...]
    sw = scw_ref[...]
    ysc_ref[0] = bg * (delayed(cp, ch, 2) * sw[0:1] + delayed(cp, ch, 1) * sw[1:2] + ch * sw[2:3])
    chprev[...] = ch[ts - SUBLANES:]

    rowf = lax.broadcasted_iota(I32, (ts, LANES), 0)
    c = _log_sigmoid(f_ref[0] + fb_ref[...])
    d = 1
    while d < ts:
        c = c + jnp.where(rowf >= d, pltpu.roll(c, d, 0), 0.0)
        d *= 2
    c = c + cprev[...]
    cprev[...] = c[ts - 1:]
    ct = c.T[:SUBLANES]
    for j in range(ts // tk):
        cumt_ref[0, j] = ct[:, j * tk:(j + 1) * tk]


def _rgsc(rg, sc, f, cw, cb, wa, ba, wi, bi, lam, scw, fb, tk, ts=512):
    B, S, _ = rg.shape
    GW = GROUP_WIDTH
    per = ts // tk

    def full(a):
        return pl.BlockSpec(a.shape, lambda b, t: (0,) * a.ndim)

    def seq(w):
        return pl.BlockSpec((1, ts, w), lambda b, t: (b, t, 0))

    params = (cw, cb, wa, ba, wi, bi, lam, scw, fb)
    return pl.pallas_call(
        functools.partial(_rgsc_kernel, ts=ts, tk=tk),
        grid=(B, S // ts),
        in_specs=[seq(2 * GW), seq(3 * GW), seq(LANES)] + [full(p) for p in params],
        out_specs=[seq(GW), seq(GW),
                   pl.BlockSpec((1, per, SUBLANES, tk), lambda b, t: (b, t, 0, 0))],
        out_shape=[jax.ShapeDtypeStruct((B, S, GW), F32), jax.ShapeDtypeStruct((B, S, GW), F32),
                   jax.ShapeDtypeStruct((B, S // tk, SUBLANES, tk), F32)],
        scratch_shapes=[pltpu.VMEM((SUBLANES, GW), F32), pltpu.VMEM((SUBLANES, GW), F32),
                        pltpu.VMEM((1, GW), F32), pltpu.VMEM((1, LANES), F32)],
        compiler_params=_params(("parallel", "arbitrary")),
        name="rgsc",
    )(rg, sc, f, *params)


HEADS_TOGETHER = GROUP_HEADS


assert math.log2(HEAD_DIM) % 2 == 0


def _fox_kernel(q_ref, k_ref, v_ref, cr_ref, o_ref, *, tq):
    qi = pl.program_id(1)
    scale = HEAD_DIM ** -0.5
    rowi = lax.broadcasted_iota(I32, (tq, tq), 0)
    coli = lax.broadcasted_iota(I32, (tq, tq), 1)
    causal = coli <= rowi
    qs = [q_ref[0, h] * scale for h in range(GROUP_HEADS)]

    def head_step(h, ki, carry, diag):
        m, l, acc = carry
        off = pl.multiple_of(ki * tq, tq)
        k = k_ref[0, h, pl.ds(off, tq), :]
        v = v_ref[0, h, pl.ds(off, tq), :]
        s = _nt_dot(qs[h], k) - cr_ref[0, ki, h:h + 1, :]
        if diag:
            s = jnp.where(causal, s, -jnp.inf)
        m_new = jnp.maximum(m, jnp.max(s, axis=-1, keepdims=True))
        alpha = jnp.exp(m - m_new)
        p = jnp.exp(s - m_new)
        l = alpha * l + jnp.sum(p, axis=-1, keepdims=True)
        acc = alpha * acc + jnp.dot(p.astype(BF16), v, preferred_element_type=F32)
        return m_new, l, acc

    def step(ki, carries, diag, heads):
        return tuple(head_step(h, ki, c, diag) for h, c in zip(heads, carries))

    init = (jnp.full((tq, 1), -jnp.inf, F32), jnp.zeros((tq, 1), F32),
            jnp.zeros((tq, HEAD_DIM), F32))
    outs = []
    for h0 in range(0, GROUP_HEADS, HEADS_TOGETHER):
        heads = range(h0, h0 + HEADS_TOGETHER)
        carries = lax.fori_loop(0, qi, functools.partial(step, diag=False, heads=heads),
                                (init,) * HEADS_TOGETHER)
        carries = step(qi, carries, True, heads)
        outs += [acc / l for _, l, acc in carries]
    o_ref[0] = jnp.concatenate(outs, axis=-1)


def _qkv_specs(S, tq):
    H, d = GROUP_HEADS, HEAD_DIM
    return [pl.BlockSpec((1, H, tq, d), lambda b, i: (b, 0, i, 0)),
            pl.BlockSpec((1, H, S, d), lambda b, i: (b, 1, 0, 0)),
            pl.BlockSpec((1, H, S, d), lambda b, i: (b, 2, 0, 0))]


def _fox(qkv, cumt, tq=256):
    B, _, S, d = qkv.shape
    H = GROUP_HEADS
    return pl.pallas_call(
        functools.partial(_fox_kernel, tq=tq),
        grid=(B, S // tq),
        in_specs=_qkv_specs(S, tq)
        + [pl.BlockSpec((1, S // tq, SUBLANES, tq), lambda b, i: (b, 0, 0, 0))],
        out_specs=pl.BlockSpec((1, tq, H * d), lambda b, i: (b, i, 0)),
        out_shape=jax.ShapeDtypeStruct((B, S, H * d), F32),
        compiler_params=_params(("parallel", "arbitrary")),
        name="fox_attn",
    )(qkv, qkv, qkv, cumt)


EXP_UNDERFLOW = -104.0


def _sb_kernel(q_ref, k_ref, v_ref, o_ref, *, tq):
    qi = pl.program_id(1)
    scale = HEAD_DIM ** -0.5
    rowi = lax.broadcasted_iota(I32, (tq, tq), 0)
    coli = lax.broadcasted_iota(I32, (tq, tq), 1)
    strict = coli < rowi
    later = jnp.where(rowi > coli, 1.0, 0.0).astype(BF16)
    qs = [q_ref[0, h] * scale for h in range(GROUP_HEADS)]
    def head_step(h, ki, carry, diag):
        rest, acc = carry
        off = pl.multiple_of(ki * tq, tq)
        k = k_ref[0, h, pl.ds(off, tq), :]
        v = v_ref[0, h, pl.ds(off, tq), :]
        z = _nt_dot(qs[h], k)
        ls = _log_sigmoid(z)
        l1m = ls - z
        if diag:
            l1m = jnp.where(strict, l1m, 0.0)
        hi = l1m.astype(BF16)
        lo = (l1m - hi.astype(F32)).astype(BF16)
        tail = (jnp.dot(hi, later, preferred_element_type=F32)
                + jnp.dot(lo, later, preferred_element_type=F32) + rest)
        w = jnp.exp(ls + tail)
        if diag:
            w = jnp.where(strict, w, 0.0)
        acc = acc + jnp.dot(w.astype(BF16), v, preferred_element_type=F32)
        rest = rest + jnp.sum(l1m, axis=-1, keepdims=True)
        return rest, acc

    def step(ki, carries, diag):
        return tuple(head_step(h, ki, carries[h], diag) for h in range(GROUP_HEADS))

    init = (jnp.zeros((tq, 1), F32), jnp.zeros((tq, HEAD_DIM), F32))
    carries = step(qi, (init,) * GROUP_HEADS, True)

    def more(c):
        j, carries = c
        top = carries[0][0]
        for rest, _ in carries[1:]:
            top = jnp.maximum(top, rest)
        return jnp.logical_and(j < qi, jnp.max(top) > EXP_UNDERFLOW)

    def further(c):
        j, carries = c
        return j + 1, step(qi - 1 - j, carries, False)

    _, carries = lax.while_loop(more, further, (jnp.int32(0), carries))
    o_ref[0] = jnp.concatenate([acc for _, acc in carries], axis=-1)


def _sb(qkv, tq=256):
    B, _, S, d = qkv.shape
    H = GROUP_HEADS
    return pl.pallas_call(
        functools.partial(_sb_kernel, tq=tq),
        grid=(B, S // tq),
        in_specs=_qkv_specs(S, tq),
        out_specs=pl.BlockSpec((1, tq, H * d), lambda b, i: (b, i, 0)),
        out_shape=jax.ShapeDtypeStruct((B, S, H * d), F32),
        compiler_params=_params(("parallel", "arbitrary")),
        name="sb_attn",
    )(qkv, qkv, qkv)


def _mixout_kernel(y0_ref, y1_ref, y2_ref, y3_ref, g_ref, w_ref, x_ref, lg_ref, lb_ref, o_ref):
    def rms(y):
        return y * lax.rsqrt(jnp.mean(y * y, axis=-1, keepdims=True) + GROUP_RMS_EPS)

    y = jnp.concatenate([rms(r[...]) for r in (y0_ref, y1_ref, y2_ref, y3_ref)], axis=-1)
    y = (y * g_ref[...]).astype(BF16)
    mix = jnp.dot(y, w_ref[...], preferred_element_type=F32)
    o_ref[...] = _layer_norm(ALPHA * x_ref[...] + mix, lg_ref[...], lb_ref[...])


def _mixout(ys, g, w, x2d, lg, lb, tm=512):
    T, D = x2d.shape
    GW = GROUP_WIDTH

    def full(a):
        return pl.BlockSpec(a.shape, lambda i: (0,) * a.ndim)

    return pl.pallas_call(
        _mixout_kernel,
        grid=(T // tm,),
        in_specs=[pl.BlockSpec((tm, GW), lambda i: (i, 0))] * 4
        + [full(g), full(w), pl.BlockSpec((tm, D), lambda i: (i, 0)), full(lg), full(lb)],
        out_specs=pl.BlockSpec((tm, D), lambda i: (i, 0)),
        out_shape=jax.ShapeDtypeStruct((T, D), F32),
        compiler_params=_params(("parallel",)),
        name="mixout",
    )(*ys, g, w, x2d, lg, lb)


def _kv_kernel(m_ref, w_ref, k_ref, v_ref):
    kv = jnp.dot(m_ref[...].astype(BF16), w_ref[...], preferred_element_type=F32)
    k_ref[...] = kv[:, :D_MODEL].astype(BF16)
    v_ref[...] = kv[:, D_MODEL:].astype(BF16)


def _kv(mem2d, wkv, tm=512):
    M, D = mem2d.shape
    return pl.pallas_call(
        _kv_kernel,
        grid=(M // tm,),
        in_specs=[pl.BlockSpec((tm, D), lambda i: (i, 0)),
                  pl.BlockSpec((D, 2 * D), lambda i: (0, 0))],
        out_specs=[pl.BlockSpec((tm, D), lambda i: (i, 0))] * 2,
        out_shape=[jax.ShapeDtypeStruct((M, D), BF16)] * 2,
        compiler_params=_params(("parallel",)),
        name="mem_kv",
    )(mem2d, wkv)


PEER_TB = 64


def _chunk_major_spec(tm):
    return pl.BlockSpec((tm // PEER_TB, ROW_CHUNKS * PEER_TB, LANES), lambda i: (i, 0, 0))


def _chunk_major_shape(T):
    return jax.ShapeDtypeStruct((T // PEER_TB, ROW_CHUNKS * PEER_TB, LANES), F32)


def _to_chunk_major(y, ref):
    for g in range(y.shape[0] // PEER_TB):
        for c in range(ROW_CHUNKS):
            ref[g, c * PEER_TB:(c + 1) * PEER_TB, :] = (
                y[g * PEER_TB:(g + 1) * PEER_TB, c * LANES:(c + 1) * LANES])


def _xattn_kernel(x_ref, wq_ref, k_ref, v_ref, wo_ref, lg_ref, lb_ref, o_ref, oc_ref):
    x = x_ref[...]
    q = jnp.dot(x.astype(BF16), wq_ref[...], preferred_element_type=F32).astype(BF16)
    scale = XA_HEAD_DIM ** -0.5
    outs = []
    for h in range(XA_HEADS):
        sl = slice(h * XA_HEAD_DIM, (h + 1) * XA_HEAD_DIM)
        s = _nt_dot(q[:, sl], k_ref[0, :, sl]) * scale
        p = jnp.exp(s - jnp.max(s, axis=-1, keepdims=True))
        p = p / jnp.sum(p, axis=-1, keepdims=True)
        outs.append(jnp.dot(p.astype(BF16), v_ref[0, :, sl], preferred_element_type=F32))
    o = jnp.concatenate(outs, axis=-1).astype(BF16)
    xa = jnp.dot(o, wo_ref[...], preferred_element_type=F32)
    y = _layer_norm(ALPHA * x + xa, lg_ref[...], lb_ref[...])
    o_ref[...] = y
    _to_chunk_major(y, oc_ref)


def _xattn(x2d, wq, k, v, wo, lg, lb, seq_len, tm=512):
    T, D = x2d.shape
    M = k.shape[1]
    per_seq = seq_len // tm

    def full(a):
        return pl.BlockSpec(a.shape, lambda i: (0,) * a.ndim)

    return pl.pallas_call(
        _xattn_kernel,
        grid=(T // tm,),
        in_specs=[pl.BlockSpec((tm, D), lambda i: (i, 0)), full(wq),
                  pl.BlockSpec((1, M, D), lambda i: (i // per_seq, 0, 0)),
                  pl.BlockSpec((1, M, D), lambda i: (i // per_seq, 0, 0)),
                  full(wo), full(lg), full(lb)],
        out_specs=[pl.BlockSpec((tm, D), lambda i: (i, 0)), _chunk_major_spec(tm)],
        out_shape=[jax.ShapeDtypeStruct((T, D), F32), _chunk_major_shape(T)],
        compiler_params=_params(("parallel",)),
        name="xattn",
    )(x2d, wq, k, v, wo, lg, lb)


E_PER_TOK = PEER_HEADS * PEER_TOPK


def _staircase():
    return [(a, b) for a in range(PEER_TOPK) for b in range(PEER_TOPK // (a + 1))]


def _route_kernel(x_ref, wq_ref, k1_ref, k2_ref, e_ref, g_ref, v_scr, i_scr, *, ts):
    xb = x_ref[...].astype(BF16)
    key_id = lax.broadcasted_iota(I32, (N_KEYS, ts), 0).astype(F32)
    qd = 2 * PEER_HALF
    for h in range(PEER_HEADS):
        qry = jnp.dot(xb, wq_ref[:, h * qd:(h + 1) * qd], preferred_element_type=F32)
        for half, kref in ((0, k1_ref), (1, k2_ref)):
            qh = qry[:, half * PEER_HALF:(half + 1) * PEER_HALF].astype(BF16)
            s0 = _nt_dot(kref[...], qh)

            s = s0
            for it in range(PEER_TOPK):
                m = jnp.max(s, axis=0, keepdims=True)
                idx = jnp.min(jnp.where(s == m, key_id, float(N_KEYS)), axis=0, keepdims=True)
                v_scr[half, it, h:h + 1, :] = m
                i_scr[half, it, h:h + 1, :] = idx.astype(I32)
                s = jnp.where(key_id == idx, -jnp.inf, s)

    cells = _staircase()
    cand = tuple(v_scr[0, a] + v_scr[1, b] for a, b in cells)
    expert = [i_scr[0, a] * N_KEYS + i_scr[1, b] for a, b in cells]

    picked = []
    for it in range(PEER_TOPK):
        best_v, best_e, best_c = cand[0], expert[0], jnp.zeros((PEER_HEADS, ts), I32)
        for c in range(1, len(cells)):
            better = cand[c] > best_v
            best_v = jnp.where(better, cand[c], best_v)
            best_e = jnp.where(better, expert[c], best_e)
            best_c = jnp.where(better, c, best_c)
        i_scr[0, it] = best_e * WORD_ROWS
        picked.append(best_v)
        cand = tuple(jnp.where(best_c == c, -jnp.inf, cand[c]) for c in range(len(cells)))
    ex = [jnp.exp(v - picked[0]) for v in picked]
    den = ex[0]
    for it in range(1, PEER_TOPK):
        den = den + ex[it]
    inv = 1.0 / den
    g_ref[...] = jnp.concatenate([e * inv for e in ex], axis=0).T
    rows = pltpu.bitcast(i_scr[0].reshape(E_PER_TOK, ts), F32)
    e_ref[...] = pltpu.bitcast(rows.T, I32)


def _route(x2d, wq, k1, k2, ts=256):
    T, D = x2d.shape
    nb = T // ts

    def full(a):
        return pl.BlockSpec(a.shape, lambda i: (0,) * a.ndim)

    blk = pl.BlockSpec((ts, E_PER_TOK), lambda i: (i, 0))
    return pl.pallas_call(
        functools.partial(_route_kernel, ts=ts),
        grid=(nb,),
        in_specs=[pl.BlockSpec((ts, D), lambda i: (i, 0)), full(wq), full(k1), full(k2)],
        out_specs=[blk, blk],
        out_shape=[jax.ShapeDtypeStruct((T, E_PER_TOK), I32),
                   jax.ShapeDtypeStruct((T, E_PER_TOK), F32)],
        scratch_shapes=[pltpu.VMEM((2, PEER_TOPK, PEER_HEADS, ts), F32),
                        pltpu.VMEM((2, PEER_TOPK, PEER_HEADS, ts), I32)],
        compiler_params=_params(("parallel",)),
        name="peer_route",
    )(x2d, wq, k1, k2)


TOK_GROUP = SUBLANES


N_GROUPS = PEER_TB // TOK_GROUP


def _gathered_rows(ids_ref, tab_ref, t):
    pieces = []
    for j in range(E_PER_TOK):
        r0 = pl.multiple_of(ids_ref[t, j], WORD_ROWS)
        pieces.append(tab_ref[pl.ds(r0, WORD_ROWS), :])
    return jnp.concatenate(pieces, axis=0)


def _token_chunks(tt):
    return pl.ds(tt, ROW_CHUNKS, stride=PEER_TB)


def _chunk_of_col():
    col = lax.broadcasted_iota(I32, (ROW_CHUNKS, E_PER_TOK * ROW_CHUNKS), 1)
    row = lax.broadcasted_iota(I32, (ROW_CHUNKS, E_PER_TOK * ROW_CHUNKS), 0)
    return (col & (ROW_CHUNKS - 1)) == row


def _with_ids(ids_hbm, bufs, sems, body):
    i = pl.program_id(0)

    def block_copy(step, slot):
        rows = pl.ds(pl.multiple_of(step * PEER_TB, PEER_TB), PEER_TB)
        return pltpu.make_async_copy(ids_hbm.at[rows], bufs[slot], sems.at[slot])

    @pl.when(i == 0)
    def _():
        block_copy(0, 0).start()

    for slot in range(2):
        @pl.when(i % 2 == slot)
        def _(slot=slot):
            block_copy(i, slot).wait()

            @pl.when(i + 1 < pl.num_programs(0))
            def _():
                block_copy(i + 1, 1 - slot).start()

            body(bufs[slot])


def _peer_score_kernel(ids_hbm, x_ref, tab_ref, s_ref, ids_a, ids_b, sems):
    _with_ids(ids_hbm, (ids_a, ids_b), sems,
              functools.partial(_peer_score_body, x_ref, tab_ref, s_ref))


def _peer_score_body(x_ref, tab_ref, s_ref, ids_ref):
    own_chunk = _chunk_of_col()
    c_id = lax.broadcasted_iota(I32, (E_PER_TOK * ROW_CHUNKS, E_PER_TOK), 0)
    e_id = lax.broadcasted_iota(I32, (E_PER_TOK * ROW_CHUNKS, E_PER_TOK), 1)
    fold = jnp.where(c_id >> CHUNK_SHIFT == e_id, 1.0, 0.0).astype(BF16)

    for g in range(N_GROUPS):
        tok0 = g * TOK_GROUP
        parts = []
        for tt in range(TOK_GROUP):
            rows = pltpu.bitcast(_gathered_rows(ids_ref, tab_ref, tok0 + tt), BF16)
            xt = x_ref[0, _token_chunks(tok0 + tt), :].astype(BF16)
            full = _nt_dot(xt, rows)
            parts.append(jnp.where(own_chunk, full, 0.0))
        part = jnp.concatenate(parts, axis=0)
        hi = part.astype(BF16)
        lo = (part - hi.astype(F32)).astype(BF16)
        sc = (jnp.dot(hi, fold, preferred_element_type=F32)
              + jnp.dot(lo, fold, preferred_element_type=F32))
        sc = jnp.sum(sc.reshape(TOK_GROUP, ROW_CHUNKS, E_PER_TOK), axis=1)
        s_ref[tok0:tok0 + TOK_GROUP, :] = sc


def _peer_mix_kernel(ids_hbm, s_ref, gate_ref, tab_ref, o_ref, ids_a, ids_b, sems):
    _with_ids(ids_hbm, (ids_a, ids_b), sems,
              functools.partial(_peer_mix_body, s_ref, gate_ref, tab_ref, o_ref))


def _peer_mix_body(s_ref, gate_ref, tab_ref, o_ref, ids_ref):
    own_chunk = _chunk_of_col()
    e_id = lax.broadcasted_iota(I32, (E_PER_TOK, E_PER_TOK * ROW_CHUNKS), 0)
    c_id = lax.broadcasted_iota(I32, (E_PER_TOK, E_PER_TOK * ROW_CHUNKS), 1)
    spread = jnp.where(c_id >> CHUNK_SHIFT == e_id, 1.0, 0.0).astype(BF16)

    for g in range(N_GROUPS):
        tok0 = g * TOK_GROUP
        sl = slice(tok0, tok0 + TOK_GROUP)
        s = s_ref[sl, :]
        act = 0.5 * s * (1.0 + lax.erf(s * (2.0 ** -0.5)))
        coef = (gate_ref[sl, :] * act).astype(BF16)
        coef = jnp.dot(coef, spread, preferred_element_type=F32)
        for tt in range(TOK_GROUP):
            rows = pltpu.bitcast(_gathered_rows(ids_ref, tab_ref, tok0 + tt), BF16)
            ct = jnp.broadcast_to(coef[tt:tt + 1], (ROW_CHUNKS, E_PER_TOK * ROW_CHUNKS))
            ct = jnp.where(own_chunk, ct, 0.0).astype(BF16)
            o_ref[0, _token_chunks(tok0 + tt), :] = jnp.dot(ct, rows, preferred_element_type=F32)


def _peer_experts(ids, gate, xc, tab_u, tab_v):
    T = ids.shape[0]
    tb = PEER_TB
    nb = T // tb
    ids_spec = pl.BlockSpec(memory_space=pl.ANY)
    ids_scratch = [pltpu.SMEM((tb, E_PER_TOK), I32), pltpu.SMEM((tb, E_PER_TOK), I32),
                   pltpu.SemaphoreType.DMA((2,))]
    tab_spec = pl.BlockSpec(tab_u.shape, lambda i: (0, 0), pipeline_mode=pl.Buffered(1))
    tok_spec = pl.BlockSpec((tb, E_PER_TOK), lambda i: (i, 0))
    row_spec = _chunk_major_spec(tb)
    s = pl.pallas_call(
        _peer_score_kernel,
        grid=(nb,),
        in_specs=[ids_spec, row_spec, tab_spec],
        out_specs=tok_spec,
        out_shape=jax.ShapeDtypeStruct((T, E_PER_TOK), F32),
        scratch_shapes=ids_scratch,
        compiler_params=_params(("arbitrary",)),
        name="peer_score",
    )(ids, xc, tab_u)
    return pl.pallas_call(
        _peer_mix_kernel,
        grid=(nb,),
        in_specs=[ids_spec, tok_spec, tok_spec, tab_spec],
        out_specs=row_spec,
        out_shape=_chunk_major_shape(T),
        scratch_shapes=ids_scratch,
        compiler_params=_params(("arbitrary",)),
        name="peer_mix",
    )(ids, s, gate, tab_v)


def _resln_kernel(x_ref, y_ref, lg_ref, lb_ref, o_ref):
    G = x_ref.shape[0]
    z = ALPHA * x_ref[...] + y_ref[...]
    zs = [z[:, c * PEER_TB:(c + 1) * PEER_TB, :] for c in range(ROW_CHUNKS)]
    tot = zs[0]
    for c in range(1, ROW_CHUNKS):
        tot = tot + zs[c]
    mu = jnp.sum(tot, axis=-1, keepdims=True) * (1.0 / D_MODEL)
    zc = [a - mu for a in zs]
    sq = zc[0] * zc[0]
    for c in range(1, ROW_CHUNKS):
        sq = sq + zc[c] * zc[c]
    rstd = lax.rsqrt(jnp.sum(sq, axis=-1, keepdims=True) * (1.0 / D_MODEL) + LN_EPS)
    for c in range(ROW_CHUNKS):
        cols = slice(c * LANES, (c + 1) * LANES)
        y = (zc[c] * rstd).reshape(G * PEER_TB, LANES)
        o_ref[:, cols] = y * lg_ref[:, cols] + lb_ref[:, cols]


def _resln(xc, yc, lg, lb, tm=512):
    T = xc.shape[0] * PEER_TB
    blk = _chunk_major_spec(tm)
    par = pl.BlockSpec((1, D_MODEL), lambda i: (0, 0))
    return pl.pallas_call(
        _resln_kernel,
        grid=(T // tm,),
        in_specs=[blk, blk, par, par],
        out_specs=pl.BlockSpec((tm, D_MODEL), lambda i: (i, 0)),
        out_shape=jax.ShapeDtypeStruct((T, D_MODEL), F32),
        compiler_params=_params(("parallel",)),
        name="res_ln",
    )(xc, yc, lg, lb)


def _pack_kernel(t_ref, o_ref):
    def bf16_bits(v):
        return pltpu.bitcast(v.astype(BF16).astype(F32), jnp.uint32)

    te = t_ref.shape[0]
    for r in range(WORD_ROWS):
        lo = bf16_bits(t_ref[:, (2 * r) * LANES:(2 * r + 1) * LANES])
        hi = bf16_bits(t_ref[:, (2 * r + 1) * LANES:(2 * r + 2) * LANES])
        words = (hi & jnp.uint32(0xFFFF0000)) | (lo >> 16)
        o_ref[pl.ds(r, te, stride=WORD_ROWS), :] = pltpu.bitcast(words, I32)


def _pack_table(t, layer, te=512):
    _, E, D = t.shape
    return pl.pallas_call(
        _pack_kernel,
        grid=(E // te,),
        in_specs=[pl.BlockSpec((None, te, D), lambda i: (layer, i, 0))],
        out_specs=pl.BlockSpec((te * WORD_ROWS, LANES), lambda i: (i, 0)),
        out_shape=jax.ShapeDtypeStruct((E * WORD_ROWS, LANES), I32),
        compiler_params=_params(("parallel",)),
        name="pack_table",
    )(t)


def _block_diag(w):
    H, d, _ = w.shape
    out = jnp.zeros((H * d, H * d), w.dtype)
    for h in range(H):
        out = out.at[h * d:(h + 1) * d, h * d:(h + 1) * d].set(w[h])
    return out


def _cat_w_in_kernel(w_ref, o_ref):
    GW = GROUP_WIDTH
    off_f = 5 * GW
    off_sb = off_f + GROUP_HEADS
    w = w_ref[...].astype(BF16)
    o_ref[:, :off_f] = w[:, :off_f]
    o_ref[:, off_f:C_F] = w[:, off_sb:N_IN]
    o_ref[:, C_F:C_END] = jnp.zeros((w.shape[0], C_END - C_F), BF16)
    o_ref[:, C_F:C_F + GROUP_HEADS] = w[:, off_f:off_sb]


def _cat_w_in(w_in, layer, tr=256):
    _, D, N = w_in.shape
    return pl.pallas_call(
        _cat_w_in_kernel,
        grid=(D // tr,),
        in_specs=[pl.BlockSpec((None, tr, N), lambda i: (layer, i, 0))],
        out_specs=pl.BlockSpec((tr, C_END), lambda i: (i, 0)),
        out_shape=jax.ShapeDtypeStruct((D, C_END), BF16),
        compiler_params=_params(("parallel",)),
        name="regroup_w_in",
    )(w_in)


def _row(v, width=None):
    v = v.reshape(1, -1).astype(F32)
    if width is not None and v.shape[1] < width:
        v = jnp.pad(v, ((0, 0), (0, width - v.shape[1])))
    return v


def _layer(x2d, kmem, vmem_, B, S, p, tables, layer, tq=256):
    T = B * S
    rg, fox, sb, sc, f = _inproj(x2d, _cat_w_in(tables[2], layer), B, S)
    yrg, ysc, cumt = _rgsc(
        rg.reshape(B, S, -1), sc.reshape(B, S, -1), f.reshape(B, S, -1),
        p["rg_conv_w"], _row(p["rg_conv_b"]), _block_diag(p["rg_wa"]).astype(BF16), _row(p["rg_ba"]),
        _block_diag(p["rg_wi"]).astype(BF16), _row(p["rg_bi"]), _row(p["rg_lambda"]),
        p["sc_conv_w"], _row(p["fox_bf"], LANES), tq)
    yfox = _fox(fox, cumt, tq=tq)
    ysb = _sb(sb, tq=tq)
    ys = (yrg.reshape(T, -1), yfox.reshape(T, -1), ysb.reshape(T, -1), ysc.reshape(T, -1))
    x1 = _mixout(ys, _row(p["mix_norm_g"]), p["w_out"].astype(BF16), x2d,
                 _row(p["ln1_g"]), _row(p["ln1_b"]))
    x2, x2c = _xattn(x1, p["xa_wq"].astype(BF16), kmem, vmem_, p["xa_wo"].astype(BF16),
                     _row(p["ln2_g"]), _row(p["ln2_b"]), S)
    ids, gate = _route(x2, p["peer_wq"].astype(BF16), p["peer_k1"].astype(BF16),
                       p["peer_k2"].astype(BF16))
    ffc = _peer_experts(ids, gate, x2c, _pack_table(tables[0], layer), _pack_table(tables[1], layer))
    return _resln(x2c, ffc, _row(p["ln3_g"]), _row(p["ln3_b"]))


_LAYER_PARAMS = ("w_in", "w_out", "rg_conv_w", "rg_conv_b", "rg_wa", "rg_ba", "rg_wi", "rg_bi",
                 "rg_lambda", "fox_bf", "sc_conv_w", "mix_norm_g", "ln1_g", "ln1_b", "xa_wq",
                 "xa_wkv", "xa_wo", "ln2_g", "ln2_b", "peer_wq", "peer_k1", "peer_k2", "peer_u",
                 "peer_v", "ln3_g", "ln3_b")


def kernel(x, mem, w_in, w_out, rg_conv_w, rg_conv_b, rg_wa, rg_ba, rg_wi, rg_bi, rg_lambda, fox_bf, sc_conv_w, mix_norm_g, ln1_g, ln1_b, xa_wq, xa_wkv, xa_wo, ln2_g, ln2_b, peer_wq, peer_k1, peer_k2, peer_u, peer_v, ln3_g, ln3_b):
    stacked = dict(zip(_LAYER_PARAMS, (
        w_in, w_out, rg_conv_w, rg_conv_b, rg_wa, rg_ba, rg_wi, rg_bi, rg_lambda, fox_bf,
        sc_conv_w, mix_norm_g, ln1_g, ln1_b, xa_wq, xa_wkv, xa_wo, ln2_g, ln2_b, peer_wq,
        peer_k1, peer_k2, peer_u, peer_v, ln3_g, ln3_b)))
    B, S, D = x.shape
    M = mem.shape[1]
    x2d = x.reshape(B * S, D)
    mem2d = mem.reshape(B * M, D)
    for l in range(w_in.shape[0]):
        p = {k: v[l] for k, v in stacked.items() if k not in ("peer_u", "peer_v", "w_in")}
        kmem, vmem_ = _kv(mem2d, p["xa_wkv"].astype(BF16))
        x2d = _layer(x2d, kmem.reshape(B, M, D), vmem_.reshape(B, M, D), B, S, p,
                     (peer_u, peer_v, w_in), l)
    return x2d.reshape(B, S, D)
```

```python
import functools
import math

import jax
import jax.numpy as jnp
from jax import lax
from jax.experimental import pallas as pl
from jax.experimental.pallas import tpu as pltpu

F32 = jnp.float32
BF16 = jnp.bfloat16
I32 = jnp.int32

D_MODEL = 1024
GROUP_WIDTH = 256
GROUP_HEADS = 4
HEAD_DIM = 64
N_IN = 2820
RGLRU_C = 8.0
XA_HEADS = 4
XA_HEAD_DIM = D_MODEL // XA_HEADS
PEER_HEADS = 8
N_KEYS = 128
PEER_HALF = 128
PEER_TOPK = 16
DEPTH = 2
ALPHA = (2.0 * DEPTH) ** 0.25
LN_EPS = 1e-5

SUBLANES = 8
LANES = 128
WORD_ROWS = D_MODEL // (2 * LANES)
ROW_CHUNKS = D_MODEL // LANES
CHUNK_SHIFT = ROW_CHUNKS.bit_length() - 1

V7X_VMEM_BYTES = 64 * 1024 * 1024
VMEM_LIMIT = V7X_VMEM_BYTES * 3 // 4
GROUP_RMS_EPS = 1e-6
GELU_TANH_CUBIC = 0.044715


def _params(sem, vmem=VMEM_LIMIT):
    return pltpu.CompilerParams(dimension_semantics=sem, vmem_limit_bytes=vmem)


def _layer_norm(z, g, b):
    mu = jnp.mean(z, axis=-1, keepdims=True)
    zc = z - mu
    var = jnp.mean(zc * zc, axis=-1, keepdims=True)
    return zc * lax.rsqrt(var + LN_EPS) * g + b


def _log_sigmoid(z):
    return jnp.minimum(z, 0.0) - jnp.log1p(jnp.exp(-jnp.abs(z)))


def _nt_dot(a, b):
    return lax.dot_general(a, b, (((1,), (1,)), ((), ())), preferred_element_type=F32)


C_RG, C_FOX, C_SB, C_SC, C_F, C_END = 0, 512, 1280, 2048, 2816, 2944


def _inproj_kernel(x_ref, w_ref, rg_ref, fox_ref, sb_ref, sc_ref, f_ref):
    xb = x_ref[...].astype(BF16)

    def mm(lo, hi):
        return jnp.dot(xb, w_ref[:, lo:hi], preferred_element_type=F32)

    def heads(ref, lo, hi):
        qkv = mm(lo, hi)
        for j in range(3 * GROUP_HEADS):
            ref[0, j] = qkv[:, j * HEAD_DIM:(j + 1) * HEAD_DIM].astype(BF16)

    rg_ref[...] = mm(C_RG, C_FOX)
    heads(fox_ref, C_FOX, C_SB)
    heads(sb_ref, C_SB, C_SC)
    sc_ref[...] = mm(C_SC, C_F)
    f_ref[...] = mm(C_F, C_END)


def _inproj(x2d, w_cat, B, S, tm=512):
    T, D = x2d.shape
    per_seq = S // tm
    flat = ((C_FOX - C_RG, F32), (C_F - C_SC, F32), (C_END - C_F, F32))
    flat_specs = [pl.BlockSpec((tm, w), lambda i: (i, 0)) for w, _ in flat]
    flat_shapes = [jax.ShapeDtypeStruct((T, w), dt) for w, dt in flat]
    head_spec = pl.BlockSpec((1, 3 * GROUP_HEADS, tm, HEAD_DIM),
                             lambda i: (i // per_seq, 0, i % per_seq, 0))
    head_shape = jax.ShapeDtypeStruct((B, 3 * GROUP_HEADS, S, HEAD_DIM), BF16)
    return pl.pallas_call(
        _inproj_kernel,
        grid=(T // tm,),
        in_specs=[pl.BlockSpec((tm, D), lambda i: (i, 0)),
                  pl.BlockSpec((D, C_END), lambda i: (0, 0))],
        out_specs=[flat_specs[0], head_spec, head_spec, flat_specs[1], flat_specs[2]],
        out_shape=[flat_shapes[0], head_shape, head_shape, flat_shapes[1], flat_shapes[2]],
        compiler_params=_params(("parallel",)),
        name="inproj",
    )(x2d, w_cat)


def _rgsc_kernel(rg_ref, sc_ref, f_ref, cw_ref, cb_ref, wa_ref, ba_ref, wi_ref, bi_ref, lam_ref,
                 scw_ref, fb_ref, yrg_ref, ysc_ref, cumt_ref, xprev, chprev, hprev, cprev,
                 *, ts, tk):
    GW = GROUP_WIDTH

    @pl.when(pl.program_id(1) == 0)
    def _():
        xprev[...] = jnp.zeros_like(xprev)
        chprev[...] = jnp.zeros_like(chprev)
        hprev[...] = jnp.zeros_like(hprev)
        cprev[...] = jnp.zeros_like(cprev)

    row = lax.broadcasted_iota(I32, (ts, GW), 0)

    def delayed(prev, cur, d):
        ext = jnp.concatenate([prev, cur], axis=0)
        return pltpu.roll(ext, d, 0)[SUBLANES:]

    xr = rg_ref[0, :, :GW]
    gate = rg_ref[0, :, GW:]
    xp = xprev[...]
    cw = cw_ref[...]
    xc = (delayed(xp, xr, 3) * cw[0:1] + delayed(xp, xr, 2) * cw[1:2]
          + delayed(xp, xr, 1) * cw[2:3] + xr * cw[3:4] + cb_ref[...])
    xprev[...] = xr[ts - SUBLANES:]
    xcb = xc.astype(BF16)
    r = jax.nn.sigmoid(jnp.dot(xcb, wa_ref[...], preferred_element_type=F32) + ba_ref[...])
    ig = jax.nn.sigmoid(jnp.dot(xcb, wi_ref[...], preferred_element_type=F32) + bi_ref[...])
    z = -lam_ref[...]
    softplus = jnp.maximum(z, 0.0) + jnp.log1p(jnp.exp(-jnp.abs(z)))
    log_a = -RGLRU_C * r * softplus
    a = jnp.exp(log_a)
    u = jnp.sqrt(-jnp.tanh(log_a) * (a * a + 1.0)) * (ig * xc)
    acc_a, acc_b = a, u
    d = 1
    while d < ts:
        keep = row >= d
        a_s = jnp.where(keep, pltpu.roll(acc_a, d, 0), 1.0)
        b_s = jnp.where(keep, pltpu.roll(acc_b, d, 0), 0.0)
        acc_b = acc_a * b_s + acc_b
        acc_a = acc_a * a_s
        d *= 2
    h = acc_b + acc_a * hprev[...]
    hprev[...] = h[ts - 1:]
    c0 = math.sqrt(2.0 / math.pi)
    gelu = 0.5 * gate * (1.0 + jnp.tanh(c0 * (gate + GELU_TANH_CUBIC * gate * gate * gate)))
    yrg_ref[0] = h * gelu

    bg = sc_ref[0, :, :GW]
    ch = sc_ref[0, :, GW:2 * GW] * sc_ref[0, :, 2 * GW:]
    cp = chprev[...]
    sw = scw_ref[...]
    ysc_ref[0] = bg * (delayed(cp, ch, 2) * sw[0:1] + delayed(cp, ch, 1) * sw[1:2] + ch * sw[2:3])
    chprev[...] = ch[ts - SUBLANES:]

    rowf = lax.broadcasted_iota(I32, (ts, LANES), 0)
    c = _log_sigmoid(f_ref[0] + fb_ref[...])
    d = 1
    while d < ts:
        c = c + jnp.where(rowf >= d, pltpu.roll(c, d, 0), 0.0)
        d *= 2
    c = c + cprev[...]
    cprev[...] = c[ts - 1:]
    ct = c.T[:SUBLANES]
    for j in range(ts // tk):
        cumt_ref[0, j] = ct[:, j * tk:(j + 1) * tk]


def _rgsc(rg, sc, f, cw, cb, wa, ba, wi, bi, lam, scw, fb, tk, ts=512):
    B, S, _ = rg.shape
    GW = GROUP_WIDTH
    per = ts // tk

    def full(a):
        return pl.BlockSpec(a.shape, lambda b, t: (0,) * a.ndim)

    def seq(w):
        return pl.BlockSpec((1, ts, w), lambda b, t: (b, t, 0))

    params = (cw, cb, wa, ba, wi, bi, lam, scw, fb)
    return pl.pallas_call(
        functools.partial(_rgsc_kernel, ts=ts, tk=tk),
        grid=(B, S // ts),
        in_specs=[seq(2 * GW), seq(3 * GW), seq(LANES)] + [full(p) for p in params],
        out_specs=[seq(GW), seq(GW),
                   pl.BlockSpec((1, per, SUBLANES, tk), lambda b, t: (b, t, 0, 0))],
        out_shape=[jax.ShapeDtypeStruct((B, S, GW), F32), jax.ShapeDtypeStruct((B, S, GW), F32),
                   jax.ShapeDtypeStruct((B, S // tk, SUBLANES, tk), F32)],
        scratch_shapes=[pltpu.VMEM((SUBLANES, GW), F32), pltpu.VMEM((SUBLANES, GW), F32),
                        pltpu.VMEM((1, GW), F32), pltpu.VMEM((1, LANES), F32)],
        compiler_params=_params(("parallel", "arbitrary")),
        name="rgsc",
    )(rg, sc, f, *params)


HEADS_TOGETHER = GROUP_HEADS


assert math.log2(HEAD_DIM) % 2 == 0


def _fox_kernel(q_ref, k_ref, v_ref, cr_ref, o_ref, *, tq):
    qi = pl.program_id(1)
    scale = HEAD_DIM ** -0.5
    rowi = lax.broadcasted_iota(I32, (tq, tq), 0)
    coli = lax.broadcasted_iota(I32, (tq, tq), 1)
    causal = coli <= rowi
    qs = [q_ref[0, h] * scale for h in range(GROUP_HEADS)]

    def head_step(h, ki, carry, diag):
        m, l, acc = carry
        off = pl.multiple_of(ki * tq, tq)
        k = k_ref[0, h, pl.ds(off, tq), :]
        v = v_ref[0, h, pl.ds(off, tq), :]
        s = _nt_dot(qs[h], k) - cr_ref[0, ki, h:h + 1, :]
        if diag:
            s = jnp.where(causal, s, -jnp.inf)
        m_new = jnp.maximum(m, jnp.max(s, axis=-1, keepdims=True))
        alpha = jnp.exp(m - m_new)
        p = jnp.exp(s - m_new)
        l = alpha * l + jnp.sum(p, axis=-1, keepdims=True)
        acc = alpha * acc + jnp.dot(p.astype(BF16), v, preferred_element_type=F32)
        return m_new, l, acc

    def step(ki, carries, diag, heads):
        return tuple(head_step(h, ki, c, diag) for h, c in zip(heads, carries))

    init = (jnp.full((tq, 1), -jnp.inf, F32), jnp.zeros((tq, 1), F32),
            jnp.zeros((tq, HEAD_DIM), F32))
    outs = []
    for h0 in range(0, GROUP_HEADS, HEADS_TOGETHER):
        heads = range(h0, h0 + HEADS_TOGETHER)
        carries = lax.fori_loop(0, qi, functools.partial(step, diag=False, heads=heads),
                                (init,) * HEADS_TOGETHER)
        carries = step(qi, carries, True, heads)
        outs += [acc / l for _, l, acc in carries]
    o_ref[0] = jnp.concatenate(outs, axis=-1)


def _qkv_specs(S, tq):
    H, d = GROUP_HEADS, HEAD_DIM
    return [pl.BlockSpec((1, H, tq, d), lambda b, i: (b, 0, i, 0)),
            pl.BlockSpec((1, H, S, d), lambda b, i: (b, 1, 0, 0)),
            pl.BlockSpec((1, H, S, d), lambda b, i: (b, 2, 0, 0))]


def _fox(qkv, cumt, tq=256):
    B, _, S, d = qkv.shape
    H = GROUP_HEADS
    return pl.pallas_call(
        functools.partial(_fox_kernel, tq=tq),
        grid=(B, S // tq),
        in_specs=_qkv_specs(S, tq)
        + [pl.BlockSpec((1, S // tq, SUBLANES, tq), lambda b, i: (b, 0, 0, 0))],
        out_specs=pl.BlockSpec((1, tq, H * d), lambda b, i: (b, i, 0)),
        out_shape=jax.ShapeDtypeStruct((B, S, H * d), F32),
        compiler_params=_params(("parallel", "arbitrary")),
        name="fox_attn",
    )(qkv, qkv, qkv, cumt)


EXP_UNDERFLOW = -104.0


def _sb_kernel(q_ref, k_ref, v_ref, o_ref, *, tq):
    qi = pl.program_id(1)
    scale = HEAD_DIM ** -0.5
    rowi = lax.broadcasted_iota(I32, (tq, tq), 0)
    coli = lax.broadcasted_iota(I32, (tq, tq), 1)
    strict = coli < rowi
    later = jnp.where(rowi > coli, 1.0, 0.0).astype(BF16)
    qs = [q_ref[0, h] * scale for h in range(GROUP_HEADS)]
    def head_step(h, ki, carry, diag):
        rest, acc = carry
        off = pl.multiple_of(ki * tq, tq)
        k = k_ref[0, h, pl.ds(off, tq), :]
        v = v_ref[0, h, pl.ds(off, tq), :]
        z = _nt_dot(qs[h], k)
        ls = _log_sigmoid(z)
        l1m = ls - z
        if diag:
            l1m = jnp.where(strict, l1m, 0.0)
        hi = l1m.astype(BF16)
        lo = (l1m - hi.astype(F32)).astype(BF16)
        tail = (jnp.dot(hi, later, preferred_element_type=F32)
                + jnp.dot(lo, later, preferred_element_type=F32) + rest)
        w = jnp.exp(ls + tail)
        if diag:
            w = jnp.where(strict, w, 0.0)
        acc = acc + jnp.dot(w.astype(BF16), v, preferred_element_type=F32)
        rest = rest + jnp.sum(l1m, axis=-1, keepdims=True)
        return rest, acc

    def step(ki, carries, diag):
        return tuple(head_step(h, ki, carries[h], diag) for h in range(GROUP_HEADS))

    init = (jnp.zeros((tq, 1), F32), jnp.zeros((tq, HEAD_DIM), F32))
    carries = step(qi, (init,) * GROUP_HEADS, True)

    def more(c):
        j, carries = c
        top = carries[0][0]
        for rest, _ in carries[1:]:
            top = jnp.maximum(top, rest)
        return jnp.logical_and(j < qi, jnp.max(top) > EXP_UNDERFLOW)

    def further(c):
        j, carries = c
        return j + 1, step(qi - 1 - j, carries, False)

    _, carries = lax.while_loop(more, further, (jnp.int32(0), carries))
    o_ref[0] = jnp.concatenate([acc for _, acc in carries], axis=-1)


def _sb(qkv, tq=256):
    B, _, S, d = qkv.shape
    H = GROUP_HEADS
    return pl.pallas_call(
        functools.partial(_sb_kernel, tq=tq),
        grid=(B, S // tq),
        in_specs=_qkv_specs(S, tq),
        out_specs=pl.BlockSpec((1, tq, H * d), lambda b, i: (b, i, 0)),
        out_shape=jax.ShapeDtypeStruct((B, S, H * d), F32),
        compiler_params=_params(("parallel", "arbitrary")),
        name="sb_attn",
    )(qkv, qkv, qkv)


def _mixout_kernel(y0_ref, y1_ref, y2_ref, y3_ref, g_ref, w_ref, x_ref, lg_ref, lb_ref, o_ref):
    def rms(y):
        return y * lax.rsqrt(jnp.mean(y * y, axis=-1, keepdims=True) + GROUP_RMS_EPS)

    y = jnp.concatenate([rms(r[...]) for r in (y0_ref, y1_ref, y2_ref, y3_ref)], axis=-1)
    y = (y * g_ref[...]).astype(BF16)
    mix = jnp.dot(y, w_ref[...], preferred_element_type=F32)
    o_ref[...] = _layer_norm(ALPHA * x_ref[...] + mix, lg_ref[...], lb_ref[...])


def _mixout(ys, g, w, x2d, lg, lb, tm=512):
    T, D = x2d.shape
    GW = GROUP_WIDTH

    def full(a):
        return pl.BlockSpec(a.shape, lambda i: (0,) * a.ndim)

    return pl.pallas_call(
        _mixout_kernel,
        grid=(T // tm,),
        in_specs=[pl.BlockSpec((tm, GW), lambda i: (i, 0))] * 4
        + [full(g), full(w), pl.BlockSpec((tm, D), lambda i: (i, 0)), full(lg), full(lb)],
        out_specs=pl.BlockSpec((tm, D), lambda i: (i, 0)),
        out_shape=jax.ShapeDtypeStruct((T, D), F32),
        compiler_params=_params(("parallel",)),
        name="mixout",
    )(*ys, g, w, x2d, lg, lb)


def _kv_kernel(m_ref, w_ref, k_ref, v_ref):
    kv = jnp.dot(m_ref[...].astype(BF16), w_ref[...], preferred_element_type=F32)
    k_ref[...] = kv[:, :D_MODEL].astype(BF16)
    v_ref[...] = kv[:, D_MODEL:].astype(BF16)


def _kv(mem2d, wkv, tm=512):
    M, D = mem2d.shape
    return pl.pallas_call(
        _kv_kernel,
        grid=(M // tm,),
        in_specs=[pl.BlockSpec((tm, D), lambda i: (i, 0)),
                  pl.BlockSpec((D, 2 * D), lambda i: (0, 0))],
        out_specs=[pl.BlockSpec((tm, D), lambda i: (i, 0))] * 2,
        out_shape=[jax.ShapeDtypeStruct((M, D), BF16)] * 2,
        compiler_params=_params(("parallel",)),
        name="mem_kv",
    )(mem2d, wkv)


PEER_TB = 128


def _chunk_major_spec(tm):
    return pl.BlockSpec((tm // PEER_TB, ROW_CHUNKS * PEER_TB, LANES), lambda i: (i, 0, 0))


def _chunk_major_shape(T):
    return jax.ShapeDtypeStruct((T // PEER_TB, ROW_CHUNKS * PEER_TB, LANES), F32)


def _to_chunk_major(y, ref):
    for g in range(y.shape[0] // PEER_TB):
        for c in range(ROW_CHUNKS):
            ref[g, c * PEER_TB:(c + 1) * PEER_TB, :] = (
                y[g * PEER_TB:(g + 1) * PEER_TB, c * LANES:(c + 1) * LANES])


def _xattn_kernel(x_ref, wq_ref, k_ref, v_ref, wo_ref, lg_ref, lb_ref, o_ref, oc_ref):
    x = x_ref[...]
    q = jnp.dot(x.astype(BF16), wq_ref[...], preferred_element_type=F32).astype(BF16)
    scale = XA_HEAD_DIM ** -0.5
    outs = []
    for h in range(XA_HEADS):
        sl = slice(h * XA_HEAD_DIM, (h + 1) * XA_HEAD_DIM)
        s = _nt_dot(q[:, sl], k_ref[0, :, sl]) * scale
        p = jnp.exp(s - jnp.max(s, axis=-1, keepdims=True))
        p = p / jnp.sum(p, axis=-1, keepdims=True)
        outs.append(jnp.dot(p.astype(BF16), v_ref[0, :, sl], preferred_element_type=F32))
    o = jnp.concatenate(outs, axis=-1).astype(BF16)
    xa = jnp.dot(o, wo_ref[...], preferred_element_type=F32)
    y = _layer_norm(ALPHA * x + xa, lg_ref[...], lb_ref[...])
    o_ref[...] = y
    _to_chunk_major(y, oc_ref)


def _xattn(x2d, wq, k, v, wo, lg, lb, seq_len, tm=512):
    T, D = x2d.shape
    M = k.shape[1]
    per_seq = seq_len // tm

    def full(a):
        return pl.BlockSpec(a.shape, lambda i: (0,) * a.ndim)

    return pl.pallas_call(
        _xattn_kernel,
        grid=(T // tm,),
        in_specs=[pl.BlockSpec((tm, D), lambda i: (i, 0)), full(wq),
                  pl.BlockSpec((1, M, D), lambda i: (i // per_seq, 0, 0)),
                  pl.BlockSpec((1, M, D), lambda i: (i // per_seq, 0, 0)),
                  full(wo), full(lg), full(lb)],
        out_specs=[pl.BlockSpec((tm, D), lambda i: (i, 0)), _chunk_major_spec(tm)],
        out_shape=[jax.ShapeDtypeStruct((T, D), F32), _chunk_major_shape(T)],
        compiler_params=_params(("parallel",)),
        name="xattn",
    )(x2d, wq, k, v, wo, lg, lb)


E_PER_TOK = PEER_HEADS * PEER_TOPK


def _staircase():
    return [(a, b) for a in range(PEER_TOPK) for b in range(PEER_TOPK // (a + 1))]


def _route_kernel(x_ref, wq_ref, k1_ref, k2_ref, e_ref, g_ref, v_scr, i_scr, *, ts):
    xb = x_ref[...].astype(BF16)
    key_id = lax.broadcasted_iota(I32, (N_KEYS, ts), 0).astype(F32)
    qd = 2 * PEER_HALF
    for h in range(PEER_HEADS):
        qry = jnp.dot(xb, wq_ref[:, h * qd:(h + 1) * qd], preferred_element_type=F32)
        for half, kref in ((0, k1_ref), (1, k2_ref)):
            qh = qry[:, half * PEER_HALF:(half + 1) * PEER_HALF].astype(BF16)
            s0 = _nt_dot(kref[...], qh)

            s = s0
            for it in range(PEER_TOPK):
                m = jnp.max(s, axis=0, keepdims=True)
                idx = jnp.min(jnp.where(s == m, key_id, float(N_KEYS)), axis=0, keepdims=True)
                v_scr[half, it, h:h + 1, :] = m
                i_scr[half, it, h:h + 1, :] = idx.astype(I32)
                s = jnp.where(key_id == idx, -jnp.inf, s)

    cells = _staircase()
    cand = tuple(v_scr[0, a] + v_scr[1, b] for a, b in cells)
    expert = [i_scr[0, a] * N_KEYS + i_scr[1, b] for a, b in cells]

    picked = []
    for it in range(PEER_TOPK):
        best_v, best_e, best_c = cand[0], expert[0], jnp.zeros((PEER_HEADS, ts), I32)
        for c in range(1, len(cells)):
            better = cand[c] > best_v
            best_v = jnp.where(better, cand[c], best_v)
            best_e = jnp.where(better, expert[c], best_e)
            best_c = jnp.where(better, c, best_c)
        i_scr[0, it] = best_e * WORD_ROWS
        picked.append(best_v)
        cand = tuple(jnp.where(best_c == c, -jnp.inf, cand[c]) for c in range(len(cells)))
    ex = [jnp.exp(v - picked[0]) for v in picked]
    den = ex[0]
    for it in range(1, PEER_TOPK):
        den = den + ex[it]
    inv = 1.0 / den
    g_ref[...] = jnp.concatenate([e * inv for e in ex], axis=0).T
    rows = pltpu.bitcast(i_scr[0].reshape(E_PER_TOK, ts), F32)
    e_ref[...] = pltpu.bitcast(rows.T, I32)


def _route(x2d, wq, k1, k2, ts=256):
    T, D = x2d.shape
    nb = T // ts

    def full(a):
        return pl.BlockSpec(a.shape, lambda i: (0,) * a.ndim)

    blk = pl.BlockSpec((ts, E_PER_TOK), lambda i: (i, 0))
    return pl.pallas_call(
        functools.partial(_route_kernel, ts=ts),
        grid=(nb,),
        in_specs=[pl.BlockSpec((ts, D), lambda i: (i, 0)), full(wq), full(k1), full(k2)],
        out_specs=[blk, blk],
        out_shape=[jax.ShapeDtypeStruct((T, E_PER_TOK), I32),
                   jax.ShapeDtypeStruct((T, E_PER_TOK), F32)],
        scratch_shapes=[pltpu.VMEM((2, PEER_TOPK, PEER_HEADS, ts), F32),
                        pltpu.VMEM((2, PEER_TOPK, PEER_HEADS, ts), I32)],
        compiler_params=_params(("parallel",)),
        name="peer_route",
    )(x2d, wq, k1, k2)


TOK_GROUP = SUBLANES


N_GROUPS = PEER_TB // TOK_GROUP


def _gathered_rows(ids_ref, tab_ref, t):
    pieces = []
    for j in range(E_PER_TOK):
        r0 = pl.multiple_of(ids_ref[t, j], WORD_ROWS)
        pieces.append(tab_ref[pl.ds(r0, WORD_ROWS), :])
    return jnp.concatenate(pieces, axis=0)


def _token_chunks(tt):
    return pl.ds(tt, ROW_CHUNKS, stride=PEER_TB)


def _chunk_of_col():
    col = lax.broadcasted_iota(I32, (ROW_CHUNKS, E_PER_TOK * ROW_CHUNKS), 1)
    row = lax.broadcasted_iota(I32, (ROW_CHUNKS, E_PER_TOK * ROW_CHUNKS), 0)
    return (col & (ROW_CHUNKS - 1)) == row


def _with_ids(ids_hbm, bufs, sems, body):
    i = pl.program_id(0)

    def block_copy(step, slot):
        rows = pl.ds(pl.multiple_of(step * PEER_TB, PEER_TB), PEER_TB)
        return pltpu.make_async_copy(ids_hbm.at[rows], bufs[slot], sems.at[slot])

    @pl.when(i == 0)
    def _():
        block_copy(0, 0).start()

    for slot in range(2):
        @pl.when(i % 2 == slot)
        def _(slot=slot):
            block_copy(i, slot).wait()

            @pl.when(i + 1 < pl.num_programs(0))
            def _():
                block_copy(i + 1, 1 - slot).start()

            body(bufs[slot])


def _peer_score_kernel(ids_hbm, x_ref, tab_ref, s_ref, ids_a, ids_b, sems):
    _with_ids(ids_hbm, (ids_a, ids_b), sems,
              functools.partial(_peer_score_body, x_ref, tab_ref, s_ref))


def _peer_score_body(x_ref, tab_ref, s_ref, ids_ref):
    own_chunk = _chunk_of_col()
    c_id = lax.broadcasted_iota(I32, (E_PER_TOK * ROW_CHUNKS, E_PER_TOK), 0)
    e_id = lax.broadcasted_iota(I32, (E_PER_TOK * ROW_CHUNKS, E_PER_TOK), 1)
    fold = jnp.where(c_id >> CHUNK_SHIFT == e_id, 1.0, 0.0).astype(BF16)

    for g in range(N_GROUPS):
        tok0 = g * TOK_GROUP
        parts = []
        for tt in range(TOK_GROUP):
            rows = pltpu.bitcast(_gathered_rows(ids_ref, tab_ref, tok0 + tt), BF16)
            xt = x_ref[0, _token_chunks(tok0 + tt), :].astype(BF16)
            full = _nt_dot(xt, rows)
            parts.append(jnp.where(own_chunk, full, 0.0))
        part = jnp.concatenate(parts, axis=0)
        hi = part.astype(BF16)
        lo = (part - hi.astype(F32)).astype(BF16)
        sc = (jnp.dot(hi, fold, preferred_element_type=F32)
              + jnp.dot(lo, fold, preferred_element_type=F32))
        sc = jnp.sum(sc.reshape(TOK_GROUP, ROW_CHUNKS, E_PER_TOK), axis=1)
        s_ref[tok0:tok0 + TOK_GROUP, :] = sc


def _peer_mix_kernel(ids_hbm, s_ref, gate_ref, tab_ref, o_ref, ids_a, ids_b, sems):
    _with_ids(ids_hbm, (ids_a, ids_b), sems,
              functools.partial(_peer_mix_body, s_ref, gate_ref, tab_ref, o_ref))


def _peer_mix_body(s_ref, gate_ref, tab_ref, o_ref, ids_ref):
    own_chunk = _chunk_of_col()
    e_id = lax.broadcasted_iota(I32, (E_PER_TOK, E_PER_TOK * ROW_CHUNKS), 0)
    c_id = lax.broadcasted_iota(I32, (E_PER_TOK, E_PER_TOK * ROW_CHUNKS), 1)
    spread = jnp.where(c_id >> CHUNK_SHIFT == e_id, 1.0, 0.0).astype(BF16)

    for g in range(N_GROUPS):
        tok0 = g * TOK_GROUP
        sl = slice(tok0, tok0 + TOK_GROUP)
        s = s_ref[sl, :]
        act = 0.5 * s * (1.0 + lax.erf(s * (2.0 ** -0.5)))
        coef = (gate_ref[sl, :] * act).astype(BF16)
        coef = jnp.dot(coef, spread, preferred_element_type=F32)
        for tt in range(TOK_GROUP):
            rows = pltpu.bitcast(_gathered_rows(ids_ref, tab_ref, tok0 + tt), BF16)
            ct = jnp.broadcast_to(coef[tt:tt + 1], (ROW_CHUNKS, E_PER_TOK * ROW_CHUNKS))
            ct = jnp.where(own_chunk, ct, 0.0).astype(BF16)
            o_ref[0, _token_chunks(tok0 + tt), :] = jnp.dot(ct, rows, preferred_element_type=F32)


def _peer_experts(ids, gate, xc, tab_u, tab_v):
    T = ids.shape[0]
    tb = PEER_TB
    nb = T // tb
    ids_spec = pl.BlockSpec(memory_space=pl.ANY)
    ids_scratch = [pltpu.SMEM((tb, E_PER_TOK), I32), pltpu.SMEM((tb, E_PER_TOK), I32),
                   pltpu.SemaphoreType.DMA((2,))]
    tab_spec = pl.BlockSpec(tab_u.shape, lambda i: (0, 0), pipeline_mode=pl.Buffered(1))
    tok_spec = pl.BlockSpec((tb, E_PER_TOK), lambda i: (i, 0))
    row_spec = _chunk_major_spec(tb)
    s = pl.pallas_call(
        _peer_score_kernel,
        grid=(nb,),
        in_specs=[ids_spec, row_spec, tab_spec],
        out_specs=tok_spec,
        out_shape=jax.ShapeDtypeStruct((T, E_PER_TOK), F32),
        scratch_shapes=ids_scratch,
        compiler_params=_params(("arbitrary",)),
        name="peer_score",
    )(ids, xc, tab_u)
    return pl.pallas_call(
        _peer_mix_kernel,
        grid=(nb,),
        in_specs=[ids_spec, tok_spec, tok_spec, tab_spec],
        out_specs=row_spec,
        out_shape=_chunk_major_shape(T),
        scratch_shapes=ids_scratch,
        compiler_params=_params(("arbitrary",)),
        name="peer_mix",
    )(ids, s, gate, tab_v)


def _resln_kernel(x_ref, y_ref, lg_ref, lb_ref, o_ref):
    G = x_ref.shape[0]
    z = ALPHA * x_ref[...] + y_ref[...]
    zs = [z[:, c * PEER_TB:(c + 1) * PEER_TB, :] for c in range(ROW_CHUNKS)]
    tot = zs[0]
    for c in range(1, ROW_CHUNKS):
        tot = tot + zs[c]
    mu = jnp.sum(tot, axis=-1, keepdims=True) * (1.0 / D_MODEL)
    zc = [a - mu for a in zs]
    sq = zc[0] * zc[0]
    for c in range(1, ROW_CHUNKS):
        sq = sq + zc[c] * zc[c]
    rstd = lax.rsqrt(jnp.sum(sq, axis=-1, keepdims=True) * (1.0 / D_MODEL) + LN_EPS)
    for c in range(ROW_CHUNKS):
        cols = slice(c * LANES, (c + 1) * LANES)
        y = (zc[c] * rstd).reshape(G * PEER_TB, LANES)
        o_ref[:, cols] = y * lg_ref[:, cols] + lb_ref[:, cols]


def _resln(xc, yc, lg, lb, tm=512):
    T = xc.shape[0] * PEER_TB
    blk = _chunk_major_spec(tm)
    par = pl.BlockSpec((1, D_MODEL), lambda i: (0, 0))
    return pl.pallas_call(
        _resln_kernel,
        grid=(T // tm,),
        in_specs=[blk, blk, par, par],
        out_specs=pl.BlockSpec((tm, D_MODEL), lambda i: (i, 0)),
        out_shape=jax.ShapeDtypeStruct((T, D_MODEL), F32),
        compiler_params=_params(("parallel",)),
        name="res_ln",
    )(xc, yc, lg, lb)


def _pack_kernel(t_ref, o_ref):
    def bf16_bits(v):
        return pltpu.bitcast(v.astype(BF16).astype(F32), jnp.uint32)

    te = t_ref.shape[0]
    for r in range(WORD_ROWS):
        lo = bf16_bits(t_ref[:, (2 * r) * LANES:(2 * r + 1) * LANES])
        hi = bf16_bits(t_ref[:, (2 * r + 1) * LANES:(2 * r + 2) * LANES])
        words = (hi & jnp.uint32(0xFFFF0000)) | (lo >> 16)
        o_ref[pl.ds(r, te, stride=WORD_ROWS), :] = pltpu.bitcast(words, I32)


def _pack_table(t, layer, te=512):
    _, E, D = t.shape
    return pl.pallas_call(
        _pack_kernel,
        grid=(E // te,),
        in_specs=[pl.BlockSpec((None, te, D), lambda i: (layer, i, 0))],
        out_specs=pl.BlockSpec((te * WORD_ROWS, LANES), lambda i: (i, 0)),
        out_shape=jax.ShapeDtypeStruct((E * WORD_ROWS, LANES), I32),
        compiler_params=_params(("parallel",)),
        name="pack_table",
    )(t)


def _block_diag(w):
    H, d, _ = w.shape
    out = jnp.zeros((H * d, H * d), w.dtype)
    for h in range(H):
        out = out.at[h * d:(h + 1) * d, h * d:(h + 1) * d].set(w[h])
    return out


def _cat_w_in_kernel(w_ref, o_ref):
    GW = GROUP_WIDTH
    off_f = 5 * GW
    off_sb = off_f + GROUP_HEADS
    w = w_ref[...].astype(BF16)
    o_ref[:, :off_f] = w[:, :off_f]
    o_ref[:, off_f:C_F] = w[:, off_sb:N_IN]
    o_ref[:, C_F:C_END] = jnp.zeros((w.shape[0], C_END - C_F), BF16)
    o_ref[:, C_F:C_F + GROUP_HEADS] = w[:, off_f:off_sb]


def _cat_w_in(w_in, layer, tr=256):
    _, D, N = w_in.shape
    return pl.pallas_call(
        _cat_w_in_kernel,
        grid=(D // tr,),
        in_specs=[pl.BlockSpec((None, tr, N), lambda i: (layer, i, 0))],
        out_specs=pl.BlockSpec((tr, C_END), lambda i: (i, 0)),
        out_shape=jax.ShapeDtypeStruct((D, C_END), BF16),
        compiler_params=_params(("parallel",)),
        name="regroup_w_in",
    )(w_in)


def _row(v, width=None):
    v = v.reshape(1, -1).astype(F32)
    if width is not None and v.shape[1] < width:
        v = jnp.pad(v, ((0, 0), (0, width - v.shape[1])))
    return v


def _layer(x2d, kmem, vmem_, B, S, p, tables, layer, tq=256):
    T = B * S
    rg, fox, sb, sc, f = _inproj(x2d, _cat_w_in(tables[2], layer), B, S)
    yrg, ysc, cumt = _rgsc(
        rg.reshape(B, S, -1), sc.reshape(B, S, -1), f.reshape(B, S, -1),
        p["rg_conv_w"], _row(p["rg_conv_b"]), _block_diag(p["rg_wa"]).astype(BF16), _row(p["rg_ba"]),
        _block_diag(p["rg_wi"]).astype(BF16), _row(p["rg_bi"]), _row(p["rg_lambda"]),
        p["sc_conv_w"], _row(p["fox_bf"], LANES), tq)
    yfox = _fox(fox, cumt, tq=tq)
    ysb = _sb(sb, tq=tq)
    ys = (yrg.reshape(T, -1), yfox.reshape(T, -1), ysb.reshape(T, -1), ysc.reshape(T, -1))
    x1 = _mixout(ys, _row(p["mix_norm_g"]), p["w_out"].astype(BF16), x2d,
                 _row(p["ln1_g"]), _row(p["ln1_b"]))
    x2, x2c = _xattn(x1, p["xa_wq"].astype(BF16), kmem, vmem_, p["xa_wo"].astype(BF16),
                     _row(p["ln2_g"]), _row(p["ln2_b"]), S)
    ids, gate = _route(x2, p["peer_wq"].astype(BF16), p["peer_k1"].astype(BF16),
                       p["peer_k2"].astype(BF16))
    ffc = _peer_experts(ids, gate, x2c, _pack_table(tables[0], layer), _pack_table(tables[1], layer))
    return _resln(x2c, ffc, _row(p["ln3_g"]), _row(p["ln3_b"]))


_LAYER_PARAMS = ("w_in", "w_out", "rg_conv_w", "rg_conv_b", "rg_wa", "rg_ba", "rg_wi", "rg_bi",
                 "rg_lambda", "fox_bf", "sc_conv_w", "mix_norm_g", "ln1_g", "ln1_b", "xa_wq",
                 "xa_wkv", "xa_wo", "ln2_g", "ln2_b", "peer_wq", "peer_k1", "peer_k2", "peer_u",
                 "peer_v", "ln3_g", "ln3_b")


def kernel(x, mem, w_in, w_out, rg_conv_w, rg_conv_b, rg_wa, rg_ba, rg_wi, rg_bi, rg_lambda, fox_bf, sc_conv_w, mix_norm_g, ln1_g, ln1_b, xa_wq, xa_wkv, xa_wo, ln2_g, ln2_b, peer_wq, peer_k1, peer_k2, peer_u, peer_v, ln3_g, ln3_b):
    stacked = dict(zip(_LAYER_PARAMS, (
        w_in, w_out, rg_conv_w, rg_conv_b, rg_wa, rg_ba, rg_wi, rg_bi, rg_lambda, fox_bf,
        sc_conv_w, mix_norm_g, ln1_g, ln1_b, xa_wq, xa_wkv, xa_wo, ln2_g, ln2_b, peer_wq,
        peer_k1, peer_k2, peer_u, peer_v, ln3_g, ln3_b)))
    B, S, D = x.shape
    M = mem.shape[1]
    x2d = x.reshape(B * S, D)
    mem2d = mem.reshape(B * M, D)
    for l in range(w_in.shape[0]):
        p = {k: v[l] for k, v in stacked.items() if k not in ("peer_u", "peer_v", "w_in")}
        kmem, vmem_ = _kv(mem2d, p["xa_wkv"].astype(BF16))
        x2d = _layer(x2d, kmem.reshape(B, M, D), vmem_.reshape(B, M, D), B, S, p,
                     (peer_u, peer_v, w_in), l)
    return x2d.reshape(B, S, D)
```

```python
import functools
import math

import jax
import jax.numpy as jnp
from jax import lax
from jax.experimental import pallas as pl
from jax.experimental.pallas import tpu as pltpu

F32 = jnp.float32
BF16 = jnp.bfloat16
I32 = jnp.int32

D_MODEL = 1024
GROUP_WIDTH = 256
GROUP_HEADS = 4
HEAD_DIM = 64
N_IN = 2820
RGLRU_C = 8.0
XA_HEADS = 4
XA_HEAD_DIM = D_MODEL // XA_HEADS
PEER_HEADS = 8
N_KEYS = 128
PEER_HALF = 128
PEER_TOPK = 16
DEPTH = 2
ALPHA = (2.0 * DEPTH) ** 0.25
LN_EPS = 1e-5

SUBLANES = 8
LANES = 128
WORD_ROWS = D_MODEL // (2 * LANES)
ROW_CHUNKS = D_MODEL // LANES
CHUNK_SHIFT = ROW_CHUNKS.bit_length() - 1

V7X_VMEM_BYTES = 64 * 1024 * 1024
VMEM_LIMIT = V7X_VMEM_BYTES * 3 // 4
GROUP_RMS_EPS = 1e-6
GELU_TANH_CUBIC = 0.044715


def _params(sem, vmem=VMEM_LIMIT):
    return pltpu.CompilerParams(dimension_semantics=sem, vmem_limit_bytes=vmem)


def _layer_norm(z, g, b):
    mu = jnp.mean(z, axis=-1, keepdims=True)
    zc = z - mu
    var = jnp.mean(zc * zc, axis=-1, keepdims=True)
    return zc * lax.rsqrt(var + LN_EPS) * g + b


def _log_sigmoid(z):
    return jnp.minimum(z, 0.0) - jnp.log1p(jnp.exp(-jnp.abs(z)))


def _nt_dot(a, b):
    return lax.dot_general(a, b, (((1,), (1,)), ((), ())), preferred_element_type=F32)


C_RG, C_FOX, C_SB, C_SC, C_F, C_END = 0, 512, 1280, 2048, 2816, 2944


def _inproj_kernel(x_ref, w_ref, rg_ref, fox_ref, sb_ref, sc_ref, f_ref):
    xb = x_ref[...].astype(BF16)

    def mm(lo, hi):
        return jnp.dot(xb, w_ref[:, lo:hi], preferred_element_type=F32)

    def heads(ref, lo, hi):
        qkv = mm(lo, hi)
        for j in range(3 * GROUP_HEADS):
            ref[0, j] = qkv[:, j * HEAD_DIM:(j + 1) * HEAD_DIM].astype(BF16)

    rg_ref[...] = mm(C_RG, C_FOX)
    heads(fox_ref, C_FOX, C_SB)
    heads(sb_ref, C_SB, C_SC)
    sc_ref[...] = mm(C_SC, C_F)
    f_ref[...] = mm(C_F, C_END)


def _inproj(x2d, w_cat, B, S, tm=512):
    T, D = x2d.shape
    per_seq = S // tm
    flat = ((C_FOX - C_RG, F32), (C_F - C_SC, F32), (C_END - C_F, F32))
    flat_specs = [pl.BlockSpec((tm, w), lambda i: (i, 0)) for w, _ in flat]
    flat_shapes = [jax.ShapeDtypeStruct((T, w), dt) for w, dt in flat]
    head_spec = pl.BlockSpec((1, 3 * GROUP_HEADS, tm, HEAD_DIM),
                             lambda i: (i // per_seq, 0, i % per_seq, 0))
    head_shape = jax.ShapeDtypeStruct((B, 3 * GROUP_HEADS, S, HEAD_DIM), BF16)
    return pl.pallas_call(
        _inproj_kernel,
        grid=(T // tm,),
        in_specs=[pl.BlockSpec((tm, D), lambda i: (i, 0)),
                  pl.BlockSpec((D, C_END), lambda i: (0, 0))],
        out_specs=[flat_specs[0], head_spec, head_spec, flat_specs[1], flat_specs[2]],
        out_shape=[flat_shapes[0], head_shape, head_shape, flat_shapes[1], flat_shapes[2]],
        compiler_params=_params(("parallel",)),
        name="inproj",
    )(x2d, w_cat)


def _rgsc_kernel(rg_ref, sc_ref, f_ref, cw_ref, cb_ref, wa_ref, ba_ref, wi_ref, bi_ref, lam_ref,
                 scw_ref, fb_ref, yrg_ref, ysc_ref, cumt_ref, xprev, chprev, hprev, cprev,
                 *, ts, tk):
    GW = GROUP_WIDTH

    @pl.when(pl.program_id(1) == 0)
    def _():
        xprev[...] = jnp.zeros_like(xprev)
        chprev[...] = jnp.zeros_like(chprev)
        hprev[...] = jnp.zeros_like(hprev)
        cprev[...] = jnp.zeros_like(cprev)

    row = lax.broadcasted_iota(I32, (ts, GW), 0)

    def delayed(prev, cur, d):
        ext = jnp.concatenate([prev, cur], axis=0)
        return pltpu.roll(ext, d, 0)[SUBLANES:]

    xr = rg_ref[0, :, :GW]
    gate = rg_ref[0, :, GW:]
    xp = xprev[...]
    cw = cw_ref[...]
    xc = (delayed(xp, xr, 3) * cw[0:1] + delayed(xp, xr, 2) * cw[1:2]
          + delayed(xp, xr, 1) * cw[2:3] + xr * cw[3:4] + cb_ref[...])
    xprev[...] = xr[ts - SUBLANES:]
    xcb = xc.astype(BF16)
    r = jax.nn.sigmoid(jnp.dot(xcb, wa_ref[...], preferred_element_type=F32) + ba_ref[...])
    ig = jax.nn.sigmoid(jnp.dot(xcb, wi_ref[...], preferred_element_type=F32) + bi_ref[...])
    z = -lam_ref[...]
    softplus = jnp.maximum(z, 0.0) + jnp.log1p(jnp.exp(-jnp.abs(z)))
    log_a = -RGLRU_C * r * softplus
    a = jnp.exp(log_a)
    u = jnp.sqrt(-jnp.tanh(log_a) * (a * a + 1.0)) * (ig * xc)
    acc_a, acc_b = a, u
    d = 1
    while d < ts:
        keep = row >= d
        a_s = jnp.where(keep, pltpu.roll(acc_a, d, 0), 1.0)
        b_s = jnp.where(keep, pltpu.roll(acc_b, d, 0), 0.0)
        acc_b = acc_a * b_s + acc_b
        acc_a = acc_a * a_s
        d *= 2
    h = acc_b + acc_a * hprev[...]
    hprev[...] = h[ts - 1:]
    c0 = math.sqrt(2.0 / math.pi)
    gelu = 0.5 * gate * (1.0 + jnp.tanh(c0 * (gate + GELU_TANH_CUBIC * gate * gate * gate)))
    yrg_ref[0] = h * gelu

    bg = sc_ref[0, :, :GW]
    ch = sc_ref[0, :, GW:2 * GW] * sc_ref[0, :, 2 * GW:]
    cp = chprev[...]
    sw = scw_ref[...]
    ysc_ref[0] = bg * (delayed(cp, ch, 2) * sw[0:1] + delayed(cp, ch, 1) * sw[1:2] + ch * sw[2:3])
    chprev[...] = ch[ts - SUBLANES:]

    rowf = lax.broadcasted_iota(I32, (ts, LANES), 0)
    c = _log_sigmoid(f_ref[0] + fb_ref[...])
    d = 1
    while d < ts:
        c = c + jnp.where(rowf >= d, pltpu.roll(c, d, 0), 0.0)
        d *= 2
    c = c + cprev[...]
    cprev[...] = c[ts - 1:]
    ct = c.T[:SUBLANES]
    for j in range(ts // tk):
        cumt_ref[0, j] = ct[:, j * tk:(j + 1) * tk]


def _rgsc(rg, sc, f, cw, cb, wa, ba, wi, bi, lam, scw, fb, tk, ts=512):
    B, S, _ = rg.shape
    GW = GROUP_WIDTH
    per = ts // tk

    def full(a):
        return pl.BlockSpec(a.shape, lambda b, t: (0,) * a.ndim)

    def seq(w):
        return pl.BlockSpec((1, ts, w), lambda b, t: (b, t, 0))

    params = (cw, cb, wa, ba, wi, bi, lam, scw, fb)
    return pl.pallas_call(
        functools.partial(_rgsc_kernel, ts=ts, tk=tk),
        grid=(B, S // ts),
        in_specs=[seq(2 * GW), seq(3 * GW), seq(LANES)] + [full(p) for p in params],
        out_specs=[seq(GW), seq(GW),
                   pl.BlockSpec((1, per, SUBLANES, tk), lambda b, t: (b, t, 0, 0))],
        out_shape=[jax.ShapeDtypeStruct((B, S, GW), F32), jax.ShapeDtypeStruct((B, S, GW), F32),
                   jax.ShapeDtypeStruct((B, S // tk, SUBLANES, tk), F32)],
        scratch_shapes=[pltpu.VMEM((SUBLANES, GW), F32), pltpu.VMEM((SUBLANES, GW), F32),
                        pltpu.VMEM((1, GW), F32), pltpu.VMEM((1, LANES), F32)],
        compiler_params=_params(("parallel", "arbitrary")),
        name="rgsc",
    )(rg, sc, f, *params)


HEADS_TOGETHER = GROUP_HEADS


assert math.log2(HEAD_DIM) % 2 == 0


def _fox_kernel(q_ref, k_ref, v_ref, cr_ref, o_ref, *, tq):
    qi = pl.program_id(1)
    scale = HEAD_DIM ** -0.5
    rowi = lax.broadcasted_iota(I32, (tq, tq), 0)
    coli = lax.broadcasted_iota(I32, (tq, tq), 1)
    causal = coli <= rowi
    qs = [q_ref[0, h] * scale for h in range(GROUP_HEADS)]

    def head_step(h, ki, carry, diag):
        m, l, acc = carry
        off = pl.multiple_of(ki * tq, tq)
        k = k_ref[0, h, pl.ds(off, tq), :]
        v = v_ref[0, h, pl.ds(off, tq), :]
        s = _nt_dot(qs[h], k) - cr_ref[0, ki, h:h + 1, :]
        if diag:
            s = jnp.where(causal, s, -jnp.inf)
        m_new = jnp.maximum(m, jnp.max(s, axis=-1, keepdims=True))
        alpha = jnp.exp(m - m_new)
        p = jnp.exp(s - m_new)
        l = alpha * l + jnp.sum(p, axis=-1, keepdims=True)
        acc = alpha * acc + jnp.dot(p.astype(BF16), v, preferred_element_type=F32)
        return m_new, l, acc

    def step(ki, carries, diag, heads):
        return tuple(head_step(h, ki, c, diag) for h, c in zip(heads, carries))

    init = (jnp.full((tq, 1), -jnp.inf, F32), jnp.zeros((tq, 1), F32),
            jnp.zeros((tq, HEAD_DIM), F32))
    outs = []
    for h0 in range(0, GROUP_HEADS, HEADS_TOGETHER):
        heads = range(h0, h0 + HEADS_TOGETHER)
        carries = lax.fori_loop(0, qi, functools.partial(step, diag=False, heads=heads),
                                (init,) * HEADS_TOGETHER)
        carries = step(qi, carries, True, heads)
        outs += [acc / l for _, l, acc in carries]
    o_ref[0] = jnp.concatenate(outs, axis=-1)


def _qkv_specs(S, tq):
    H, d = GROUP_HEADS, HEAD_DIM
    return [pl.BlockSpec((1, H, tq, d), lambda b, i: (b, 0, i, 0)),
            pl.BlockSpec((1, H, S, d), lambda b, i: (b, 1, 0, 0)),
            pl.BlockSpec((1, H, S, d), lambda b, i: (b, 2, 0, 0))]


def _fox(qkv, cumt, tq=256):
    B, _, S, d = qkv.shape
    H = GROUP_HEADS
    return pl.pallas_call(
        functools.partial(_fox_kernel, tq=tq),
        grid=(B, S // tq),
        in_specs=_qkv_specs(S, tq)
        + [pl.BlockSpec((1, S // tq, SUBLANES, tq), lambda b, i: (b, 0, 0, 0))],
        out_specs=pl.BlockSpec((1, tq, H * d), lambda b, i: (b, i, 0)),
        out_shape=jax.ShapeDtypeStruct((B, S, H * d), F32),
        compiler_params=_params(("parallel", "arbitrary")),
        name="fox_attn",
    )(qkv, qkv, qkv, cumt)


EXP_UNDERFLOW = -104.0


def _sb_kernel(q_ref, k_ref, v_ref, o_ref, *, tq):
    qi = pl.program_id(1)
    scale = HEAD_DIM ** -0.5
    rowi = lax.broadcasted_iota(I32, (tq, tq), 0)
    coli = lax.broadcasted_iota(I32, (tq, tq), 1)
    strict = coli < rowi
    later = jnp.where(rowi > coli, 1.0, 0.0).astype(BF16)
    qs = [q_ref[0, h] * scale for h in range(GROUP_HEADS)]
    def head_step(h, ki, carry, diag):
        rest, acc = carry
        off = pl.multiple_of(ki * tq, tq)
        k = k_ref[0, h, pl.ds(off, tq), :]
        v = v_ref[0, h, pl.ds(off, tq), :]
        z = _nt_dot(qs[h], k)
        ls = _log_sigmoid(z)
        l1m = ls - z
        if diag:
            l1m = jnp.where(strict, l1m, 0.0)
        hi = l1m.astype(BF16)
        lo = (l1m - hi.astype(F32)).astype(BF16)
        tail = (jnp.dot(hi, later, preferred_element_type=F32)
                + jnp.dot(lo, later, preferred_element_type=F32) + rest)
        w = jnp.exp(ls + tail)
        if diag:
            w = jnp.where(strict, w, 0.0)
        acc = acc + jnp.dot(w.astype(BF16), v, preferred_element_type=F32)
        rest = rest + jnp.sum(l1m, axis=-1, keepdims=True)
        return rest, acc

    def step(ki, carries, diag):
        return tuple(head_step(h, ki, carries[h], diag) for h in range(GROUP_HEADS))

    init = (jnp.zeros((tq, 1), F32), jnp.zeros((tq, HEAD_DIM), F32))
    carries = step(qi, (init,) * GROUP_HEADS, True)

    def more(c):
        j, carries = c
        top = carries[0][0]
        for rest, _ in carries[1:]:
            top = jnp.maximum(top, rest)
        return jnp.logical_and(j < qi, jnp.max(top) > EXP_UNDERFLOW)

    def further(c):
        j, carries = c
        return j + 1, step(qi - 1 - j, carries, False)

    _, carries = lax.while_loop(more, further, (jnp.int32(0), carries))
    o_ref[0] = jnp.concatenate([acc for _, acc in carries], axis=-1)


def _sb(qkv, tq=256):
    B, _, S, d = qkv.shape
    H = GROUP_HEADS
    return pl.pallas_call(
        functools.partial(_sb_kernel, tq=tq),
        grid=(B, S // tq),
        in_specs=_qkv_specs(S, tq),
        out_specs=pl.BlockSpec((1, tq, H * d), lambda b, i: (b, i, 0)),
        out_shape=jax.ShapeDtypeStruct((B, S, H * d), F32),
        compiler_params=_params(("parallel", "arbitrary")),
        name="sb_attn",
    )(qkv, qkv, qkv)


def _mixout_kernel(y0_ref, y1_ref, y2_ref, y3_ref, g_ref, w_ref, x_ref, lg_ref, lb_ref, o_ref):
    def rms(y):
        return y * lax.rsqrt(jnp.mean(y * y, axis=-1, keepdims=True) + GROUP_RMS_EPS)

    y = jnp.concatenate([rms(r[...]) for r in (y0_ref, y1_ref, y2_ref, y3_ref)], axis=-1)
    y = (y * g_ref[...]).astype(BF16)
    mix = jnp.dot(y, w_ref[...], preferred_element_type=F32)
    o_ref[...] = _layer_norm(ALPHA * x_ref[...] + mix, lg_ref[...], lb_ref[...])


def _mixout(ys, g, w, x2d, lg, lb, tm=512):
    T, D = x2d.shape
    GW = GROUP_WIDTH

    def full(a):
        return pl.BlockSpec(a.shape, lambda i: (0,) * a.ndim)

    return pl.pallas_call(
        _mixout_kernel,
        grid=(T // tm,),
        in_specs=[pl.BlockSpec((tm, GW), lambda i: (i, 0))] * 4
        + [full(g), full(w), pl.BlockSpec((tm, D), lambda i: (i, 0)), full(lg), full(lb)],
        out_specs=pl.BlockSpec((tm, D), lambda i: (i, 0)),
        out_shape=jax.ShapeDtypeStruct((T, D), F32),
        compiler_params=_params(("parallel",)),
        name="mixout",
    )(*ys, g, w, x2d, lg, lb)


def _kv_kernel(m_ref, w_ref, k_ref, v_ref):
    kv = jnp.dot(m_ref[...].astype(BF16), w_ref[...], preferred_element_type=F32)
    k_ref[...] = kv[:, :D_MODEL].astype(BF16)
    v_ref[...] = kv[:, D_MODEL:].astype(BF16)


def _kv(mem2d, wkv, tm=512):
    M, D = mem2d.shape
    return pl.pallas_call(
        _kv_kernel,
        grid=(M // tm,),
        in_specs=[pl.BlockSpec((tm, D), lambda i: (i, 0)),
                  pl.BlockSpec((D, 2 * D), lambda i: (0, 0))],
        out_specs=[pl.BlockSpec((tm, D), lambda i: (i, 0))] * 2,
        out_shape=[jax.ShapeDtypeStruct((M, D), BF16)] * 2,
        compiler_params=_params(("parallel",)),
        name="mem_kv",
    )(mem2d, wkv)


PEER_TB = 256


def _chunk_major_spec(tm):
    return pl.BlockSpec((tm // PEER_TB, ROW_CHUNKS * PEER_TB, LANES), lambda i: (i, 0, 0))


def _chunk_major_shape(T):
    return jax.ShapeDtypeStruct((T // PEER_TB, ROW_CHUNKS * PEER_TB, LANES), F32)


def _to_chunk_major(y, ref):
    for g in range(y.shape[0] // PEER_TB):
        for c in range(ROW_CHUNKS):
            ref[g, c * PEER_TB:(c + 1) * PEER_TB, :] = (
                y[g * PEER_TB:(g + 1) * PEER_TB, c * LANES:(c + 1) * LANES])


def _xattn_kernel(x_ref, wq_ref, k_ref, v_ref, wo_ref, lg_ref, lb_ref, o_ref, oc_ref):
    x = x_ref[...]
    q = jnp.dot(x.astype(BF16), wq_ref[...], preferred_element_type=F32).astype(BF16)
    scale = XA_HEAD_DIM ** -0.5
    outs = []
    for h in range(XA_HEADS):
        sl = slice(h * XA_HEAD_DIM, (h + 1) * XA_HEAD_DIM)
        s = _nt_dot(q[:, sl], k_ref[0, :, sl]) * scale
        p = jnp.exp(s - jnp.max(s, axis=-1, keepdims=True))
        p = p / jnp.sum(p, axis=-1, keepdims=True)
        outs.append(jnp.dot(p.astype(BF16), v_ref[0, :, sl], preferred_element_type=F32))
    o = jnp.concatenate(outs, axis=-1).astype(BF16)
    xa = jnp.dot(o, wo_ref[...], preferred_element_type=F32)
    y = _layer_norm(ALPHA * x + xa, lg_ref[...], lb_ref[...])
    o_ref[...] = y
    _to_chunk_major(y, oc_ref)


def _xattn(x2d, wq, k, v, wo, lg, lb, seq_len, tm=512):
    T, D = x2d.shape
    M = k.shape[1]
    per_seq = seq_len // tm

    def full(a):
        return pl.BlockSpec(a.shape, lambda i: (0,) * a.ndim)

    return pl.pallas_call(
        _xattn_kernel,
        grid=(T // tm,),
        in_specs=[pl.BlockSpec((tm, D), lambda i: (i, 0)), full(wq),
                  pl.BlockSpec((1, M, D), lambda i: (i // per_seq, 0, 0)),
                  pl.BlockSpec((1, M, D), lambda i: (i // per_seq, 0, 0)),
                  full(wo), full(lg), full(lb)],
        out_specs=[pl.BlockSpec((tm, D), lambda i: (i, 0)), _chunk_major_spec(tm)],
        out_shape=[jax.ShapeDtypeStruct((T, D), F32), _chunk_major_shape(T)],
        compiler_params=_params(("parallel",)),
        name="xattn",
    )(x2d, wq, k, v, wo, lg, lb)


E_PER_TOK = PEER_HEADS * PEER_TOPK


def _staircase():
    return [(a, b) for a in range(PEER_TOPK) for b in range(PEER_TOPK // (a + 1))]


def _route_kernel(x_ref, wq_ref, k1_ref, k2_ref, e_ref, g_ref, v_scr, i_scr, *, ts):
    xb = x_ref[...].astype(BF16)
    key_id = lax.broadcasted_iota(I32, (N_KEYS, ts), 0).astype(F32)
    qd = 2 * PEER_HALF
    for h in range(PEER_HEADS):
        qry = jnp.dot(xb, wq_ref[:, h * qd:(h + 1) * qd], preferred_element_type=F32)
        for half, kref in ((0, k1_ref), (1, k2_ref)):
            qh = qry[:, half * PEER_HALF:(half + 1) * PEER_HALF].astype(BF16)
            s0 = _nt_dot(kref[...], qh)

            s = s0
            for it in range(PEER_TOPK):
                m = jnp.max(s, axis=0, keepdims=True)
                idx = jnp.min(jnp.where(s == m, key_id, float(N_KEYS)), axis=0, keepdims=True)
                v_scr[half, it, h:h + 1, :] = m
                i_scr[half, it, h:h + 1, :] = idx.astype(I32)
                s = jnp.where(key_id == idx, -jnp.inf, s)

    cells = _staircase()
    cand = tuple(v_scr[0, a] + v_scr[1, b] for a, b in cells)
    expert = [i_scr[0, a] * N_KEYS + i_scr[1, b] for a, b in cells]

    picked = []
    for it in range(PEER_TOPK):
        best_v, best_e, best_c = cand[0], expert[0], jnp.zeros((PEER_HEADS, ts), I32)
        for c in range(1, len(cells)):
            better = cand[c] > best_v
            best_v = jnp.where(better, cand[c], best_v)
            best_e = jnp.where(better, expert[c], best_e)
            best_c = jnp.where(better, c, best_c)
        i_scr[0, it] = best_e * WORD_ROWS
        picked.append(best_v)
        cand = tuple(jnp.where(best_c == c, -jnp.inf, cand[c]) for c in range(len(cells)))
    ex = [jnp.exp(v - picked[0]) for v in picked]
    den = ex[0]
    for it in range(1, PEER_TOPK):
        den = den + ex[it]
    inv = 1.0 / den
    g_ref[...] = jnp.concatenate([e * inv for e in ex], axis=0).T
    rows = pltpu.bitcast(i_scr[0].reshape(E_PER_TOK, ts), F32)
    e_ref[...] = pltpu.bitcast(rows.T, I32)


def _route(x2d, wq, k1, k2, ts=256):
    T, D = x2d.shape
    nb = T // ts

    def full(a):
        return pl.BlockSpec(a.shape, lambda i: (0,) * a.ndim)

    blk = pl.BlockSpec((ts, E_PER_TOK), lambda i: (i, 0))
    return pl.pallas_call(
        functools.partial(_route_kernel, ts=ts),
        grid=(nb,),
        in_specs=[pl.BlockSpec((ts, D), lambda i: (i, 0)), full(wq), full(k1), full(k2)],
        out_specs=[blk, blk],
        out_shape=[jax.ShapeDtypeStruct((T, E_PER_TOK), I32),
                   jax.ShapeDtypeStruct((T, E_PER_TOK), F32)],
        scratch_shapes=[pltpu.VMEM((2, PEER_TOPK, PEER_HEADS, ts), F32),
                        pltpu.VMEM((2, PEER_TOPK, PEER_HEADS, ts), I32)],
        compiler_params=_params(("parallel",)),
        name="peer_route",
    )(x2d, wq, k1, k2)


TOK_GROUP = SUBLANES


N_GROUPS = PEER_TB // TOK_GROUP


def _gathered_rows(ids_ref, tab_ref, t):
    pieces = []
    for j in range(E_PER_TOK):
        r0 = pl.multiple_of(ids_ref[t, j], WORD_ROWS)
        pieces.append(tab_ref[pl.ds(r0, WORD_ROWS), :])
    return jnp.concatenate(pieces, axis=0)


def _token_chunks(tt):
    return pl.ds(tt, ROW_CHUNKS, stride=PEER_TB)


def _chunk_of_col():
    col = lax.broadcasted_iota(I32, (ROW_CHUNKS, E_PER_TOK * ROW_CHUNKS), 1)
    row = lax.broadcasted_iota(I32, (ROW_CHUNKS, E_PER_TOK * ROW_CHUNKS), 0)
    return (col & (ROW_CHUNKS - 1)) == row


def _with_ids(ids_hbm, bufs, sems, body):
    i = pl.program_id(0)

    def block_copy(step, slot):
        rows = pl.ds(pl.multiple_of(step * PEER_TB, PEER_TB), PEER_TB)
        return pltpu.make_async_copy(ids_hbm.at[rows], bufs[slot], sems.at[slot])

    @pl.when(i == 0)
    def _():
        block_copy(0, 0).start()

    for slot in range(2):
        @pl.when(i % 2 == slot)
        def _(slot=slot):
            block_copy(i, slot).wait()

            @pl.when(i + 1 < pl.num_programs(0))
            def _():
                block_copy(i + 1, 1 - slot).start()

            body(bufs[slot])


def _peer_score_kernel(ids_hbm, x_ref, tab_ref, s_ref, ids_a, ids_b, sems):
    _with_ids(ids_hbm, (ids_a, ids_b), sems,
              functools.partial(_peer_score_body, x_ref, tab_ref, s_ref))


def _peer_score_body(x_ref, tab_ref, s_ref, ids_ref):
    own_chunk = _chunk_of_col()
    c_id = lax.broadcasted_iota(I32, (E_PER_TOK * ROW_CHUNKS, E_PER_TOK), 0)
    e_id = lax.broadcasted_iota(I32, (E_PER_TOK * ROW_CHUNKS, E_PER_TOK), 1)
    fold = jnp.where(c_id >> CHUNK_SHIFT == e_id, 1.0, 0.0).astype(BF16)

    for g in range(N_GROUPS):
        tok0 = g * TOK_GROUP
        parts = []
        for tt in range(TOK_GROUP):
            rows = pltpu.bitcast(_gathered_rows(ids_ref, tab_ref, tok0 + tt), BF16)
            xt = x_ref[0, _token_chunks(tok0 + tt), :].astype(BF16)
            full = _nt_dot(xt, rows)
            parts.append(jnp.where(own_chunk, full, 0.0))
        part = jnp.concatenate(parts, axis=0)
        hi = part.astype(BF16)
        lo = (part - hi.astype(F32)).astype(BF16)
        sc = (jnp.dot(hi, fold, preferred_element_type=F32)
              + jnp.dot(lo, fold, preferred_element_type=F32))
        sc = jnp.sum(sc.reshape(TOK_GROUP, ROW_CHUNKS, E_PER_TOK), axis=1)
        s_ref[tok0:tok0 + TOK_GROUP, :] = sc


def _peer_mix_kernel(ids_hbm, s_ref, gate_ref, tab_ref, o_ref, ids_a, ids_b, sems):
    _with_ids(ids_hbm, (ids_a, ids_b), sems,
              functools.partial(_peer_mix_body, s_ref, gate_ref, tab_ref, o_ref))


def _peer_mix_body(s_ref, gate_ref, tab_ref, o_ref, ids_ref):
    own_chunk = _chunk_of_col()
    e_id = lax.broadcasted_iota(I32, (E_PER_TOK, E_PER_TOK * ROW_CHUNKS), 0)
    c_id = lax.broadcasted_iota(I32, (E_PER_TOK, E_PER_TOK * ROW_CHUNKS), 1)
    spread = jnp.where(c_id >> CHUNK_SHIFT == e_id, 1.0, 0.0).astype(BF16)

    for g in range(N_GROUPS):
        tok0 = g * TOK_GROUP
        sl = slice(tok0, tok0 + TOK_GROUP)
        s = s_ref[sl, :]
        act = 0.5 * s * (1.0 + lax.erf(s * (2.0 ** -0.5)))
        coef = (gate_ref[sl, :] * act).astype(BF16)
        coef = jnp.dot(coef, spread, preferred_element_type=F32)
        for tt in range(TOK_GROUP):
            rows = pltpu.bitcast(_gathered_rows(ids_ref, tab_ref, tok0 + tt), BF16)
            ct = jnp.broadcast_to(coef[tt:tt + 1], (ROW_CHUNKS, E_PER_TOK * ROW_CHUNKS))
            ct = jnp.where(own_chunk, ct, 0.0).astype(BF16)
            o_ref[0, _token_chunks(tok0 + tt), :] = jnp.dot(ct, rows, preferred_element_type=F32)


def _peer_experts(ids, gate, xc, tab_u, tab_v):
    T = ids.shape[0]
    tb = PEER_TB
    nb = T // tb
    ids_spec = pl.BlockSpec(memory_space=pl.ANY)
    ids_scratch = [pltpu.SMEM((tb, E_PER_TOK), I32), pltpu.SMEM((tb, E_PER_TOK), I32),
                   pltpu.SemaphoreType.DMA((2,))]
    tab_spec = pl.BlockSpec(tab_u.shape, lambda i: (0, 0), pipeline_mode=pl.Buffered(1))
    tok_spec = pl.BlockSpec((tb, E_PER_TOK), lambda i: (i, 0))
    row_spec = _chunk_major_spec(tb)
    s = pl.pallas_call(
        _peer_score_kernel,
        grid=(nb,),
        in_specs=[ids_spec, row_spec, tab_spec],
        out_specs=tok_spec,
        out_shape=jax.ShapeDtypeStruct((T, E_PER_TOK), F32),
        scratch_shapes=ids_scratch,
        compiler_params=_params(("arbitrary",)),
        name="peer_score",
    )(ids, xc, tab_u)
    return pl.pallas_call(
        _peer_mix_kernel,
        grid=(nb,),
        in_specs=[ids_spec, tok_spec, tok_spec, tab_spec],
        out_specs=row_spec,
        out_shape=_chunk_major_shape(T),
        scratch_shapes=ids_scratch,
        compiler_params=_params(("arbitrary",)),
        name="peer_mix",
    )(ids, s, gate, tab_v)


def _resln_kernel(x_ref, y_ref, lg_ref, lb_ref, o_ref):
    G = x_ref.shape[0]
    z = ALPHA * x_ref[...] + y_ref[...]
    zs = [z[:, c * PEER_TB:(c + 1) * PEER_TB, :] for c in range(ROW_CHUNKS)]
    tot = zs[0]
    for c in range(1, ROW_CHUNKS):
        tot = tot + zs[c]
    mu = jnp.sum(tot, axis=-1, keepdims=True) * (1.0 / D_MODEL)
    zc = [a - mu for a in zs]
    sq = zc[0] * zc[0]
    for c in range(1, ROW_CHUNKS):
        sq = sq + zc[c] * zc[c]
    rstd = lax.rsqrt(jnp.sum(sq, axis=-1, keepdims=True) * (1.0 / D_MODEL) + LN_EPS)
    for c in range(ROW_CHUNKS):
        cols = slice(c * LANES, (c + 1) * LANES)
        y = (zc[c] * rstd).reshape(G * PEER_TB, LANES)
        o_ref[:, cols] = y * lg_ref[:, cols] + lb_ref[:, cols]


def _resln(xc, yc, lg, lb, tm=512):
    T = xc.shape[0] * PEER_TB
    blk = _chunk_major_spec(tm)
    par = pl.BlockSpec((1, D_MODEL), lambda i: (0, 0))
    return pl.pallas_call(
        _resln_kernel,
        grid=(T // tm,),
        in_specs=[blk, blk, par, par],
        out_specs=pl.BlockSpec((tm, D_MODEL), lambda i: (i, 0)),
        out_shape=jax.ShapeDtypeStruct((T, D_MODEL), F32),
        compiler_params=_params(("parallel",)),
        name="res_ln",
    )(xc, yc, lg, lb)


def _pack_kernel(t_ref, o_ref):
    def bf16_bits(v):
        return pltpu.bitcast(v.astype(BF16).astype(F32), jnp.uint32)

    te = t_ref.shape[0]
    for r in range(WORD_ROWS):
        lo = bf16_bits(t_ref[:, (2 * r) * LANES:(2 * r + 1) * LANES])
        hi = bf16_bits(t_ref[:, (2 * r + 1) * LANES:(2 * r + 2) * LANES])
        words = (hi & jnp.uint32(0xFFFF0000)) | (lo >> 16)
        o_ref[pl.ds(r, te, stride=WORD_ROWS), :] = pltpu.bitcast(words, I32)


def _pack_table(t, layer, te=512):
    _, E, D = t.shape
    return pl.pallas_call(
        _pack_kernel,
        grid=(E // te,),
        in_specs=[pl.BlockSpec((None, te, D), lambda i: (layer, i, 0))],
        out_specs=pl.BlockSpec((te * WORD_ROWS, LANES), lambda i: (i, 0)),
        out_shape=jax.ShapeDtypeStruct((E * WORD_ROWS, LANES), I32),
        compiler_params=_params(("parallel",)),
        name="pack_table",
    )(t)


def _block_diag(w):
    H, d, _ = w.shape
    out = jnp.zeros((H * d, H * d), w.dtype)
    for h in range(H):
        out = out.at[h * d:(h + 1) * d, h * d:(h + 1) * d].set(w[h])
    return out


def _cat_w_in_kernel(w_ref, o_ref):
    GW = GROUP_WIDTH
    off_f = 5 * GW
    off_sb = off_f + GROUP_HEADS
    w = w_ref[...].astype(BF16)
    o_ref[:, :off_f] = w[:, :off_f]
    o_ref[:, off_f:C_F] = w[:, off_sb:N_IN]
    o_ref[:, C_F:C_END] = jnp.zeros((w.shape[0], C_END - C_F), BF16)
    o_ref[:, C_F:C_F + GROUP_HEADS] = w[:, off_f:off_sb]


def _cat_w_in(w_in, layer, tr=256):
    _, D, N = w_in.shape
    return pl.pallas_call(
        _cat_w_in_kernel,
        grid=(D // tr,),
        in_specs=[pl.BlockSpec((None, tr, N), lambda i: (layer, i, 0))],
        out_specs=pl.BlockSpec((tr, C_END), lambda i: (i, 0)),
        out_shape=jax.ShapeDtypeStruct((D, C_END), BF16),
        compiler_params=_params(("parallel",)),
        name="regroup_w_in",
    )(w_in)


def _row(v, width=None):
    v = v.reshape(1, -1).astype(F32)
    if width is not None and v.shape[1] < width:
        v = jnp.pad(v, ((0, 0), (0, width - v.shape[1])))
    return v


def _layer(x2d, kmem, vmem_, B, S, p, tables, layer, tq=256):
    T = B * S
    rg, fox, sb, sc, f = _inproj(x2d, _cat_w_in(tables[2], layer), B, S)
    yrg, ysc, cumt = _rgsc(
        rg.reshape(B, S, -1), sc.reshape(B, S, -1), f.reshape(B, S, -1),
        p["rg_conv_w"], _row(p["rg_conv_b"]), _block_diag(p["rg_wa"]).astype(BF16), _row(p["rg_ba"]),
        _block_diag(p["rg_wi"]).astype(BF16), _row(p["rg_bi"]), _row(p["rg_lambda"]),
        p["sc_conv_w"], _row(p["fox_bf"], LANES), tq)
    yfox = _fox(fox, cumt, tq=tq)
    ysb = _sb(sb, tq=tq)
    ys = (yrg.reshape(T, -1), yfox.reshape(T, -1), ysb.reshape(T, -1), ysc.reshape(T, -1))
    x1 = _mixout(ys, _row(p["mix_norm_g"]), p["w_out"].astype(BF16), x2d,
                 _row(p["ln1_g"]), _row(p["ln1_b"]))
    x2, x2c = _xattn(x1, p["xa_wq"].astype(BF16), kmem, vmem_, p["xa_wo"].astype(BF16),
                     _row(p["ln2_g"]), _row(p["ln2_b"]), S)
    ids, gate = _route(x2, p["peer_wq"].astype(BF16), p["peer_k1"].astype(BF16),
                       p["peer_k2"].astype(BF16))
    ffc = _peer_experts(ids, gate, x2c, _pack_table(tables[0], layer), _pack_table(tables[1], layer))
    return _resln(x2c, ffc, _row(p["ln3_g"]), _row(p["ln3_b"]))


_LAYER_PARAMS = ("w_in", "w_out", "rg_conv_w", "rg_conv_b", "rg_wa", "rg_ba", "rg_wi", "rg_bi",
                 "rg_lambda", "fox_bf", "sc_conv_w", "mix_norm_g", "ln1_g", "ln1_b", "xa_wq",
                 "xa_wkv", "xa_wo", "ln2_g", "ln2_b", "peer_wq", "peer_k1", "peer_k2", "peer_u",
                 "peer_v", "ln3_g", "ln3_b")


def kernel(x, mem, w_in, w_out, rg_conv_w, rg_conv_b, rg_wa, rg_ba, rg_wi, rg_bi, rg_lambda, fox_bf, sc_conv_w, mix_norm_g, ln1_g, ln1_b, xa_wq, xa_wkv, xa_wo, ln2_g, ln2_b, peer_wq, peer_k1, peer_k2, peer_u, peer_v, ln3_g, ln3_b):
    stacked = dict(zip(_LAYER_PARAMS, (
        w_in, w_out, rg_conv_w, rg_conv_b, rg_wa, rg_ba, rg_wi, rg_bi, rg_lambda, fox_bf,
        sc_conv_w, mix_norm_g, ln1_g, ln1_b, xa_wq, xa_wkv, xa_wo, ln2_g, ln2_b, peer_wq,
        peer_k1, peer_k2, peer_u, peer_v, ln3_g, ln3_b)))
    B, S, D = x.shape
    M = mem.shape[1]
    x2d = x.reshape(B * S, D)
    mem2d = mem.reshape(B * M, D)
    for l in range(w_in.shape[0]):
        p = {k: v[l] for k, v in stacked.items() if k not in ("peer_u", "peer_v", "w_in")}
        kmem, vmem_ = _kv(mem2d, p["xa_wkv"].astype(BF16))
        x2d = _layer(x2d, kmem.reshape(B, M, D), vmem_.reshape(B, M, D), B, S, p,
                     (peer_u, peer_v, w_in), l)
    return x2d.reshape(B, S, D)
```

```python
import functools
import math

import jax
import jax.numpy as jnp
from jax import lax
from jax.experimental import pallas as pl
from jax.experimental.pallas import tpu as pltpu

F32 = jnp.float32
BF16 = jnp.bfloat16
I32 = jnp.int32

D_MODEL = 1024
GROUP_WIDTH = 256
GROUP_HEADS = 4
HEAD_DIM = 64
N_IN = 2820
RGLRU_C = 8.0
XA_HEADS = 4
XA_HEAD_DIM = D_MODEL // XA_HEADS
PEER_HEADS = 8
N_KEYS = 128
PEER_HALF = 128
PEER_TOPK = 16
DEPTH = 2
ALPHA = (2.0 * DEPTH) ** 0.25
LN_EPS = 1e-5

SUBLANES = 8
LANES = 128
WORD_ROWS = D_MODEL // (2 * LANES)
ROW_CHUNKS = D_MODEL // LANES
CHUNK_SHIFT = ROW_CHUNKS.bit_length() - 1

V7X_VMEM_BYTES = 64 * 1024 * 1024
VMEM_LIMIT = V7X_VMEM_BYTES * 3 // 4
GROUP_RMS_EPS = 1e-6
GELU_TANH_CUBIC = 0.044715


def _params(sem, vmem=VMEM_LIMIT):
    return pltpu.CompilerParams(dimension_semantics=sem, vmem_limit_bytes=vmem)


def _layer_norm(z, g, b):
    mu = jnp.mean(z, axis=-1, keepdims=True)
    zc = z - mu
    var = jnp.mean(zc * zc, axis=-1, keepdims=True)
    return zc * lax.rsqrt(var + LN_EPS) * g + b


def _log_sigmoid(z):
    return jnp.minimum(z, 0.0) - jnp.log1p(jnp.exp(-jnp.abs(z)))


def _nt_dot(a, b):
    return lax.dot_general(a, b, (((1,), (1,)), ((), ())), preferred_element_type=F32)


C_RG, C_FOX, C_SB, C_SC, C_F, C_END = 0, 512, 1280, 2048, 2816, 2944


def _inproj_kernel(x_ref, w_ref, rg_ref, fox_ref, sb_ref, sc_ref, f_ref):
    xb = x_ref[...].astype(BF16)

    def mm(lo, hi):
        return jnp.dot(xb, w_ref[:, lo:hi], preferred_element_type=F32)

    def heads(ref, lo, hi):
        qkv = mm(lo, hi)
        for j in range(3 * GROUP_HEADS):
            ref[0, j] = qkv[:, j * HEAD_DIM:(j + 1) * HEAD_DIM].astype(BF16)

    rg_ref[...] = mm(C_RG, C_FOX)
    heads(fox_ref, C_FOX, C_SB)
    heads(sb_ref, C_SB, C_SC)
    sc_ref[...] = mm(C_SC, C_F)
    f_ref[...] = mm(C_F, C_END)


def _inproj(x2d, w_cat, B, S, tm=512):
    T, D = x2d.shape
    per_seq = S // tm
    flat = ((C_FOX - C_RG, F32), (C_F - C_SC, F32), (C_END - C_F, F32))
    flat_specs = [pl.BlockSpec((tm, w), lambda i: (i, 0)) for w, _ in flat]
    flat_shapes = [jax.ShapeDtypeStruct((T, w), dt) for w, dt in flat]
    head_spec = pl.BlockSpec((1, 3 * GROUP_HEADS, tm, HEAD_DIM),
                             lambda i: (i // per_seq, 0, i % per_seq, 0))
    head_shape = jax.ShapeDtypeStruct((B, 3 * GROUP_HEADS, S, HEAD_DIM), BF16)
    return pl.pallas_call(
        _inproj_kernel,
        grid=(T // tm,),
        in_specs=[pl.BlockSpec((tm, D), lambda i: (i, 0)),
                  pl.BlockSpec((D, C_END), lambda i: (0, 0))],
        out_specs=[flat_specs[0], head_spec, head_spec, flat_specs[1], flat_specs[2]],
        out_shape=[flat_shapes[0], head_shape, head_shape, flat_shapes[1], flat_shapes[2]],
        compiler_params=_params(("parallel",)),
        name="inproj",
    )(x2d, w_cat)


def _rgsc_kernel(rg_ref, sc_ref, f_ref, cw_ref, cb_ref, wa_ref, ba_ref, wi_ref, bi_ref, lam_ref,
                 scw_ref, fb_ref, yrg_ref, ysc_ref, cumt_ref, xprev, chprev, hprev, cprev,
                 *, ts, tk):
    GW = GROUP_WIDTH

    @pl.when(pl.program_id(1) == 0)
    def _():
        xprev[...] = jnp.zeros_like(xprev)
        chprev[...] = jnp.zeros_like(chprev)
        hprev[...] = jnp.zeros_like(hprev)
        cprev[...] = jnp.zeros_like(cprev)

    row = lax.broadcasted_iota(I32, (ts, GW), 0)

    def delayed(prev, cur, d):
        ext = jnp.concatenate([prev, cur], axis=0)
        return pltpu.roll(ext, d, 0)[SUBLANES:]

    xr = rg_ref[0, :, :GW]
    gate = rg_ref[0, :, GW:]
    xp = xprev[...]
    cw = cw_ref[...]
    xc = (delayed(xp, xr, 3) * cw[0:1] + delayed(xp, xr, 2) * cw[1:2]
          + delayed(xp, xr, 1) * cw[2:3] + xr * cw[3:4] + cb_ref[...])
    xprev[...] = xr[ts - SUBLANES:]
    xcb = xc.astype(BF16)
    r = jax.nn.sigmoid(jnp.dot(xcb, wa_ref[...], preferred_element_type=F32) + ba_ref[...])
    ig = jax.nn.sigmoid(jnp.dot(xcb, wi_ref[...], preferred_element_type=F32) + bi_ref[...])
    z = -lam_ref[...]
    softplus = jnp.maximum(z, 0.0) + jnp.log1p(jnp.exp(-jnp.abs(z)))
    log_a = -RGLRU_C * r * softplus
    a = jnp.exp(log_a)
    u = jnp.sqrt(-jnp.tanh(log_a) * (a * a + 1.0)) * (ig * xc)
    acc_a, acc_b = a, u
    d = 1
    while d < ts:
        keep = row >= d
        a_s = jnp.where(keep, pltpu.roll(acc_a, d, 0), 1.0)
        b_s = jnp.where(keep, pltpu.roll(acc_b, d, 0), 0.0)
        acc_b = acc_a * b_s + acc_b
        acc_a = acc_a * a_s
        d *= 2
    h = acc_b + acc_a * hprev[...]
    hprev[...] = h[ts - 1:]
    c0 = math.sqrt(2.0 / math.pi)
    gelu = 0.5 * gate * (1.0 + jnp.tanh(c0 * (gate + GELU_TANH_CUBIC * gate * gate * gate)))
    yrg_ref[0] = h * gelu

    bg = sc_ref[0, :, :GW]
    ch = sc_ref[0, :, GW:2 * GW] * sc_ref[0, :, 2 * GW:]
    cp = chprev[...]
    sw = scw_ref[...]
    ysc_ref[0] = bg * (delayed(cp, ch, 2) * sw[0:1] + delayed(cp, ch, 1) * sw[1:2] + ch * sw[2:3])
    chprev[...] = ch[ts - SUBLANES:]

    rowf = lax.broadcasted_iota(I32, (ts, LANES), 0)
    c = _log_sigmoid(f_ref[0] + fb_ref[...])
    d = 1
    while d < ts:
        c = c + jnp.where(rowf >= d, pltpu.roll(c, d, 0), 0.0)
        d *= 2
    c = c + cprev[...]
    cprev[...] = c[ts - 1:]
    ct = c.T[:SUBLANES]
    for j in range(ts // tk):
        cumt_ref[0, j] = ct[:, j * tk:(j + 1) * tk]


def _rgsc(rg, sc, f, cw, cb, wa, ba, wi, bi, lam, scw, fb, tk, ts=512):
    B, S, _ = rg.shape
    GW = GROUP_WIDTH
    per = ts // tk

    def full(a):
        return pl.BlockSpec(a.shape, lambda b, t: (0,) * a.ndim)

    def seq(w):
        return pl.BlockSpec((1, ts, w), lambda b, t: (b, t, 0))

    params = (cw, cb, wa, ba, wi, bi, lam, scw, fb)
    return pl.pallas_call(
        functools.partial(_rgsc_kernel, ts=ts, tk=tk),
        grid=(B, S // ts),
        in_specs=[seq(2 * GW), seq(3 * GW), seq(LANES)] + [full(p) for p in params],
        out_specs=[seq(GW), seq(GW),
                   pl.BlockSpec((1, per, SUBLANES, tk), lambda b, t: (b, t, 0, 0))],
        out_shape=[jax.ShapeDtypeStruct((B, S, GW), F32), jax.ShapeDtypeStruct((B, S, GW), F32),
                   jax.ShapeDtypeStruct((B, S // tk, SUBLANES, tk), F32)],
        scratch_shapes=[pltpu.VMEM((SUBLANES, GW), F32), pltpu.VMEM((SUBLANES, GW), F32),
                        pltpu.VMEM((1, GW), F32), pltpu.VMEM((1, LANES), F32)],
        compiler_params=_params(("parallel", "arbitrary")),
        name="rgsc",
    )(rg, sc, f, *params)


HEADS_TOGETHER = GROUP_HEADS


assert math.log2(HEAD_DIM) % 2 == 0


def _fox_kernel(q_ref, k_ref, v_ref, cr_ref, o_ref, *, tq):
    qi = pl.program_id(1)
    scale = HEAD_DIM ** -0.5
    rowi = lax.broadcasted_iota(I32, (tq, tq), 0)
    coli = lax.broadcasted_iota(I32, (tq, tq), 1)
    causal = coli <= rowi
    qs = [q_ref[0, h] * scale for h in range(GROUP_HEADS)]

    def head_step(h, ki, carry, diag):
        m, l, acc = carry
        off = pl.multiple_of(ki * tq, tq)
        k = k_ref[0, h, pl.ds(off, tq), :]
        v = v_ref[0, h, pl.ds(off, tq), :]
        s = _nt_dot(qs[h], k) - cr_ref[0, ki, h:h + 1, :]
        if diag:
            s = jnp.where(causal, s, -jnp.inf)
        m_new = jnp.maximum(m, jnp.max(s, axis=-1, keepdims=True))
        alpha = jnp.exp(m - m_new)
        p = jnp.exp(s - m_new)
        l = alpha * l + jnp.sum(p, axis=-1, keepdims=True)
        acc = alpha * acc + jnp.dot(p.astype(BF16), v, preferred_element_type=F32)
        return m_new, l, acc

    def step(ki, carries, diag, heads):
        return tuple(head_step(h, ki, c, diag) for h, c in zip(heads, carries))

    init = (jnp.full((tq, 1), -jnp.inf, F32), jnp.zeros((tq, 1), F32),
            jnp.zeros((tq, HEAD_DIM), F32))
    outs = []
    for h0 in range(0, GROUP_HEADS, HEADS_TOGETHER):
        heads = range(h0, h0 + HEADS_TOGETHER)
        carries = lax.fori_loop(0, qi, functools.partial(step, diag=False, heads=heads),
                                (init,) * HEADS_TOGETHER)
        carries = step(qi, carries, True, heads)
        outs += [acc / l for _, l, acc in carries]
    o_ref[0] = jnp.concatenate(outs, axis=-1)


def _qkv_specs(S, tq):
    H, d = GROUP_HEADS, HEAD_DIM
    return [pl.BlockSpec((1, H, tq, d), lambda b, i: (b, 0, i, 0)),
            pl.BlockSpec((1, H, S, d), lambda b, i: (b, 1, 0, 0)),
            pl.BlockSpec((1, H, S, d), lambda b, i: (b, 2, 0, 0))]


def _fox(qkv, cumt, tq=256):
    B, _, S, d = qkv.shape
    H = GROUP_HEADS
    return pl.pallas_call(
        functools.partial(_fox_kernel, tq=tq),
        grid=(B, S // tq),
        in_specs=_qkv_specs(S, tq)
        + [pl.BlockSpec((1, S // tq, SUBLANES, tq), lambda b, i: (b, 0, 0, 0))],
        out_specs=pl.BlockSpec((1, tq, H * d), lambda b, i: (b, i, 0)),
        out_shape=jax.ShapeDtypeStruct((B, S, H * d), F32),
        compiler_params=_params(("parallel", "arbitrary")),
        name="fox_attn",
    )(qkv, qkv, qkv, cumt)


EXP_UNDERFLOW = -104.0


def _sb_kernel(q_ref, k_ref, v_ref, o_ref, *, tq):
    qi = pl.program_id(1)
    scale = HEAD_DIM ** -0.5
    rowi = lax.broadcasted_iota(I32, (tq, tq), 0)
    coli = lax.broadcasted_iota(I32, (tq, tq), 1)
    strict = coli < rowi
    later = jnp.where(rowi > coli, 1.0, 0.0).astype(BF16)
    qs = [q_ref[0, h] * scale for h in range(GROUP_HEADS)]
    def head_step(h, ki, carry, diag):
        rest, acc = carry
        off = pl.multiple_of(ki * tq, tq)
        k = k_ref[0, h, pl.ds(off, tq), :]
        v = v_ref[0, h, pl.ds(off, tq), :]
        z = _nt_dot(qs[h], k)
        ls = _log_sigmoid(z)
        l1m = ls - z
        if diag:
            l1m = jnp.where(strict, l1m, 0.0)
        hi = l1m.astype(BF16)
        lo = (l1m - hi.astype(F32)).astype(BF16)
        tail = (jnp.dot(hi, later, preferred_element_type=F32)
                + jnp.dot(lo, later, preferred_element_type=F32) + rest)
        w = jnp.exp(ls + tail)
        if diag:
            w = jnp.where(strict, w, 0.0)
        acc = acc + jnp.dot(w.astype(BF16), v, preferred_element_type=F32)
        rest = rest + jnp.sum(l1m, axis=-1, keepdims=True)
        return rest, acc

    def step(ki, carries, diag):
        return tuple(head_step(h, ki, carries[h], diag) for h in range(GROUP_HEADS))

    init = (jnp.zeros((tq, 1), F32), jnp.zeros((tq, HEAD_DIM), F32))
    carries = step(qi, (init,) * GROUP_HEADS, True)

    def more(c):
        j, carries = c
        top = carries[0][0]
        for rest, _ in carries[1:]:
            top = jnp.maximum(top, rest)
        return jnp.logical_and(j < qi, jnp.max(top) > EXP_UNDERFLOW)

    def further(c):
        j, carries = c
        return j + 1, step(qi - 1 - j, carries, False)

    _, carries = lax.while_loop(more, further, (jnp.int32(0), carries))
    o_ref[0] = jnp.concatenate([acc for _, acc in carries], axis=-1)


def _sb(qkv, tq=256):
    B, _, S, d = qkv.shape
    H = GROUP_HEADS
    return pl.pallas_call(
        functools.partial(_sb_kernel, tq=tq),
        grid=(B, S // tq),
        in_specs=_qkv_specs(S, tq),
        out_specs=pl.BlockSpec((1, tq, H * d), lambda b, i: (b, i, 0)),
        out_shape=jax.ShapeDtypeStruct((B, S, H * d), F32),
        compiler_params=_params(("parallel", "arbitrary")),
        name="sb_attn",
    )(qkv, qkv, qkv)


def _mixout_kernel(y0_ref, y1_ref, y2_ref, y3_ref, g_ref, w_ref, x_ref, lg_ref, lb_ref, o_ref):
    def rms(y):
        return y * lax.rsqrt(jnp.mean(y * y, axis=-1, keepdims=True) + GROUP_RMS_EPS)

    y = jnp.concatenate([rms(r[...]) for r in (y0_ref, y1_ref, y2_ref, y3_ref)], axis=-1)
    y = (y * g_ref[...]).astype(BF16)
    mix = jnp.dot(y, w_ref[...], preferred_element_type=F32)
    o_ref[...] = _layer_norm(ALPHA * x_ref[...] + mix, lg_ref[...], lb_ref[...])


def _mixout(ys, g, w, x2d, lg, lb, tm=512):
    T, D = x2d.shape
    GW = GROUP_WIDTH

    def full(a):
        return pl.BlockSpec(a.shape, lambda i: (0,) * a.ndim)

    return pl.pallas_call(
        _mixout_kernel,
        grid=(T // tm,),
        in_specs=[pl.BlockSpec((tm, GW), lambda i: (i, 0))] * 4
        + [full(g), full(w), pl.BlockSpec((tm, D), lambda i: (i, 0)), full(lg), full(lb)],
        out_specs=pl.BlockSpec((tm, D), lambda i: (i, 0)),
        out_shape=jax.ShapeDtypeStruct((T, D), F32),
        compiler_params=_params(("parallel",)),
        name="mixout",
    )(*ys, g, w, x2d, lg, lb)


def _kv_kernel(m_ref, w_ref, k_ref, v_ref):
    kv = jnp.dot(m_ref[...].astype(BF16), w_ref[...], preferred_element_type=F32)
    k_ref[...] = kv[:, :D_MODEL].astype(BF16)
    v_ref[...] = kv[:, D_MODEL:].astype(BF16)


def _kv(mem2d, wkv, tm=512):
    M, D = mem2d.shape
    return pl.pallas_call(
        _kv_kernel,
        grid=(M // tm,),
        in_specs=[pl.BlockSpec((tm, D), lambda i: (i, 0)),
                  pl.BlockSpec((D, 2 * D), lambda i: (0, 0))],
        out_specs=[pl.BlockSpec((tm, D), lambda i: (i, 0))] * 2,
        out_shape=[jax.ShapeDtypeStruct((M, D), BF16)] * 2,
        compiler_params=_params(("parallel",)),
        name="mem_kv",
    )(mem2d, wkv)


PEER_TB = 128


def _chunk_major_spec(tm):
    return pl.BlockSpec((tm // PEER_TB, ROW_CHUNKS * PEER_TB, LANES), lambda i: (i, 0, 0))


def _chunk_major_shape(T):
    return jax.ShapeDtypeStruct((T // PEER_TB, ROW_CHUNKS * PEER_TB, LANES), F32)


def _to_chunk_major(y, ref):
    for g in range(y.shape[0] // PEER_TB):
        for c in range(ROW_CHUNKS):
            ref[g, c * PEER_TB:(c + 1) * PEER_TB, :] = (
                y[g * PEER_TB:(g + 1) * PEER_TB, c * LANES:(c + 1) * LANES])


def _xattn_kernel(x_ref, wq_ref, k_ref, v_ref, wo_ref, lg_ref, lb_ref, o_ref, oc_ref):
    x = x_ref[...]
    q = jnp.dot(x.astype(BF16), wq_ref[...], preferred_element_type=F32).astype(BF16)
    scale = XA_HEAD_DIM ** -0.5
    outs = []
    for h in range(XA_HEADS):
        sl = slice(h * XA_HEAD_DIM, (h + 1) * XA_HEAD_DIM)
        s = _nt_dot(q[:, sl], k_ref[0, :, sl]) * scale
        p = jnp.exp(s - jnp.max(s, axis=-1, keepdims=True))
        p = p / jnp.sum(p, axis=-1, keepdims=True)
        outs.append(jnp.dot(p.astype(BF16), v_ref[0, :, sl], preferred_element_type=F32))
    o = jnp.concatenate(outs, axis=-1).astype(BF16)
    xa = jnp.dot(o, wo_ref[...], preferred_element_type=F32)
    y = _layer_norm(ALPHA * x + xa, lg_ref[...], lb_ref[...])
    o_ref[...] = y
    _to_chunk_major(y, oc_ref)


def _mix_xattn_kernel(y0_ref, y1_ref, y2_ref, y3_ref, g_ref, w_ref, x_ref, l1g_ref, l1b_ref,
                      wq_ref, k_ref, v_ref, wo_ref, l2g_ref, l2b_ref, o_ref, oc_ref, x1_scr):
    _mixout_kernel(y0_ref, y1_ref, y2_ref, y3_ref, g_ref, w_ref, x_ref, l1g_ref, l1b_ref, x1_scr)
    _xattn_kernel(x1_scr, wq_ref, k_ref, v_ref, wo_ref, l2g_ref, l2b_ref, o_ref, oc_ref)


def _mix_xattn(ys, g, w, x2d, l1g, l1b, wq, k, v, wo, l2g, l2b, seq_len, tm=512):
    T, D = x2d.shape
    M = k.shape[1]
    GW = GROUP_WIDTH
    per_seq = seq_len // tm

    def full(a):
        return pl.BlockSpec(a.shape, lambda i: (0,) * a.ndim)

    tok = pl.BlockSpec((tm, D), lambda i: (i, 0))
    mem = pl.BlockSpec((1, M, D), lambda i: (i // per_seq, 0, 0))
    return pl.pallas_call(
        _mix_xattn_kernel,
        grid=(T // tm,),
        in_specs=[pl.BlockSpec((tm, GW), lambda i: (i, 0))] * 4
        + [full(g), full(w), tok, full(l1g), full(l1b), full(wq), mem, mem, full(wo),
           full(l2g), full(l2b)],
        out_specs=[tok, _chunk_major_spec(tm)],
        out_shape=[jax.ShapeDtypeStruct((T, D), F32), _chunk_major_shape(T)],
        scratch_shapes=[pltpu.VMEM((tm, D), F32)],
        compiler_params=_params(("parallel",)),
        name="mix_xattn",
    )(*ys, g, w, x2d, l1g, l1b, wq, k, v, wo, l2g, l2b)


def _xattn(x2d, wq, k, v, wo, lg, lb, seq_len, tm=512):
    T, D = x2d.shape
    M = k.shape[1]
    per_seq = seq_len // tm

    def full(a):
        return pl.BlockSpec(a.shape, lambda i: (0,) * a.ndim)

    return pl.pallas_call(
        _xattn_kernel,
        grid=(T // tm,),
        in_specs=[pl.BlockSpec((tm, D), lambda i: (i, 0)), full(wq),
                  pl.BlockSpec((1, M, D), lambda i: (i // per_seq, 0, 0)),
                  pl.BlockSpec((1, M, D), lambda i: (i // per_seq, 0, 0)),
                  full(wo), full(lg), full(lb)],
        out_specs=[pl.BlockSpec((tm, D), lambda i: (i, 0)), _chunk_major_spec(tm)],
        out_shape=[jax.ShapeDtypeStruct((T, D), F32), _chunk_major_shape(T)],
        compiler_params=_params(("parallel",)),
        name="xattn",
    )(x2d, wq, k, v, wo, lg, lb)


E_PER_TOK = PEER_HEADS * PEER_TOPK


def _staircase():
    return [(a, b) for a in range(PEER_TOPK) for b in range(PEER_TOPK // (a + 1))]


def _route_kernel(x_ref, wq_ref, k1_ref, k2_ref, e_ref, g_ref, v_scr, i_scr, *, ts):
    xb = x_ref[...].astype(BF16)
    key_id = lax.broadcasted_iota(I32, (N_KEYS, ts), 0).astype(F32)
    qd = 2 * PEER_HALF
    for h in range(PEER_HEADS):
        qry = jnp.dot(xb, wq_ref[:, h * qd:(h + 1) * qd], preferred_element_type=F32)
        for half, kref in ((0, k1_ref), (1, k2_ref)):
            qh = qry[:, half * PEER_HALF:(half + 1) * PEER_HALF].astype(BF16)
            s0 = _nt_dot(kref[...], qh)

            s = s0
            for it in range(PEER_TOPK):
                m = jnp.max(s, axis=0, keepdims=True)
                idx = jnp.min(jnp.where(s == m, key_id, float(N_KEYS)), axis=0, keepdims=True)
                v_scr[half, it, h:h + 1, :] = m
                i_scr[half, it, h:h + 1, :] = idx.astype(I32)
                s = jnp.where(key_id == idx, -jnp.inf, s)

    cells = _staircase()
    cand = tuple(v_scr[0, a] + v_scr[1, b] for a, b in cells)
    expert = [i_scr[0, a] * N_KEYS + i_scr[1, b] for a, b in cells]

    picked = []
    for it in range(PEER_TOPK):
        best_v, best_e, best_c = cand[0], expert[0], jnp.zeros((PEER_HEADS, ts), I32)
        for c in range(1, len(cells)):
            better = cand[c] > best_v
            best_v = jnp.where(better, cand[c], best_v)
            best_e = jnp.where(better, expert[c], best_e)
            best_c = jnp.where(better, c, best_c)
        i_scr[0, it] = best_e * WORD_ROWS
        picked.append(best_v)
        cand = tuple(jnp.where(best_c == c, -jnp.inf, cand[c]) for c in range(len(cells)))
    ex = [jnp.exp(v - picked[0]) for v in picked]
    den = ex[0]
    for it in range(1, PEER_TOPK):
        den = den + ex[it]
    inv = 1.0 / den
    g_ref[...] = jnp.concatenate([e * inv for e in ex], axis=0).T
    rows = pltpu.bitcast(i_scr[0].reshape(E_PER_TOK, ts), F32)
    e_ref[...] = pltpu.bitcast(rows.T, I32)


def _route(x2d, wq, k1, k2, ts=256):
    T, D = x2d.shape
    nb = T // ts

    def full(a):
        return pl.BlockSpec(a.shape, lambda i: (0,) * a.ndim)

    blk = pl.BlockSpec((ts, E_PER_TOK), lambda i: (i, 0))
    return pl.pallas_call(
        functools.partial(_route_kernel, ts=ts),
        grid=(nb,),
        in_specs=[pl.BlockSpec((ts, D), lambda i: (i, 0)), full(wq), full(k1), full(k2)],
        out_specs=[blk, blk],
        out_shape=[jax.ShapeDtypeStruct((T, E_PER_TOK), I32),
                   jax.ShapeDtypeStruct((T, E_PER_TOK), F32)],
        scratch_shapes=[pltpu.VMEM((2, PEER_TOPK, PEER_HEADS, ts), F32),
                        pltpu.VMEM((2, PEER_TOPK, PEER_HEADS, ts), I32)],
        compiler_params=_params(("parallel",)),
        name="peer_route",
    )(x2d, wq, k1, k2)


TOK_GROUP = SUBLANES


N_GROUPS = PEER_TB // TOK_GROUP


def _gathered_rows(ids_ref, tab_ref, t):
    pieces = []
    for j in range(E_PER_TOK):
        r0 = pl.multiple_of(ids_ref[t, j], WORD_ROWS)
        pieces.append(tab_ref[pl.ds(r0, WORD_ROWS), :])
    return jnp.concatenate(pieces, axis=0)


def _token_chunks(tt):
    return pl.ds(tt, ROW_CHUNKS, stride=PEER_TB)


def _chunk_of_col():
    col = lax.broadcasted_iota(I32, (ROW_CHUNKS, E_PER_TOK * ROW_CHUNKS), 1)
    row = lax.broadcasted_iota(I32, (ROW_CHUNKS, E_PER_TOK * ROW_CHUNKS), 0)
    return (col & (ROW_CHUNKS - 1)) == row


def _with_ids(ids_hbm, bufs, sems, body):
    i = pl.program_id(0)

    def block_copy(step, slot):
        rows = pl.ds(pl.multiple_of(step * PEER_TB, PEER_TB), PEER_TB)
        return pltpu.make_async_copy(ids_hbm.at[rows], bufs[slot], sems.at[slot])

    @pl.when(i == 0)
    def _():
        block_copy(0, 0).start()

    for slot in range(2):
        @pl.when(i % 2 == slot)
        def _(slot=slot):
            block_copy(i, slot).wait()

            @pl.when(i + 1 < pl.num_programs(0))
            def _():
                block_copy(i + 1, 1 - slot).start()

            body(bufs[slot])


def _peer_score_kernel(ids_hbm, x_ref, tab_ref, s_ref, ids_a, ids_b, sems):
    _with_ids(ids_hbm, (ids_a, ids_b), sems,
              functools.partial(_peer_score_body, x_ref, tab_ref, s_ref))


def _peer_score_body(x_ref, tab_ref, s_ref, ids_ref):
    own_chunk = _chunk_of_col()
    c_id = lax.broadcasted_iota(I32, (E_PER_TOK * ROW_CHUNKS, E_PER_TOK), 0)
    e_id = lax.broadcasted_iota(I32, (E_PER_TOK * ROW_CHUNKS, E_PER_TOK), 1)
    fold = jnp.where(c_id >> CHUNK_SHIFT == e_id, 1.0, 0.0).astype(BF16)

    for g in range(N_GROUPS):
        tok0 = g * TOK_GROUP
        parts = []
        for tt in range(TOK_GROUP):
            rows = pltpu.bitcast(_gathered_rows(ids_ref, tab_ref, tok0 + tt), BF16)
            xt = x_ref[0, _token_chunks(tok0 + tt), :].astype(BF16)
            full = _nt_dot(xt, rows)
            parts.append(jnp.where(own_chunk, full, 0.0))
        part = jnp.concatenate(parts, axis=0)
        hi = part.astype(BF16)
        lo = (part - hi.astype(F32)).astype(BF16)
        sc = (jnp.dot(hi, fold, preferred_element_type=F32)
              + jnp.dot(lo, fold, preferred_element_type=F32))
        sc = jnp.sum(sc.reshape(TOK_GROUP, ROW_CHUNKS, E_PER_TOK), axis=1)
        s_ref[tok0:tok0 + TOK_GROUP, :] = sc


def _peer_mix_kernel(ids_hbm, s_ref, gate_ref, tab_ref, o_ref, ids_a, ids_b, sems):
    _with_ids(ids_hbm, (ids_a, ids_b), sems,
              functools.partial(_peer_mix_body, s_ref, gate_ref, tab_ref, o_ref))


def _peer_mix_body(s_ref, gate_ref, tab_ref, o_ref, ids_ref):
    own_chunk = _chunk_of_col()
    e_id = lax.broadcasted_iota(I32, (E_PER_TOK, E_PER_TOK * ROW_CHUNKS), 0)
    c_id = lax.broadcasted_iota(I32, (E_PER_TOK, E_PER_TOK * ROW_CHUNKS), 1)
    spread = jnp.where(c_id >> CHUNK_SHIFT == e_id, 1.0, 0.0).astype(BF16)

    for g in range(N_GROUPS):
        tok0 = g * TOK_GROUP
        sl = slice(tok0, tok0 + TOK_GROUP)
        s = s_ref[sl, :]
        act = 0.5 * s * (1.0 + lax.erf(s * (2.0 ** -0.5)))
        coef = (gate_ref[sl, :] * act).astype(BF16)
        coef = jnp.dot(coef, spread, preferred_element_type=F32)
        for tt in range(TOK_GROUP):
            rows = pltpu.bitcast(_gathered_rows(ids_ref, tab_ref, tok0 + tt), BF16)
            ct = jnp.broadcast_to(coef[tt:tt + 1], (ROW_CHUNKS, E_PER_TOK * ROW_CHUNKS))
            ct = jnp.where(own_chunk, ct, 0.0).astype(BF16)
            o_ref[0, _token_chunks(tok0 + tt), :] = jnp.dot(ct, rows, preferred_element_type=F32)


def _peer_experts(ids, gate, xc, tab_u, tab_v):
    T = ids.shape[0]
    tb = PEER_TB
    nb = T // tb
    ids_spec = pl.BlockSpec(memory_space=pl.ANY)
    ids_scratch = [pltpu.SMEM((tb, E_PER_TOK), I32), pltpu.SMEM((tb, E_PER_TOK), I32),
                   pltpu.SemaphoreType.DMA((2,))]
    tab_spec = pl.BlockSpec(tab_u.shape, lambda i: (0, 0), pipeline_mode=pl.Buffered(1))
    tok_spec = pl.BlockSpec((tb, E_PER_TOK), lambda i: (i, 0))
    row_spec = _chunk_major_spec(tb)
    s = pl.pallas_call(
        _peer_score_kernel,
        grid=(nb,),
        in_specs=[ids_spec, row_spec, tab_spec],
        out_specs=tok_spec,
        out_shape=jax.ShapeDtypeStruct((T, E_PER_TOK), F32),
        scratch_shapes=ids_scratch,
        compiler_params=_params(("arbitrary",)),
        name="peer_score",
    )(ids, xc, tab_u)
    return pl.pallas_call(
        _peer_mix_kernel,
        grid=(nb,),
        in_specs=[ids_spec, tok_spec, tok_spec, tab_spec],
        out_specs=row_spec,
        out_shape=_chunk_major_shape(T),
        scratch_shapes=ids_scratch,
        compiler_params=_params(("arbitrary",)),
        name="peer_mix",
    )(ids, s, gate, tab_v)


def _resln_kernel(x_ref, y_ref, lg_ref, lb_ref, o_ref):
    G = x_ref.shape[0]
    z = ALPHA * x_ref[...] + y_ref[...]
    zs = [z[:, c * PEER_TB:(c + 1) * PEER_TB, :] for c in range(ROW_CHUNKS)]
    tot = zs[0]
    for c in range(1, ROW_CHUNKS):
        tot = tot + zs[c]
    mu = jnp.sum(tot, axis=-1, keepdims=True) * (1.0 / D_MODEL)
    zc = [a - mu for a in zs]
    sq = zc[0] * zc[0]
    for c in range(1, ROW_CHUNKS):
        sq = sq + zc[c] * zc[c]
    rstd = lax.rsqrt(jnp.sum(sq, axis=-1, keepdims=True) * (1.0 / D_MODEL) + LN_EPS)
    for c in range(ROW_CHUNKS):
        cols = slice(c * LANES, (c + 1) * LANES)
        y = (zc[c] * rstd).reshape(G * PEER_TB, LANES)
        o_ref[:, cols] = y * lg_ref[:, cols] + lb_ref[:, cols]


def _resln(xc, yc, lg, lb, tm=512):
    T = xc.shape[0] * PEER_TB
    blk = _chunk_major_spec(tm)
    par = pl.BlockSpec((1, D_MODEL), lambda i: (0, 0))
    return pl.pallas_call(
        _resln_kernel,
        grid=(T // tm,),
        in_specs=[blk, blk, par, par],
        out_specs=pl.BlockSpec((tm, D_MODEL), lambda i: (i, 0)),
        out_shape=jax.ShapeDtypeStruct((T, D_MODEL), F32),
        compiler_params=_params(("parallel",)),
        name="res_ln",
    )(xc, yc, lg, lb)


def _pack_kernel(t_ref, o_ref):
    def bf16_bits(v):
        return pltpu.bitcast(v.astype(BF16).astype(F32), jnp.uint32)

    te = t_ref.shape[0]
    for r in range(WORD_ROWS):
        lo = bf16_bits(t_ref[:, (2 * r) * LANES:(2 * r + 1) * LANES])
        hi = bf16_bits(t_ref[:, (2 * r + 1) * LANES:(2 * r + 2) * LANES])
        words = (hi & jnp.uint32(0xFFFF0000)) | (lo >> 16)
        o_ref[pl.ds(r, te, stride=WORD_ROWS), :] = pltpu.bitcast(words, I32)


def _pack_table(t, layer, te=512):
    _, E, D = t.shape
    return pl.pallas_call(
        _pack_kernel,
        grid=(E // te,),
        in_specs=[pl.BlockSpec((None, te, D), lambda i: (layer, i, 0))],
        out_specs=pl.BlockSpec((te * WORD_ROWS, LANES), lambda i: (i, 0)),
        out_shape=jax.ShapeDtypeStruct((E * WORD_ROWS, LANES), I32),
        compiler_params=_params(("parallel",)),
        name="pack_table",
    )(t)


def _block_diag(w):
    H, d, _ = w.shape
    out = jnp.zeros((H * d, H * d), w.dtype)
    for h in range(H):
        out = out.at[h * d:(h + 1) * d, h * d:(h + 1) * d].set(w[h])
    return out


def _cat_w_in_kernel(w_ref, o_ref):
    GW = GROUP_WIDTH
    off_f = 5 * GW
    off_sb = off_f + GROUP_HEADS
    w = w_ref[...].astype(BF16)
    o_ref[:, :off_f] = w[:, :off_f]
    o_ref[:, off_f:C_F] = w[:, off_sb:N_IN]
    o_ref[:, C_F:C_END] = jnp.zeros((w.shape[0], C_END - C_F), BF16)
    o_ref[:, C_F:C_F + GROUP_HEADS] = w[:, off_f:off_sb]


def _cat_w_in(w_in, layer, tr=256):
    _, D, N = w_in.shape
    return pl.pallas_call(
        _cat_w_in_kernel,
        grid=(D // tr,),
        in_specs=[pl.BlockSpec((None, tr, N), lambda i: (layer, i, 0))],
        out_specs=pl.BlockSpec((tr, C_END), lambda i: (i, 0)),
        out_shape=jax.ShapeDtypeStruct((D, C_END), BF16),
        compiler_params=_params(("parallel",)),
        name="regroup_w_in",
    )(w_in)


def _row(v, width=None):
    v = v.reshape(1, -1).astype(F32)
    if width is not None and v.shape[1] < width:
        v = jnp.pad(v, ((0, 0), (0, width - v.shape[1])))
    return v


def _layer(x2d, kmem, vmem_, B, S, p, tables, layer, tq=256):
    T = B * S
    rg, fox, sb, sc, f = _inproj(x2d, _cat_w_in(tables[2], layer), B, S)
    yrg, ysc, cumt = _rgsc(
        rg.reshape(B, S, -1), sc.reshape(B, S, -1), f.reshape(B, S, -1),
        p["rg_conv_w"], _row(p["rg_conv_b"]), _block_diag(p["rg_wa"]).astype(BF16), _row(p["rg_ba"]),
        _block_diag(p["rg_wi"]).astype(BF16), _row(p["rg_bi"]), _row(p["rg_lambda"]),
        p["sc_conv_w"], _row(p["fox_bf"], LANES), tq)
    yfox = _fox(fox, cumt, tq=tq)
    ysb = _sb(sb, tq=tq)
    ys = (yrg.reshape(T, -1), yfox.reshape(T, -1), ysb.reshape(T, -1), ysc.reshape(T, -1))
    x2, x2c = _mix_xattn(ys, _row(p["mix_norm_g"]), p["w_out"].astype(BF16), x2d,
                         _row(p["ln1_g"]), _row(p["ln1_b"]), p["xa_wq"].astype(BF16), kmem, vmem_,
                         p["xa_wo"].astype(BF16), _row(p["ln2_g"]), _row(p["ln2_b"]), S)
    ids, gate = _route(x2, p["peer_wq"].astype(BF16), p["peer_k1"].astype(BF16),
                       p["peer_k2"].astype(BF16))
    ffc = _peer_experts(ids, gate, x2c, _pack_table(tables[0], layer), _pack_table(tables[1], layer))
    return _resln(x2c, ffc, _row(p["ln3_g"]), _row(p["ln3_b"]))


_LAYER_PARAMS = ("w_in", "w_out", "rg_conv_w", "rg_conv_b", "rg_wa", "rg_ba", "rg_wi", "rg_bi",
                 "rg_lambda", "fox_bf", "sc_conv_w", "mix_norm_g", "ln1_g", "ln1_b", "xa_wq",
                 "xa_wkv", "xa_wo", "ln2_g", "ln2_b", "peer_wq", "peer_k1", "peer_k2", "peer_u",
                 "peer_v", "ln3_g", "ln3_b")


def kernel(x, mem, w_in, w_out, rg_conv_w, rg_conv_b, rg_wa, rg_ba, rg_wi, rg_bi, rg_lambda, fox_bf, sc_conv_w, mix_norm_g, ln1_g, ln1_b, xa_wq, xa_wkv, xa_wo, ln2_g, ln2_b, peer_wq, peer_k1, peer_k2, peer_u, peer_v, ln3_g, ln3_b):
    stacked = dict(zip(_LAYER_PARAMS, (
        w_in, w_out, rg_conv_w, rg_conv_b, rg_wa, rg_ba, rg_wi, rg_bi, rg_lambda, fox_bf,
        sc_conv_w, mix_norm_g, ln1_g, ln1_b, xa_wq, xa_wkv, xa_wo, ln2_g, ln2_b, peer_wq,
        peer_k1, peer_k2, peer_u, peer_v, ln3_g, ln3_b)))
    B, S, D = x.shape
    M = mem.shape[1]
    x2d = x.reshape(B * S, D)
    mem2d = mem.reshape(B * M, D)
    for l in range(w_in.shape[0]):
        p = {k: v[l] for k, v in stacked.items() if k not in ("peer_u", "peer_v", "w_in")}
        kmem, vmem_ = _kv(mem2d, p["xa_wkv"].astype(BF16))
        x2d = _layer(x2d, kmem.reshape(B, M, D), vmem_.reshape(B, M, D), B, S, p,
                     (peer_u, peer_v, w_in), l)
    return x2d.reshape(B, S, D)
```
